```python
import math
import jax, jax.numpy as jnp
from jax import lax
import numpy as np

D_MODEL = 2048
BATCH = 8
SEQ = 8192
DEPTH = 2

PLE_DIM = 256
BLOCK_Q = 128
NORM_EPS = 1e-6
N_BRANCH = 3
BRANCH_W = D_MODEL // 2

SSM_GROUP = 16
SSM_STATE = 64
SSM_GROUPS = BRANCH_W // SSM_GROUP
DT_MIN = 1e-3
DT_MAX = 1e-1

MLA_NOPE = 128
MLA_ROPE = 64
MLA_V = 128
MLA_HEADS = BRANCH_W // MLA_V
MLA_Q_RANK = D_MODEL // 4
MLA_KV_RANK = D_MODEL // 4
ROPE_THETA = 10000.0

SB_HEAD_DIM = 128
SB_HEADS = BRANCH_W // SB_HEAD_DIM

N_SSM_IN = BRANCH_W
N_SB_IN = 3 * BRANCH_W
N_GATE_PATH = N_BRANCH * BRANCH_W
N_MERGE = N_BRANCH * D_MODEL
SPLIT_SIZES = (N_SSM_IN, MLA_Q_RANK, MLA_KV_RANK, MLA_ROPE, N_SB_IN, N_GATE_PATH, N_MERGE)
N_IN = N_SSM_IN + MLA_Q_RANK + MLA_KV_RANK + MLA_ROPE + N_SB_IN + N_GATE_PATH + N_MERGE

kernel_name = "hybrid_s5_mla_stickbreaking_gated_merge"


def rms_norm(x, g):
    xf = x.astype(jnp.float32)
    y = xf * lax.rsqrt(jnp.mean(xf * xf, axis=-1, keepdims=True) + NORM_EPS)
    return (y * g.astype(jnp.float32)).astype(x.dtype)


def apply_rope(x, cos, sin):
    half = x.shape[-1] // 2
    xf = x.astype(jnp.float32)
    x1, x2 = xf[..., :half], xf[..., half:]
    shape = (1, x.shape[1]) + (1,) * (x.ndim - 3) + (half,)
    c = cos.reshape(shape)
    s = sin.reshape(shape)
    return jnp.concatenate([x1 * c - x2 * s, x1 * s + x2 * c], axis=-1).astype(x.dtype)


def s5_mixer(u, lam_re, lam_im, log_dt, b_re, b_im, c_re, c_im, d_skip, w_glu):
    bsz, seqlen, _ = u.shape
    f32 = jnp.float32
    uf = u.astype(f32).reshape(bsz, seqlen, SSM_GROUPS, SSM_GROUP)
    lr = lam_re.astype(f32)
    li = lam_im.astype(f32)
    dt = jnp.exp(log_dt.astype(f32))[:, None]
    mag = jnp.exp(lr * dt)
    ab_re = mag * jnp.cos(li * dt)
    ab_im = mag * jnp.sin(li * dt)
    den = lr * lr + li * li
    nr = ab_re - 1.0
    ni = ab_im
    coef_re = (nr * lr + ni * li) / den
    coef_im = (ni * lr - nr * li) / den
    br = b_re.astype(f32)
    bi = b_im.astype(f32)
    bb_re = coef_re[..., None] * br - coef_im[..., None] * bi
    bb_im = coef_re[..., None] * bi + coef_im[..., None] * br
    bu_re = jnp.einsum('bsgh,gph->bsgp', uf, bb_re)
    bu_im = jnp.einsum('bsgh,gph->bsgp', uf, bb_im)
    a_re = jnp.broadcast_to(ab_re, bu_re.shape)
    a_im = jnp.broadcast_to(ab_im, bu_im.shape)

    def combine(e1, e2):
        a1r, a1i, b1r, b1i = e1
        a2r, a2i, b2r, b2i = e2
        return (a1r * a2r - a1i * a2i,
                a1r * a2i + a1i * a2r,
                a2r * b1r - a2i * b1i + b2r,
                a2r * b1i + a2i * b1r + b2i)

    _, _, h_re, h_im = lax.associative_scan(combine, (a_re, a_im, bu_re, bu_im), axis=1)
    y = (jnp.einsum('bsgp,ghp->bsgh', h_re, c_re.astype(f32))
         - jnp.einsum('bsgp,ghp->bsgh', h_im, c_im.astype(f32))
         + d_skip.astype(f32).reshape(SSM_GROUPS, SSM_GROUP) * uf)
    y = jax.nn.gelu(y.reshape(bsz, seqlen, BRANCH_W))
    y = y * jax.nn.sigmoid(y @ w_glu.astype(f32))
    return y.astype(u.dtype)


def mla_mixer(q_lat, kv_lat, k_rope_raw, g_q, g_kv, w_uq, w_ukv, cos, sin):
    bsz, seqlen, _ = q_lat.shape
    q = (rms_norm(q_lat, g_q) @ w_uq).reshape(bsz, seqlen, MLA_HEADS, MLA_NOPE + MLA_ROPE)
    q_nope = q[..., :MLA_NOPE]
    q_rope = apply_rope(q[..., MLA_NOPE:], cos, sin)
    kv = (rms_norm(kv_lat, g_kv) @ w_ukv).reshape(bsz, seqlen, MLA_HEADS, MLA_NOPE + MLA_V)
    k_nope = kv[..., :MLA_NOPE]
    v = kv[..., MLA_NOPE:]
    k_rope = apply_rope(k_rope_raw, cos, sin)
    scale = (MLA_NOPE + MLA_ROPE) ** -0.5
    outs = []
    for blk in range(seqlen // BLOCK_Q):
        q0, q1 = blk * BLOCK_Q, (blk + 1) * BLOCK_Q
        s = (jnp.einsum('bqhd,bkhd->bhqk', q_nope[:, q0:q1], k_nope[:, :q1])
             + jnp.einsum('bqhr,bkr->bhqk', q_rope[:, q0:q1], k_rope[:, :q1]))
        s = s.astype(jnp.float32) * scale
        mask = jnp.arange(q1)[None, :] <= (q0 + jnp.arange(BLOCK_Q))[:, None]
        s = jnp.where(mask, s, -jnp.inf)
        pr = jax.nn.softmax(s, axis=-1).astype(v.dtype)
        outs.append(jnp.einsum('bhqk,bkhd->bqhd', pr, v[:, :q1]))
    o = jnp.concatenate(outs, axis=1)
    return o.reshape(bsz, seqlen, MLA_HEADS * MLA_V)


def stick_breaking_mixer(q, k, v):
    bsz, seqlen = q.shape[0], q.shape[1]
    scale = SB_HEAD_DIM ** -0.5
    outs = []
    for blk in range(seqlen // BLOCK_Q):
        q0, q1 = blk * BLOCK_Q, (blk + 1) * BLOCK_Q
        z = jnp.einsum('bqhd,bkhd->bhqk', q[:, q0:q1], k[:, :q1]).astype(jnp.float32) * scale
        mask = jnp.arange(q1)[None, :] < (q0 + jnp.arange(BLOCK_Q))[:, None]
        log_beta = jax.nn.log_sigmoid(z)
        log_rest = jnp.where(mask, jax.nn.log_sigmoid(-z), 0.0)
        later = lax.cumsum(log_rest, axis=3, reverse=True) - log_rest
        w = jnp.where(mask, jnp.exp(log_beta + later), 0.0).astype(v.dtype)
        outs.append(jnp.einsum('bhqk,bkhd->bqhd', w, v[:, :q1]))
    o = jnp.concatenate(outs, axis=1)
    return o.reshape(bsz, seqlen, SB_HEADS * SB_HEAD_DIM)


def hybrid_layer(x, p_i, ln_g, w_in, lam_re, lam_im, log_dt, b_re, b_im, c_re, c_im, d_skip, w_glu,
                 g_q, g_kv, w_uq, w_ukv, w_branch, w_out, ple_g, w_ple_gate, w_ple_proj, cos, sin):
    bsz, seqlen, _ = x.shape
    h = rms_norm(x, ln_g)
    proj = h @ w_in
    idx = []
    acc = 0
    for size in SPLIT_SIZES[:-1]:
        acc += size
        idx.append(acc)
    u_ssm, q_lat, kv_lat, k_rope_raw, qkv_sb, gate_path, merge_logits = jnp.split(proj, idx, axis=-1)

    y_ssm = s5_mixer(u_ssm, lam_re, lam_im, log_dt, b_re, b_im, c_re, c_im, d_skip, w_glu)
    y_mla = mla_mixer(q_lat, kv_lat, k_rope_raw, g_q, g_kv, w_uq, w_ukv, cos, sin)
    q_sb, k_sb, v_sb = jnp.split(qkv_sb.reshape(bsz, seqlen, 3, SB_HEADS, SB_HEAD_DIM), 3, axis=2)
    y_sb = stick_breaking_mixer(q_sb[:, :, 0], k_sb[:, :, 0], v_sb[:, :, 0])

    ys = jnp.stack([y_ssm, y_mla, y_sb], axis=2)
    ys = ys * jax.nn.silu(gate_path.reshape(bsz, seqlen, N_BRANCH, BRANCH_W))
    branch_out = jnp.einsum('bsnw,nwd->bsnd', ys, w_branch)
    merge_gate = jax.nn.sigmoid(merge_logits.reshape(bsz, seqlen, N_BRANCH, D_MODEL))
    merged = jnp.sum(merge_gate * branch_out, axis=2)
    x = x + merged @ w_out

    e = p_i @ w_ple_proj
    g = jax.nn.sigmoid(rms_norm(x, ple_g) @ w_ple_gate)
    return x + g * e


def _fwd_setup_inputs(seed: int = 0) -> dict:
    key = jax.random.key(seed)
    ks = jax.random.split(key, 23)
    f32 = jnp.float32

    def nrm(k, shape, scale):
        return jax.random.normal(k, shape, f32) * scale

    def gain(k, shape):
        return 1.0 + nrm(k, shape, 0.02)

    G, P, HG, W = SSM_GROUPS, SSM_STATE, SSM_GROUP, BRANCH_W
    n_idx = jnp.arange(P, dtype=f32)
    return {
        'x': nrm(ks[0], (BATCH, SEQ, D_MODEL), 1.0),
        'p': nrm(ks[1], (DEPTH, BATCH, SEQ, PLE_DIM), 1.0),
        'ln_g': gain(ks[2], (DEPTH, D_MODEL)),
        'w_in': nrm(ks[3], (DEPTH, D_MODEL, N_IN), D_MODEL ** -0.5),
        'ssm_lam_re': -0.5 + nrm(ks[4], (DEPTH, G, P), 0.01),
        'ssm_lam_im': math.pi * n_idx + nrm(ks[5], (DEPTH, G, P), 0.01),
        'ssm_log_dt': jax.random.uniform(ks[6], (DEPTH, G), f32, math.log(DT_MIN), math.log(DT_MAX)),
        'ssm_b_re': nrm(ks[7], (DEPTH, G, P, HG), (2.0 * HG) ** -0.5),
        'ssm_b_im': nrm(ks[8], (DEPTH, G, P, HG), (2.0 * HG) ** -0.5),
        'ssm_c_re': nrm(ks[9], (DEPTH, G, HG, P), (2.0 * P) ** -0.5),
        'ssm_c_im': nrm(ks[10], (DEPTH, G, HG, P), (2.0 * P) ** -0.5),
        'ssm_d': nrm(ks[11], (DEPTH, W), 1.0),
        'ssm_w_glu': nrm(ks[12], (DEPTH, W, W), W ** -0.5),
        'mla_g_q': gain(ks[13], (DEPTH, MLA_Q_RANK)),
        'mla_g_kv': gain(ks[14], (DEPTH, MLA_KV_RANK)),
        'mla_w_uq': nrm(ks[15], (DEPTH, MLA_Q_RANK, MLA_HEADS * (MLA_NOPE + MLA_ROPE)), MLA_Q_RANK ** -0.5),
        'mla_w_ukv': nrm(ks[16], (DEPTH, MLA_KV_RANK, MLA_HEADS * (MLA_NOPE + MLA_V)), MLA_KV_RANK ** -0.5),
        'w_branch': nrm(ks[17], (DEPTH, N_BRANCH, W, D_MODEL), W ** -0.5),
        'w_out': nrm(ks[18], (DEPTH, D_MODEL, D_MODEL), D_MODEL ** -0.5),
        'ple_g': gain(ks[19], (DEPTH, D_MODEL)),
        'w_ple_gate': nrm(ks[20], (DEPTH, D_MODEL, D_MODEL), D_MODEL ** -0.5),
        'w_ple_proj': nrm(ks[21], (DEPTH, PLE_DIM, D_MODEL), PLE_DIM ** -0.5),
        'final_g': gain(ks[22], (D_MODEL,)),
    }


def _fwd_reference(x, p, ln_g, w_in, ssm_lam_re, ssm_lam_im, ssm_log_dt, ssm_b_re, ssm_b_im, ssm_c_re, ssm_c_im,
              ssm_d, ssm_w_glu, mla_g_q, mla_g_kv, mla_w_uq, mla_w_ukv, w_branch, w_out, ple_g, w_ple_gate,
              w_ple_proj, final_g):
    seqlen = x.shape[1]
    pos = jnp.arange(seqlen, dtype=jnp.float32)
    inv_freq = ROPE_THETA ** (-jnp.arange(0, MLA_ROPE, 2, dtype=jnp.float32) / MLA_ROPE)
    ang = pos[:, None] * inv_freq[None, :]
    cos, sin = jnp.cos(ang), jnp.sin(ang)
    for i in range(DEPTH):
        x = hybrid_layer(x, p[i], ln_g[i], w_in[i], ssm_lam_re[i], ssm_lam_im[i], ssm_log_dt[i],
                         ssm_b_re[i], ssm_b_im[i], ssm_c_re[i], ssm_c_im[i], ssm_d[i], ssm_w_glu[i],
                         mla_g_q[i], mla_g_kv[i], mla_w_uq[i], mla_w_ukv[i], w_branch[i], w_out[i],
                         ple_g[i], w_ple_gate[i], w_ple_proj[i], cos, sin)
    return rms_norm(x, final_g)


import jax as _jax
import jax.numpy as _jnp

TWIN_FORMAT = 'train_step'
FWD_PARAMS = ['x', 'p', 'ln_g', 'w_in', 'ssm_lam_re', 'ssm_lam_im', 'ssm_log_dt', 'ssm_b_re', 'ssm_b_im', 'ssm_c_re', 'ssm_c_im', 'ssm_d', 'ssm_w_glu', 'mla_g_q', 'mla_g_kv', 'mla_w_uq', 'mla_w_ukv', 'w_branch', 'w_out', 'ple_g', 'w_ple_gate', 'w_ple_proj', 'final_g']
TWIN_WEIGHTS = ['ln_g', 'w_in', 'ssm_lam_re', 'ssm_lam_im', 'ssm_log_dt', 'ssm_b_re', 'ssm_b_im', 'ssm_c_re', 'ssm_c_im', 'ssm_d', 'ssm_w_glu', 'mla_g_q', 'mla_g_kv', 'mla_w_uq', 'mla_w_ukv', 'w_branch', 'w_out', 'ple_g', 'w_ple_gate', 'w_ple_proj', 'final_g']
TWIN_DIFF_INPUT = 'x'
TWIN_INPUTS = ['x', 'p', 'ln_g', 'w_in', 'ssm_lam_re', 'ssm_lam_im', 'ssm_log_dt', 'ssm_b_re', 'ssm_b_im', 'ssm_c_re', 'ssm_c_im', 'ssm_d', 'ssm_w_glu', 'mla_g_q', 'mla_g_kv', 'mla_w_uq', 'mla_w_ukv', 'w_branch', 'w_out', 'ple_g', 'w_ple_gate', 'w_ple_proj', 'final_g', 'loss_target', 'm_ln_g', 'm_w_in', 'm_ssm_lam_re', 'm_ssm_lam_im', 'm_ssm_log_dt', 'm_ssm_b_re', 'm_ssm_b_im', 'm_ssm_c_re', 'm_ssm_c_im', 'm_ssm_d', 'm_ssm_w_glu', 'm_mla_g_q', 'm_mla_g_kv', 'm_mla_w_uq', 'm_mla_w_ukv', 'm_w_branch', 'm_w_out', 'm_ple_g', 'm_w_ple_gate', 'm_w_ple_proj', 'm_final_g', 'v_ln_g', 'v_w_in', 'v_ssm_lam_re', 'v_ssm_lam_im', 'v_ssm_log_dt', 'v_ssm_b_re', 'v_ssm_b_im', 'v_ssm_c_re', 'v_ssm_c_im', 'v_ssm_d', 'v_ssm_w_glu', 'v_mla_g_q', 'v_mla_g_kv', 'v_mla_w_uq', 'v_mla_w_ukv', 'v_w_branch', 'v_w_out', 'v_ple_g', 'v_w_ple_gate', 'v_w_ple_proj', 'v_final_g']
TWIN_OUTPUTS = ['loss', 'grad_x', 'grad_ln_g', 'grad_w_in', 'grad_ssm_lam_re', 'grad_ssm_lam_im', 'grad_ssm_log_dt', 'grad_ssm_b_re', 'grad_ssm_b_im', 'grad_ssm_c_re', 'grad_ssm_c_im', 'grad_ssm_d', 'grad_ssm_w_glu', 'grad_mla_g_q', 'grad_mla_g_kv', 'grad_mla_w_uq', 'grad_mla_w_ukv', 'grad_w_branch', 'grad_w_out', 'grad_ple_g', 'grad_w_ple_gate', 'grad_w_ple_proj', 'grad_final_g', 'delta_ln_g', 'delta_w_in', 'delta_ssm_lam_re', 'delta_ssm_lam_im', 'delta_ssm_log_dt', 'delta_ssm_b_re', 'delta_ssm_b_im', 'delta_ssm_c_re', 'delta_ssm_c_im', 'delta_ssm_d', 'delta_ssm_w_glu', 'delta_mla_g_q', 'delta_mla_g_kv', 'delta_mla_w_uq', 'delta_mla_w_ukv', 'delta_w_branch', 'delta_w_out', 'delta_ple_g', 'delta_w_ple_gate', 'delta_w_ple_proj', 'delta_final_g', 'new_m_ln_g', 'new_m_w_in', 'new_m_ssm_lam_re', 'new_m_ssm_lam_im', 'new_m_ssm_log_dt', 'new_m_ssm_b_re', 'new_m_ssm_b_im', 'new_m_ssm_c_re', 'new_m_ssm_c_im', 'new_m_ssm_d', 'new_m_ssm_w_glu', 'new_m_mla_g_q', 'new_m_mla_g_kv', 'new_m_mla_w_uq', 'new_m_mla_w_ukv', 'new_m_w_branch', 'new_m_w_out', 'new_m_ple_g', 'new_m_w_ple_gate', 'new_m_w_ple_proj', 'new_m_final_g', 'new_v_ln_g', 'new_v_w_in', 'new_v_ssm_lam_re', 'new_v_ssm_lam_im', 'new_v_ssm_log_dt', 'new_v_ssm_b_re', 'new_v_ssm_b_im', 'new_v_ssm_c_re', 'new_v_ssm_c_im', 'new_v_ssm_d', 'new_v_ssm_w_glu', 'new_v_mla_g_q', 'new_v_mla_g_kv', 'new_v_mla_w_uq', 'new_v_mla_w_ukv', 'new_v_w_branch', 'new_v_w_out', 'new_v_ple_g', 'new_v_w_ple_gate', 'new_v_w_ple_proj', 'new_v_final_g']
TWIN_LEAF_KINDS = {'loss': 'loss', 'grad_x': 'grad_x', 'grad_ln_g': 'grad_w', 'grad_w_in': 'grad_w', 'grad_ssm_lam_re': 'grad_w', 'grad_ssm_lam_im': 'grad_w', 'grad_ssm_log_dt': 'grad_w', 'grad_ssm_b_re': 'grad_w', 'grad_ssm_b_im': 'grad_w', 'grad_ssm_c_re': 'grad_w', 'grad_ssm_c_im': 'grad_w', 'grad_ssm_d': 'grad_w', 'grad_ssm_w_glu': 'grad_w', 'grad_mla_g_q': 'grad_w', 'grad_mla_g_kv': 'grad_w', 'grad_mla_w_uq': 'grad_w', 'grad_mla_w_ukv': 'grad_w', 'grad_w_branch': 'grad_w', 'grad_w_out': 'grad_w', 'grad_ple_g': 'grad_w', 'grad_w_ple_gate': 'grad_w', 'grad_w_ple_proj': 'grad_w', 'grad_final_g': 'grad_w', 'delta_ln_g': 'delta_w', 'delta_w_in': 'delta_w', 'delta_ssm_lam_re': 'delta_w', 'delta_ssm_lam_im': 'delta_w', 'delta_ssm_log_dt': 'delta_w', 'delta_ssm_b_re': 'delta_w', 'delta_ssm_b_im': 'delta_w', 'delta_ssm_c_re': 'delta_w', 'delta_ssm_c_im': 'delta_w', 'delta_ssm_d': 'delta_w', 'delta_ssm_w_glu': 'delta_w', 'delta_mla_g_q': 'delta_w', 'delta_mla_g_kv': 'delta_w', 'delta_mla_w_uq': 'delta_w', 'delta_mla_w_ukv': 'delta_w', 'delta_w_branch': 'delta_w', 'delta_w_out': 'delta_w', 'delta_ple_g': 'delta_w', 'delta_w_ple_gate': 'delta_w', 'delta_w_ple_proj': 'delta_w', 'delta_final_g': 'delta_w', 'new_m_ln_g': 'new_m', 'new_m_w_in': 'new_m', 'new_m_ssm_lam_re': 'new_m', 'new_m_ssm_lam_im': 'new_m', 'new_m_ssm_log_dt': 'new_m', 'new_m_ssm_b_re': 'new_m', 'new_m_ssm_b_im': 'new_m', 'new_m_ssm_c_re': 'new_m', 'new_m_ssm_c_im': 'new_m', 'new_m_ssm_d': 'new_m', 'new_m_ssm_w_glu': 'new_m', 'new_m_mla_g_q': 'new_m', 'new_m_mla_g_kv': 'new_m', 'new_m_mla_w_uq': 'new_m', 'new_m_mla_w_ukv': 'new_m', 'new_m_w_branch': 'new_m', 'new_m_w_out': 'new_m', 'new_m_ple_g': 'new_m', 'new_m_w_ple_gate': 'new_m', 'new_m_w_ple_proj': 'new_m', 'new_m_final_g': 'new_m', 'new_v_ln_g': 'new_v', 'new_v_w_in': 'new_v', 'new_v_ssm_lam_re': 'new_v', 'new_v_ssm_lam_im': 'new_v', 'new_v_ssm_log_dt': 'new_v', 'new_v_ssm_b_re': 'new_v', 'new_v_ssm_b_im': 'new_v', 'new_v_ssm_c_re': 'new_v', 'new_v_ssm_c_im': 'new_v', 'new_v_ssm_d': 'new_v', 'new_v_ssm_w_glu': 'new_v', 'new_v_mla_g_q': 'new_v', 'new_v_mla_g_kv': 'new_v', 'new_v_mla_w_uq': 'new_v', 'new_v_mla_w_ukv': 'new_v', 'new_v_w_branch': 'new_v', 'new_v_w_out': 'new_v', 'new_v_ple_g': 'new_v', 'new_v_w_ple_gate': 'new_v', 'new_v_w_ple_proj': 'new_v', 'new_v_final_g': 'new_v'}


def _forward(args):
    return _fwd_reference(*[args[k] for k in FWD_PARAMS])


def _output_shape():
    def fwd():
        inp = _fwd_setup_inputs(0)
        return _fwd_reference(*[inp[k] for k in FWD_PARAMS])
    out = _jax.eval_shape(fwd)
    return out.shape, out.dtype

N_MICROBATCH = 1
ADAM_LR = 0.001
ADAM_B1 = 0.9
ADAM_B2 = 0.999
ADAM_EPS = 1e-08
ADAM_WD = 0.01
ADAM_STEP = 10
PER_EXAMPLE_BATCH_AXIS = {'x': 0, 'p': 1, 'loss_target': 0}
SHARED_INPUTS = []
_WEIGHT_DTYPES = {'ln_g': _jnp.float32, 'w_in': _jnp.float32, 'ssm_lam_re': _jnp.float32, 'ssm_lam_im': _jnp.float32, 'ssm_log_dt': _jnp.float32, 'ssm_b_re': _jnp.float32, 'ssm_b_im': _jnp.float32, 'ssm_c_re': _jnp.float32, 'ssm_c_im': _jnp.float32, 'ssm_d': _jnp.float32, 'ssm_w_glu': _jnp.float32, 'mla_g_q': _jnp.float32, 'mla_g_kv': _jnp.float32, 'mla_w_uq': _jnp.float32, 'mla_w_ukv': _jnp.float32, 'w_branch': _jnp.float32, 'w_out': _jnp.float32, 'ple_g': _jnp.float32, 'w_ple_gate': _jnp.float32, 'w_ple_proj': _jnp.float32, 'final_g': _jnp.float32}
MOMENT_SCALE = {'ln_g': 4.072217e-02, 'w_in': 1.532264e-02, 'ssm_lam_re': 9.954087e-04, 'ssm_lam_im': 9.466889e-04, 'ssm_log_dt': 8.403819e-01, 'ssm_b_re': 5.861325e-04, 'ssm_b_im': 5.754152e-04, 'ssm_c_re': 1.161994e-03, 'ssm_c_im': 1.172842e-03, 'ssm_d': 1.767360e-02, 'ssm_w_glu': 4.606316e-03, 'mla_g_q': 1.230137e-02, 'mla_g_kv': 1.740016e-02, 'mla_w_uq': 6.797184e-03, 'mla_w_ukv': 8.636678e-03, 'w_branch': 1.439266e-02, 'w_out': 2.490955e-02, 'ple_g': 2.205507e-02, 'w_ple_gate': 2.109222e-02, 'w_ple_proj': 5.441875e-02, 'final_g': 3.196974e+01}


def _to_microbatches(a, axis):
    t = _jnp.moveaxis(a, axis, 0)
    t = t.reshape((N_MICROBATCH, t.shape[0] // N_MICROBATCH) + t.shape[1:])
    return _jnp.moveaxis(t, 1, axis + 1)


def setup_inputs(seed: int = 0) -> dict:
    inp = _fwd_setup_inputs(seed)
    key = _jax.random.fold_in(_jax.random.key(seed), 7919)
    shape, _ = _output_shape()
    out = dict(inp)
    out["loss_target"] = _jax.random.normal(_jax.random.fold_in(key, 0), shape, _jnp.float32)
    for i, name in enumerate(TWIN_WEIGHTS):
        w = inp[name].astype(_jnp.float32)
        if MOMENT_SCALE is None:
            s = _jnp.sqrt(_jnp.mean(_jnp.square(w)) + 1e-30)
        else:
            s = MOMENT_SCALE[name]
        km, kv = _jax.random.split(_jax.random.fold_in(key, i + 1))
        out[name] = w
        out["m_" + name] = s * _jax.random.normal(km, w.shape, _jnp.float32)
        out["v_" + name] = (s * s) * _jax.random.uniform(kv, w.shape, _jnp.float32, 0.5, 1.5)
    if N_MICROBATCH > 1:
        for name, axis in PER_EXAMPLE_BATCH_AXIS.items():
            out[name] = _to_microbatches(out[name], axis)
    return {'x': out['x'], 'p': out['p'], 'ln_g': out['ln_g'], 'w_in': out['w_in'], 'ssm_lam_re': out['ssm_lam_re'], 'ssm_lam_im': out['ssm_lam_im'], 'ssm_log_dt': out['ssm_log_dt'], 'ssm_b_re': out['ssm_b_re'], 'ssm_b_im': out['ssm_b_im'], 'ssm_c_re': out['ssm_c_re'], 'ssm_c_im': out['ssm_c_im'], 'ssm_d': out['ssm_d'], 'ssm_w_glu': out['ssm_w_glu'], 'mla_g_q': out['mla_g_q'], 'mla_g_kv': out['mla_g_kv'], 'mla_w_uq': out['mla_w_uq'], 'mla_w_ukv': out['mla_w_ukv'], 'w_branch': out['w_branch'], 'w_out': out['w_out'], 'ple_g': out['ple_g'], 'w_ple_gate': out['w_ple_gate'], 'w_ple_proj': out['w_ple_proj'], 'final_g': out['final_g'], 'loss_target': out['loss_target'], 'm_ln_g': out['m_ln_g'], 'm_w_in': out['m_w_in'], 'm_ssm_lam_re': out['m_ssm_lam_re'], 'm_ssm_lam_im': out['m_ssm_lam_im'], 'm_ssm_log_dt': out['m_ssm_log_dt'], 'm_ssm_b_re': out['m_ssm_b_re'], 'm_ssm_b_im': out['m_ssm_b_im'], 'm_ssm_c_re': out['m_ssm_c_re'], 'm_ssm_c_im': out['m_ssm_c_im'], 'm_ssm_d': out['m_ssm_d'], 'm_ssm_w_glu': out['m_ssm_w_glu'], 'm_mla_g_q': out['m_mla_g_q'], 'm_mla_g_kv': out['m_mla_g_kv'], 'm_mla_w_uq': out['m_mla_w_uq'], 'm_mla_w_ukv': out['m_mla_w_ukv'], 'm_w_branch': out['m_w_branch'], 'm_w_out': out['m_w_out'], 'm_ple_g': out['m_ple_g'], 'm_w_ple_gate': out['m_w_ple_gate'], 'm_w_ple_proj': out['m_w_ple_proj'], 'm_final_g': out['m_final_g'], 'v_ln_g': out['v_ln_g'], 'v_w_in': out['v_w_in'], 'v_ssm_lam_re': out['v_ssm_lam_re'], 'v_ssm_lam_im': out['v_ssm_lam_im'], 'v_ssm_log_dt': out['v_ssm_log_dt'], 'v_ssm_b_re': out['v_ssm_b_re'], 'v_ssm_b_im': out['v_ssm_b_im'], 'v_ssm_c_re': out['v_ssm_c_re'], 'v_ssm_c_im': out['v_ssm_c_im'], 'v_ssm_d': out['v_ssm_d'], 'v_ssm_w_glu': out['v_ssm_w_glu'], 'v_mla_g_q': out['v_mla_g_q'], 'v_mla_g_kv': out['v_mla_g_kv'], 'v_mla_w_uq': out['v_mla_w_uq'], 'v_mla_w_ukv': out['v_mla_w_ukv'], 'v_w_branch': out['v_w_branch'], 'v_w_out': out['v_w_out'], 'v_ple_g': out['v_ple_g'], 'v_w_ple_gate': out['v_w_ple_gate'], 'v_w_ple_proj': out['v_w_ple_proj'], 'v_final_g': out['v_final_g']}


def _loss(weights, diff, rest, loss_target):
    with _jax.named_scope("forward"):
        args = {**rest, TWIN_DIFF_INPUT: diff, **{k: w.astype(_WEIGHT_DTYPES[k]) for k, w in weights.items()}}
        y = _forward(args)
    with _jax.named_scope("loss_head"):
        err = _jnp.square(y.astype(_jnp.float32) - loss_target)
        return 0.5 * _jnp.sum(_jnp.mean(err, axis=-1)) if err.ndim else 0.5 * err


def _adamw(w, g, m, v):
    m = ADAM_B1 * m + (1.0 - ADAM_B1) * g
    v = ADAM_B2 * v + (1.0 - ADAM_B2) * _jnp.square(g)
    m_hat = m / (1.0 - ADAM_B1 ** ADAM_STEP)
    v_hat = v / (1.0 - ADAM_B2 ** ADAM_STEP)
    delta = -ADAM_LR * (m_hat / (_jnp.sqrt(v_hat) + ADAM_EPS) + ADAM_WD * w)
    return delta, m, v


def reference(x, p, ln_g, w_in, ssm_lam_re, ssm_lam_im, ssm_log_dt, ssm_b_re, ssm_b_im, ssm_c_re, ssm_c_im, ssm_d, ssm_w_glu, mla_g_q, mla_g_kv, mla_w_uq, mla_w_ukv, w_branch, w_out, ple_g, w_ple_gate, w_ple_proj, final_g, loss_target, m_ln_g, m_w_in, m_ssm_lam_re, m_ssm_lam_im, m_ssm_log_dt, m_ssm_b_re, m_ssm_b_im, m_ssm_c_re, m_ssm_c_im, m_ssm_d, m_ssm_w_glu, m_mla_g_q, m_mla_g_kv, m_mla_w_uq, m_mla_w_ukv, m_w_branch, m_w_out, m_ple_g, m_w_ple_gate, m_w_ple_proj, m_final_g, v_ln_g, v_w_in, v_ssm_lam_re, v_ssm_lam_im, v_ssm_log_dt, v_ssm_b_re, v_ssm_b_im, v_ssm_c_re, v_ssm_c_im, v_ssm_d, v_ssm_w_glu, v_mla_g_q, v_mla_g_kv, v_mla_w_uq, v_mla_w_ukv, v_w_branch, v_w_out, v_ple_g, v_w_ple_gate, v_w_ple_proj, v_final_g):
    given = dict(x=x, p=p, ln_g=ln_g, w_in=w_in, ssm_lam_re=ssm_lam_re, ssm_lam_im=ssm_lam_im, ssm_log_dt=ssm_log_dt, ssm_b_re=ssm_b_re, ssm_b_im=ssm_b_im, ssm_c_re=ssm_c_re, ssm_c_im=ssm_c_im, ssm_d=ssm_d, ssm_w_glu=ssm_w_glu, mla_g_q=mla_g_q, mla_g_kv=mla_g_kv, mla_w_uq=mla_w_uq, mla_w_ukv=mla_w_ukv, w_branch=w_branch, w_out=w_out, ple_g=ple_g, w_ple_gate=w_ple_gate, w_ple_proj=w_ple_proj, final_g=final_g, loss_target=loss_target, m_ln_g=m_ln_g, m_w_in=m_w_in, m_ssm_lam_re=m_ssm_lam_re, m_ssm_lam_im=m_ssm_lam_im, m_ssm_log_dt=m_ssm_log_dt, m_ssm_b_re=m_ssm_b_re, m_ssm_b_im=m_ssm_b_im, m_ssm_c_re=m_ssm_c_re, m_ssm_c_im=m_ssm_c_im, m_ssm_d=m_ssm_d, m_ssm_w_glu=m_ssm_w_glu, m_mla_g_q=m_mla_g_q, m_mla_g_kv=m_mla_g_kv, m_mla_w_uq=m_mla_w_uq, m_mla_w_ukv=m_mla_w_ukv, m_w_branch=m_w_branch, m_w_out=m_w_out, m_ple_g=m_ple_g, m_w_ple_gate=m_w_ple_gate, m_w_ple_proj=m_w_ple_proj, m_final_g=m_final_g, v_ln_g=v_ln_g, v_w_in=v_w_in, v_ssm_lam_re=v_ssm_lam_re, v_ssm_lam_im=v_ssm_lam_im, v_ssm_log_dt=v_ssm_log_dt, v_ssm_b_re=v_ssm_b_re, v_ssm_b_im=v_ssm_b_im, v_ssm_c_re=v_ssm_c_re, v_ssm_c_im=v_ssm_c_im, v_ssm_d=v_ssm_d, v_ssm_w_glu=v_ssm_w_glu, v_mla_g_q=v_mla_g_q, v_mla_g_kv=v_mla_g_kv, v_mla_w_uq=v_mla_w_uq, v_mla_w_ukv=v_mla_w_ukv, v_w_branch=v_w_branch, v_w_out=v_w_out, v_ple_g=v_ple_g, v_w_ple_gate=v_w_ple_gate, v_w_ple_proj=v_w_ple_proj, v_final_g=v_final_g)
    weights = {n: given[n] for n in TWIN_WEIGHTS}
    shared = {n: given[n] for n in SHARED_INPUTS}
    per_example = {n: given[n] for n in ['x', 'p']}
    grad_fn = _jax.value_and_grad(_loss, argnums=(0, 1))

    def one_microbatch(ex, loss_target):
        ex = dict(ex)
        diff = ex.pop(TWIN_DIFF_INPUT)
        return grad_fn(weights, diff, {**shared, **ex}, loss_target)

    if N_MICROBATCH == 1:
        loss, (grad_w, grad_x) = one_microbatch(per_example, given["loss_target"])
    else:
        def body(carry, xs):
            loss_sum, grad_sum = carry
            l_k, (gw_k, gx_k) = one_microbatch(xs[0], xs[1])
            with _jax.named_scope("update"):
                return (loss_sum + l_k, _jax.tree.map(_jnp.add, grad_sum, gw_k)), gx_k

        init = (_jnp.zeros((), _jnp.float32), _jax.tree.map(_jnp.zeros_like, weights))
        (loss, grad_w), grad_x = _jax.lax.scan(body, init, (per_example, given["loss_target"]))
    with _jax.named_scope("update"):
        delta_w, new_m, new_v = {}, {}, {}
        for n in TWIN_WEIGHTS:
            delta_w[n], new_m[n], new_v[n] = _adamw(weights[n], grad_w[n], given["m_" + n], given["v_" + n])
    return (loss, grad_x, *[grad_w[n] for n in TWIN_WEIGHTS], *[delta_w[n] for n in TWIN_WEIGHTS],
            *[new_m[n] for n in TWIN_WEIGHTS], *[new_v[n] for n in TWIN_WEIGHTS])
```

```python
import functools
import math

import jax
import jax.numpy as jnp
from jax import lax
from jax.experimental import pallas as pl
from jax.experimental.pallas import tpu as pltpu

F32 = jnp.float32
BF16 = jnp.bfloat16

N_DEV = 8
D_MODEL = 2048
BRANCH_W = 1024
N_HEADS = 8
HEAD_D = 128
SSM_GROUPS = 64
SSM_GROUP = 16
SSM_STATE = 64
SSM_GB = 8
SSM_CHUNK = 256
MLA_ROPE = 64
MLA_NOPE = 128
ROPE_THETA = 10000.0
NORM_EPS = 1e-6
DT_MIN = 1e-3
DT_MAX = 1e-1
PACK_COLS = 1024

ADAM_LR = 0.001
ADAM_B1 = 0.9
ADAM_B2 = 0.999
ADAM_EPS = 1e-08
ADAM_WD = 0.01
ADAM_STEP = 10

VMEM_LIMIT = 56 * 2 ** 20
NEG_BIG = -1e30

SHARDED = (
    ("w_in", (2048, 14400), 1),
    ("ssm_w_glu", (1024, 1024), 0),
    ("mla_w_uq", (512, 1536), 1),
    ("mla_w_ukv", (512, 2048), 1),
    ("w_branch", (3, 1024, 2048), 2),
    ("w_out", (2048, 2048), 0),
    ("w_ple_gate", (2048, 2048), 0),
    ("w_ple_proj", (256, 2048), 1),
)
SMALL = ("ln_g", "ssm_lam_re", "ssm_lam_im", "ssm_log_dt", "ssm_b_re", "ssm_b_im", "ssm_c_re", "ssm_c_im",
         "ssm_d", "mla_g_q", "mla_g_kv", "ple_g", "final_g")
WEIGHT_ORDER = ("ln_g", "w_in", "ssm_lam_re", "ssm_lam_im", "ssm_log_dt", "ssm_b_re", "ssm_b_im", "ssm_c_re",
                "ssm_c_im", "ssm_d", "ssm_w_glu", "mla_g_q", "mla_g_kv", "mla_w_uq", "mla_w_ukv", "w_branch",
                "w_out", "ple_g", "w_ple_gate", "w_ple_proj", "final_g")


def _cparams(sem=None):
    return pltpu.CompilerParams(dimension_semantics=sem, vmem_limit_bytes=VMEM_LIMIT)


def _pick(n, prefs):
    for t in prefs:
        if n % t == 0:
            return t
    return n


def _dot(a, b):
    return lax.dot_general(a, b, (((1,), (0,)), ((), ())), preferred_element_type=F32)


def _dot_nt(a, b):
    return lax.dot_general(a, b, (((1,), (1,)), ((), ())), preferred_element_type=F32)


def _dot_tn(a, b):
    return lax.dot_general(a, b, (((0,), (0,)), ((), ())), preferred_element_type=F32)


def _sigmoid(x):
    return 1.0 / (1.0 + jnp.exp(-x))


def _mm(a, b, *, name, ta=False, tb=False, extras=(), epilogue=None, out_dtypes=(F32,)):
    m, k = (a.shape[1], a.shape[0]) if ta else a.shape
    n = b.shape[0] if tb else b.shape[1]
    assert (b.shape[1] if tb else b.shape[0]) == k
    tm = _pick(m, (512, 256, 128))
    tn = _pick(n, (1024, 512, 256, 128))
    tk = _pick(k, (1024, 512, 256, 128))
    nk = k // tk
    n_ex = len(extras)
    n_out = len(out_dtypes)

    def body(*refs):
        a_ref, b_ref = refs[0], refs[1]
        ex_refs = refs[2:2 + n_ex]
        out_refs = refs[2 + n_ex:2 + n_ex + n_out]
        acc_ref = refs[-1]
        kk = pl.program_id(2)

        @pl.when(kk == 0)
        def _():
            acc_ref[...] = jnp.zeros_like(acc_ref)

        av = a_ref[...].astype(BF16)
        bv = b_ref[...].astype(BF16)
        dims = (((0 if ta else 1,), (1 if tb else 0,)), ((), ()))
        acc_ref[...] += lax.dot_general(av, bv, dims, preferred_element_type=F32)

        @pl.when(kk == nk - 1)
        def _():
            acc = acc_ref[...]
            res = epilogue(acc, *[r[...] for r in ex_refs]) if epilogue is not None else (acc,)
            for r, v in zip(out_refs, res):
                r[...] = v.astype(r.dtype)

    a_spec = pl.BlockSpec((tk, tm), lambda i, j, q: (q, i)) if ta else pl.BlockSpec((tm, tk), lambda i, j, q: (i, q))
    b_spec = pl.BlockSpec((tn, tk), lambda i, j, q: (j, q)) if tb else pl.BlockSpec((tk, tn), lambda i, j, q: (q, j))
    ex_specs = []
    for arr, off in extras:
        assert off % tn == 0 and arr.shape[0] == m
        ex_specs.append(pl.BlockSpec((tm, tn), functools.partial(lambda i, j, q, o: (i, j + o), o=off // tn)))
    outs = pl.pallas_call(
        body, name=name,
        grid=(m // tm, n // tn, nk),
        in_specs=[a_spec, b_spec] + ex_specs,
        out_specs=[pl.BlockSpec((tm, tn), lambda i, j, q: (i, j)) for _ in out_dtypes],
        out_shape=[jax.ShapeDtypeStruct((m, n), dt) for dt in out_dtypes],
        scratch_shapes=[pltpu.VMEM((tm, tn), F32)],
        compiler_params=_cparams(("parallel", "parallel", "arbitrary")),
    )(a, b, *[e[0] for e in extras])
    return outs if n_out > 1 else outs[0]


def _row_tile(m, bytes_per_row):
    for t in (1024, 512, 256, 128, 64, 32, 16):
        if m % t == 0 and 2 * t * bytes_per_row <= 20 * 2 ** 20:
            return t
    return 16 if m % 16 == 0 else m


def _ew(fn, ins, outs, *, name):
    m = max(x.shape[0] for x in ins)
    bpr = sum(x.shape[1] * x.dtype.itemsize for x in ins if x.shape[0] == m)
    bpr += sum(no * jnp.dtype(dt).itemsize for no, dt in outs)
    tm = _row_tile(m, bpr)
    n_in = len(ins)

    def body(*refs):
        res = fn(*[r[...] for r in refs[:n_in]])
        if not isinstance(res, (tuple, list)):
            res = (res,)
        for r, v in zip(refs[n_in:], res):
            r[...] = v.astype(r.dtype)

    in_specs = []
    for x in ins:
        if x.shape[0] == m:
            in_specs.append(pl.BlockSpec((tm, x.shape[1]), lambda i: (i, 0)))
        else:
            in_specs.append(pl.BlockSpec(x.shape, lambda i: (0, 0)))
    res = pl.pallas_call(
        body, name=name,
        grid=(m // tm,),
        in_specs=in_specs,
        out_specs=[pl.BlockSpec((tm, no), lambda i: (i, 0)) for no, _ in outs],
        out_shape=[jax.ShapeDtypeStruct((m, no), dt) for no, dt in outs],
        compiler_params=_cparams(("parallel",)),
    )(*ins)
    return res if len(outs) > 1 else res[0]


def _rmsnorm(x, g, *, name, out_dtype=BF16):
    d = x.shape[1]

    def fn(xv, gv):
        r = lax.rsqrt(jnp.mean(xv * xv, axis=-1, keepdims=True) + NORM_EPS)
        return xv * r * gv

    return _ew(fn, [x, g.reshape(1, d)], [(d, out_dtype)], name=name)


def _rmsnorm_bwd(x, g, dy, res, *, name):
    m, d = x.shape
    has_res = res is not None
    tm = _row_tile(m, d * 4 * (4 if has_res else 3))

    def body(*refs):
        x_ref, g_ref, dy_ref = refs[:3]
        res_ref = refs[3] if has_res else None
        dx_ref, dg_ref = refs[-2], refs[-1]

        @pl.when(pl.program_id(0) == 0)
        def _():
            dg_ref[...] = jnp.zeros_like(dg_ref)

        xv = x_ref[...]
        dyv = dy_ref[...].astype(F32)
        r = lax.rsqrt(jnp.mean(xv * xv, axis=-1, keepdims=True) + NORM_EPS)
        xh = xv * r
        dg_ref[...] += jnp.sum(dyv * xh, axis=0, keepdims=True)
        dyg = dyv * g_ref[...]
        dx = r * (dyg - xh * jnp.mean(dyg * xh, axis=-1, keepdims=True))
        if has_res:
            dx = dx + res_ref[...]
        dx_ref[...] = dx

    row = pl.BlockSpec((tm, d), lambda i: (i, 0))
    vec = pl.BlockSpec((1, d), lambda i: (0, 0))
    ins = [x, g.reshape(1, d), dy] + ([res] if has_res else [])
    return pl.pallas_call(
        body, name=name,
        grid=(m // tm,),
        in_specs=[row, vec, row] + ([row] if has_res else []),
        out_specs=[row, vec],
        out_shape=[jax.ShapeDtypeStruct((m, d), F32), jax.ShapeDtypeStruct((1, d), F32)],
        compiler_params=_cparams(("arbitrary",)),
    )(*ins)


def _loss_head(x, g, target, *, name):
    m, d = x.shape
    tm = _row_tile(m, d * 4 * 3)

    def body(x_ref, g_ref, t_ref, dx_ref, dg_ref, loss_ref):
        @pl.when(pl.program_id(0) == 0)
        def _():
            dg_ref[...] = jnp.zeros_like(dg_ref)
            loss_ref[...] = jnp.zeros_like(loss_ref)

        xv = x_ref[...]
        gv = g_ref[...]
        r = lax.rsqrt(jnp.mean(xv * xv, axis=-1, keepdims=True) + NORM_EPS)
        xh = xv * r
        diff = xh * gv - t_ref[...]
        loss_ref[...] += 0.5 * jnp.sum(jnp.mean(diff * diff, axis=-1, keepdims=True), axis=0, keepdims=True)
        dyv = diff * (1.0 / d)
        dg_ref[...] += jnp.sum(dyv * xh, axis=0, keepdims=True)
        dyg = dyv * gv
        dx_ref[...] = r * (dyg - xh * jnp.mean(dyg * xh, axis=-1, keepdims=True))

    row = pl.BlockSpec((tm, d), lambda i: (i, 0))
    vec = pl.BlockSpec((1, d), lambda i: (0, 0))
    return pl.pallas_call(
        body, name=name,
        grid=(m // tm,),
        in_specs=[row, vec, row],
        out_specs=[row, vec, pl.BlockSpec((1, 128), lambda i: (0, 0))],
        out_shape=[jax.ShapeDtypeStruct((m, d), F32), jax.ShapeDtypeStruct((1, d), F32),
                   jax.ShapeDtypeStruct((1, 128), F32)],
        compiler_params=_cparams(("arbitrary",)),
    )(x, g.reshape(1, d), target)


def _rope_tables(seqlen):
    pos = jnp.arange(seqlen, dtype=F32)
    inv_freq = ROPE_THETA ** (-jnp.arange(0, MLA_ROPE, 2, dtype=F32) / MLA_ROPE)
    ang = pos[:, None] * inv_freq[None, :]
    c, s = jnp.cos(ang), jnp.sin(ang)
    z = jnp.zeros_like(c)
    return (jnp.concatenate([c, c, z, z], axis=1), jnp.concatenate([z, s, z, z], axis=1),
            jnp.concatenate([-s, z, z, z], axis=1))


def _rope_group(xg, ct, sp, sm):
    return xg * ct + pltpu.roll(xg, 32, 1) * sp + pltpu.roll(xg, 96, 1) * sm


def _rope_group_t(dg, ct, sp, sm):
    return dg * ct + pltpu.roll(dg * sp, 96, 1) + pltpu.roll(dg * sm, 32, 1)


def _s5_disc(lam_re, lam_im, log_dt, b_re, b_im):
    dt = jnp.exp(log_dt)[:, None]
    mag = jnp.exp(lam_re * dt)
    ab_re = mag * jnp.cos(lam_im * dt)
    ab_im = mag * jnp.sin(lam_im * dt)
    den = lam_re * lam_re + lam_im * lam_im
    nr = ab_re - 1.0
    ni = ab_im
    coef_re = (nr * lam_re + ni * lam_im) / den
    coef_im = (ni * lam_re - nr * lam_im) / den
    bb_re = coef_re[..., None] * b_re - coef_im[..., None] * b_im
    bb_im = coef_re[..., None] * b_im + coef_im[..., None] * b_re
    return ab_re, ab_im, bb_re, bb_im


def _blockdiag(x):
    nb, ng, r, c = x.shape
    eye = jnp.eye(ng, dtype=x.dtype)
    return (x[:, :, :, None, :] * eye[None, :, None, :, None]).reshape(nb, ng * r, ng * c)


def _blockdiag_extract(x, r, c):
    nb = x.shape[0]
    x5 = x.reshape(nb, SSM_GB, r, SSM_GB, c)
    eye = jnp.eye(SSM_GB, dtype=x.dtype)
    return jnp.sum(x5 * eye[None, :, None, :, None], axis=3)


def _s5_tables(lam_re, lam_im, log_dt, t_chunk):
    dt = jnp.exp(log_dt)[:, None]
    lr = (lam_re * dt).reshape(1, -1)
    li = (lam_im * dt).reshape(1, -1)
    nlog = int(math.log2(t_chunk))
    n1 = jnp.arange(1, t_chunk + 1, dtype=F32)[:, None]
    n2 = (2.0 ** jnp.arange(nlog, dtype=F32))[:, None]

    def powers(n):
        mag = jnp.exp(n * lr)
        return mag * jnp.cos(n * li), mag * jnp.sin(n * li)

    p_re, p_im = powers(n1)
    a_re, a_im = powers(n2)
    return p_re, p_im, a_re, a_im


def _hs_scan(h_re, h_im, a_re_ref, a_im_ref, rows, t_chunk, reverse):
    nlog = int(math.log2(t_chunk))
    for k in range(nlog):
        sh = 1 << k
        ar = a_re_ref[pl.ds(k, 1), :]
        ai = a_im_ref[pl.ds(k, 1), :]
        if reverse:
            s_re = pltpu.roll(h_re, t_chunk - sh, 0)
            s_im = pltpu.roll(h_im, t_chunk - sh, 0)
            keep = rows < t_chunk - sh
            ai = -ai
        else:
            s_re = pltpu.roll(h_re, sh, 0)
            s_im = pltpu.roll(h_im, sh, 0)
            keep = rows >= sh
        s_re = jnp.where(keep, s_re, 0.0)
        s_im = jnp.where(keep, s_im, 0.0)
        h_re, h_im = h_re + ar * s_re - ai * s_im, h_im + ar * s_im + ai * s_re
    return h_re, h_im


def _s5_fwd(u, bb_re, bb_im, c_re, c_im, dsk, tabs, *, name):
    s = u.shape[0]
    t = min(SSM_CHUNK, s)
    nt = s // t
    ns = SSM_GB * SSM_STATE
    p_re, p_im, a_re, a_im = tabs
    nlog = a_re.shape[0]

    def body(u_ref, bbr_ref, bbi_ref, cr_ref, ci_ref, d_ref, pr_ref, pi_ref, ar_ref, ai_ref,
             y_ref, hcr_ref, hci_ref, car_re, car_im):
        @pl.when(pl.program_id(1) == 0)
        def _():
            car_re[...] = jnp.zeros_like(car_re)
            car_im[...] = jnp.zeros_like(car_im)

        cin_re = car_re[...]
        cin_im = car_im[...]
        hcr_ref[0] = cin_re
        hci_ref[0] = cin_im
        uv = u_ref[...]
        ub = uv.astype(BF16)
        h_re = _dot(ub, bbr_ref[0])
        h_im = _dot(ub, bbi_ref[0])
        rows = lax.broadcasted_iota(jnp.int32, (t, ns), 0)
        h_re, h_im = _hs_scan(h_re, h_im, ar_ref, ai_ref, rows, t, False)
        pr = pr_ref[...]
        pi = pi_ref[...]
        h_re = h_re + pr * cin_re - pi * cin_im
        h_im = h_im + pr * cin_im + pi * cin_re
        last = rows == t - 1
        car_re[...] = jnp.sum(jnp.where(last, h_re, 0.0), axis=0, keepdims=True)
        car_im[...] = jnp.sum(jnp.where(last, h_im, 0.0), axis=0, keepdims=True)
        y = _dot(h_re.astype(BF16), cr_ref[0]) - _dot(h_im.astype(BF16), ci_ref[0]) + d_ref[...] * uv
        y_ref[...] = y

    blk = lambda r, c: pl.BlockSpec((1, r, c), lambda g, i: (g, 0, 0))
    tab = lambda r: pl.BlockSpec((r, ns), lambda g, i: (0, g))
    return pl.pallas_call(
        body, name=name,
        grid=(SSM_GB, nt),
        in_specs=[pl.BlockSpec((t, 128), lambda g, i: (i, g)), blk(128, ns), blk(128, ns), blk(ns, 128), blk(ns, 128),
                  pl.BlockSpec((1, 128), lambda g, i: (0, g)), tab(t), tab(t), tab(nlog), tab(nlog)],
        out_specs=[pl.BlockSpec((t, 128), lambda g, i: (i, g)),
                   pl.BlockSpec((1, 1, ns), lambda g, i: (i, 0, g)), pl.BlockSpec((1, 1, ns), lambda g, i: (i, 0, g))],
        out_shape=[jax.ShapeDtypeStruct((s, BRANCH_W), F32),
                   jax.ShapeDtypeStruct((nt, 1, SSM_GB * ns), F32), jax.ShapeDtypeStruct((nt, 1, SSM_GB * ns), F32)],
        scratch_shapes=[pltpu.VMEM((1, ns), F32), pltpu.VMEM((1, ns), F32)],
        compiler_params=_cparams(("parallel", "arbitrary")),
    )(u, bb_re, bb_im, c_re, c_im, dsk, p_re, p_im, a_re, a_im)


def _s5_bwd(u, dy, hc_re, hc_im, bb_re, bb_im, bbt_re, bbt_im, c_re, c_im, ct_re, ct_im, dsk, tabs, *, name):
    s = u.shape[0]
    t = min(SSM_CHUNK, s)
    nt = s // t
    ns = SSM_GB * SSM_STATE
    p_re, p_im, a_re, a_im = tabs
    q_re, q_im = p_re[::-1], p_im[::-1]
    nlog = a_re.shape[0]

    def body(u_ref, dy_ref, hcr_ref, hci_ref, bbr_ref, bbi_ref, bbtr_ref, bbti_ref, cr_ref, ci_ref, ctr_ref, cti_ref,
             d_ref, pr_ref, pi_ref, qr_ref, qi_ref, ar_ref, ai_ref,
             du_ref, dbbr_ref, dbbi_ref, dcr_ref, dci_ref, dar_ref, dai_ref, dd_ref, lam_re_c, lam_im_c):
        @pl.when(pl.program_id(1) == 0)
        def _():
            lam_re_c[...] = jnp.zeros_like(lam_re_c)
            lam_im_c[...] = jnp.zeros_like(lam_im_c)
            dbbr_ref[...] = jnp.zeros_like(dbbr_ref)
            dbbi_ref[...] = jnp.zeros_like(dbbi_ref)
            dcr_ref[...] = jnp.zeros_like(dcr_ref)
            dci_ref[...] = jnp.zeros_like(dci_ref)
            dar_ref[...] = jnp.zeros_like(dar_ref)
            dai_ref[...] = jnp.zeros_like(dai_ref)
            dd_ref[...] = jnp.zeros_like(dd_ref)

        cin_re = hcr_ref[0]
        cin_im = hci_ref[0]
        uv = u_ref[...]
        ub = uv.astype(BF16)
        dyv = dy_ref[...]
        dyb = dyv.astype(BF16)
        rows = lax.broadcasted_iota(jnp.int32, (t, ns), 0)
        h_re = _dot(ub, bbr_ref[0])
        h_im = _dot(ub, bbi_ref[0])
        h_re, h_im = _hs_scan(h_re, h_im, ar_ref, ai_ref, rows, t, False)
        pr = pr_ref[...]
        pi = pi_ref[...]
        h_re = h_re + pr * cin_re - pi * cin_im
        h_im = h_im + pr * cin_im + pi * cin_re
        dcr_ref[0] += _dot_tn(h_re.astype(BF16), dyb)
        dci_ref[0] -= _dot_tn(h_im.astype(BF16), dyb)
        first = rows == 0
        hp_re = jnp.where(first, cin_re, pltpu.roll(h_re, 1, 0))
        hp_im = jnp.where(first, cin_im, pltpu.roll(h_im, 1, 0))
        l_re = _dot(dyb, ctr_ref[0])
        l_im = -_dot(dyb, cti_ref[0])
        l_re, l_im = _hs_scan(l_re, l_im, ar_ref, ai_ref, rows, t, True)
        qr = qr_ref[...]
        qi = qi_ref[...]
        nin_re = lam_re_c[...]
        nin_im = lam_im_c[...]
        l_re = l_re + qr * nin_re + qi * nin_im
        l_im = l_im + qr * nin_im - qi * nin_re
        lam_re_c[...] = jnp.sum(jnp.where(first, l_re, 0.0), axis=0, keepdims=True)
        lam_im_c[...] = jnp.sum(jnp.where(first, l_im, 0.0), axis=0, keepdims=True)
        lrb = l_re.astype(BF16)
        lib = l_im.astype(BF16)
        du_ref[...] = _dot(lrb, bbtr_ref[0]) + _dot(lib, bbti_ref[0]) + d_ref[...] * dyv
        dbbr_ref[0] += _dot_tn(ub, lrb)
        dbbi_ref[0] += _dot_tn(ub, lib)
        dar_ref[0] += jnp.sum(l_re * hp_re + l_im * hp_im, axis=0, keepdims=True)
        dai_ref[0] += jnp.sum(l_im * hp_re - l_re * hp_im, axis=0, keepdims=True)
        dd_ref[0] += jnp.sum(dyv * uv, axis=0, keepdims=True)

    rev = lambda g, i: (nt - 1 - i, g)
    blk = lambda r, c: pl.BlockSpec((1, r, c), lambda g, i: (g, 0, 0))
    tab = lambda r: pl.BlockSpec((r, ns), lambda g, i: (0, g))
    car = pl.BlockSpec((1, 1, ns), lambda g, i: (nt - 1 - i, 0, g))
    return pl.pallas_call(
        body, name=name,
        grid=(SSM_GB, nt),
        in_specs=[pl.BlockSpec((t, 128), rev), pl.BlockSpec((t, 128), rev), car, car,
                  blk(128, ns), blk(128, ns), blk(ns, 128), blk(ns, 128), blk(ns, 128), blk(ns, 128),
                  blk(128, ns), blk(128, ns), pl.BlockSpec((1, 128), lambda g, i: (0, g)),
                  tab(t), tab(t), tab(t), tab(t), tab(nlog), tab(nlog)],
        out_specs=[pl.BlockSpec((t, 128), rev), blk(128, ns), blk(128, ns), blk(ns, 128), blk(ns, 128),
                   blk(1, ns), blk(1, ns), blk(1, 128)],
        out_shape=[jax.ShapeDtypeStruct((s, BRANCH_W), F32),
                   jax.ShapeDtypeStruct((SSM_GB, 128, ns), F32), jax.ShapeDtypeStruct((SSM_GB, 128, ns), F32),
                   jax.ShapeDtypeStruct((SSM_GB, ns, 128), F32), jax.ShapeDtypeStruct((SSM_GB, ns, 128), F32),
                   jax.ShapeDtypeStruct((SSM_GB, 1, ns), F32), jax.ShapeDtypeStruct((SSM_GB, 1, ns), F32),
                   jax.ShapeDtypeStruct((SSM_GB, 1, 128), F32)],
        scratch_shapes=[pltpu.VMEM((1, ns), F32), pltpu.VMEM((1, ns), F32)],
        compiler_params=_cparams(("parallel", "arbitrary")),
    )(u, dy, hc_re, hc_im, bb_re, bb_im, bbt_re, bbt_im, c_re, c_im, ct_re, ct_im, dsk,
      p_re, p_im, q_re, q_im, a_re, a_im)


def _attn_block(s):
    return min(256, s)


def _mla_fwd(q, kv, kr, *, name):
    s = q.shape[0]
    bq = bk = _attn_block(s)
    scale = float((MLA_NOPE + MLA_ROPE) ** -0.5)

    def body(qn_ref, qr_ref, kn_ref, v_ref, kr_ref, o_ref, lse_ref):
        qi = pl.program_id(1)
        qn = qn_ref[...]
        qr = qr_ref[...]
        row = qi * bq + lax.broadcasted_iota(jnp.int32, (bq, bk), 0)
        colb = lax.broadcasted_iota(jnp.int32, (bq, bk), 1)

        def step(j, carry):
            m, l, acc = carry
            ks = pl.ds(pl.multiple_of(j * bk, bk), bk)
            sc = (_dot_nt(qn, kn_ref[ks, :]) + _dot_nt(qr, kr_ref[ks, :])) * scale
            sc = jnp.where(j * bk + colb <= row, sc, NEG_BIG)
            m_new = jnp.maximum(m, jnp.max(sc, axis=1, keepdims=True))
            alpha = jnp.exp(m - m_new)
            p = jnp.exp(sc - m_new)
            l = alpha * l + jnp.sum(p, axis=1, keepdims=True)
            acc = alpha * acc + _dot(p.astype(BF16), v_ref[ks, :])
            return m_new, l, acc

        init = (jnp.full((bq, 1), NEG_BIG, F32), jnp.zeros((bq, 1), F32), jnp.zeros((bq, HEAD_D), F32))
        m, l, acc = lax.fori_loop(0, qi + 1, step, init)
        o_ref[...] = acc / l
        lse_ref[0] = m + jnp.log(l)

    res = pl.BlockSpec((s, HEAD_D), lambda h, i: (0, h))
    return pl.pallas_call(
        body, name=name,
        grid=(N_HEADS, s // bq),
        in_specs=[pl.BlockSpec((bq, HEAD_D), lambda h, i: (i, h)),
                  pl.BlockSpec((bq, HEAD_D), lambda h, i: (i, N_HEADS + h)),
                  res, pl.BlockSpec((s, HEAD_D), lambda h, i: (0, N_HEADS + h)),
                  pl.BlockSpec((s, HEAD_D), lambda h, i: (0, 0))],
        out_specs=[pl.BlockSpec((bq, HEAD_D), lambda h, i: (i, h)), pl.BlockSpec((1, bq, 1), lambda h, i: (h, i, 0))],
        out_shape=[jax.ShapeDtypeStruct((s, N_HEADS * HEAD_D), F32), jax.ShapeDtypeStruct((N_HEADS, s, 1), F32)],
        compiler_params=_cparams(("parallel", "arbitrary")),
    )(q, q, kv, kv, kr)


def _mla_bwd(q, kv, kr, o, do, lse, *, name):
    s = q.shape[0]
    bq = bk = _attn_block(s)
    scale = float((MLA_NOPE + MLA_ROPE) ** -0.5)

    def body(qn_ref, qr_ref, kn_ref, v_ref, kr_ref, o_ref, do_ref, lse_ref,
             dqn_ref, dqr_ref, dkn_ref, dv_ref, dkr_ref):
        qi = pl.program_id(1)

        @pl.when(qi == 0)
        def _():
            dkn_ref[...] = jnp.zeros_like(dkn_ref)
            dv_ref[...] = jnp.zeros_like(dv_ref)
            dkr_ref[...] = jnp.zeros_like(dkr_ref)

        qn = qn_ref[...]
        qr = qr_ref[...]
        dov = do_ref[...]
        dob = dov.astype(BF16)
        delta = jnp.sum(dov * o_ref[...], axis=1, keepdims=True)
        lse = lse_ref[0]
        row = qi * bq + lax.broadcasted_iota(jnp.int32, (bq, bk), 0)
        colb = lax.broadcasted_iota(jnp.int32, (bq, bk), 1)

        def step(j, carry):
            dqn, dqr = carry
            ks = pl.ds(pl.multiple_of(j * bk, bk), bk)
            kn = kn_ref[ks, :]
            krv = kr_ref[ks, :]
            sc = (_dot_nt(qn, kn) + _dot_nt(qr, krv)) * scale
            p = jnp.where(j * bk + colb <= row, jnp.exp(sc - lse), 0.0)
            dp = _dot_nt(dob, v_ref[ks, :])
            ds = (p * (dp - delta) * scale).astype(BF16)
            pb = p.astype(BF16)
            dkn_ref[ks, :] += _dot_tn(ds, qn)
            dkr_ref[ks, :] += _dot_tn(ds, qr)
            dv_ref[ks, :] += _dot_tn(pb, dob)
            return dqn + _dot(ds, kn), dqr + _dot(ds, krv)

        zero = jnp.zeros((bq, HEAD_D), F32)
        dqn, dqr = lax.fori_loop(0, qi + 1, step, (zero, zero))
        dqn_ref[...] = dqn
        dqr_ref[...] = dqr

    tile = lambda off: pl.BlockSpec((bq, HEAD_D), functools.partial(lambda h, i, o: (i, h + o), o=off))
    res = lambda off: pl.BlockSpec((s, HEAD_D), functools.partial(lambda h, i, o: (0, h + o), o=off))
    wide = jax.ShapeDtypeStruct((s, N_HEADS * HEAD_D), F32)
    return pl.pallas_call(
        body, name=name,
        grid=(N_HEADS, s // bq),
        in_specs=[tile(0), tile(N_HEADS), res(0), res(N_HEADS), pl.BlockSpec((s, HEAD_D), lambda h, i: (0, 0)),
                  tile(0), tile(0), pl.BlockSpec((1, bq, 1), lambda h, i: (h, i, 0))],
        out_specs=[tile(0), tile(0), res(0), res(0), res(0)],
        out_shape=[wide] * 5,
        compiler_params=_cparams(("parallel", "arbitrary")),
    )(q, q, kv, kv, kr, o, do, lse)


def _split_dot(x, tri):
    hi = x.astype(BF16)
    lo = (x - hi.astype(F32)).astype(BF16)
    return _dot(hi, tri) + _dot(lo, tri)


def _sb_fwd(qkv, *, name):
    s = qkv.shape[0]
    bq = bk = _attn_block(s)
    scale = float(HEAD_D ** -0.5)

    def body(q_ref, k_ref, v_ref, o_ref, tot_ref):
        qi = pl.program_id(1)
        qv = q_ref[...]
        rowb = lax.broadcasted_iota(jnp.int32, (bq, bk), 0)
        colb = lax.broadcasted_iota(jnp.int32, (bq, bk), 1)
        row = qi * bq + rowb
        after = (lax.broadcasted_iota(jnp.int32, (bk, bk), 0) > lax.broadcasted_iota(jnp.int32, (bk, bk), 1)
                 ).astype(BF16)

        def step(i, carry):
            car, acc = carry
            j = qi - i
            ks = pl.ds(pl.multiple_of(j * bk, bk), bk)
            z = _dot_nt(qv, k_ref[ks, :]) * scale
            mask = j * bk + colb < row
            lb = jnp.minimum(z, 0.0) - jnp.log(1.0 + jnp.exp(-jnp.abs(z)))
            lr = jnp.where(mask, lb - z, 0.0)
            later = _split_dot(lr, after) + car
            w = jnp.where(mask, jnp.exp(lb + later), 0.0)
            acc = acc + _dot(w.astype(BF16), v_ref[ks, :])
            return car + jnp.sum(lr, axis=1, keepdims=True), acc

        car, acc = lax.fori_loop(0, qi + 1, step, (jnp.zeros((bq, 1), F32), jnp.zeros((bq, HEAD_D), F32)))
        o_ref[...] = acc
        tot_ref[0] = car

    res = lambda off: pl.BlockSpec((s, HEAD_D), functools.partial(lambda h, i, o: (0, h + o), o=off))
    return pl.pallas_call(
        body, name=name,
        grid=(N_HEADS, s // bq),
        in_specs=[pl.BlockSpec((bq, HEAD_D), lambda h, i: (i, h)), res(N_HEADS), res(2 * N_HEADS)],
        out_specs=[pl.BlockSpec((bq, HEAD_D), lambda h, i: (i, h)), pl.BlockSpec((1, bq, 1), lambda h, i: (h, i, 0))],
        out_shape=[jax.ShapeDtypeStruct((s, N_HEADS * HEAD_D), F32), jax.ShapeDtypeStruct((N_HEADS, s, 1), F32)],
        compiler_params=_cparams(("parallel", "arbitrary")),
    )(qkv, qkv, qkv)


def _sb_bwd(qkv, do, tot, *, name):
    s = qkv.shape[0]
    bq = bk = _attn_block(s)
    scale = float(HEAD_D ** -0.5)

    def body(q_ref, k_ref, v_ref, do_ref, tot_ref, dq_ref, dk_ref, dv_ref):
        qi = pl.program_id(1)

        @pl.when(qi == 0)
        def _():
            dk_ref[...] = jnp.zeros_like(dk_ref)
            dv_ref[...] = jnp.zeros_like(dv_ref)

        qv = q_ref[...]
        dob = do_ref[...].astype(BF16)
        rowb = lax.broadcasted_iota(jnp.int32, (bq, bk), 0)
        colb = lax.broadcasted_iota(jnp.int32, (bq, bk), 1)
        row = qi * bq + rowb
        r0 = lax.broadcasted_iota(jnp.int32, (bk, bk), 0)
        c0 = lax.broadcasted_iota(jnp.int32, (bk, bk), 1)
        upto = (r0 <= c0).astype(BF16)
        before = (r0 < c0).astype(BF16)

        def step(j, carry):
            rest, gsum, dq = carry
            ks = pl.ds(pl.multiple_of(j * bk, bk), bk)
            kv_ = k_ref[ks, :]
            z = _dot_nt(qv, kv_) * scale
            mask = j * bk + colb < row
            lb = jnp.minimum(z, 0.0) - jnp.log(1.0 + jnp.exp(-jnp.abs(z)))
            lr = jnp.where(mask, lb - z, 0.0)
            later = rest - _split_dot(lr, upto)
            w = jnp.where(mask, jnp.exp(lb + later), 0.0)
            g = _dot_nt(dob, v_ref[ks, :]) * w
            dlr = gsum + _dot(g.astype(BF16), before)
            dz = jnp.where(mask, g * jnp.exp(lr) - dlr * jnp.exp(lb), 0.0) * scale
            dzb = dz.astype(BF16)
            dk_ref[ks, :] += _dot_tn(dzb, qv)
            dv_ref[ks, :] += _dot_tn(w.astype(BF16), dob)
            return (rest - jnp.sum(lr, axis=1, keepdims=True), gsum + jnp.sum(g, axis=1, keepdims=True),
                    dq + _dot(dzb, kv_))

        init = (tot_ref[0], jnp.zeros((bq, 1), F32), jnp.zeros((bq, HEAD_D), F32))
        _, _, dq = lax.fori_loop(0, qi + 1, step, init)
        dq_ref[...] = dq

    tile = pl.BlockSpec((bq, HEAD_D), lambda h, i: (i, h))
    res = lambda off: pl.BlockSpec((s, HEAD_D), functools.partial(lambda h, i, o: (0, h + o), o=off))
    wide = jax.ShapeDtypeStruct((s, N_HEADS * HEAD_D), F32)
    return pl.pallas_call(
        body, name=name,
        grid=(N_HEADS, s // bq),
        in_specs=[tile, res(N_HEADS), res(2 * N_HEADS), tile, pl.BlockSpec((1, bq, 1), lambda h, i: (h, i, 0))],
        out_specs=[tile, res(0), res(0)],
        out_shape=[wide] * 3,
        compiler_params=_cparams(("parallel", "arbitrary")),
    )(qkv, qkv, qkv, do, tot)


def _exchange(buf, *, all_gather, name):
    shape = buf.shape if not all_gather else (N_DEV,) + buf.shape

    def body(src_ref, out_ref, send_sems, recv_sems, local_sem):
        x, y, c = lax.axis_index("x"), lax.axis_index("y"), lax.axis_index("c")
        me = 4 * x + 2 * y + c

        def src_for(idx):
            return src_ref if all_gather else src_ref.at[idx]

        own = pltpu.make_async_copy(src_for(me), out_ref.at[me], local_sem)
        own.start()
        peers = []
        for k in range(1, N_DEV):
            px = (1 - x) if (k >> 2) & 1 else x
            py = (1 - y) if (k >> 1) & 1 else y
            pc = (1 - c) if k & 1 else c
            peers.append((k - 1, (px, py, pc), 4 * px + 2 * py + pc))
        sends = []
        for slot, dev, idx in peers:
            cp = pltpu.make_async_remote_copy(
                src_ref=src_for(idx), dst_ref=out_ref.at[me], send_sem=send_sems.at[slot], recv_sem=recv_sems.at[slot],
                device_id=dev, device_id_type=pl.DeviceIdType.MESH)
            cp.start()
            sends.append(cp)
        for slot, dev, idx in peers:
            pltpu.make_async_remote_copy(
                src_ref=src_for(idx), dst_ref=out_ref.at[idx], send_sem=send_sems.at[slot], recv_sem=recv_sems.at[slot],
                device_id=dev, device_id_type=pl.DeviceIdType.MESH).wait_recv()
        for cp in sends:
            cp.wait_send()
        own.wait()

    return pl.pallas_call(
        body, name=name,
        in_specs=[pl.BlockSpec(memory_space=pl.ANY)],
        out_specs=pl.BlockSpec(memory_space=pl.ANY),
        out_shape=jax.ShapeDtypeStruct(shape, buf.dtype),
        scratch_shapes=[pltpu.SemaphoreType.DMA((N_DEV - 1,)), pltpu.SemaphoreType.DMA((N_DEV - 1,)),
                        pltpu.SemaphoreType.DMA],
    )(buf)


def _sum8(buf, *, name):
    _, r, c = buf.shape
    tr = _pick(r, (256, 128, 64, 32, 16, 8))

    def body(b_ref, o_ref):
        acc = b_ref[0].astype(F32)
        for q in range(1, N_DEV):
            acc = acc + b_ref[q].astype(F32)
        o_ref[...] = acc

    return pl.pallas_call(
        body, name=name,
        grid=(r // tr,),
        in_specs=[pl.BlockSpec((N_DEV, tr, c), lambda i: (0, i, 0))],
        out_specs=pl.BlockSpec((tr, c), lambda i: (i, 0)),
        out_shape=jax.ShapeDtypeStruct((r, c), F32),
        compiler_params=_cparams(("parallel",)),
    )(buf)


def _pack_rows(flat, mult):
    n = flat.shape[0]
    chunk = mult * PACK_COLS
    tot = -(-n // chunk) * chunk
    return jnp.pad(flat, (0, tot - n)).reshape(tot // PACK_COLS, PACK_COLS)


def _gather_layer(shards, layer):
    parts = [shards[nm][layer].astype(BF16).reshape(-1) for nm, _, _ in SHARDED]
    packed = _pack_rows(jnp.concatenate(parts), 16)
    got = _exchange(packed, all_gather=True, name="gather_weights").reshape(N_DEV, -1)
    full = {}
    off = 0
    for (nm, fshape, ax), part in zip(SHARDED, parts):
        n = part.shape[0]
        sshape = list(fshape)
        sshape[ax] //= N_DEV
        blk = got[:, off:off + n].reshape([N_DEV] + sshape)
        full[nm] = jnp.moveaxis(blk, 0, ax).reshape(fshape)
        off += n
    return full


def _scatter_layer(grads):
    parts = []
    for nm, fshape, ax in SHARDED:
        g = grads[nm]
        sshape = list(fshape)
        sshape[ax] //= N_DEV
        g = g.reshape(list(fshape[:ax]) + [N_DEV, sshape[ax]] + list(fshape[ax + 1:]))
        parts.append(jnp.moveaxis(g, ax, 0).reshape(N_DEV, -1))
    flat = jnp.concatenate(parts, axis=1)
    n = flat.shape[1]
    chunk = 8 * PACK_COLS
    tot = -(-n // chunk) * chunk
    packed = jnp.pad(flat, ((0, 0), (0, tot - n))).reshape(N_DEV, tot // PACK_COLS, PACK_COLS)
    got = _exchange(packed, all_gather=False, name="scatter_grads")
    summed = _sum8(got, name="sum_grads").reshape(-1)
    out = {}
    off = 0
    for (nm, fshape, ax), part in zip(SHARDED, parts):
        sshape = list(fshape)
        sshape[ax] //= N_DEV
        cnt = part.shape[1]
        out[nm] = summed[off:off + cnt].reshape(sshape)
        off += cnt
    return out


def _adamw(w, g, m, v, *, name):
    shape = w.shape
    cols = shape[-1] if len(shape) > 1 else shape[0]
    to2d = lambda a: a.reshape(-1, cols)
    bc1 = 1.0 - ADAM_B1 ** ADAM_STEP
    bc2 = 1.0 - ADAM_B2 ** ADAM_STEP

    def fn(wv, gv, mv, vv):
        mn = ADAM_B1 * mv + (1.0 - ADAM_B1) * gv
        vn = ADAM_B2 * vv + (1.0 - ADAM_B2) * (gv * gv)
        m_hat = mn / bc1
        v_hat = vn / bc2
        delta = -ADAM_LR * (m_hat / (jnp.sqrt(v_hat) + ADAM_EPS) + ADAM_WD * wv)
        return delta, mn, vn

    d, mn, vn = _ew(fn, [to2d(w), to2d(g), to2d(m), to2d(v)], [(cols, F32)] * 3, name=name)
    return d.reshape(shape), mn.reshape(shape), vn.reshape(shape)


def _prep_weights(full):
    w_in = full["w_in"]
    rope_cols = jnp.pad(w_in[:, 2048:2048 + MLA_ROPE], ((0, 0), (0, 128 - MLA_ROPE)))
    wq = full["mla_w_uq"].reshape(512, N_HEADS, MLA_NOPE + MLA_ROPE)
    wq_rope = jnp.pad(wq[:, :, MLA_NOPE:], ((0, 0), (0, 0), (0, 128 - MLA_ROPE)))
    wkv = full["mla_w_ukv"].reshape(512, N_HEADS, 2 * HEAD_D)
    return dict(
        wa=w_in[:, :2048], wb=w_in[:, 2112:5184], wc=w_in[:, 5184:8256], wd=w_in[:, 8256:14400], we=rope_cols,
        w_uq=jnp.concatenate([wq[:, :, :MLA_NOPE].reshape(512, 1024), wq_rope.reshape(512, 1024)], axis=1),
        w_ukv=jnp.concatenate([wkv[:, :, :HEAD_D].reshape(512, 1024), wkv[:, :, HEAD_D:].reshape(512, 1024)], axis=1),
        w_glu=full["ssm_w_glu"], w_branch=full["w_branch"], w_out=full["w_out"],
        w_ple_gate=full["w_ple_gate"], w_ple_proj=full["w_ple_proj"])


def _prep_ssm(sp):
    ab_re, ab_im, bb_re, bb_im = _s5_disc(sp["lam_re"], sp["lam_im"], sp["log_dt"], sp["b_re"], sp["b_im"])
    g4 = lambda a: a.reshape(SSM_GB, SSM_GB, a.shape[1], a.shape[2])
    bbd_re = _blockdiag(jnp.swapaxes(g4(bb_re), 2, 3)).astype(BF16)
    bbd_im = _blockdiag(jnp.swapaxes(g4(bb_im), 2, 3)).astype(BF16)
    cd_re = _blockdiag(jnp.swapaxes(g4(sp["c_re"]), 2, 3)).astype(BF16)
    cd_im = _blockdiag(jnp.swapaxes(g4(sp["c_im"]), 2, 3)).astype(BF16)
    return dict(bb_re=bbd_re, bb_im=bbd_im, bbt_re=jnp.swapaxes(bbd_re, 1, 2), bbt_im=jnp.swapaxes(bbd_im, 1, 2),
                c_re=cd_re, c_im=cd_im, ct_re=jnp.swapaxes(cd_re, 1, 2), ct_im=jnp.swapaxes(cd_im, 1, 2),
                dsk=sp["d"].reshape(1, BRANCH_W))


GELU_C = math.sqrt(2.0 / math.pi)


def _gelu(x):
    return 0.5 * x * (1.0 + jnp.tanh(GELU_C * (x + 0.044715 * x * x * x)))


def _gelu_grad(x):
    t = jnp.tanh(GELU_C * (x + 0.044715 * x * x * x))
    return 0.5 * (1.0 + t) + 0.5 * x * (1.0 - t * t) * GELU_C * (1.0 + 3.0 * 0.044715 * x * x)


def _layer_fwd(x_in, p_l, w, ssm, tabs, sm, rope, li):
    tg = lambda nm: f"l{li}_{nm}"
    ct, sp_, sm_ = rope
    h = _rmsnorm(x_in, sm["ln_g"], name=tg("ln"))
    seg_a = _mm(h, w["wa"], name=tg("in_a"))
    seg_b = _mm(h, w["wb"], name=tg("in_b"), out_dtypes=(BF16,))
    seg_c = _mm(h, w["wc"], name=tg("in_c"))
    seg_d = _mm(h, w["wd"], name=tg("in_d"))
    seg_e = _mm(h, w["we"], name=tg("in_e"))

    u = seg_a[:, :BRANCH_W]
    y_pre, hc_re, hc_im = _s5_fwd(u, ssm["bb_re"], ssm["bb_im"], ssm["c_re"], ssm["c_im"], ssm["dsk"], tabs,
                                  name=tg("s5_fwd"))
    yg = _ew(_gelu, [y_pre], [(BRANCH_W, F32)], name=tg("gelu"))
    y_ssm, z_glu = _mm(yg, w["w_glu"], name=tg("glu"), extras=[(yg, 0)],
                       epilogue=lambda acc, ygv: (ygv * _sigmoid(acc), acc), out_dtypes=(F32, F32))

    q_lat = seg_a[:, 1024:1536]
    kv_lat = seg_a[:, 1536:2048]
    qn = _rmsnorm(q_lat, sm["mla_g_q"], name=tg("q_norm"))
    kvn = _rmsnorm(kv_lat, sm["mla_g_kv"], name=tg("kv_norm"))
    q_raw = _mm(qn, w["w_uq"], name=tg("uq"))
    kv_b = _mm(kvn, w["w_ukv"], name=tg("ukv"), out_dtypes=(BF16,))

    def rope_q(qv, c_, p_, m_):
        outs = [qv[:, :1024]]
        for hh in range(N_HEADS):
            outs.append(_rope_group(qv[:, 1024 + 128 * hh:1152 + 128 * hh], c_, p_, m_))
        return jnp.concatenate(outs, axis=1)

    q_b = _ew(rope_q, [q_raw, ct, sp_, sm_], [(2048, BF16)], name=tg("rope_q"))
    kr_b = _ew(_rope_group, [seg_e, ct, sp_, sm_], [(128, BF16)], name=tg("rope_k"))
    o_mla, lse = _mla_fwd(q_b, kv_b, kr_b, name=tg("mla_fwd"))

    o_sb, tot = _sb_fwd(seg_b, name=tg("sb_fwd"))

    def gate_fn(a, b, c_, gp):
        ys = jnp.concatenate([a, b, c_], axis=1)
        return ys * (gp * _sigmoid(gp))

    ys = _ew(gate_fn, [y_ssm, o_mla, o_sb, seg_c], [(3 * BRANCH_W, BF16)], name=tg("gate"))
    merged = None
    bos = []
    for b in range(3):
        if b == 0:
            ep = lambda acc, ml: (_sigmoid(ml) * acc, acc)
            ex = [(seg_d, 0)]
        else:
            ep = lambda acc, ml, prev: (prev + _sigmoid(ml) * acc, acc)
            ex = [(seg_d, b * D_MODEL), (merged, 0)]
        merged, bo = _mm(ys[:, b * BRANCH_W:(b + 1) * BRANCH_W], w["w_branch"][b], name=tg(f"branch{b}"),
                         extras=ex, epilogue=ep, out_dtypes=(F32, BF16))
        bos.append(bo)
    merged_b = merged.astype(BF16)
    x1 = _mm(merged_b, w["w_out"], name=tg("out"), extras=[(x_in, 0)], epilogue=lambda acc, xv: (xv + acc,))

    hn2 = _rmsnorm(x1, sm["ple_g"], name=tg("ple_norm"))
    e = _mm(p_l, w["w_ple_proj"], name=tg("ple_proj"))
    x2, gl = _mm(hn2, w["w_ple_gate"], name=tg("ple_gate"), extras=[(x1, 0), (e, 0)],
                 epilogue=lambda acc, xv, ev: (xv + _sigmoid(acc) * ev, acc), out_dtypes=(F32, F32))
    saved = dict(x_in=x_in, h=h, seg_a=seg_a, seg_b=seg_b, seg_c=seg_c, seg_d=seg_d, u=u, y_pre=y_pre, hc_re=hc_re,
                 hc_im=hc_im, yg=yg, y_ssm=y_ssm, z_glu=z_glu, q_lat=q_lat, kv_lat=kv_lat, qn=qn, kvn=kvn, q_b=q_b,
                 kv_b=kv_b, kr_b=kr_b, o_mla=o_mla, lse=lse, o_sb=o_sb, tot=tot, ys=ys, bos=bos, merged_b=merged_b,
                 x1=x1, hn2=hn2, e=e, gl=gl, p_l=p_l)
    return x2, saved


def _layer_bwd(dx2, sv, w, ssm, tabs, sm, rope, li):
    tg = lambda nm: f"l{li}_{nm}"
    ct, sp_, sm_ = rope
    gw, gs = {}, {}

    def ple_fn(d, ev, glv):
        gt = _sigmoid(glv)
        return d * ev * gt * (1.0 - gt), d * gt

    dgl, de = _ew(ple_fn, [dx2, sv["e"], sv["gl"]], [(D_MODEL, BF16), (D_MODEL, BF16)], name=tg("ple_bwd"))
    gw["w_ple_proj"] = _mm(sv["p_l"], de, ta=True, name=tg("d_ple_proj"))
    gw["w_ple_gate"] = _mm(sv["hn2"], dgl, ta=True, name=tg("d_ple_gate"))
    dhn2 = _mm(dgl, w["w_ple_gate"], tb=True, name=tg("ple_gate_t"))
    dx1, gs["ple_g"] = _rmsnorm_bwd(sv["x1"], sm["ple_g"], dhn2, dx2, name=tg("ple_norm_bwd"))

    dx1_b = dx1.astype(BF16)
    gw["w_out"] = _mm(sv["merged_b"], dx1_b, ta=True, name=tg("d_out"))
    dmerged = _mm(dx1_b, w["w_out"], tb=True, name=tg("out_t"))

    def merge_fn(dm, ml, b0, b1, b2):
        dbo, dml = [], []
        for b, bo in enumerate((b0, b1, b2)):
            gt = _sigmoid(ml[:, b * D_MODEL:(b + 1) * D_MODEL])
            dbo.append(dm * gt)
            dml.append(dm * bo.astype(F32) * gt * (1.0 - gt))
        return jnp.concatenate(dbo, axis=1), jnp.concatenate(dml, axis=1)

    dbo, dseg_d = _ew(merge_fn, [dmerged, sv["seg_d"]] + sv["bos"], [(3 * D_MODEL, BF16), (3 * D_MODEL, BF16)],
                      name=tg("merge_bwd"))
    dys, dwb = [], []
    for b in range(3):
        dbo_b = dbo[:, b * D_MODEL:(b + 1) * D_MODEL]
        dwb.append(_mm(sv["ys"][:, b * BRANCH_W:(b + 1) * BRANCH_W], dbo_b, ta=True, name=tg(f"d_branch{b}")))
        dys.append(_mm(dbo_b, w["w_branch"][b], tb=True, name=tg(f"branch{b}_t")))
    gw["w_branch"] = jnp.stack(dwb)

    def gate_bwd(d0, d1, d2, a, b, c_, gp):
        dy_all = jnp.concatenate([d0, d1, d2], axis=1)
        y_all = jnp.concatenate([a, b, c_], axis=1)
        sg = _sigmoid(gp)
        silu = gp * sg
        dsilu = sg * (1.0 + gp * (1.0 - sg))
        dyv = dy_all * silu
        return (dyv[:, :BRANCH_W], dyv[:, BRANCH_W:2 * BRANCH_W], dyv[:, 2 * BRANCH_W:], dy_all * y_all * dsilu)

    dy_ssm, do_mla, do_sb, dseg_c = _ew(
        gate_bwd, dys + [sv["y_ssm"], sv["o_mla"], sv["o_sb"], sv["seg_c"]],
        [(BRANCH_W, F32), (BRANCH_W, F32), (BRANCH_W, F32), (3 * BRANCH_W, BF16)], name=tg("gate_bwd"))

    dq_sb, dk_sb, dv_sb = _sb_bwd(sv["seg_b"], do_sb, sv["tot"], name=tg("sb_bwd"))
    dseg_b = jnp.concatenate([dq_sb, dk_sb, dv_sb], axis=1).astype(BF16)

    dqn_, dqr_, dkn_, dv_, dkr_ = _mla_bwd(sv["q_b"], sv["kv_b"], sv["kr_b"], sv["o_mla"], do_mla, sv["lse"],
                                           name=tg("mla_bwd"))

    def unrope_q(dn, dr, c_, p_, m_):
        outs = [dn]
        for hh in range(N_HEADS):
            outs.append(_rope_group_t(dr[:, 128 * hh:128 * hh + 128], c_, p_, m_))
        return jnp.concatenate(outs, axis=1)

    dq_raw = _ew(unrope_q, [dqn_, dqr_, ct, sp_, sm_], [(2048, BF16)], name=tg("unrope_q"))

    def unrope_k(dr, c_, p_, m_):
        tot_ = dr[:, :128]
        for hh in range(1, N_HEADS):
            tot_ = tot_ + dr[:, 128 * hh:128 * hh + 128]
        return _rope_group_t(tot_, c_, p_, m_)

    dseg_e = _ew(unrope_k, [dkr_, ct, sp_, sm_], [(128, BF16)], name=tg("unrope_k"))
    dkv_raw = jnp.concatenate([dkn_, dv_], axis=1).astype(BF16)
    dw_uq = _mm(sv["qn"], dq_raw, ta=True, name=tg("d_uq"))
    dw_ukv = _mm(sv["kvn"], dkv_raw, ta=True, name=tg("d_ukv"))
    dqn = _mm(dq_raw, w["w_uq"], tb=True, name=tg("uq_t"))
    dkvn = _mm(dkv_raw, w["w_ukv"], tb=True, name=tg("ukv_t"))
    dq_lat, gs["mla_g_q"] = _rmsnorm_bwd(sv["q_lat"], sm["mla_g_q"], dqn, None, name=tg("q_norm_bwd"))
    dkv_lat, gs["mla_g_kv"] = _rmsnorm_bwd(sv["kv_lat"], sm["mla_g_kv"], dkvn, None, name=tg("kv_norm_bwd"))
    dwq = dw_uq[:, :1024].reshape(512, N_HEADS, MLA_NOPE)
    dwr = dw_uq[:, 1024:].reshape(512, N_HEADS, 128)[:, :, :MLA_ROPE]
    gw["mla_w_uq"] = jnp.concatenate([dwq, dwr], axis=2).reshape(512, N_HEADS * (MLA_NOPE + MLA_ROPE))
    dwk = dw_ukv[:, :1024].reshape(512, N_HEADS, HEAD_D)
    dwv = dw_ukv[:, 1024:].reshape(512, N_HEADS, HEAD_D)
    gw["mla_w_ukv"] = jnp.concatenate([dwk, dwv], axis=2).reshape(512, 2 * N_HEADS * HEAD_D)

    def glu_bwd(d, ygv, zv):
        sg = _sigmoid(zv)
        return d * ygv * sg * (1.0 - sg), d * sg

    dz, dyg0 = _ew(glu_bwd, [dy_ssm, sv["yg"], sv["z_glu"]], [(BRANCH_W, BF16), (BRANCH_W, F32)], name=tg("glu_bwd"))
    gw["ssm_w_glu"] = _mm(sv["yg"], dz, ta=True, name=tg("d_glu"))
    dy_pre = _mm(dz, w["w_glu"], tb=True, name=tg("glu_t"), extras=[(dyg0, 0), (sv["y_pre"], 0)],
                 epilogue=lambda acc, d0, yp: ((acc + d0) * _gelu_grad(yp),))
    du, dbb_re, dbb_im, dc_re, dc_im, da_re, da_im, dd = _s5_bwd(
        sv["u"], dy_pre, sv["hc_re"], sv["hc_im"], ssm["bb_re"], ssm["bb_im"], ssm["bbt_re"], ssm["bbt_im"],
        ssm["c_re"], ssm["c_im"], ssm["ct_re"], ssm["ct_im"], ssm["dsk"], tabs, name=tg("s5_bwd"))
    gs["ssm_d"] = dd.reshape(BRANCH_W)
    gs["ssm_c_re"] = jnp.swapaxes(_blockdiag_extract(dc_re, SSM_STATE, SSM_GROUP), 2, 3).reshape(
        SSM_GROUPS, SSM_GROUP, SSM_STATE)
    gs["ssm_c_im"] = jnp.swapaxes(_blockdiag_extract(dc_im, SSM_STATE, SSM_GROUP), 2, 3).reshape(
        SSM_GROUPS, SSM_GROUP, SSM_STATE)
    gs["_dbb_re"] = jnp.swapaxes(_blockdiag_extract(dbb_re, SSM_GROUP, SSM_STATE), 2, 3).reshape(
        SSM_GROUPS, SSM_STATE, SSM_GROUP)
    gs["_dbb_im"] = jnp.swapaxes(_blockdiag_extract(dbb_im, SSM_GROUP, SSM_STATE), 2, 3).reshape(
        SSM_GROUPS, SSM_STATE, SSM_GROUP)
    gs["_da_re"] = da_re.reshape(SSM_GROUPS, SSM_STATE)
    gs["_da_im"] = da_im.reshape(SSM_GROUPS, SSM_STATE)

    dseg_a = jnp.concatenate([du, dq_lat, dkv_lat], axis=1).astype(BF16)
    hb = sv["h"]
    dwa = _mm(hb, dseg_a, ta=True, name=tg("d_in_a"))
    dwb_ = _mm(hb, dseg_b, ta=True, name=tg("d_in_b"))
    dwc = _mm(hb, dseg_c, ta=True, name=tg("d_in_c"))
    dwd = _mm(hb, dseg_d, ta=True, name=tg("d_in_d"))
    dwe = _mm(hb, dseg_e, ta=True, name=tg("d_in_e"))
    gw["w_in"] = jnp.concatenate([dwa, dwe[:, :MLA_ROPE], dwb_, dwc, dwd], axis=1)
    add = lambda acc, prev: (acc + prev,)
    dh = _mm(dseg_a, w["wa"], tb=True, name=tg("in_a_t"))
    dh = _mm(dseg_b, w["wb"], tb=True, name=tg("in_b_t"), extras=[(dh, 0)], epilogue=add)
    dh = _mm(dseg_c, w["wc"], tb=True, name=tg("in_c_t"), extras=[(dh, 0)], epilogue=add)
    dh = _mm(dseg_d, w["wd"], tb=True, name=tg("in_d_t"), extras=[(dh, 0)], epilogue=add)
    dh = _mm(dseg_e, w["we"], tb=True, name=tg("in_e_t"), extras=[(dh, 0)], epilogue=add)
    dx_in, gs["ln_g"] = _rmsnorm_bwd(sv["x_in"], sm["ln_g"], dh, dx1, name=tg("ln_bwd"))
    return dx_in, gw, gs


def _local_step(x, p, target, fulls, small):
    s = x.shape[0]
    depth = len(fulls)
    rope = _rope_tables(s)
    ws, ssms, tabss, sms, saved = [], [], [], [], []
    xc = x
    for li in range(depth):
        sp = dict(lam_re=small["ssm_lam_re"][li], lam_im=small["ssm_lam_im"][li], log_dt=small["ssm_log_dt"][li],
                  b_re=small["ssm_b_re"][li], b_im=small["ssm_b_im"][li], c_re=small["ssm_c_re"][li],
                  c_im=small["ssm_c_im"][li], d=small["ssm_d"][li])
        w = _prep_weights(fulls[li])
        ssm = _prep_ssm(sp)
        tabs = _s5_tables(sp["lam_re"], sp["lam_im"], sp["log_dt"], min(SSM_CHUNK, s))
        sm = dict(ln_g=small["ln_g"][li], mla_g_q=small["mla_g_q"][li], mla_g_kv=small["mla_g_kv"][li],
                  ple_g=small["ple_g"][li])
        xc, sv = _layer_fwd(xc, p[li], w, ssm, tabs, sm, rope, li)
        ws.append(w), ssms.append(ssm), tabss.append(tabs), sms.append(sm), saved.append(sv)
    dx, dgf, loss = _loss_head(xc, small["final_g"], target, name="loss_head")
    gws, gss = [None] * depth, [None] * depth
    for li in reversed(range(depth)):
        dx, gw, gs = _layer_bwd(dx, saved[li], ws[li], ssms[li], tabss[li], sms[li], rope, li)
        args = (small["ssm_lam_re"][li], small["ssm_lam_im"][li], small["ssm_log_dt"][li], small["ssm_b_re"][li],
                small["ssm_b_im"][li])
        _, vjp = jax.vjp(_s5_disc, *args)
        g_lr, g_li, g_dt, g_br, g_bi = vjp((gs.pop("_da_re"), gs.pop("_da_im"), gs.pop("_dbb_re"), gs.pop("_dbb_im")))
        gs.update(ssm_lam_re=g_lr, ssm_lam_im=g_li, ssm_log_dt=g_dt, ssm_b_re=g_br, ssm_b_im=g_bi)
        gws[li], gss[li] = gw, gs
    gsmall = {nm: jnp.stack([gss[li][nm].reshape(small[nm].shape[1:]) for li in range(depth)])
              for nm in SMALL if nm != "final_g"}
    gsmall["final_g"] = dgf.reshape(-1)
    return loss[0, 0], dx, gws, gsmall


def kernel(x, p, ln_g, w_in, ssm_lam_re, ssm_lam_im, ssm_log_dt, ssm_b_re, ssm_b_im, ssm_c_re, ssm_c_im, ssm_d, ssm_w_glu, mla_g_q, mla_g_kv, mla_w_uq, mla_w_ukv, w_branch, w_out, ple_g, w_ple_gate, w_ple_proj, final_g, loss_target, m_ln_g, m_w_in, m_ssm_lam_re, m_ssm_lam_im, m_ssm_log_dt, m_ssm_b_re, m_ssm_b_im, m_ssm_c_re, m_ssm_c_im, m_ssm_d, m_ssm_w_glu, m_mla_g_q, m_mla_g_kv, m_mla_w_uq, m_mla_w_ukv, m_w_branch, m_w_out, m_ple_g, m_w_ple_gate, m_w_ple_proj, m_final_g, v_ln_g, v_w_in, v_ssm_lam_re, v_ssm_lam_im, v_ssm_log_dt, v_ssm_b_re, v_ssm_b_im, v_ssm_c_re, v_ssm_c_im, v_ssm_d, v_ssm_w_glu, v_mla_g_q, v_mla_g_kv, v_mla_w_uq, v_mla_w_ukv, v_w_branch, v_w_out, v_ple_g, v_w_ple_gate, v_w_ple_proj, v_final_g):
    weights = dict(ln_g=ln_g, w_in=w_in, ssm_lam_re=ssm_lam_re, ssm_lam_im=ssm_lam_im, ssm_log_dt=ssm_log_dt,
                   ssm_b_re=ssm_b_re, ssm_b_im=ssm_b_im, ssm_c_re=ssm_c_re, ssm_c_im=ssm_c_im, ssm_d=ssm_d,
                   ssm_w_glu=ssm_w_glu, mla_g_q=mla_g_q, mla_g_kv=mla_g_kv, mla_w_uq=mla_w_uq, mla_w_ukv=mla_w_ukv,
                   w_branch=w_branch, w_out=w_out, ple_g=ple_g, w_ple_gate=w_ple_gate, w_ple_proj=w_ple_proj,
                   final_g=final_g)
    mom_m = dict(ln_g=m_ln_g, w_in=m_w_in, ssm_lam_re=m_ssm_lam_re, ssm_lam_im=m_ssm_lam_im, ssm_log_dt=m_ssm_log_dt,
                 ssm_b_re=m_ssm_b_re, ssm_b_im=m_ssm_b_im, ssm_c_re=m_ssm_c_re, ssm_c_im=m_ssm_c_im, ssm_d=m_ssm_d,
                 ssm_w_glu=m_ssm_w_glu, mla_g_q=m_mla_g_q, mla_g_kv=m_mla_g_kv, mla_w_uq=m_mla_w_uq,
                 mla_w_ukv=m_mla_w_ukv, w_branch=m_w_branch, w_out=m_w_out, ple_g=m_ple_g, w_ple_gate=m_w_ple_gate,
                 w_ple_proj=m_w_ple_proj, final_g=m_final_g)
    mom_v = dict(ln_g=v_ln_g, w_in=v_w_in, ssm_lam_re=v_ssm_lam_re, ssm_lam_im=v_ssm_lam_im, ssm_log_dt=v_ssm_log_dt,
                 ssm_b_re=v_ssm_b_re, ssm_b_im=v_ssm_b_im, ssm_c_re=v_ssm_c_re, ssm_c_im=v_ssm_c_im, ssm_d=v_ssm_d,
                 ssm_w_glu=v_ssm_w_glu, mla_g_q=v_mla_g_q, mla_g_kv=v_mla_g_kv, mla_w_uq=v_mla_w_uq,
                 mla_w_ukv=v_mla_w_ukv, w_branch=v_w_branch, w_out=v_w_out, ple_g=v_ple_g, w_ple_gate=v_w_ple_gate,
                 w_ple_proj=v_w_ple_proj, final_g=v_final_g)
    depth = ln_g.shape[0]
    fulls = [_gather_layer(weights, li) for li in range(depth)]
    small = {nm: weights[nm] for nm in SMALL}
    loss_local, dx, gws, gsmall = _local_step(x[0], p[:, 0], loss_target[0], fulls, small)
    loss = lax.psum(loss_local, ("x", "y", "c"))

    grads = {}
    shard_g = [_scatter_layer(gws[li]) for li in range(depth)]
    for nm, _, _ in SHARDED:
        grads[nm] = jnp.stack([shard_g[li][nm] for li in range(depth)])
    flat = jnp.concatenate([gsmall[nm].reshape(-1) for nm in SMALL])
    n_small = flat.shape[0]
    got = _exchange(_pack_rows(flat, 8), all_gather=True, name="gather_small_grads")
    gsum = _sum8(got, name="sum_small_grads").reshape(-1)[:n_small]
    off = 0
    for nm in SMALL:
        cnt = weights[nm].size
        grads[nm] = gsum[off:off + cnt].reshape(weights[nm].shape)
        off += cnt

    delta, new_m, new_v = {}, {}, {}
    for nm, _, _ in SHARDED:
        delta[nm], new_m[nm], new_v[nm] = _adamw(weights[nm], grads[nm], mom_m[nm], mom_v[nm], name=f"adamw_{nm}")
    pk = lambda d: _pack_rows(jnp.concatenate([d[nm].reshape(-1) for nm in SMALL]), 8)
    d_s, m_s, v_s = _adamw(pk(weights), pk(grads), pk(mom_m), pk(mom_v), name="adamw_small")
    off = 0
    for nm in SMALL:
        cnt = weights[nm].size
        shp = weights[nm].shape
        delta[nm] = d_s.reshape(-1)[off:off + cnt].reshape(shp)
        new_m[nm] = m_s.reshape(-1)[off:off + cnt].reshape(shp)
        new_v[nm] = v_s.reshape(-1)[off:off + cnt].reshape(shp)
        off += cnt
    return (loss, dx[None], *[grads[nm] for nm in WEIGHT_ORDER], *[delta[nm] for nm in WEIGHT_ORDER],
            *[new_m[nm] for nm in WEIGHT_ORDER], *[new_v[nm] for nm in WEIGHT_ORDER])
```

```python
import functools
import math

import jax
import jax.numpy as jnp
from jax import lax
from jax.experimental import pallas as pl
from jax.experimental.pallas import tpu as pltpu

F32 = jnp.float32
BF16 = jnp.bfloat16

N_DEV = 8
D_MODEL = 2048
BRANCH_W = 1024
N_HEADS = 8
HEAD_D = 128
SSM_GROUPS = 64
SSM_GROUP = 16
SSM_STATE = 64
SSM_GB = 8
SSM_CHUNK = 256
MLA_ROPE = 64
MLA_NOPE = 128
ROPE_THETA = 10000.0
NORM_EPS = 1e-6
DT_MIN = 1e-3
DT_MAX = 1e-1
PACK_COLS = 1024

ADAM_LR = 0.001
ADAM_B1 = 0.9
ADAM_B2 = 0.999
ADAM_EPS = 1e-08
ADAM_WD = 0.01
ADAM_STEP = 10

VMEM_LIMIT = 56 * 2 ** 20
NEG_BIG = -1e30
MM_VMEM_BUDGET = 36 * 2 ** 20

SHARDED = (
    ("w_in", (2048, 14400), 1),
    ("ssm_w_glu", (1024, 1024), 0),
    ("mla_w_uq", (512, 1536), 1),
    ("mla_w_ukv", (512, 2048), 1),
    ("w_branch", (3, 1024, 2048), 2),
    ("w_out", (2048, 2048), 0),
    ("w_ple_gate", (2048, 2048), 0),
    ("w_ple_proj", (256, 2048), 1),
)
SMALL = ("ln_g", "ssm_lam_re", "ssm_lam_im", "ssm_log_dt", "ssm_b_re", "ssm_b_im", "ssm_c_re", "ssm_c_im",
         "ssm_d", "mla_g_q", "mla_g_kv", "ple_g", "final_g")
WEIGHT_ORDER = ("ln_g", "w_in", "ssm_lam_re", "ssm_lam_im", "ssm_log_dt", "ssm_b_re", "ssm_b_im", "ssm_c_re",
                "ssm_c_im", "ssm_d", "ssm_w_glu", "mla_g_q", "mla_g_kv", "mla_w_uq", "mla_w_ukv", "w_branch",
                "w_out", "ple_g", "w_ple_gate", "w_ple_proj", "final_g")


def _cparams(sem=None):
    return pltpu.CompilerParams(dimension_semantics=sem, vmem_limit_bytes=VMEM_LIMIT)


def _pick(n, prefs):
    for t in prefs:
        if n % t == 0:
            return t
    return n


def _dot(a, b):
    return lax.dot_general(a, b, (((1,), (0,)), ((), ())), preferred_element_type=F32)


def _dot_nt(a, b):
    return lax.dot_general(a, b, (((1,), (1,)), ((), ())), preferred_element_type=F32)


def _dot_tn(a, b):
    return lax.dot_general(a, b, (((0,), (0,)), ((), ())), preferred_element_type=F32)


def _sigmoid(x):
    return 1.0 / (1.0 + jnp.exp(-x))


def _mm(a, b, *, name, ta=False, tb=False, extras=(), epilogue=None, out_dtypes=(F32,)):
    m, k = (a.shape[1], a.shape[0]) if ta else a.shape
    n = b.shape[0] if tb else b.shape[1]
    assert (b.shape[1] if tb else b.shape[0]) == k
    n_ex = len(extras)
    n_out = len(out_dtypes)
    tn = _pick(n, (1024, 512, 256, 128))
    tk = _pick(k, (2048, 1024, 512, 256, 128))
    for tm in (1024, 512, 256, 128):
        if m % tm:
            continue
        est = 2 * (tm * tk * a.dtype.itemsize + tk * tn * b.dtype.itemsize) + tm * tn * 4
        est += 2 * tm * tn * (sum(e[0].dtype.itemsize for e in extras) + sum(jnp.dtype(d).itemsize for d in out_dtypes))
        if est <= MM_VMEM_BUDGET:
            break
    nk = k // tk

    def body(*refs):
        a_ref, b_ref = refs[0], refs[1]
        ex_refs = refs[2:2 + n_ex]
        out_refs = refs[2 + n_ex:2 + n_ex + n_out]
        acc_ref = refs[-1]
        kk = pl.program_id(2)

        @pl.when(kk == 0)
        def _():
            acc_ref[...] = jnp.zeros_like(acc_ref)

        av = a_ref[...].astype(BF16)
        bv = b_ref[...].astype(BF16)
        dims = (((0 if ta else 1,), (1 if tb else 0,)), ((), ()))
        acc_ref[...] += lax.dot_general(av, bv, dims, preferred_element_type=F32)

        @pl.when(kk == nk - 1)
        def _():
            acc = acc_ref[...]
            res = epilogue(acc, *[r[...] for r in ex_refs]) if epilogue is not None else (acc,)
            for r, v in zip(out_refs, res):
                r[...] = v.astype(r.dtype)

    a_spec = pl.BlockSpec((tk, tm), lambda i, j, q: (q, i)) if ta else pl.BlockSpec((tm, tk), lambda i, j, q: (i, q))
    b_spec = pl.BlockSpec((tn, tk), lambda i, j, q: (j, q)) if tb else pl.BlockSpec((tk, tn), lambda i, j, q: (q, j))
    ex_specs = []
    for arr, off in extras:
        assert off % tn == 0 and arr.shape[0] == m
        ex_specs.append(pl.BlockSpec((tm, tn), functools.partial(lambda i, j, q, o: (i, j + o), o=off // tn)))
    outs = pl.pallas_call(
        body, name=name,
        grid=(m // tm, n // tn, nk),
        in_specs=[a_spec, b_spec] + ex_specs,
        out_specs=[pl.BlockSpec((tm, tn), lambda i, j, q: (i, j)) for _ in out_dtypes],
        out_shape=[jax.ShapeDtypeStruct((m, n), dt) for dt in out_dtypes],
        scratch_shapes=[pltpu.VMEM((tm, tn), F32)],
        compiler_params=_cparams(("parallel", "parallel", "arbitrary")),
    )(a, b, *[e[0] for e in extras])
    return outs if n_out > 1 else outs[0]


def _row_tile(m, bytes_per_row):
    for t in (1024, 512, 256, 128, 64, 32, 16):
        if m % t == 0 and 2 * t * bytes_per_row <= 20 * 2 ** 20:
            return t
    return 16 if m % 16 == 0 else m


def _ew(fn, ins, outs, *, name):
    m = max(x.shape[0] for x in ins)
    bpr = sum(x.shape[1] * x.dtype.itemsize for x in ins if x.shape[0] == m)
    bpr += sum(no * jnp.dtype(dt).itemsize for no, dt in outs)
    tm = _row_tile(m, bpr)
    n_in = len(ins)

    def body(*refs):
        res = fn(*[r[...] for r in refs[:n_in]])
        if not isinstance(res, (tuple, list)):
            res = (res,)
        for r, v in zip(refs[n_in:], res):
            r[...] = v.astype(r.dtype)

    in_specs = []
    for x in ins:
        if x.shape[0] == m:
            in_specs.append(pl.BlockSpec((tm, x.shape[1]), lambda i: (i, 0)))
        else:
            in_specs.append(pl.BlockSpec(x.shape, lambda i: (0, 0)))
    res = pl.pallas_call(
        body, name=name,
        grid=(m // tm,),
        in_specs=in_specs,
        out_specs=[pl.BlockSpec((tm, no), lambda i: (i, 0)) for no, _ in outs],
        out_shape=[jax.ShapeDtypeStruct((m, no), dt) for no, dt in outs],
        compiler_params=_cparams(("parallel",)),
    )(*ins)
    return res if len(outs) > 1 else res[0]


def _rmsnorm(x, g, *, name, out_dtype=BF16):
    d = x.shape[1]

    def fn(xv, gv):
        r = lax.rsqrt(jnp.mean(xv * xv, axis=-1, keepdims=True) + NORM_EPS)
        return xv * r * gv

    return _ew(fn, [x, g.reshape(1, d)], [(d, out_dtype)], name=name)


def _rmsnorm_bwd(x, g, dy, res, *, name):
    m, d = x.shape
    has_res = res is not None
    tm = _row_tile(m, d * 4 * (4 if has_res else 3))

    def body(*refs):
        x_ref, g_ref, dy_ref = refs[:3]
        res_ref = refs[3] if has_res else None
        dx_ref, dg_ref = refs[-2], refs[-1]

        @pl.when(pl.program_id(0) == 0)
        def _():
            dg_ref[...] = jnp.zeros_like(dg_ref)

        xv = x_ref[...]
        dyv = dy_ref[...].astype(F32)
        r = lax.rsqrt(jnp.mean(xv * xv, axis=-1, keepdims=True) + NORM_EPS)
        xh = xv * r
        dg_ref[...] += jnp.sum(dyv * xh, axis=0, keepdims=True)
        dyg = dyv * g_ref[...]
        dx = r * (dyg - xh * jnp.mean(dyg * xh, axis=-1, keepdims=True))
        if has_res:
            dx = dx + res_ref[...]
        dx_ref[...] = dx

    row = pl.BlockSpec((tm, d), lambda i: (i, 0))
    vec = pl.BlockSpec((1, d), lambda i: (0, 0))
    ins = [x, g.reshape(1, d), dy] + ([res] if has_res else [])
    return pl.pallas_call(
        body, name=name,
        grid=(m // tm,),
        in_specs=[row, vec, row] + ([row] if has_res else []),
        out_specs=[row, vec],
        out_shape=[jax.ShapeDtypeStruct((m, d), F32), jax.ShapeDtypeStruct((1, d), F32)],
        compiler_params=_cparams(("arbitrary",)),
    )(*ins)


def _loss_head(x, g, target, *, name):
    m, d = x.shape
    tm = _row_tile(m, d * 4 * 3)

    def body(x_ref, g_ref, t_ref, dx_ref, dg_ref, loss_ref):
        @pl.when(pl.program_id(0) == 0)
        def _():
            dg_ref[...] = jnp.zeros_like(dg_ref)
            loss_ref[...] = jnp.zeros_like(loss_ref)

        xv = x_ref[...]
        gv = g_ref[...]
        r = lax.rsqrt(jnp.mean(xv * xv, axis=-1, keepdims=True) + NORM_EPS)
        xh = xv * r
        diff = xh * gv - t_ref[...]
        loss_ref[...] += 0.5 * jnp.sum(jnp.mean(diff * diff, axis=-1, keepdims=True), axis=0, keepdims=True)
        dyv = diff * (1.0 / d)
        dg_ref[...] += jnp.sum(dyv * xh, axis=0, keepdims=True)
        dyg = dyv * gv
        dx_ref[...] = r * (dyg - xh * jnp.mean(dyg * xh, axis=-1, keepdims=True))

    row = pl.BlockSpec((tm, d), lambda i: (i, 0))
    vec = pl.BlockSpec((1, d), lambda i: (0, 0))
    return pl.pallas_call(
        body, name=name,
        grid=(m // tm,),
        in_specs=[row, vec, row],
        out_specs=[row, vec, pl.BlockSpec((1, 128), lambda i: (0, 0))],
        out_shape=[jax.ShapeDtypeStruct((m, d), F32), jax.ShapeDtypeStruct((1, d), F32),
                   jax.ShapeDtypeStruct((1, 128), F32)],
        compiler_params=_cparams(("arbitrary",)),
    )(x, g.reshape(1, d), target)


def _rope_tables(seqlen):
    pos = jnp.arange(seqlen, dtype=F32)
    inv_freq = ROPE_THETA ** (-jnp.arange(0, MLA_ROPE, 2, dtype=F32) / MLA_ROPE)
    ang = pos[:, None] * inv_freq[None, :]
    c, s = jnp.cos(ang), jnp.sin(ang)
    z = jnp.zeros_like(c)
    return (jnp.concatenate([c, c, z, z], axis=1), jnp.concatenate([z, s, z, z], axis=1),
            jnp.concatenate([-s, z, z, z], axis=1))


def _rope_group(xg, ct, sp, sm):
    return xg * ct + pltpu.roll(xg, 32, 1) * sp + pltpu.roll(xg, 96, 1) * sm


def _rope_group_t(dg, ct, sp, sm):
    return dg * ct + pltpu.roll(dg * sp, 96, 1) + pltpu.roll(dg * sm, 32, 1)


def _s5_disc(lam_re, lam_im, log_dt, b_re, b_im):
    dt = jnp.exp(log_dt)[:, None]
    mag = jnp.exp(lam_re * dt)
    ab_re = mag * jnp.cos(lam_im * dt)
    ab_im = mag * jnp.sin(lam_im * dt)
    den = lam_re * lam_re + lam_im * lam_im
    nr = ab_re - 1.0
    ni = ab_im
    coef_re = (nr * lam_re + ni * lam_im) / den
    coef_im = (ni * lam_re - nr * lam_im) / den
    bb_re = coef_re[..., None] * b_re - coef_im[..., None] * b_im
    bb_im = coef_re[..., None] * b_im + coef_im[..., None] * b_re
    return ab_re, ab_im, bb_re, bb_im


def _blockdiag(x):
    nb, ng, r, c = x.shape
    eye = jnp.eye(ng, dtype=x.dtype)
    return (x[:, :, :, None, :] * eye[None, :, None, :, None]).reshape(nb, ng * r, ng * c)


def _blockdiag_extract(x, r, c):
    nb = x.shape[0]
    x5 = x.reshape(nb, SSM_GB, r, SSM_GB, c)
    eye = jnp.eye(SSM_GB, dtype=x.dtype)
    return jnp.sum(x5 * eye[None, :, None, :, None], axis=3)


def _s5_tables(lam_re, lam_im, log_dt, t_chunk):
    dt = jnp.exp(log_dt)[:, None]
    lr = (lam_re * dt).reshape(1, -1)
    li = (lam_im * dt).reshape(1, -1)
    nlog = int(math.log2(t_chunk))
    n1 = jnp.arange(1, t_chunk + 1, dtype=F32)[:, None]
    n2 = (2.0 ** jnp.arange(nlog, dtype=F32))[:, None]

    def powers(n):
        mag = jnp.exp(n * lr)
        return mag * jnp.cos(n * li), mag * jnp.sin(n * li)

    p_re, p_im = powers(n1)
    a_re, a_im = powers(n2)
    return p_re, p_im, a_re, a_im


def _hs_scan(h_re, h_im, a_re_ref, a_im_ref, rows, t_chunk, reverse):
    nlog = int(math.log2(t_chunk))
    for k in range(nlog):
        sh = 1 << k
        ar = a_re_ref[pl.ds(k, 1), :]
        ai = a_im_ref[pl.ds(k, 1), :]
        if reverse:
            s_re = pltpu.roll(h_re, t_chunk - sh, 0)
            s_im = pltpu.roll(h_im, t_chunk - sh, 0)
            keep = rows < t_chunk - sh
            ai = -ai
        else:
            s_re = pltpu.roll(h_re, sh, 0)
            s_im = pltpu.roll(h_im, sh, 0)
            keep = rows >= sh
        s_re = jnp.where(keep, s_re, 0.0)
        s_im = jnp.where(keep, s_im, 0.0)
        h_re, h_im = h_re + ar * s_re - ai * s_im, h_im + ar * s_im + ai * s_re
    return h_re, h_im


def _s5_fwd(u, bb_re, bb_im, c_re, c_im, dsk, tabs, *, name):
    s = u.shape[0]
    t = min(SSM_CHUNK, s)
    nt = s // t
    ns = SSM_GB * SSM_STATE
    p_re, p_im, a_re, a_im = tabs
    nlog = a_re.shape[0]

    def body(u_ref, bbr_ref, bbi_ref, cr_ref, ci_ref, d_ref, pr_ref, pi_ref, ar_ref, ai_ref,
             y_ref, hcr_ref, hci_ref, car_re, car_im):
        @pl.when(pl.program_id(1) == 0)
        def _():
            car_re[...] = jnp.zeros_like(car_re)
            car_im[...] = jnp.zeros_like(car_im)

        cin_re = car_re[...]
        cin_im = car_im[...]
        hcr_ref[0] = cin_re
        hci_ref[0] = cin_im
        uv = u_ref[...]
        ub = uv.astype(BF16)
        h_re = _dot(ub, bbr_ref[0])
        h_im = _dot(ub, bbi_ref[0])
        rows = lax.broadcasted_iota(jnp.int32, (t, ns), 0)
        h_re, h_im = _hs_scan(h_re, h_im, ar_ref, ai_ref, rows, t, False)
        pr = pr_ref[...]
        pi = pi_ref[...]
        h_re = h_re + pr * cin_re - pi * cin_im
        h_im = h_im + pr * cin_im + pi * cin_re
        last = rows == t - 1
        car_re[...] = jnp.sum(jnp.where(last, h_re, 0.0), axis=0, keepdims=True)
        car_im[...] = jnp.sum(jnp.where(last, h_im, 0.0), axis=0, keepdims=True)
        y = _dot(h_re.astype(BF16), cr_ref[0]) - _dot(h_im.astype(BF16), ci_ref[0]) + d_ref[...] * uv
        y_ref[...] = y

    blk = lambda r, c: pl.BlockSpec((1, r, c), lambda g, i: (g, 0, 0))
    tab = lambda r: pl.BlockSpec((r, ns), lambda g, i: (0, g))
    return pl.pallas_call(
        body, name=name,
        grid=(SSM_GB, nt),
        in_specs=[pl.BlockSpec((t, 128), lambda g, i: (i, g)), blk(128, ns), blk(128, ns), blk(ns, 128), blk(ns, 128),
                  pl.BlockSpec((1, 128), lambda g, i: (0, g)), tab(t), tab(t), tab(nlog), tab(nlog)],
        out_specs=[pl.BlockSpec((t, 128), lambda g, i: (i, g)),
                   pl.BlockSpec((1, 1, ns), lambda g, i: (i, 0, g)), pl.BlockSpec((1, 1, ns), lambda g, i: (i, 0, g))],
        out_shape=[jax.ShapeDtypeStruct((s, BRANCH_W), F32),
                   jax.ShapeDtypeStruct((nt, 1, SSM_GB * ns), F32), jax.ShapeDtypeStruct((nt, 1, SSM_GB * ns), F32)],
        scratch_shapes=[pltpu.VMEM((1, ns), F32), pltpu.VMEM((1, ns), F32)],
        compiler_params=_cparams(("parallel", "arbitrary")),
    )(u, bb_re, bb_im, c_re, c_im, dsk, p_re, p_im, a_re, a_im)


def _s5_bwd(u, dy, hc_re, hc_im, bb_re, bb_im, bbt_re, bbt_im, c_re, c_im, ct_re, ct_im, dsk, tabs, *, name):
    s = u.shape[0]
    t = min(SSM_CHUNK, s)
    nt = s // t
    ns = SSM_GB * SSM_STATE
    p_re, p_im, a_re, a_im = tabs
    q_re, q_im = p_re[::-1], p_im[::-1]
    nlog = a_re.shape[0]

    def body(u_ref, dy_ref, hcr_ref, hci_ref, bbr_ref, bbi_ref, bbtr_ref, bbti_ref, cr_ref, ci_ref, ctr_ref, cti_ref,
             d_ref, pr_ref, pi_ref, qr_ref, qi_ref, ar_ref, ai_ref,
             du_ref, dbbr_ref, dbbi_ref, dcr_ref, dci_ref, dar_ref, dai_ref, dd_ref, lam_re_c, lam_im_c):
        @pl.when(pl.program_id(1) == 0)
        def _():
            lam_re_c[...] = jnp.zeros_like(lam_re_c)
            lam_im_c[...] = jnp.zeros_like(lam_im_c)
            dbbr_ref[...] = jnp.zeros_like(dbbr_ref)
            dbbi_ref[...] = jnp.zeros_like(dbbi_ref)
            dcr_ref[...] = jnp.zeros_like(dcr_ref)
            dci_ref[...] = jnp.zeros_like(dci_ref)
            dar_ref[...] = jnp.zeros_like(dar_ref)
            dai_ref[...] = jnp.zeros_like(dai_ref)
            dd_ref[...] = jnp.zeros_like(dd_ref)

        cin_re = hcr_ref[0]
        cin_im = hci_ref[0]
        uv = u_ref[...]
        ub = uv.astype(BF16)
        dyv = dy_ref[...]
        dyb = dyv.astype(BF16)
        rows = lax.broadcasted_iota(jnp.int32, (t, ns), 0)
        h_re = _dot(ub, bbr_ref[0])
        h_im = _dot(ub, bbi_ref[0])
        h_re, h_im = _hs_scan(h_re, h_im, ar_ref, ai_ref, rows, t, False)
        pr = pr_ref[...]
        pi = pi_ref[...]
        h_re = h_re + pr * cin_re - pi * cin_im
        h_im = h_im + pr * cin_im + pi * cin_re
        dcr_ref[0] += _dot_tn(h_re.astype(BF16), dyb)
        dci_ref[0] -= _dot_tn(h_im.astype(BF16), dyb)
        first = rows == 0
        hp_re = jnp.where(first, cin_re, pltpu.roll(h_re, 1, 0))
        hp_im = jnp.where(first, cin_im, pltpu.roll(h_im, 1, 0))
        l_re = _dot(dyb, ctr_ref[0])
        l_im = -_dot(dyb, cti_ref[0])
        l_re, l_im = _hs_scan(l_re, l_im, ar_ref, ai_ref, rows, t, True)
        qr = qr_ref[...]
        qi = qi_ref[...]
        nin_re = lam_re_c[...]
        nin_im = lam_im_c[...]
        l_re = l_re + qr * nin_re + qi * nin_im
        l_im = l_im + qr * nin_im - qi * nin_re
        lam_re_c[...] = jnp.sum(jnp.where(first, l_re, 0.0), axis=0, keepdims=True)
        lam_im_c[...] = jnp.sum(jnp.where(first, l_im, 0.0), axis=0, keepdims=True)
        lrb = l_re.astype(BF16)
        lib = l_im.astype(BF16)
        du_ref[...] = _dot(lrb, bbtr_ref[0]) + _dot(lib, bbti_ref[0]) + d_ref[...] * dyv
        dbbr_ref[0] += _dot_tn(ub, lrb)
        dbbi_ref[0] += _dot_tn(ub, lib)
        dar_ref[0] += jnp.sum(l_re * hp_re + l_im * hp_im, axis=0, keepdims=True)
        dai_ref[0] += jnp.sum(l_im * hp_re - l_re * hp_im, axis=0, keepdims=True)
        dd_ref[0] += jnp.sum(dyv * uv, axis=0, keepdims=True)

    rev = lambda g, i: (nt - 1 - i, g)
    blk = lambda r, c: pl.BlockSpec((1, r, c), lambda g, i: (g, 0, 0))
    tab = lambda r: pl.BlockSpec((r, ns), lambda g, i: (0, g))
    car = pl.BlockSpec((1, 1, ns), lambda g, i: (nt - 1 - i, 0, g))
    return pl.pallas_call(
        body, name=name,
        grid=(SSM_GB, nt),
        in_specs=[pl.BlockSpec((t, 128), rev), pl.BlockSpec((t, 128), rev), car, car,
                  blk(128, ns), blk(128, ns), blk(ns, 128), blk(ns, 128), blk(ns, 128), blk(ns, 128),
                  blk(128, ns), blk(128, ns), pl.BlockSpec((1, 128), lambda g, i: (0, g)),
                  tab(t), tab(t), tab(t), tab(t), tab(nlog), tab(nlog)],
        out_specs=[pl.BlockSpec((t, 128), rev), blk(128, ns), blk(128, ns), blk(ns, 128), blk(ns, 128),
                   blk(1, ns), blk(1, ns), blk(1, 128)],
        out_shape=[jax.ShapeDtypeStruct((s, BRANCH_W), F32),
                   jax.ShapeDtypeStruct((SSM_GB, 128, ns), F32), jax.ShapeDtypeStruct((SSM_GB, 128, ns), F32),
                   jax.ShapeDtypeStruct((SSM_GB, ns, 128), F32), jax.ShapeDtypeStruct((SSM_GB, ns, 128), F32),
                   jax.ShapeDtypeStruct((SSM_GB, 1, ns), F32), jax.ShapeDtypeStruct((SSM_GB, 1, ns), F32),
                   jax.ShapeDtypeStruct((SSM_GB, 1, 128), F32)],
        scratch_shapes=[pltpu.VMEM((1, ns), F32), pltpu.VMEM((1, ns), F32)],
        compiler_params=_cparams(("parallel", "arbitrary")),
    )(u, dy, hc_re, hc_im, bb_re, bb_im, bbt_re, bbt_im, c_re, c_im, ct_re, ct_im, dsk,
      p_re, p_im, q_re, q_im, a_re, a_im)


ATT_BQ = 512
ATT_BK = 512


def _attn_blocks(s):
    return min(ATT_BQ, s), min(ATT_BK, s)


def _rel_index(bq, bk):
    return lax.broadcasted_iota(jnp.int32, (bq, bk), 0) - lax.broadcasted_iota(jnp.int32, (bq, bk), 1)


def _mla_fwd(q, kv, kr, *, name):
    s = q.shape[0]
    bq, bk = _attn_blocks(s)
    r = bq // bk
    scale = float((MLA_NOPE + MLA_ROPE) ** -0.5)

    def body(qn_ref, qr_ref, kn_ref, v_ref, kr_ref, o_ref, lse_ref):
        qi = pl.program_id(1)
        qn = qn_ref[...]
        qr = qr_ref[...]
        nfull = qi * r
        rel = _rel_index(bq, bk)

        def step(j, carry, d):
            m, l, acc = carry
            ks = pl.ds(pl.multiple_of(j * bk, bk), bk)
            sc = (_dot_nt(qn, kn_ref[ks, :]) + _dot_nt(qr, kr_ref[ks, :])) * scale
            if d is not None:
                sc = jnp.where(rel >= d * bk, sc, NEG_BIG)
            m_new = jnp.maximum(m, jnp.max(sc, axis=1, keepdims=True))
            alpha = jnp.exp(m - m_new)
            p = jnp.exp(sc - m_new)
            l = alpha * l + jnp.sum(p, axis=1, keepdims=True)
            acc = alpha * acc + _dot(p.astype(BF16), v_ref[ks, :])
            return m_new, l, acc

        carry = (jnp.full((bq, 1), NEG_BIG, F32), jnp.zeros((bq, 1), F32), jnp.zeros((bq, HEAD_D), F32))
        carry = lax.fori_loop(0, nfull, lambda j, c: step(j, c, None), carry)
        for d in range(r):
            carry = step(nfull + d, carry, d)
        m, l, acc = carry
        o_ref[...] = acc / l
        lse_ref[0] = m + jnp.log(l)

    res = pl.BlockSpec((s, HEAD_D), lambda h, i: (0, h))
    return pl.pallas_call(
        body, name=name,
        grid=(N_HEADS, s // bq),
        in_specs=[pl.BlockSpec((bq, HEAD_D), lambda h, i: (i, h)),
                  pl.BlockSpec((bq, HEAD_D), lambda h, i: (i, N_HEADS + h)),
                  res, pl.BlockSpec((s, HEAD_D), lambda h, i: (0, N_HEADS + h)),
                  pl.BlockSpec((s, HEAD_D), lambda h, i: (0, 0))],
        out_specs=[pl.BlockSpec((bq, HEAD_D), lambda h, i: (i, h)), pl.BlockSpec((1, bq, 1), lambda h, i: (h, i, 0))],
        out_shape=[jax.ShapeDtypeStruct((s, N_HEADS * HEAD_D), F32), jax.ShapeDtypeStruct((N_HEADS, s, 1), F32)],
        compiler_params=_cparams(("parallel", "arbitrary")),
    )(q, q, kv, kv, kr)


def _mla_bwd(q, kv, kr, o, do, lse, *, name):
    s = q.shape[0]
    bq, bk = _attn_blocks(s)
    r = bq // bk
    scale = float((MLA_NOPE + MLA_ROPE) ** -0.5)

    def body(qn_ref, qr_ref, kn_ref, v_ref, kr_ref, o_ref, do_ref, lse_ref,
             dqn_ref, dqr_ref, dkn_ref, dv_ref, dkr_ref):
        qi = pl.program_id(1)

        @pl.when(qi == 0)
        def _():
            dkn_ref[...] = jnp.zeros_like(dkn_ref)
            dv_ref[...] = jnp.zeros_like(dv_ref)
            dkr_ref[...] = jnp.zeros_like(dkr_ref)

        qn = qn_ref[...]
        qr = qr_ref[...]
        dov = do_ref[...]
        dob = dov.astype(BF16)
        delta = jnp.sum(dov * o_ref[...], axis=1, keepdims=True)
        lse = lse_ref[0]
        nfull = qi * r
        rel = _rel_index(bq, bk)

        def step(j, carry, d):
            dqn, dqr = carry
            ks = pl.ds(pl.multiple_of(j * bk, bk), bk)
            kn = kn_ref[ks, :]
            krv = kr_ref[ks, :]
            sc = (_dot_nt(qn, kn) + _dot_nt(qr, krv)) * scale
            p = jnp.exp(sc - lse)
            if d is not None:
                p = jnp.where(rel >= d * bk, p, 0.0)
            dp = _dot_nt(dob, v_ref[ks, :])
            ds = (p * (dp - delta) * scale).astype(BF16)
            pb = p.astype(BF16)
            dkn_ref[ks, :] += _dot_tn(ds, qn)
            dkr_ref[ks, :] += _dot_tn(ds, qr)
            dv_ref[ks, :] += _dot_tn(pb, dob)
            return dqn + _dot(ds, kn), dqr + _dot(ds, krv)

        zero = jnp.zeros((bq, HEAD_D), F32)
        carry = lax.fori_loop(0, nfull, lambda j, c: step(j, c, None), (zero, zero))
        for d in range(r):
            carry = step(nfull + d, carry, d)
        dqn_ref[...] = carry[0]
        dqr_ref[...] = carry[1]

    tile = lambda off: pl.BlockSpec((bq, HEAD_D), functools.partial(lambda h, i, o: (i, h + o), o=off))
    res = lambda off: pl.BlockSpec((s, HEAD_D), functools.partial(lambda h, i, o: (0, h + o), o=off))
    wide = jax.ShapeDtypeStruct((s, N_HEADS * HEAD_D), F32)
    return pl.pallas_call(
        body, name=name,
        grid=(N_HEADS, s // bq),
        in_specs=[tile(0), tile(N_HEADS), res(0), res(N_HEADS), pl.BlockSpec((s, HEAD_D), lambda h, i: (0, 0)),
                  tile(0), tile(0), pl.BlockSpec((1, bq, 1), lambda h, i: (h, i, 0))],
        out_specs=[tile(0), tile(0), res(0), res(0), res(0)],
        out_shape=[wide] * 5,
        compiler_params=_cparams(("parallel", "arbitrary")),
    )(q, q, kv, kv, kr, o, do, lse)


def _split_dot(x, tri):
    hi = x.astype(BF16)
    lo = (x - hi.astype(F32)).astype(BF16)
    return _dot(hi, tri) + _dot(lo, tri)


def _log_sigmoid(z):
    return jnp.minimum(z, 0.0) - jnp.log(1.0 + jnp.exp(-jnp.abs(z)))


def _sb_fwd(qkv, *, name):
    s = qkv.shape[0]
    bq, bk = _attn_blocks(s)
    r = bq // bk
    scale = float(HEAD_D ** -0.5)

    def body(q_ref, k_ref, v_ref, o_ref, tot_ref):
        qi = pl.program_id(1)
        qv = q_ref[...]
        nfull = qi * r
        rel = _rel_index(bq, bk)
        after = (lax.broadcasted_iota(jnp.int32, (bk, bk), 0) > lax.broadcasted_iota(jnp.int32, (bk, bk), 1)
                 ).astype(BF16)

        def step(j, carry, d):
            car, acc = carry
            ks = pl.ds(pl.multiple_of(j * bk, bk), bk)
            z = _dot_nt(qv, k_ref[ks, :]) * scale
            lb = _log_sigmoid(z)
            lr = lb - z
            if d is not None:
                mask = rel > d * bk
                lr = jnp.where(mask, lr, 0.0)
            w = jnp.exp(lb + _split_dot(lr, after) + car)
            if d is not None:
                w = jnp.where(mask, w, 0.0)
            acc = acc + _dot(w.astype(BF16), v_ref[ks, :])
            return car + jnp.sum(lr, axis=1, keepdims=True), acc

        carry = (jnp.zeros((bq, 1), F32), jnp.zeros((bq, HEAD_D), F32))
        for d in reversed(range(r)):
            carry = step(nfull + d, carry, d)
        car, acc = lax.fori_loop(0, nfull, lambda i, c: step(nfull - 1 - i, c, None), carry)
        o_ref[...] = acc
        tot_ref[0] = car

    res = lambda off: pl.BlockSpec((s, HEAD_D), functools.partial(lambda h, i, o: (0, h + o), o=off))
    return pl.pallas_call(
        body, name=name,
        grid=(N_HEADS, s // bq),
        in_specs=[pl.BlockSpec((bq, HEAD_D), lambda h, i: (i, h)), res(N_HEADS), res(2 * N_HEADS)],
        out_specs=[pl.BlockSpec((bq, HEAD_D), lambda h, i: (i, h)), pl.BlockSpec((1, bq, 1), lambda h, i: (h, i, 0))],
        out_shape=[jax.ShapeDtypeStruct((s, N_HEADS * HEAD_D), F32), jax.ShapeDtypeStruct((N_HEADS, s, 1), F32)],
        compiler_params=_cparams(("parallel", "arbitrary")),
    )(qkv, qkv, qkv)


def _sb_bwd(qkv, do, tot, *, name):
    s = qkv.shape[0]
    bq, bk = _attn_blocks(s)
    r = bq // bk
    scale = float(HEAD_D ** -0.5)

    def body(q_ref, k_ref, v_ref, do_ref, tot_ref, dq_ref, dk_ref, dv_ref):
        qi = pl.program_id(1)

        @pl.when(qi == 0)
        def _():
            dk_ref[...] = jnp.zeros_like(dk_ref)
            dv_ref[...] = jnp.zeros_like(dv_ref)

        qv = q_ref[...]
        dob = do_ref[...].astype(BF16)
        nfull = qi * r
        rel = _rel_index(bq, bk)
        r0 = lax.broadcasted_iota(jnp.int32, (bk, bk), 0)
        c0 = lax.broadcasted_iota(jnp.int32, (bk, bk), 1)
        upto = (r0 <= c0).astype(BF16)
        before = (r0 < c0).astype(BF16)

        def step(j, carry, d):
            rest, gsum, dq = carry
            ks = pl.ds(pl.multiple_of(j * bk, bk), bk)
            kv_ = k_ref[ks, :]
            z = _dot_nt(qv, kv_) * scale
            lb = _log_sigmoid(z)
            lr = lb - z
            if d is not None:
                mask = rel > d * bk
                lr = jnp.where(mask, lr, 0.0)
            w = jnp.exp(lb + rest - _split_dot(lr, upto))
            if d is not None:
                w = jnp.where(mask, w, 0.0)
            g = _dot_nt(dob, v_ref[ks, :]) * w
            dlr = gsum + _dot(g.astype(BF16), before)
            dz = (g * jnp.exp(lr) - dlr * jnp.exp(lb)) * scale
            if d is not None:
                dz = jnp.where(mask, dz, 0.0)
            dzb = dz.astype(BF16)
            dk_ref[ks, :] += _dot_tn(dzb, qv)
            dv_ref[ks, :] += _dot_tn(w.astype(BF16), dob)
            return (rest - jnp.sum(lr, axis=1, keepdims=True), gsum + jnp.sum(g, axis=1, keepdims=True),
                    dq + _dot(dzb, kv_))

        carry = (tot_ref[0], jnp.zeros((bq, 1), F32), jnp.zeros((bq, HEAD_D), F32))
        carry = lax.fori_loop(0, nfull, lambda j, c: step(j, c, None), carry)
        for d in range(r):
            carry = step(nfull + d, carry, d)
        dq_ref[...] = carry[2]

    tile = pl.BlockSpec((bq, HEAD_D), lambda h, i: (i, h))
    res = lambda off: pl.BlockSpec((s, HEAD_D), functools.partial(lambda h, i, o: (0, h + o), o=off))
    wide = jax.ShapeDtypeStruct((s, N_HEADS * HEAD_D), F32)
    return pl.pallas_call(
        body, name=name,
        grid=(N_HEADS, s // bq),
        in_specs=[tile, res(N_HEADS), res(2 * N_HEADS), tile, pl.BlockSpec((1, bq, 1), lambda h, i: (h, i, 0))],
        out_specs=[tile, res(0), res(0)],
        out_shape=[wide] * 3,
        compiler_params=_cparams(("parallel", "arbitrary")),
    )(qkv, qkv, qkv, do, tot)


def _exchange(bufs, *, all_gather, name):
    nb = len(bufs)
    shapes = [((N_DEV,) + b.shape) if all_gather else b.shape for b in bufs]

    def body(*refs):
        srcs, outs = refs[:nb], refs[nb:2 * nb]
        send_sems, recv_sems, local_sems = refs[2 * nb:]
        x, y, c = lax.axis_index("x"), lax.axis_index("y"), lax.axis_index("c")
        me = 4 * x + 2 * y + c

        def src_for(b, idx):
            return srcs[b] if all_gather else srcs[b].at[idx]

        def copy(slot, b, dev, src_idx, dst_idx):
            return pltpu.make_async_remote_copy(
                src_ref=src_for(b, src_idx), dst_ref=outs[b].at[dst_idx], send_sem=send_sems.at[slot * nb + b],
                recv_sem=recv_sems.at[slot * nb + b], device_id=dev, device_id_type=pl.DeviceIdType.MESH)

        owns = [pltpu.make_async_copy(src_for(b, me), outs[b].at[me], local_sems.at[b]) for b in range(nb)]
        for own in owns:
            own.start()
        peers = []
        for k in range(1, N_DEV):
            px = (1 - x) if (k >> 2) & 1 else x
            py = (1 - y) if (k >> 1) & 1 else y
            pc = (1 - c) if k & 1 else c
            peers.append((k - 1, (px, py, pc), 4 * px + 2 * py + pc))
        sends = [copy(slot, b, dev, idx, me) for slot, dev, idx in peers for b in range(nb)]
        for cp in sends:
            cp.start()
        for slot, dev, idx in peers:
            for b in range(nb):
                copy(slot, b, dev, idx, idx).wait_recv()
        for cp in sends:
            cp.wait_send()
        for own in owns:
            own.wait()

    any_spec = pl.BlockSpec(memory_space=pl.ANY)
    return pl.pallas_call(
        body, name=name,
        in_specs=[any_spec] * nb,
        out_specs=[any_spec] * nb,
        out_shape=[jax.ShapeDtypeStruct(sh, b.dtype) for sh, b in zip(shapes, bufs)],
        scratch_shapes=[pltpu.SemaphoreType.DMA(((N_DEV - 1) * nb,)), pltpu.SemaphoreType.DMA(((N_DEV - 1) * nb,)),
                        pltpu.SemaphoreType.DMA((nb,))],
    )(*bufs)


def _sum8(buf, *, name):
    _, r, c = buf.shape
    tr = _pick(r, (256, 128, 64, 32, 16, 8))

    def body(b_ref, o_ref):
        acc = b_ref[0].astype(F32)
        for q in range(1, N_DEV):
            acc = acc + b_ref[q].astype(F32)
        o_ref[...] = acc

    return pl.pallas_call(
        body, name=name,
        grid=(r // tr,),
        in_specs=[pl.BlockSpec((N_DEV, tr, c), lambda i: (0, i, 0))],
        out_specs=pl.BlockSpec((tr, c), lambda i: (i, 0)),
        out_shape=jax.ShapeDtypeStruct((r, c), F32),
        compiler_params=_cparams(("parallel",)),
    )(buf)


def _pack_rows(flat, mult):
    n = flat.shape[0]
    chunk = mult * PACK_COLS
    tot = -(-n // chunk) * chunk
    return jnp.pad(flat, (0, tot - n)).reshape(tot // PACK_COLS, PACK_COLS)


def _gather_layer(shards, layer):
    parts = [shards[nm][layer].astype(BF16) for nm, _, _ in SHARDED]
    got = _exchange(parts, all_gather=True, name="gather_weights")
    return {nm: jnp.moveaxis(g, 0, ax).reshape(fshape) for (nm, fshape, ax), g in zip(SHARDED, got)}


def _scatter_layer(grads):
    parts = []
    for nm, fshape, ax in SHARDED:
        g = grads[nm].reshape(tuple(fshape[:ax]) + (N_DEV, fshape[ax] // N_DEV) + tuple(fshape[ax + 1:]))
        parts.append(jnp.moveaxis(g, ax, 0))
    got = _exchange(parts, all_gather=False, name="scatter_grads")
    return {nm: g for (nm, _, _), g in zip(SHARDED, got)}


_ADAM_BC1 = 1.0 - ADAM_B1 ** ADAM_STEP
_ADAM_BC2 = 1.0 - ADAM_B2 ** ADAM_STEP


def _adamw_math(wv, gv, mv, vv):
    mn = ADAM_B1 * mv + (1.0 - ADAM_B1) * gv
    vn = ADAM_B2 * vv + (1.0 - ADAM_B2) * (gv * gv)
    m_hat = mn / _ADAM_BC1
    v_hat = vn / _ADAM_BC2
    delta = -ADAM_LR * (m_hat / (jnp.sqrt(v_hat) + ADAM_EPS) + ADAM_WD * wv)
    return delta, mn, vn


def _adamw(w, g, m, v, *, name):
    shape = w.shape
    cols = shape[-1]
    to2d = lambda a: a.reshape(-1, cols)
    d, mn, vn = _ew(_adamw_math, [to2d(w), to2d(g), to2d(m), to2d(v)], [(cols, F32)] * 3, name=name)
    return d.reshape(shape), mn.reshape(shape), vn.reshape(shape)


def _adamw_sharded(w, parts, m, v, *, name):
    depth = w.shape[0]
    cols = w.shape[-1]
    rows = w[0].size // cols
    to3 = lambda a: a.reshape(depth, rows, cols)
    w3, m3, v3 = to3(w), to3(m), to3(v)
    bpr = cols * (3 * 4 + N_DEV * parts[0].dtype.itemsize + 4 * 4)
    tm = _row_tile(rows, bpr)
    outs = None
    for li in range(depth):
        carried = li > 0

        def body(*refs):
            w_ref, p_ref, m_ref, v_ref = refs[:4]
            g_ref, d_ref, mo_ref, vo_ref = refs[-4:]
            gv = p_ref[0].astype(F32)
            for q in range(1, N_DEV):
                gv = gv + p_ref[q].astype(F32)
            delta, mn, vn = _adamw_math(w_ref[0], gv, m_ref[0], v_ref[0])
            g_ref[0] = gv
            d_ref[0] = delta
            mo_ref[0] = mn
            vo_ref[0] = vn

        lay = pl.BlockSpec((1, tm, cols), functools.partial(lambda i, l: (l, i, 0), l=li))
        any_spec = pl.BlockSpec(memory_space=pl.ANY)
        ins = [w3, parts[li].reshape(N_DEV, rows, cols), m3, v3] + (list(outs) if carried else [])
        outs = pl.pallas_call(
            body, name=f"{name}_l{li}",
            grid=(rows // tm,),
            in_specs=[lay, pl.BlockSpec((N_DEV, tm, cols), lambda i: (0, i, 0)), lay, lay]
            + ([any_spec] * 4 if carried else []),
            out_specs=[lay] * 4,
            out_shape=[jax.ShapeDtypeStruct((depth, rows, cols), F32)] * 4,
            input_output_aliases={4 + q: q for q in range(4)} if carried else {},
            compiler_params=_cparams(("parallel",)),
        )(*ins)
    return [o.reshape(w.shape) for o in outs]


def _prep_weights(full):
    w_in = full["w_in"]
    rope_cols = jnp.pad(w_in[:, 2048:2048 + MLA_ROPE], ((0, 0), (0, 128 - MLA_ROPE)))
    wq = full["mla_w_uq"].reshape(512, N_HEADS, MLA_NOPE + MLA_ROPE)
    wq_rope = jnp.pad(wq[:, :, MLA_NOPE:], ((0, 0), (0, 0), (0, 128 - MLA_ROPE)))
    wkv = full["mla_w_ukv"].reshape(512, N_HEADS, 2 * HEAD_D)
    return dict(
        wa=w_in[:, :2048], wb=w_in[:, 2112:5184], wc=w_in[:, 5184:8256], wd=w_in[:, 8256:14400], we=rope_cols,
        w_uq=jnp.concatenate([wq[:, :, :MLA_NOPE].reshape(512, 1024), wq_rope.reshape(512, 1024)], axis=1),
        w_ukv=jnp.concatenate([wkv[:, :, :HEAD_D].reshape(512, 1024), wkv[:, :, HEAD_D:].reshape(512, 1024)], axis=1),
        w_glu=full["ssm_w_glu"], w_branch=full["w_branch"], w_out=full["w_out"],
        w_ple_gate=full["w_ple_gate"], w_ple_proj=full["w_ple_proj"])


def _prep_ssm(sp):
    ab_re, ab_im, bb_re, bb_im = _s5_disc(sp["lam_re"], sp["lam_im"], sp["log_dt"], sp["b_re"], sp["b_im"])
    g4 = lambda a: a.reshape(SSM_GB, SSM_GB, a.shape[1], a.shape[2])
    bbd_re = _blockdiag(jnp.swapaxes(g4(bb_re), 2, 3)).astype(BF16)
    bbd_im = _blockdiag(jnp.swapaxes(g4(bb_im), 2, 3)).astype(BF16)
    cd_re = _blockdiag(jnp.swapaxes(g4(sp["c_re"]), 2, 3)).astype(BF16)
    cd_im = _blockdiag(jnp.swapaxes(g4(sp["c_im"]), 2, 3)).astype(BF16)
    return dict(bb_re=bbd_re, bb_im=bbd_im, bbt_re=jnp.swapaxes(bbd_re, 1, 2), bbt_im=jnp.swapaxes(bbd_im, 1, 2),
                c_re=cd_re, c_im=cd_im, ct_re=jnp.swapaxes(cd_re, 1, 2), ct_im=jnp.swapaxes(cd_im, 1, 2),
                dsk=sp["d"].reshape(1, BRANCH_W))


GELU_C = math.sqrt(2.0 / math.pi)


def _gelu(x):
    return 0.5 * x * (1.0 + jnp.tanh(GELU_C * (x + 0.044715 * x * x * x)))


def _gelu_grad(x):
    t = jnp.tanh(GELU_C * (x + 0.044715 * x * x * x))
    return 0.5 * (1.0 + t) + 0.5 * x * (1.0 - t * t) * GELU_C * (1.0 + 3.0 * 0.044715 * x * x)


def _layer_fwd(x_in, p_l, w, ssm, tabs, sm, rope, li):
    tg = lambda nm: f"l{li}_{nm}"
    ct, sp_, sm_ = rope
    h = _rmsnorm(x_in, sm["ln_g"], name=tg("ln"))
    seg_a = _mm(h, w["wa"], name=tg("in_a"))
    seg_b = _mm(h, w["wb"], name=tg("in_b"), out_dtypes=(BF16,))
    seg_c = _mm(h, w["wc"], name=tg("in_c"))
    seg_d = _mm(h, w["wd"], name=tg("in_d"))
    seg_e = _mm(h, w["we"], name=tg("in_e"))

    u = seg_a[:, :BRANCH_W]
    y_pre, hc_re, hc_im = _s5_fwd(u, ssm["bb_re"], ssm["bb_im"], ssm["c_re"], ssm["c_im"], ssm["dsk"], tabs,
                                  name=tg("s5_fwd"))
    yg = _ew(_gelu, [y_pre], [(BRANCH_W, F32)], name=tg("gelu"))
    y_ssm, z_glu = _mm(yg, w["w_glu"], name=tg("glu"), extras=[(yg, 0)],
                       epilogue=lambda acc, ygv: (ygv * _sigmoid(acc), acc), out_dtypes=(F32, F32))

    q_lat = seg_a[:, 1024:1536]
    kv_lat = seg_a[:, 1536:2048]
    qn = _rmsnorm(q_lat, sm["mla_g_q"], name=tg("q_norm"))
    kvn = _rmsnorm(kv_lat, sm["mla_g_kv"], name=tg("kv_norm"))
    q_raw = _mm(qn, w["w_uq"], name=tg("uq"))
    kv_b = _mm(kvn, w["w_ukv"], name=tg("ukv"), out_dtypes=(BF16,))

    def rope_q(qv, c_, p_, m_):
        outs = [qv[:, :1024]]
        for hh in range(N_HEADS):
            outs.append(_rope_group(qv[:, 1024 + 128 * hh:1152 + 128 * hh], c_, p_, m_))
        return jnp.concatenate(outs, axis=1)

    q_b = _ew(rope_q, [q_raw, ct, sp_, sm_], [(2048, BF16)], name=tg("rope_q"))
    kr_b = _ew(_rope_group, [seg_e, ct, sp_, sm_], [(128, BF16)], name=tg("rope_k"))
    o_mla, lse = _mla_fwd(q_b, kv_b, kr_b, name=tg("mla_fwd"))

    o_sb, tot = _sb_fwd(seg_b, name=tg("sb_fwd"))

    def gate_fn(a, b, c_, gp):
        ys = jnp.concatenate([a, b, c_], axis=1)
        return ys * (gp * _sigmoid(gp))

    ys = _ew(gate_fn, [y_ssm, o_mla, o_sb, seg_c], [(3 * BRANCH_W, BF16)], name=tg("gate"))
    merged = None
    bos = []
    for b in range(3):
        if b == 0:
            ep = lambda acc, ml: (_sigmoid(ml) * acc, acc)
            ex = [(seg_d, 0)]
        else:
            ep = lambda acc, ml, prev: (prev + _sigmoid(ml) * acc, acc)
            ex = [(seg_d, b * D_MODEL), (merged, 0)]
        merged, bo = _mm(ys[:, b * BRANCH_W:(b + 1) * BRANCH_W], w["w_branch"][b], name=tg(f"branch{b}"),
                         extras=ex, epilogue=ep, out_dtypes=(F32, BF16))
        bos.append(bo)
    merged_b = merged.astype(BF16)
    x1 = _mm(merged_b, w["w_out"], name=tg("out"), extras=[(x_in, 0)], epilogue=lambda acc, xv: (xv + acc,))

    hn2 = _rmsnorm(x1, sm["ple_g"], name=tg("ple_norm"))
    e = _mm(p_l, w["w_ple_proj"], name=tg("ple_proj"))
    x2, gl = _mm(hn2, w["w_ple_gate"], name=tg("ple_gate"), extras=[(x1, 0), (e, 0)],
                 epilogue=lambda acc, xv, ev: (xv + _sigmoid(acc) * ev, acc), out_dtypes=(F32, F32))
    saved = dict(x_in=x_in, h=h, seg_a=seg_a, seg_b=seg_b, seg_c=seg_c, seg_d=seg_d, u=u, y_pre=y_pre, hc_re=hc_re,
                 hc_im=hc_im, yg=yg, y_ssm=y_ssm, z_glu=z_glu, q_lat=q_lat, kv_lat=kv_lat, qn=qn, kvn=kvn, q_b=q_b,
                 kv_b=kv_b, kr_b=kr_b, o_mla=o_mla, lse=lse, o_sb=o_sb, tot=tot, ys=ys, bos=bos, merged_b=merged_b,
                 x1=x1, hn2=hn2, e=e, gl=gl, p_l=p_l)
    return x2, saved


def _layer_bwd(dx2, sv, w, ssm, tabs, sm, rope, li):
    tg = lambda nm: f"l{li}_{nm}"
    ct, sp_, sm_ = rope
    gw, gs = {}, {}

    def ple_fn(d, ev, glv):
        gt = _sigmoid(glv)
        return d * ev * gt * (1.0 - gt), d * gt

    dgl, de = _ew(ple_fn, [dx2, sv["e"], sv["gl"]], [(D_MODEL, BF16), (D_MODEL, BF16)], name=tg("ple_bwd"))
    gw["w_ple_proj"] = _mm(sv["p_l"], de, ta=True, out_dtypes=(BF16,), name=tg("d_ple_proj"))
    gw["w_ple_gate"] = _mm(sv["hn2"], dgl, ta=True, out_dtypes=(BF16,), name=tg("d_ple_gate"))
    dhn2 = _mm(dgl, w["w_ple_gate"], tb=True, name=tg("ple_gate_t"))
    dx1, gs["ple_g"] = _rmsnorm_bwd(sv["x1"], sm["ple_g"], dhn2, dx2, name=tg("ple_norm_bwd"))

    dx1_b = dx1.astype(BF16)
    gw["w_out"] = _mm(sv["merged_b"], dx1_b, ta=True, out_dtypes=(BF16,), name=tg("d_out"))
    dmerged = _mm(dx1_b, w["w_out"], tb=True, name=tg("out_t"))

    def merge_fn(dm, ml, b0, b1, b2):
        dbo, dml = [], []
        for b, bo in enumerate((b0, b1, b2)):
            gt = _sigmoid(ml[:, b * D_MODEL:(b + 1) * D_MODEL])
            dbo.append(dm * gt)
            dml.append(dm * bo.astype(F32) * gt * (1.0 - gt))
        return jnp.concatenate(dbo, axis=1), jnp.concatenate(dml, axis=1)

    dbo, dseg_d = _ew(merge_fn, [dmerged, sv["seg_d"]] + sv["bos"], [(3 * D_MODEL, BF16), (3 * D_MODEL, BF16)],
                      name=tg("merge_bwd"))
    dys, dwb = [], []
    for b in range(3):
        dbo_b = dbo[:, b * D_MODEL:(b + 1) * D_MODEL]
        dwb.append(_mm(sv["ys"][:, b * BRANCH_W:(b + 1) * BRANCH_W], dbo_b, ta=True, out_dtypes=(BF16,), name=tg(f"d_branch{b}")))
        dys.append(_mm(dbo_b, w["w_branch"][b], tb=True, name=tg(f"branch{b}_t")))
    gw["w_branch"] = jnp.stack(dwb)

    def gate_bwd(d0, d1, d2, a, b, c_, gp):
        dy_all = jnp.concatenate([d0, d1, d2], axis=1)
        y_all = jnp.concatenate([a, b, c_], axis=1)
        sg = _sigmoid(gp)
        silu = gp * sg
        dsilu = sg * (1.0 + gp * (1.0 - sg))
        dyv = dy_all * silu
        return (dyv[:, :BRANCH_W], dyv[:, BRANCH_W:2 * BRANCH_W], dyv[:, 2 * BRANCH_W:], dy_all * y_all * dsilu)

    dy_ssm, do_mla, do_sb, dseg_c = _ew(
        gate_bwd, dys + [sv["y_ssm"], sv["o_mla"], sv["o_sb"], sv["seg_c"]],
        [(BRANCH_W, F32), (BRANCH_W, F32), (BRANCH_W, F32), (3 * BRANCH_W, BF16)], name=tg("gate_bwd"))

    dq_sb, dk_sb, dv_sb = _sb_bwd(sv["seg_b"], do_sb, sv["tot"], name=tg("sb_bwd"))
    dseg_b = jnp.concatenate([dq_sb, dk_sb, dv_sb], axis=1).astype(BF16)

    dqn_, dqr_, dkn_, dv_, dkr_ = _mla_bwd(sv["q_b"], sv["kv_b"], sv["kr_b"], sv["o_mla"], do_mla, sv["lse"],
                                           name=tg("mla_bwd"))

    def unrope_q(dn, dr, c_, p_, m_):
        outs = [dn]
        for hh in range(N_HEADS):
            outs.append(_rope_group_t(dr[:, 128 * hh:128 * hh + 128], c_, p_, m_))
        return jnp.concatenate(outs, axis=1)

    dq_raw = _ew(unrope_q, [dqn_, dqr_, ct, sp_, sm_], [(2048, BF16)], name=tg("unrope_q"))

    def unrope_k(dr, c_, p_, m_):
        tot_ = dr[:, :128]
        for hh in range(1, N_HEADS):
            tot_ = tot_ + dr[:, 128 * hh:128 * hh + 128]
        return _rope_group_t(tot_, c_, p_, m_)

    dseg_e = _ew(unrope_k, [dkr_, ct, sp_, sm_], [(128, BF16)], name=tg("unrope_k"))
    dkv_raw = jnp.concatenate([dkn_, dv_], axis=1).astype(BF16)
    dw_uq = _mm(sv["qn"], dq_raw, ta=True, out_dtypes=(BF16,), name=tg("d_uq"))
    dw_ukv = _mm(sv["kvn"], dkv_raw, ta=True, out_dtypes=(BF16,), name=tg("d_ukv"))
    dqn = _mm(dq_raw, w["w_uq"], tb=True, name=tg("uq_t"))
    dkvn = _mm(dkv_raw, w["w_ukv"], tb=True, name=tg("ukv_t"))
    dq_lat, gs["mla_g_q"] = _rmsnorm_bwd(sv["q_lat"], sm["mla_g_q"], dqn, None, name=tg("q_norm_bwd"))
    dkv_lat, gs["mla_g_kv"] = _rmsnorm_bwd(sv["kv_lat"], sm["mla_g_kv"], dkvn, None, name=tg("kv_norm_bwd"))
    dwq = dw_uq[:, :1024].reshape(512, N_HEADS, MLA_NOPE)
    dwr = dw_uq[:, 1024:].reshape(512, N_HEADS, 128)[:, :, :MLA_ROPE]
    gw["mla_w_uq"] = jnp.concatenate([dwq, dwr], axis=2).reshape(512, N_HEADS * (MLA_NOPE + MLA_ROPE))
    dwk = dw_ukv[:, :1024].reshape(512, N_HEADS, HEAD_D)
    dwv = dw_ukv[:, 1024:].reshape(512, N_HEADS, HEAD_D)
    gw["mla_w_ukv"] = jnp.concatenate([dwk, dwv], axis=2).reshape(512, 2 * N_HEADS * HEAD_D)

    def glu_bwd(d, ygv, zv):
        sg = _sigmoid(zv)
        return d * ygv * sg * (1.0 - sg), d * sg

    dz, dyg0 = _ew(glu_bwd, [dy_ssm, sv["yg"], sv["z_glu"]], [(BRANCH_W, BF16), (BRANCH_W, F32)], name=tg("glu_bwd"))
    gw["ssm_w_glu"] = _mm(sv["yg"], dz, ta=True, out_dtypes=(BF16,), name=tg("d_glu"))
    dy_pre = _mm(dz, w["w_glu"], tb=True, name=tg("glu_t"), extras=[(dyg0, 0), (sv["y_pre"], 0)],
                 epilogue=lambda acc, d0, yp: ((acc + d0) * _gelu_grad(yp),))
    du, dbb_re, dbb_im, dc_re, dc_im, da_re, da_im, dd = _s5_bwd(
        sv["u"], dy_pre, sv["hc_re"], sv["hc_im"], ssm["bb_re"], ssm["bb_im"], ssm["bbt_re"], ssm["bbt_im"],
        ssm["c_re"], ssm["c_im"], ssm["ct_re"], ssm["ct_im"], ssm["dsk"], tabs, name=tg("s5_bwd"))
    gs["ssm_d"] = dd.reshape(BRANCH_W)
    gs["ssm_c_re"] = jnp.swapaxes(_blockdiag_extract(dc_re, SSM_STATE, SSM_GROUP), 2, 3).reshape(
        SSM_GROUPS, SSM_GROUP, SSM_STATE)
    gs["ssm_c_im"] = jnp.swapaxes(_blockdiag_extract(dc_im, SSM_STATE, SSM_GROUP), 2, 3).reshape(
        SSM_GROUPS, SSM_GROUP, SSM_STATE)
    gs["_dbb_re"] = jnp.swapaxes(_blockdiag_extract(dbb_re, SSM_GROUP, SSM_STATE), 2, 3).reshape(
        SSM_GROUPS, SSM_STATE, SSM_GROUP)
    gs["_dbb_im"] = jnp.swapaxes(_blockdiag_extract(dbb_im, SSM_GROUP, SSM_STATE), 2, 3).reshape(
        SSM_GROUPS, SSM_STATE, SSM_GROUP)
    gs["_da_re"] = da_re.reshape(SSM_GROUPS, SSM_STATE)
    gs["_da_im"] = da_im.reshape(SSM_GROUPS, SSM_STATE)

    dseg_a = jnp.concatenate([du, dq_lat, dkv_lat], axis=1).astype(BF16)
    hb = sv["h"]
    dwa = _mm(hb, dseg_a, ta=True, out_dtypes=(BF16,), name=tg("d_in_a"))
    dwb_ = _mm(hb, dseg_b, ta=True, out_dtypes=(BF16,), name=tg("d_in_b"))
    dwc = _mm(hb, dseg_c, ta=True, out_dtypes=(BF16,), name=tg("d_in_c"))
    dwd = _mm(hb, dseg_d, ta=True, out_dtypes=(BF16,), name=tg("d_in_d"))
    dwe = _mm(hb, dseg_e, ta=True, out_dtypes=(BF16,), name=tg("d_in_e"))
    gw["w_in"] = jnp.concatenate([dwa, dwe[:, :MLA_ROPE], dwb_, dwc, dwd], axis=1)
    add = lambda acc, prev: (acc + prev,)
    dh = _mm(dseg_a, w["wa"], tb=True, name=tg("in_a_t"))
    dh = _mm(dseg_b, w["wb"], tb=True, name=tg("in_b_t"), extras=[(dh, 0)], epilogue=add)
    dh = _mm(dseg_c, w["wc"], tb=True, name=tg("in_c_t"), extras=[(dh, 0)], epilogue=add)
    dh = _mm(dseg_d, w["wd"], tb=True, name=tg("in_d_t"), extras=[(dh, 0)], epilogue=add)
    dh = _mm(dseg_e, w["we"], tb=True, name=tg("in_e_t"), extras=[(dh, 0)], epilogue=add)
    dx_in, gs["ln_g"] = _rmsnorm_bwd(sv["x_in"], sm["ln_g"], dh, dx1, name=tg("ln_bwd"))
    return dx_in, gw, gs


def _local_step(x, p, target, fulls, small):
    s = x.shape[0]
    depth = len(fulls)
    rope = _rope_tables(s)
    ws, ssms, tabss, sms, saved = [], [], [], [], []
    xc = x
    for li in range(depth):
        sp = dict(lam_re=small["ssm_lam_re"][li], lam_im=small["ssm_lam_im"][li], log_dt=small["ssm_log_dt"][li],
                  b_re=small["ssm_b_re"][li], b_im=small["ssm_b_im"][li], c_re=small["ssm_c_re"][li],
                  c_im=small["ssm_c_im"][li], d=small["ssm_d"][li])
        w = _prep_weights(fulls[li])
        ssm = _prep_ssm(sp)
        tabs = _s5_tables(sp["lam_re"], sp["lam_im"], sp["log_dt"], min(SSM_CHUNK, s))
        sm = dict(ln_g=small["ln_g"][li], mla_g_q=small["mla_g_q"][li], mla_g_kv=small["mla_g_kv"][li],
                  ple_g=small["ple_g"][li])
        xc, sv = _layer_fwd(xc, p[li], w, ssm, tabs, sm, rope, li)
        ws.append(w), ssms.append(ssm), tabss.append(tabs), sms.append(sm), saved.append(sv)
    dx, dgf, loss = _loss_head(xc, small["final_g"], target, name="loss_head")
    gws, gss = [None] * depth, [None] * depth
    for li in reversed(range(depth)):
        dx, gw, gs = _layer_bwd(dx, saved[li], ws[li], ssms[li], tabss[li], sms[li], rope, li)
        args = (small["ssm_lam_re"][li], small["ssm_lam_im"][li], small["ssm_log_dt"][li], small["ssm_b_re"][li],
                small["ssm_b_im"][li])
        _, vjp = jax.vjp(_s5_disc, *args)
        g_lr, g_li, g_dt, g_br, g_bi = vjp((gs.pop("_da_re"), gs.pop("_da_im"), gs.pop("_dbb_re"), gs.pop("_dbb_im")))
        gs.update(ssm_lam_re=g_lr, ssm_lam_im=g_li, ssm_log_dt=g_dt, ssm_b_re=g_br, ssm_b_im=g_bi)
        gws[li], gss[li] = gw, gs
    gsmall = {nm: jnp.stack([gss[li][nm].reshape(small[nm].shape[1:]) for li in range(depth)])
              for nm in SMALL if nm != "final_g"}
    gsmall["final_g"] = dgf.reshape(-1)
    return loss[0, 0], dx, gws, gsmall


def kernel(x, p, ln_g, w_in, ssm_lam_re, ssm_lam_im, ssm_log_dt, ssm_b_re, ssm_b_im, ssm_c_re, ssm_c_im, ssm_d, ssm_w_glu, mla_g_q, mla_g_kv, mla_w_uq, mla_w_ukv, w_branch, w_out, ple_g, w_ple_gate, w_ple_proj, final_g, loss_target, m_ln_g, m_w_in, m_ssm_lam_re, m_ssm_lam_im, m_ssm_log_dt, m_ssm_b_re, m_ssm_b_im, m_ssm_c_re, m_ssm_c_im, m_ssm_d, m_ssm_w_glu, m_mla_g_q, m_mla_g_kv, m_mla_w_uq, m_mla_w_ukv, m_w_branch, m_w_out, m_ple_g, m_w_ple_gate, m_w_ple_proj, m_final_g, v_ln_g, v_w_in, v_ssm_lam_re, v_ssm_lam_im, v_ssm_log_dt, v_ssm_b_re, v_ssm_b_im, v_ssm_c_re, v_ssm_c_im, v_ssm_d, v_ssm_w_glu, v_mla_g_q, v_mla_g_kv, v_mla_w_uq, v_mla_w_ukv, v_w_branch, v_w_out, v_ple_g, v_w_ple_gate, v_w_ple_proj, v_final_g):
    weights = dict(ln_g=ln_g, w_in=w_in, ssm_lam_re=ssm_lam_re, ssm_lam_im=ssm_lam_im, ssm_log_dt=ssm_log_dt,
                   ssm_b_re=ssm_b_re, ssm_b_im=ssm_b_im, ssm_c_re=ssm_c_re, ssm_c_im=ssm_c_im, ssm_d=ssm_d,
                   ssm_w_glu=ssm_w_glu, mla_g_q=mla_g_q, mla_g_kv=mla_g_kv, mla_w_uq=mla_w_uq, mla_w_ukv=mla_w_ukv,
                   w_branch=w_branch, w_out=w_out, ple_g=ple_g, w_ple_gate=w_ple_gate, w_ple_proj=w_ple_proj,
                   final_g=final_g)
    mom_m = dict(ln_g=m_ln_g, w_in=m_w_in, ssm_lam_re=m_ssm_lam_re, ssm_lam_im=m_ssm_lam_im, ssm_log_dt=m_ssm_log_dt,
                 ssm_b_re=m_ssm_b_re, ssm_b_im=m_ssm_b_im, ssm_c_re=m_ssm_c_re, ssm_c_im=m_ssm_c_im, ssm_d=m_ssm_d,
                 ssm_w_glu=m_ssm_w_glu, mla_g_q=m_mla_g_q, mla_g_kv=m_mla_g_kv, mla_w_uq=m_mla_w_uq,
                 mla_w_ukv=m_mla_w_ukv, w_branch=m_w_branch, w_out=m_w_out, ple_g=m_ple_g, w_ple_gate=m_w_ple_gate,
                 w_ple_proj=m_w_ple_proj, final_g=m_final_g)
    mom_v = dict(ln_g=v_ln_g, w_in=v_w_in, ssm_lam_re=v_ssm_lam_re, ssm_lam_im=v_ssm_lam_im, ssm_log_dt=v_ssm_log_dt,
                 ssm_b_re=v_ssm_b_re, ssm_b_im=v_ssm_b_im, ssm_c_re=v_ssm_c_re, ssm_c_im=v_ssm_c_im, ssm_d=v_ssm_d,
                 ssm_w_glu=v_ssm_w_glu, mla_g_q=v_mla_g_q, mla_g_kv=v_mla_g_kv, mla_w_uq=v_mla_w_uq,
                 mla_w_ukv=v_mla_w_ukv, w_branch=v_w_branch, w_out=v_w_out, ple_g=v_ple_g, w_ple_gate=v_w_ple_gate,
                 w_ple_proj=v_w_ple_proj, final_g=v_final_g)
    depth = ln_g.shape[0]
    fulls = [_gather_layer(weights, li) for li in range(depth)]
    small = {nm: weights[nm] for nm in SMALL}
    loss_local, dx, gws, gsmall = _local_step(x[0], p[:, 0], loss_target[0], fulls, small)
    loss = lax.psum(loss_local, ("x", "y", "c"))

    grads, delta, new_m, new_v = {}, {}, {}, {}
    flat = jnp.concatenate([gsmall[nm].reshape(-1) for nm in SMALL])
    n_small = flat.shape[0]
    got = _exchange([_pack_rows(flat, 8)], all_gather=True, name="gather_small_grads")[0]
    gsum = _sum8(got, name="sum_small_grads").reshape(-1)[:n_small]
    off = 0
    for nm in SMALL:
        cnt = weights[nm].size
        grads[nm] = gsum[off:off + cnt].reshape(weights[nm].shape)
        off += cnt
    pk = lambda d: _pack_rows(jnp.concatenate([d[nm].reshape(-1) for nm in SMALL]), 8)
    d_s, m_s, v_s = _adamw(pk(weights), pk(grads), pk(mom_m), pk(mom_v), name="adamw_small")
    off = 0
    for nm in SMALL:
        cnt = weights[nm].size
        shp = weights[nm].shape
        delta[nm] = d_s.reshape(-1)[off:off + cnt].reshape(shp)
        new_m[nm] = m_s.reshape(-1)[off:off + cnt].reshape(shp)
        new_v[nm] = v_s.reshape(-1)[off:off + cnt].reshape(shp)
        off += cnt

    parts = [_scatter_layer(gws[li]) for li in range(depth)]
    for nm, _, _ in SHARDED:
        grads[nm], delta[nm], new_m[nm], new_v[nm] = _adamw_sharded(
            weights[nm], [parts[li][nm] for li in range(depth)], mom_m[nm], mom_v[nm], name=f"adamw_{nm}")
    return (loss, dx[None], *[grads[nm] for nm in WEIGHT_ORDER], *[delta[nm] for nm in WEIGHT_ORDER],
            *[new_m[nm] for nm in WEIGHT_ORDER], *[new_v[nm] for nm in WEIGHT_ORDER])
```

```python
import functools
import math

import jax
import jax.numpy as jnp
from jax import lax
from jax.experimental import pallas as pl
from jax.experimental.pallas import tpu as pltpu

F32 = jnp.float32
BF16 = jnp.bfloat16

N_DEV = 8
D_MODEL = 2048
BRANCH_W = 1024
N_HEADS = 8
HEAD_D = 128
SSM_GROUPS = 64
SSM_GROUP = 16
SSM_STATE = 64
SSM_GB = 8
SSM_CHUNK = 256
MLA_ROPE = 64
MLA_NOPE = 128
ROPE_THETA = 10000.0
NORM_EPS = 1e-6
DT_MIN = 1e-3
DT_MAX = 1e-1
PACK_COLS = 1024

ADAM_LR = 0.001
ADAM_B1 = 0.9
ADAM_B2 = 0.999
ADAM_EPS = 1e-08
ADAM_WD = 0.01
ADAM_STEP = 10

VMEM_LIMIT = 56 * 2 ** 20
NEG_BIG = -1e30
MM_VMEM_BUDGET = 36 * 2 ** 20

SHARDED = (
    ("w_in", (2048, 14400), 1),
    ("ssm_w_glu", (1024, 1024), 0),
    ("mla_w_uq", (512, 1536), 1),
    ("mla_w_ukv", (512, 2048), 1),
    ("w_branch", (3, 1024, 2048), 2),
    ("w_out", (2048, 2048), 0),
    ("w_ple_gate", (2048, 2048), 0),
    ("w_ple_proj", (256, 2048), 1),
)
SMALL = ("ln_g", "ssm_lam_re", "ssm_lam_im", "ssm_log_dt", "ssm_b_re", "ssm_b_im", "ssm_c_re", "ssm_c_im",
         "ssm_d", "mla_g_q", "mla_g_kv", "ple_g", "final_g")
WEIGHT_ORDER = ("ln_g", "w_in", "ssm_lam_re", "ssm_lam_im", "ssm_log_dt", "ssm_b_re", "ssm_b_im", "ssm_c_re",
                "ssm_c_im", "ssm_d", "ssm_w_glu", "mla_g_q", "mla_g_kv", "mla_w_uq", "mla_w_ukv", "w_branch",
                "w_out", "ple_g", "w_ple_gate", "w_ple_proj", "final_g")


def _cparams(sem=None):
    return pltpu.CompilerParams(dimension_semantics=sem, vmem_limit_bytes=VMEM_LIMIT)


def _pick(n, prefs):
    for t in prefs:
        if n % t == 0:
            return t
    return n


def _dot(a, b):
    return lax.dot_general(a, b, (((1,), (0,)), ((), ())), preferred_element_type=F32)


def _dot_nt(a, b):
    return lax.dot_general(a, b, (((1,), (1,)), ((), ())), preferred_element_type=F32)


def _dot_tn(a, b):
    return lax.dot_general(a, b, (((0,), (0,)), ((), ())), preferred_element_type=F32)


def _sigmoid(x):
    return 1.0 / (1.0 + jnp.exp(-x))


def _mm(a, b, *, name, ta=False, tb=False, extras=(), epilogue=None, out_dtypes=(F32,)):
    m, k = (a.shape[1], a.shape[0]) if ta else a.shape
    n = b.shape[0] if tb else b.shape[1]
    assert (b.shape[1] if tb else b.shape[0]) == k
    n_ex = len(extras)
    n_out = len(out_dtypes)
    tn = _pick(n, (1024, 512, 256, 128))
    tk = _pick(k, (2048, 1024, 512, 256, 128))
    for tm in (1024, 512, 256, 128):
        if m % tm:
            continue
        est = 2 * (tm * tk * a.dtype.itemsize + tk * tn * b.dtype.itemsize) + tm * tn * 4
        est += 2 * tm * tn * (sum(e[0].dtype.itemsize for e in extras) + sum(jnp.dtype(d).itemsize for d in out_dtypes))
        if est <= MM_VMEM_BUDGET:
            break
    nk = k // tk

    def body(*refs):
        a_ref, b_ref = refs[0], refs[1]
        ex_refs = refs[2:2 + n_ex]
        out_refs = refs[2 + n_ex:2 + n_ex + n_out]
        acc_ref = refs[-1]
        kk = pl.program_id(2)

        @pl.when(kk == 0)
        def _():
            acc_ref[...] = jnp.zeros_like(acc_ref)

        av = a_ref[...].astype(BF16)
        bv = b_ref[...].astype(BF16)
        dims = (((0 if ta else 1,), (1 if tb else 0,)), ((), ()))
        acc_ref[...] += lax.dot_general(av, bv, dims, preferred_element_type=F32)

        @pl.when(kk == nk - 1)
        def _():
            acc = acc_ref[...]
            res = epilogue(acc, *[r[...] for r in ex_refs]) if epilogue is not None else (acc,)
            for r, v in zip(out_refs, res):
                r[...] = v.astype(r.dtype)

    a_spec = pl.BlockSpec((tk, tm), lambda i, j, q: (q, i)) if ta else pl.BlockSpec((tm, tk), lambda i, j, q: (i, q))
    b_spec = pl.BlockSpec((tn, tk), lambda i, j, q: (j, q)) if tb else pl.BlockSpec((tk, tn), lambda i, j, q: (q, j))
    ex_specs = []
    for arr, off in extras:
        assert off % tn == 0 and arr.shape[0] == m
        ex_specs.append(pl.BlockSpec((tm, tn), functools.partial(lambda i, j, q, o: (i, j + o), o=off // tn)))
    outs = pl.pallas_call(
        body, name=name,
        grid=(m // tm, n // tn, nk),
        in_specs=[a_spec, b_spec] + ex_specs,
        out_specs=[pl.BlockSpec((tm, tn), lambda i, j, q: (i, j)) for _ in out_dtypes],
        out_shape=[jax.ShapeDtypeStruct((m, n), dt) for dt in out_dtypes],
        scratch_shapes=[pltpu.VMEM((tm, tn), F32)],
        compiler_params=_cparams(("parallel", "parallel", "arbitrary")),
    )(a, b, *[e[0] for e in extras])
    return outs if n_out > 1 else outs[0]


def _row_tile(m, bytes_per_row):
    for t in (1024, 512, 256, 128, 64, 32, 16):
        if m % t == 0 and 2 * t * bytes_per_row <= 20 * 2 ** 20:
            return t
    return 16 if m % 16 == 0 else m


def _ew(fn, ins, outs, *, name):
    m = max(x.shape[0] for x in ins)
    bpr = sum(x.shape[1] * x.dtype.itemsize for x in ins if x.shape[0] == m)
    bpr += sum(no * jnp.dtype(dt).itemsize for no, dt in outs)
    tm = _row_tile(m, bpr)
    n_in = len(ins)

    def body(*refs):
        res = fn(*[r[...] for r in refs[:n_in]])
        if not isinstance(res, (tuple, list)):
            res = (res,)
        for r, v in zip(refs[n_in:], res):
            r[...] = v.astype(r.dtype)

    in_specs = []
    for x in ins:
        if x.shape[0] == m:
            in_specs.append(pl.BlockSpec((tm, x.shape[1]), lambda i: (i, 0)))
        else:
            in_specs.append(pl.BlockSpec(x.shape, lambda i: (0, 0)))
    res = pl.pallas_call(
        body, name=name,
        grid=(m // tm,),
        in_specs=in_specs,
        out_specs=[pl.BlockSpec((tm, no), lambda i: (i, 0)) for no, _ in outs],
        out_shape=[jax.ShapeDtypeStruct((m, no), dt) for no, dt in outs],
        compiler_params=_cparams(("parallel",)),
    )(*ins)
    return res if len(outs) > 1 else res[0]


def _rmsnorm(x, g, *, name, out_dtype=BF16):
    d = x.shape[1]

    def fn(xv, gv):
        r = lax.rsqrt(jnp.mean(xv * xv, axis=-1, keepdims=True) + NORM_EPS)
        return xv * r * gv

    return _ew(fn, [x, g.reshape(1, d)], [(d, out_dtype)], name=name)


def _rmsnorm_bwd(x, g, dy, res, *, name):
    m, d = x.shape
    has_res = res is not None
    tm = _row_tile(m, d * 4 * (4 if has_res else 3))

    def body(*refs):
        x_ref, g_ref, dy_ref = refs[:3]
        res_ref = refs[3] if has_res else None
        dx_ref, dg_ref = refs[-2], refs[-1]

        @pl.when(pl.program_id(0) == 0)
        def _():
            dg_ref[...] = jnp.zeros_like(dg_ref)

        xv = x_ref[...]
        dyv = dy_ref[...].astype(F32)
        r = lax.rsqrt(jnp.mean(xv * xv, axis=-1, keepdims=True) + NORM_EPS)
        xh = xv * r
        dg_ref[...] += jnp.sum(dyv * xh, axis=0, keepdims=True)
        dyg = dyv * g_ref[...]
        dx = r * (dyg - xh * jnp.mean(dyg * xh, axis=-1, keepdims=True))
        if has_res:
            dx = dx + res_ref[...]
        dx_ref[...] = dx

    row = pl.BlockSpec((tm, d), lambda i: (i, 0))
    vec = pl.BlockSpec((1, d), lambda i: (0, 0))
    ins = [x, g.reshape(1, d), dy] + ([res] if has_res else [])
    return pl.pallas_call(
        body, name=name,
        grid=(m // tm,),
        in_specs=[row, vec, row] + ([row] if has_res else []),
        out_specs=[row, vec],
        out_shape=[jax.ShapeDtypeStruct((m, d), F32), jax.ShapeDtypeStruct((1, d), F32)],
        compiler_params=_cparams(("arbitrary",)),
    )(*ins)


def _loss_head(x, g, target, *, name):
    m, d = x.shape
    tm = _row_tile(m, d * 4 * 3)

    def body(x_ref, g_ref, t_ref, dx_ref, dg_ref, loss_ref):
        @pl.when(pl.program_id(0) == 0)
        def _():
            dg_ref[...] = jnp.zeros_like(dg_ref)
            loss_ref[...] = jnp.zeros_like(loss_ref)

        xv = x_ref[...]
        gv = g_ref[...]
        r = lax.rsqrt(jnp.mean(xv * xv, axis=-1, keepdims=True) + NORM_EPS)
        xh = xv * r
        diff = xh * gv - t_ref[...]
        loss_ref[...] += 0.5 * jnp.sum(jnp.mean(diff * diff, axis=-1, keepdims=True), axis=0, keepdims=True)
        dyv = diff * (1.0 / d)
        dg_ref[...] += jnp.sum(dyv * xh, axis=0, keepdims=True)
        dyg = dyv * gv
        dx_ref[...] = r * (dyg - xh * jnp.mean(dyg * xh, axis=-1, keepdims=True))

    row = pl.BlockSpec((tm, d), lambda i: (i, 0))
    vec = pl.BlockSpec((1, d), lambda i: (0, 0))
    return pl.pallas_call(
        body, name=name,
        grid=(m // tm,),
        in_specs=[row, vec, row],
        out_specs=[row, vec, pl.BlockSpec((1, 128), lambda i: (0, 0))],
        out_shape=[jax.ShapeDtypeStruct((m, d), F32), jax.ShapeDtypeStruct((1, d), F32),
                   jax.ShapeDtypeStruct((1, 128), F32)],
        compiler_params=_cparams(("arbitrary",)),
    )(x, g.reshape(1, d), target)


def _rope_tables(seqlen):
    pos = jnp.arange(seqlen, dtype=F32)
    inv_freq = ROPE_THETA ** (-jnp.arange(0, MLA_ROPE, 2, dtype=F32) / MLA_ROPE)
    ang = pos[:, None] * inv_freq[None, :]
    c, s = jnp.cos(ang), jnp.sin(ang)
    z = jnp.zeros_like(c)
    return (jnp.concatenate([c, c, z, z], axis=1), jnp.concatenate([z, s, z, z], axis=1),
            jnp.concatenate([-s, z, z, z], axis=1))


def _rope_group(xg, ct, sp, sm):
    return xg * ct + pltpu.roll(xg, 32, 1) * sp + pltpu.roll(xg, 96, 1) * sm


def _rope_group_t(dg, ct, sp, sm):
    return dg * ct + pltpu.roll(dg * sp, 96, 1) + pltpu.roll(dg * sm, 32, 1)


def _s5_disc(lam_re, lam_im, log_dt, b_re, b_im):
    dt = jnp.exp(log_dt)[:, None]
    mag = jnp.exp(lam_re * dt)
    ab_re = mag * jnp.cos(lam_im * dt)
    ab_im = mag * jnp.sin(lam_im * dt)
    den = lam_re * lam_re + lam_im * lam_im
    nr = ab_re - 1.0
    ni = ab_im
    coef_re = (nr * lam_re + ni * lam_im) / den
    coef_im = (ni * lam_re - nr * lam_im) / den
    bb_re = coef_re[..., None] * b_re - coef_im[..., None] * b_im
    bb_im = coef_re[..., None] * b_im + coef_im[..., None] * b_re
    return ab_re, ab_im, bb_re, bb_im


def _blockdiag(x):
    nb, ng, r, c = x.shape
    eye = jnp.eye(ng, dtype=x.dtype)
    return (x[:, :, :, None, :] * eye[None, :, None, :, None]).reshape(nb, ng * r, ng * c)


def _blockdiag_extract(x, r, c):
    nb = x.shape[0]
    x5 = x.reshape(nb, SSM_GB, r, SSM_GB, c)
    eye = jnp.eye(SSM_GB, dtype=x.dtype)
    return jnp.sum(x5 * eye[None, :, None, :, None], axis=3)


def _tile_scan(h_re, h_im, a_re_ref, a_im_ref, t_chunk, reverse):
    n = h_re.shape[1]
    h_re = h_re.reshape(t_chunk // 8, 8, n)
    h_im = h_im.reshape(t_chunk // 8, 8, n)
    sub = lax.broadcasted_iota(jnp.int32, (t_chunk // 8, 8, n), 1)
    for k in range(3):
        sh = 1 << k
        ar = a_re_ref[pl.ds(k, 1), :]
        ai = a_im_ref[pl.ds(k, 1), :]
        if reverse:
            s_re = pltpu.roll(h_re, 8 - sh, 1)
            s_im = pltpu.roll(h_im, 8 - sh, 1)
            keep = sub < 8 - sh
            ai = -ai
        else:
            s_re = pltpu.roll(h_re, sh, 1)
            s_im = pltpu.roll(h_im, sh, 1)
            keep = sub >= sh
        s_re = jnp.where(keep, s_re, 0.0)
        s_im = jnp.where(keep, s_im, 0.0)
        h_re, h_im = h_re + ar * s_re - ai * s_im, h_im + ar * s_im + ai * s_re
    return h_re.reshape(t_chunk, n), h_im.reshape(t_chunk, n)


def _tile_chain(h_re, h_im, w_re, w_im, c_re, c_im, t_chunk, reverse):
    nt = t_chunk // 8
    out_re, out_im = [None] * nt, [None] * nt
    edge = 0 if reverse else 7
    for j in (reversed(range(nt)) if reverse else range(nt)):
        tr = h_re[8 * j:8 * j + 8, :] + w_re * c_re - w_im * c_im
        ti = h_im[8 * j:8 * j + 8, :] + w_re * c_im + w_im * c_re
        c_re = tr[edge:edge + 1, :]
        c_im = ti[edge:edge + 1, :]
        out_re[j], out_im[j] = tr, ti
    return jnp.concatenate(out_re, axis=0), jnp.concatenate(out_im, axis=0), c_re, c_im


def _s5_tables(lam_re, lam_im, log_dt):
    dt = jnp.exp(log_dt)[:, None]
    lr = (lam_re * dt).reshape(1, -1)
    li = (lam_im * dt).reshape(1, -1)

    def powers(n):
        mag = jnp.exp(n * lr)
        return mag * jnp.cos(n * li), mag * jnp.sin(n * li)

    p_re, p_im = powers(jnp.arange(1, 9, dtype=F32)[:, None])
    pad = jnp.zeros((5, lr.shape[1]), F32)
    a_re = jnp.concatenate([p_re[0:1], p_re[1:2], p_re[3:4], pad], axis=0)
    a_im = jnp.concatenate([p_im[0:1], p_im[1:2], p_im[3:4], pad], axis=0)
    return p_re, p_im, p_re[::-1], p_im[::-1], a_re, a_im


def _s5_fwd(u, bb_re, bb_im, c_re, c_im, dsk, tabs, *, name):
    s = u.shape[0]
    t = min(SSM_CHUNK, s)
    nt = s // t
    ns = SSM_GB * SSM_STATE
    p_re, p_im, _, _, a_re, a_im = tabs

    def body(u_ref, bbr_ref, bbi_ref, cr_ref, ci_ref, d_ref, pr_ref, pi_ref, ar_ref, ai_ref,
             y_ref, hcr_ref, hci_ref, car_re, car_im):
        @pl.when(pl.program_id(1) == 0)
        def _():
            car_re[...] = jnp.zeros_like(car_re)
            car_im[...] = jnp.zeros_like(car_im)

        cin_re = car_re[...]
        cin_im = car_im[...]
        hcr_ref[0] = cin_re
        hci_ref[0] = cin_im
        uv = u_ref[...]
        ub = uv.astype(BF16)
        h_re = _dot(ub, bbr_ref[0])
        h_im = _dot(ub, bbi_ref[0])
        rows = lax.broadcasted_iota(jnp.int32, (t, ns), 0)
        h_re, h_im = _tile_scan(h_re, h_im, ar_ref, ai_ref, t, False)
        h_re, h_im, c_re_, c_im_ = _tile_chain(h_re, h_im, pr_ref[...], pi_ref[...], cin_re, cin_im, t, False)
        car_re[...] = c_re_
        car_im[...] = c_im_
        y = _dot(h_re.astype(BF16), cr_ref[0]) - _dot(h_im.astype(BF16), ci_ref[0]) + d_ref[...] * uv
        y_ref[...] = y

    blk = lambda r, c: pl.BlockSpec((1, r, c), lambda g, i: (g, 0, 0))
    tab = pl.BlockSpec((8, ns), lambda g, i: (0, g))
    return pl.pallas_call(
        body, name=name,
        grid=(SSM_GB, nt),
        in_specs=[pl.BlockSpec((t, 128), lambda g, i: (i, g)), blk(128, ns), blk(128, ns), blk(ns, 128), blk(ns, 128),
                  pl.BlockSpec((1, 128), lambda g, i: (0, g)), tab, tab, tab, tab],
        out_specs=[pl.BlockSpec((t, 128), lambda g, i: (i, g)),
                   pl.BlockSpec((1, 1, ns), lambda g, i: (i, 0, g)), pl.BlockSpec((1, 1, ns), lambda g, i: (i, 0, g))],
        out_shape=[jax.ShapeDtypeStruct((s, BRANCH_W), F32),
                   jax.ShapeDtypeStruct((nt, 1, SSM_GB * ns), F32), jax.ShapeDtypeStruct((nt, 1, SSM_GB * ns), F32)],
        scratch_shapes=[pltpu.VMEM((1, ns), F32), pltpu.VMEM((1, ns), F32)],
        compiler_params=_cparams(("parallel", "arbitrary")),
    )(u, bb_re, bb_im, c_re, c_im, dsk, p_re, p_im, a_re, a_im)


def _s5_bwd(u, dy, hc_re, hc_im, bb_re, bb_im, bbt_re, bbt_im, c_re, c_im, ct_re, ct_im, dsk, tabs, *, name):
    s = u.shape[0]
    t = min(SSM_CHUNK, s)
    nt = s // t
    ns = SSM_GB * SSM_STATE
    p_re, p_im, q_re, q_im, a_re, a_im = tabs

    def body(u_ref, dy_ref, hcr_ref, hci_ref, bbr_ref, bbi_ref, bbtr_ref, bbti_ref, cr_ref, ci_ref, ctr_ref, cti_ref,
             d_ref, pr_ref, pi_ref, qr_ref, qi_ref, ar_ref, ai_ref,
             du_ref, dbbr_ref, dbbi_ref, dcr_ref, dci_ref, dar_ref, dai_ref, dd_ref, lam_re_c, lam_im_c):
        @pl.when(pl.program_id(1) == 0)
        def _():
            lam_re_c[...] = jnp.zeros_like(lam_re_c)
            lam_im_c[...] = jnp.zeros_like(lam_im_c)
            dbbr_ref[...] = jnp.zeros_like(dbbr_ref)
            dbbi_ref[...] = jnp.zeros_like(dbbi_ref)
            dcr_ref[...] = jnp.zeros_like(dcr_ref)
            dci_ref[...] = jnp.zeros_like(dci_ref)
            dar_ref[...] = jnp.zeros_like(dar_ref)
            dai_ref[...] = jnp.zeros_like(dai_ref)
            dd_ref[...] = jnp.zeros_like(dd_ref)

        cin_re = hcr_ref[0]
        cin_im = hci_ref[0]
        uv = u_ref[...]
        ub = uv.astype(BF16)
        dyv = dy_ref[...]
        dyb = dyv.astype(BF16)
        rows = lax.broadcasted_iota(jnp.int32, (t, ns), 0)
        h_re = _dot(ub, bbr_ref[0])
        h_im = _dot(ub, bbi_ref[0])
        h_re, h_im = _tile_scan(h_re, h_im, ar_ref, ai_ref, t, False)
        h_re, h_im, _, _ = _tile_chain(h_re, h_im, pr_ref[...], pi_ref[...], cin_re, cin_im, t, False)
        dcr_ref[0] += _dot_tn(h_re.astype(BF16), dyb)
        dci_ref[0] -= _dot_tn(h_im.astype(BF16), dyb)
        first = rows == 0
        hp_re = jnp.where(first, cin_re, pltpu.roll(h_re, 1, 0))
        hp_im = jnp.where(first, cin_im, pltpu.roll(h_im, 1, 0))
        l_re = _dot(dyb, ctr_ref[0])
        l_im = -_dot(dyb, cti_ref[0])
        l_re, l_im = _tile_scan(l_re, l_im, ar_ref, ai_ref, t, True)
        l_re, l_im, n_re, n_im = _tile_chain(l_re, l_im, qr_ref[...], -qi_ref[...], lam_re_c[...], lam_im_c[...],
                                             t, True)
        lam_re_c[...] = n_re
        lam_im_c[...] = n_im
        lrb = l_re.astype(BF16)
        lib = l_im.astype(BF16)
        du_ref[...] = _dot(lrb, bbtr_ref[0]) + _dot(lib, bbti_ref[0]) + d_ref[...] * dyv
        dbbr_ref[0] += _dot_tn(ub, lrb)
        dbbi_ref[0] += _dot_tn(ub, lib)
        dar_ref[0] += jnp.sum(l_re * hp_re + l_im * hp_im, axis=0, keepdims=True)
        dai_ref[0] += jnp.sum(l_im * hp_re - l_re * hp_im, axis=0, keepdims=True)
        dd_ref[0] += jnp.sum(dyv * uv, axis=0, keepdims=True)

    rev = lambda g, i: (nt - 1 - i, g)
    blk = lambda r, c: pl.BlockSpec((1, r, c), lambda g, i: (g, 0, 0))
    tab = pl.BlockSpec((8, ns), lambda g, i: (0, g))
    car = pl.BlockSpec((1, 1, ns), lambda g, i: (nt - 1 - i, 0, g))
    return pl.pallas_call(
        body, name=name,
        grid=(SSM_GB, nt),
        in_specs=[pl.BlockSpec((t, 128), rev), pl.BlockSpec((t, 128), rev), car, car,
                  blk(128, ns), blk(128, ns), blk(ns, 128), blk(ns, 128), blk(ns, 128), blk(ns, 128),
                  blk(128, ns), blk(128, ns), pl.BlockSpec((1, 128), lambda g, i: (0, g)),
                  tab, tab, tab, tab, tab, tab],
        out_specs=[pl.BlockSpec((t, 128), rev), blk(128, ns), blk(128, ns), blk(ns, 128), blk(ns, 128),
                   blk(1, ns), blk(1, ns), blk(1, 128)],
        out_shape=[jax.ShapeDtypeStruct((s, BRANCH_W), F32),
                   jax.ShapeDtypeStruct((SSM_GB, 128, ns), F32), jax.ShapeDtypeStruct((SSM_GB, 128, ns), F32),
                   jax.ShapeDtypeStruct((SSM_GB, ns, 128), F32), jax.ShapeDtypeStruct((SSM_GB, ns, 128), F32),
                   jax.ShapeDtypeStruct((SSM_GB, 1, ns), F32), jax.ShapeDtypeStruct((SSM_GB, 1, ns), F32),
                   jax.ShapeDtypeStruct((SSM_GB, 1, 128), F32)],
        scratch_shapes=[pltpu.VMEM((1, ns), F32), pltpu.VMEM((1, ns), F32)],
        compiler_params=_cparams(("parallel", "arbitrary")),
    )(u, dy, hc_re, hc_im, bb_re, bb_im, bbt_re, bbt_im, c_re, c_im, ct_re, ct_im, dsk,
      p_re, p_im, q_re, q_im, a_re, a_im)


ATT_BQ = 512
ATT_BK = 512


def _attn_blocks(s):
    return min(ATT_BQ, s), min(ATT_BK, s)


def _rel_index(bq, bk):
    return lax.broadcasted_iota(jnp.int32, (bq, bk), 0) - lax.broadcasted_iota(jnp.int32, (bq, bk), 1)


def _mla_fwd(q, kv, kr, *, name):
    s = q.shape[0]
    bq, bk = _attn_blocks(s)
    r = bq // bk
    scale = float((MLA_NOPE + MLA_ROPE) ** -0.5)

    def body(qn_ref, qr_ref, kn_ref, v_ref, kr_ref, o_ref, lse_ref):
        qi = pl.program_id(1)
        qn = qn_ref[...]
        qr = qr_ref[...]
        nfull = qi * r
        rel = _rel_index(bq, bk)

        def step(j, carry, d):
            m, l, acc = carry
            ks = pl.ds(pl.multiple_of(j * bk, bk), bk)
            sc = (_dot_nt(qn, kn_ref[ks, :]) + _dot_nt(qr, kr_ref[ks, :])) * scale
            if d is not None:
                sc = jnp.where(rel >= d * bk, sc, NEG_BIG)
            m_new = jnp.maximum(m, jnp.max(sc, axis=1, keepdims=True))
            alpha = jnp.exp(m - m_new)
            p = jnp.exp(sc - m_new)
            l = alpha * l + jnp.sum(p, axis=1, keepdims=True)
            acc = alpha * acc + _dot(p.astype(BF16), v_ref[ks, :])
            return m_new, l, acc

        carry = (jnp.full((bq, 1), NEG_BIG, F32), jnp.zeros((bq, 1), F32), jnp.zeros((bq, HEAD_D), F32))
        carry = lax.fori_loop(0, nfull, lambda j, c: step(j, c, None), carry)
        for d in range(r):
            carry = step(nfull + d, carry, d)
        m, l, acc = carry
        o_ref[...] = acc / l
        lse_ref[0] = m + jnp.log(l)

    res = pl.BlockSpec((s, HEAD_D), lambda h, i: (0, h))
    return pl.pallas_call(
        body, name=name,
        grid=(N_HEADS, s // bq),
        in_specs=[pl.BlockSpec((bq, HEAD_D), lambda h, i: (i, h)),
                  pl.BlockSpec((bq, HEAD_D), lambda h, i: (i, N_HEADS + h)),
                  res, pl.BlockSpec((s, HEAD_D), lambda h, i: (0, N_HEADS + h)),
                  pl.BlockSpec((s, HEAD_D), lambda h, i: (0, 0))],
        out_specs=[pl.BlockSpec((bq, HEAD_D), lambda h, i: (i, h)), pl.BlockSpec((1, bq, 1), lambda h, i: (h, i, 0))],
        out_shape=[jax.ShapeDtypeStruct((s, N_HEADS * HEAD_D), F32), jax.ShapeDtypeStruct((N_HEADS, s, 1), F32)],
        compiler_params=_cparams(("parallel", "arbitrary")),
    )(q, q, kv, kv, kr)


def _mla_bwd(q, kv, kr, o, do, lse, *, name):
    s = q.shape[0]
    bq, bk = _attn_blocks(s)
    r = bq // bk
    scale = float((MLA_NOPE + MLA_ROPE) ** -0.5)

    def body(qn_ref, qr_ref, kn_ref, v_ref, kr_ref, o_ref, do_ref, lse_ref,
             dqn_ref, dqr_ref, dkn_ref, dv_ref, dkr_ref):
        qi = pl.program_id(1)

        @pl.when(qi == 0)
        def _():
            dkn_ref[...] = jnp.zeros_like(dkn_ref)
            dv_ref[...] = jnp.zeros_like(dv_ref)
            dkr_ref[...] = jnp.zeros_like(dkr_ref)

        qn = qn_ref[...]
        qr = qr_ref[...]
        dov = do_ref[...]
        dob = dov.astype(BF16)
        delta = jnp.sum(dov * o_ref[...], axis=1, keepdims=True)
        lse = lse_ref[0]
        nfull = qi * r
        rel = _rel_index(bq, bk)

        def step(j, carry, d):
            dqn, dqr = carry
            ks = pl.ds(pl.multiple_of(j * bk, bk), bk)
            kn = kn_ref[ks, :]
            krv = kr_ref[ks, :]
            sc = (_dot_nt(qn, kn) + _dot_nt(qr, krv)) * scale
            p = jnp.exp(sc - lse)
            if d is not None:
                p = jnp.where(rel >= d * bk, p, 0.0)
            dp = _dot_nt(dob, v_ref[ks, :])
            ds = (p * (dp - delta) * scale).astype(BF16)
            pb = p.astype(BF16)
            dkn_ref[ks, :] += _dot_tn(ds, qn)
            dkr_ref[ks, :] += _dot_tn(ds, qr)
            dv_ref[ks, :] += _dot_tn(pb, dob)
            return dqn + _dot(ds, kn), dqr + _dot(ds, krv)

        zero = jnp.zeros((bq, HEAD_D), F32)
        carry = lax.fori_loop(0, nfull, lambda j, c: step(j, c, None), (zero, zero))
        for d in range(r):
            carry = step(nfull + d, carry, d)
        dqn_ref[...] = carry[0]
        dqr_ref[...] = carry[1]

    tile = lambda off: pl.BlockSpec((bq, HEAD_D), functools.partial(lambda h, i, o: (i, h + o), o=off))
    res = lambda off: pl.BlockSpec((s, HEAD_D), functools.partial(lambda h, i, o: (0, h + o), o=off))
    wide = jax.ShapeDtypeStruct((s, N_HEADS * HEAD_D), F32)
    return pl.pallas_call(
        body, name=name,
        grid=(N_HEADS, s // bq),
        in_specs=[tile(0), tile(N_HEADS), res(0), res(N_HEADS), pl.BlockSpec((s, HEAD_D), lambda h, i: (0, 0)),
                  tile(0), tile(0), pl.BlockSpec((1, bq, 1), lambda h, i: (h, i, 0))],
        out_specs=[tile(0), tile(0), res(0), res(0), res(0)],
        out_shape=[wide] * 5,
        compiler_params=_cparams(("parallel", "arbitrary")),
    )(q, q, kv, kv, kr, o, do, lse)


def _split_dot(x, tri):
    hi = x.astype(BF16)
    lo = (x - hi.astype(F32)).astype(BF16)
    return _dot(hi, tri) + _dot(lo, tri)


def _log_sigmoid(z):
    return jnp.minimum(z, 0.0) - jnp.log(1.0 + jnp.exp(-jnp.abs(z)))


def _sb_fwd(qkv, *, name):
    s = qkv.shape[0]
    bq, bk = _attn_blocks(s)
    r = bq // bk
    scale = float(HEAD_D ** -0.5)

    def body(q_ref, k_ref, v_ref, o_ref, tot_ref):
        qi = pl.program_id(1)
        qv = q_ref[...]
        nfull = qi * r
        rel = _rel_index(bq, bk)
        after = (lax.broadcasted_iota(jnp.int32, (bk, bk), 0) > lax.broadcasted_iota(jnp.int32, (bk, bk), 1)
                 ).astype(BF16)

        def step(j, carry, d):
            car, acc = carry
            ks = pl.ds(pl.multiple_of(j * bk, bk), bk)
            z = _dot_nt(qv, k_ref[ks, :]) * scale
            lb = _log_sigmoid(z)
            lr = lb - z
            if d is not None:
                mask = rel > d * bk
                lr = jnp.where(mask, lr, 0.0)
            w = jnp.exp(lb + _split_dot(lr, after) + car)
            if d is not None:
                w = jnp.where(mask, w, 0.0)
            acc = acc + _dot(w.astype(BF16), v_ref[ks, :])
            return car + jnp.sum(lr, axis=1, keepdims=True), acc

        carry = (jnp.zeros((bq, 1), F32), jnp.zeros((bq, HEAD_D), F32))
        for d in reversed(range(r)):
            carry = step(nfull + d, carry, d)
        car, acc = lax.fori_loop(0, nfull, lambda i, c: step(nfull - 1 - i, c, None), carry)
        o_ref[...] = acc
        tot_ref[0] = car

    res = lambda off: pl.BlockSpec((s, HEAD_D), functools.partial(lambda h, i, o: (0, h + o), o=off))
    return pl.pallas_call(
        body, name=name,
        grid=(N_HEADS, s // bq),
        in_specs=[pl.BlockSpec((bq, HEAD_D), lambda h, i: (i, h)), res(N_HEADS), res(2 * N_HEADS)],
        out_specs=[pl.BlockSpec((bq, HEAD_D), lambda h, i: (i, h)), pl.BlockSpec((1, bq, 1), lambda h, i: (h, i, 0))],
        out_shape=[jax.ShapeDtypeStruct((s, N_HEADS * HEAD_D), F32), jax.ShapeDtypeStruct((N_HEADS, s, 1), F32)],
        compiler_params=_cparams(("parallel", "arbitrary")),
    )(qkv, qkv, qkv)


def _sb_bwd(qkv, do, tot, *, name):
    s = qkv.shape[0]
    bq, bk = _attn_blocks(s)
    r = bq // bk
    scale = float(HEAD_D ** -0.5)

    def body(q_ref, k_ref, v_ref, do_ref, tot_ref, dq_ref, dk_ref, dv_ref):
        qi = pl.program_id(1)

        @pl.when(qi == 0)
        def _():
            dk_ref[...] = jnp.zeros_like(dk_ref)
            dv_ref[...] = jnp.zeros_like(dv_ref)

        qv = q_ref[...]
        dob = do_ref[...].astype(BF16)
        nfull = qi * r
        rel = _rel_index(bq, bk)
        r0 = lax.broadcasted_iota(jnp.int32, (bk, bk), 0)
        c0 = lax.broadcasted_iota(jnp.int32, (bk, bk), 1)
        upto = (r0 <= c0).astype(BF16)
        before = (r0 < c0).astype(BF16)

        def step(j, carry, d):
            rest, gsum, dq = carry
            ks = pl.ds(pl.multiple_of(j * bk, bk), bk)
            kv_ = k_ref[ks, :]
            z = _dot_nt(qv, kv_) * scale
            lb = _log_sigmoid(z)
            lr = lb - z
            if d is not None:
                mask = rel > d * bk
                lr = jnp.where(mask, lr, 0.0)
            w = jnp.exp(lb + rest - _split_dot(lr, upto))
            if d is not None:
                w = jnp.where(mask, w, 0.0)
            g = _dot_nt(dob, v_ref[ks, :]) * w
            dlr = gsum + _dot(g.astype(BF16), before)
            dz = (g * jnp.exp(lr) - dlr * jnp.exp(lb)) * scale
            if d is not None:
                dz = jnp.where(mask, dz, 0.0)
            dzb = dz.astype(BF16)
            dk_ref[ks, :] += _dot_tn(dzb, qv)
            dv_ref[ks, :] += _dot_tn(w.astype(BF16), dob)
            return (rest - jnp.sum(lr, axis=1, keepdims=True), gsum + jnp.sum(g, axis=1, keepdims=True),
                    dq + _dot(dzb, kv_))

        carry = (tot_ref[0], jnp.zeros((bq, 1), F32), jnp.zeros((bq, HEAD_D), F32))
        carry = lax.fori_loop(0, nfull, lambda j, c: step(j, c, None), carry)
        for d in range(r):
            carry = step(nfull + d, carry, d)
        dq_ref[...] = carry[2]

    tile = pl.BlockSpec((bq, HEAD_D), lambda h, i: (i, h))
    res = lambda off: pl.BlockSpec((s, HEAD_D), functools.partial(lambda h, i, o: (0, h + o), o=off))
    wide = jax.ShapeDtypeStruct((s, N_HEADS * HEAD_D), F32)
    return pl.pallas_call(
        body, name=name,
        grid=(N_HEADS, s // bq),
        in_specs=[tile, res(N_HEADS), res(2 * N_HEADS), tile, pl.BlockSpec((1, bq, 1), lambda h, i: (h, i, 0))],
        out_specs=[tile, res(0), res(0)],
        out_shape=[wide] * 3,
        compiler_params=_cparams(("parallel", "arbitrary")),
    )(qkv, qkv, qkv, do, tot)


def _exchange(bufs, *, all_gather, name):
    nb = len(bufs)
    shapes = [((N_DEV,) + b.shape) if all_gather else b.shape for b in bufs]

    def body(*refs):
        srcs, outs = refs[:nb], refs[nb:2 * nb]
        send_sems, recv_sems, local_sems = refs[2 * nb:]
        x, y, c = lax.axis_index("x"), lax.axis_index("y"), lax.axis_index("c")
        me = 4 * x + 2 * y + c

        def src_for(b, idx):
            return srcs[b] if all_gather else srcs[b].at[idx]

        def copy(slot, b, dev, src_idx, dst_idx):
            return pltpu.make_async_remote_copy(
                src_ref=src_for(b, src_idx), dst_ref=outs[b].at[dst_idx], send_sem=send_sems.at[slot * nb + b],
                recv_sem=recv_sems.at[slot * nb + b], device_id=dev, device_id_type=pl.DeviceIdType.MESH)

        owns = [pltpu.make_async_copy(src_for(b, me), outs[b].at[me], local_sems.at[b]) for b in range(nb)]
        for own in owns:
            own.start()
        peers = []
        for k in range(1, N_DEV):
            px = (1 - x) if (k >> 2) & 1 else x
            py = (1 - y) if (k >> 1) & 1 else y
            pc = (1 - c) if k & 1 else c
            peers.append((k - 1, (px, py, pc), 4 * px + 2 * py + pc))
        sends = [copy(slot, b, dev, idx, me) for slot, dev, idx in peers for b in range(nb)]
        for cp in sends:
            cp.start()
        for slot, dev, idx in peers:
            for b in range(nb):
                copy(slot, b, dev, idx, idx).wait_recv()
        for cp in sends:
            cp.wait_send()
        for own in owns:
            own.wait()

    any_spec = pl.BlockSpec(memory_space=pl.ANY)
    return pl.pallas_call(
        body, name=name,
        in_specs=[any_spec] * nb,
        out_specs=[any_spec] * nb,
        out_shape=[jax.ShapeDtypeStruct(sh, b.dtype) for sh, b in zip(shapes, bufs)],
        scratch_shapes=[pltpu.SemaphoreType.DMA(((N_DEV - 1) * nb,)), pltpu.SemaphoreType.DMA(((N_DEV - 1) * nb,)),
                        pltpu.SemaphoreType.DMA((nb,))],
    )(*bufs)


_HBM = pl.BlockSpec(memory_space=pltpu.HBM)
_SEM = pl.BlockSpec(memory_space=pltpu.SEMAPHORE)
_EFFECT = pltpu.SideEffectType.DATAFLOW_SIDE_EFFECTING


def _peers():
    x, y, c = lax.axis_index("x"), lax.axis_index("y"), lax.axis_index("c")
    peers = []
    for k in range(1, N_DEV):
        px = (1 - x) if (k >> 2) & 1 else x
        py = (1 - y) if (k >> 1) & 1 else y
        pc = (1 - c) if k & 1 else c
        peers.append((k - 1, (px, py, pc), 4 * px + 2 * py + pc))
    return 4 * x + 2 * y + c, peers


def _exchange_start(bufs, *, all_gather, name):
    nb = len(bufs)
    shapes = [((N_DEV,) + b.shape) if all_gather else b.shape for b in bufs]
    ncp = (N_DEV - 1) * nb

    def body(*refs):
        srcs, lands = refs[:nb], refs[nb:2 * nb]
        send_sems, recv_sems = refs[2 * nb], refs[2 * nb + 1]
        token = refs[-1]
        me, peers = _peers()
        for slot, dev, idx in peers:
            for b in range(nb):
                pltpu.make_async_remote_copy(
                    src_ref=srcs[b] if all_gather else srcs[b].at[idx], dst_ref=lands[b].at[me],
                    send_sem=send_sems.at[slot * nb + b], recv_sem=recv_sems.at[slot * nb + b],
                    device_id=dev, device_id_type=pl.DeviceIdType.MESH).start()
        token[...] = jnp.zeros_like(token)

    ins = [pltpu.with_memory_space_constraint(b, pltpu.HBM) for b in bufs]
    ins += [pltpu.with_memory_space_constraint(lax.empty(sh, b.dtype), pltpu.HBM) for sh, b in zip(shapes, bufs)]
    outs = pl.pallas_call(
        body, name=name,
        out_shape=(pltpu.SemaphoreType.DMA((ncp,)), pltpu.SemaphoreType.DMA((ncp,)),
                   *[pltpu.HBM(b.shape, b.dtype) for b in bufs], *[pltpu.HBM(sh, b.dtype) for sh, b in zip(shapes, bufs)],
                   jax.ShapeDtypeStruct((8, 128), F32)),
        in_specs=[_HBM] * (2 * nb),
        out_specs=(_SEM, _SEM, *([_HBM] * (2 * nb)), pl.BlockSpec(memory_space=pltpu.VMEM)),
        input_output_aliases={i: 2 + i for i in range(2 * nb)},
        compiler_params=pltpu.CompilerParams(has_side_effects=_EFFECT),
    )(*ins)
    return outs[:-1], outs[-1]


def _exchange_finish(started, after, *, all_gather, name):
    nb = (len(started) - 2) // 2
    send_sems, recv_sems = started[0], started[1]
    thru = started[2:]

    def body(*refs):
        srcs, lands = refs[:nb], refs[nb:2 * nb]
        ssem, rsem = refs[2 * nb], refs[2 * nb + 1]
        me, peers = _peers()
        for slot, dev, idx in peers:
            for b in range(nb):
                cp = pltpu.make_async_remote_copy(
                    src_ref=srcs[b] if all_gather else srcs[b].at[idx], dst_ref=lands[b].at[idx],
                    send_sem=ssem.at[slot * nb + b], recv_sem=rsem.at[slot * nb + b],
                    device_id=dev, device_id_type=pl.DeviceIdType.MESH)
                cp.wait_send()
                cp.wait_recv()

    done = pl.pallas_call(
        body, name=name,
        out_shape=tuple(pltpu.HBM(t.shape, t.dtype) for t in thru),
        in_specs=[_HBM] * (2 * nb) + [_SEM, _SEM, pl.BlockSpec(memory_space=pl.ANY)],
        out_specs=[_HBM] * (2 * nb),
        input_output_aliases={i: i for i in range(2 * nb)},
        compiler_params=pltpu.CompilerParams(has_side_effects=_EFFECT),
    )(*thru, send_sems, recv_sems, after)
    srcs, lands = done[:nb], done[nb:]

    def own_body(*refs):
        s_refs, o_refs, sems = refs[:nb], refs[2 * nb:3 * nb], refs[-1]
        me, _ = _peers()
        cps = [pltpu.make_async_copy(s_refs[b] if all_gather else s_refs[b].at[me], o_refs[b].at[me], sems.at[b])
               for b in range(nb)]
        for cp in cps:
            cp.start()
        for cp in cps:
            cp.wait()

    any_spec = pl.BlockSpec(memory_space=pl.ANY)
    return pl.pallas_call(
        own_body, name=name + "_own",
        in_specs=[any_spec] * (2 * nb), out_specs=[any_spec] * nb,
        out_shape=[jax.ShapeDtypeStruct(l.shape, l.dtype) for l in lands],
        input_output_aliases={nb + b: b for b in range(nb)},
        scratch_shapes=[pltpu.SemaphoreType.DMA((nb,))],
    )(*srcs, *lands)


def _sum8(buf, *, name):
    _, r, c = buf.shape
    tr = _pick(r, (256, 128, 64, 32, 16, 8))

    def body(b_ref, o_ref):
        acc = b_ref[0].astype(F32)
        for q in range(1, N_DEV):
            acc = acc + b_ref[q].astype(F32)
        o_ref[...] = acc

    return pl.pallas_call(
        body, name=name,
        grid=(r // tr,),
        in_specs=[pl.BlockSpec((N_DEV, tr, c), lambda i: (0, i, 0))],
        out_specs=pl.BlockSpec((tr, c), lambda i: (i, 0)),
        out_shape=jax.ShapeDtypeStruct((r, c), F32),
        compiler_params=_cparams(("parallel",)),
    )(buf)


def _pack_rows(flat, mult):
    n = flat.shape[0]
    chunk = mult * PACK_COLS
    tot = -(-n // chunk) * chunk
    return jnp.pad(flat, (0, tot - n)).reshape(tot // PACK_COLS, PACK_COLS)


_ADAM_BC1 = 1.0 - ADAM_B1 ** ADAM_STEP
_ADAM_BC2 = 1.0 - ADAM_B2 ** ADAM_STEP


def _adamw_math(wv, gv, mv, vv):
    mn = ADAM_B1 * mv + (1.0 - ADAM_B1) * gv
    vn = ADAM_B2 * vv + (1.0 - ADAM_B2) * (gv * gv)
    m_hat = mn / _ADAM_BC1
    v_hat = vn / _ADAM_BC2
    delta = -ADAM_LR * (m_hat / (jnp.sqrt(v_hat) + ADAM_EPS) + ADAM_WD * wv)
    return delta, mn, vn


def _adamw(w, g, m, v, *, name):
    shape = w.shape
    cols = shape[-1]
    to2d = lambda a: a.reshape(-1, cols)
    d, mn, vn = _ew(_adamw_math, [to2d(w), to2d(g), to2d(m), to2d(v)], [(cols, F32)] * 3, name=name)
    return d.reshape(shape), mn.reshape(shape), vn.reshape(shape)


def _adamw_sharded(w, parts, m, v, *, name):
    depth = w.shape[0]
    cols = w.shape[-1]
    rows = w[0].size // cols
    to3 = lambda a: a.reshape(depth, rows, cols)
    w3, m3, v3 = to3(w), to3(m), to3(v)
    bpr = cols * (3 * 4 + N_DEV * parts[0].dtype.itemsize + 4 * 4)
    tm = _row_tile(rows, bpr)
    outs = None
    for li in range(depth):
        carried = li > 0

        def body(*refs):
            w_ref, p_ref, m_ref, v_ref = refs[:4]
            g_ref, d_ref, mo_ref, vo_ref = refs[-4:]
            gv = p_ref[0].astype(F32)
            for q in range(1, N_DEV):
                gv = gv + p_ref[q].astype(F32)
            delta, mn, vn = _adamw_math(w_ref[0], gv, m_ref[0], v_ref[0])
            g_ref[0] = gv
            d_ref[0] = delta
            mo_ref[0] = mn
            vo_ref[0] = vn

        lay = pl.BlockSpec((1, tm, cols), functools.partial(lambda i, l: (l, i, 0), l=li))
        any_spec = pl.BlockSpec(memory_space=pl.ANY)
        ins = [w3, parts[li].reshape(N_DEV, rows, cols), m3, v3] + (list(outs) if carried else [])
        outs = pl.pallas_call(
            body, name=f"{name}_l{li}",
            grid=(rows // tm,),
            in_specs=[lay, pl.BlockSpec((N_DEV, tm, cols), lambda i: (0, i, 0)), lay, lay]
            + ([any_spec] * 4 if carried else []),
            out_specs=[lay] * 4,
            out_shape=[jax.ShapeDtypeStruct((depth, rows, cols), F32)] * 4,
            input_output_aliases={4 + q: q for q in range(4)} if carried else {},
            compiler_params=_cparams(("parallel",)),
        )(*ins)
    return [o.reshape(w.shape) for o in outs]


def _prep_in(w_in):
    rope_cols = jnp.pad(w_in[:, 2048:2048 + MLA_ROPE], ((0, 0), (0, 128 - MLA_ROPE)))
    return dict(wa=w_in[:, :2048], wb=w_in[:, 2112:5184], wc=w_in[:, 5184:8256], wd=w_in[:, 8256:14400], we=rope_cols)


def _prep_rest(full):
    wq = full["mla_w_uq"].reshape(512, N_HEADS, MLA_NOPE + MLA_ROPE)
    wq_rope = jnp.pad(wq[:, :, MLA_NOPE:], ((0, 0), (0, 0), (0, 128 - MLA_ROPE)))
    wkv = full["mla_w_ukv"].reshape(512, N_HEADS, 2 * HEAD_D)
    return dict(
        w_uq=jnp.concatenate([wq[:, :, :MLA_NOPE].reshape(512, 1024), wq_rope.reshape(512, 1024)], axis=1),
        w_ukv=jnp.concatenate([wkv[:, :, :HEAD_D].reshape(512, 1024), wkv[:, :, HEAD_D:].reshape(512, 1024)], axis=1),
        w_glu=full["ssm_w_glu"], w_branch=full["w_branch"], w_out=full["w_out"],
        w_ple_gate=full["w_ple_gate"], w_ple_proj=full["w_ple_proj"])


def _prep_ssm(sp):
    ab_re, ab_im, bb_re, bb_im = _s5_disc(sp["lam_re"], sp["lam_im"], sp["log_dt"], sp["b_re"], sp["b_im"])
    g4 = lambda a: a.reshape(SSM_GB, SSM_GB, a.shape[1], a.shape[2])
    bbd_re = _blockdiag(jnp.swapaxes(g4(bb_re), 2, 3)).astype(BF16)
    bbd_im = _blockdiag(jnp.swapaxes(g4(bb_im), 2, 3)).astype(BF16)
    cd_re = _blockdiag(jnp.swapaxes(g4(sp["c_re"]), 2, 3)).astype(BF16)
    cd_im = _blockdiag(jnp.swapaxes(g4(sp["c_im"]), 2, 3)).astype(BF16)
    return dict(bb_re=bbd_re, bb_im=bbd_im, bbt_re=jnp.swapaxes(bbd_re, 1, 2), bbt_im=jnp.swapaxes(bbd_im, 1, 2),
                c_re=cd_re, c_im=cd_im, ct_re=jnp.swapaxes(cd_re, 1, 2), ct_im=jnp.swapaxes(cd_im, 1, 2),
                dsk=sp["d"].reshape(1, BRANCH_W))


GELU_C = math.sqrt(2.0 / math.pi)


def _gelu(x):
    return 0.5 * x * (1.0 + jnp.tanh(GELU_C * (x + 0.044715 * x * x * x)))


def _gelu_grad(x):
    t = jnp.tanh(GELU_C * (x + 0.044715 * x * x * x))
    return 0.5 * (1.0 + t) + 0.5 * x * (1.0 - t * t) * GELU_C * (1.0 + 3.0 * 0.044715 * x * x)


def _layer_fwd(x_in, p_l, hooks, ssm, tabs, sm, rope, li):
    tg = lambda nm: nm
    ct, sp_, sm_ = rope
    h = _rmsnorm(x_in, sm["ln_g"], name=tg("ln"))
    w = dict(hooks.weights_in(li, x_in))
    seg_a = _mm(h, w["wa"], name=tg("in_a"))
    seg_b = _mm(h, w["wb"], name=tg("in_b"), out_dtypes=(BF16,))
    seg_c = _mm(h, w["wc"], name=tg("in_c"))
    seg_d = _mm(h, w["wd"], name=tg("in_d"))
    seg_e = _mm(h, w["we"], name=tg("in_e"))

    o_sb, tot = _sb_fwd(seg_b, name=tg("sb_fwd"))
    w.update(hooks.weights_rest(li, o_sb))

    u = seg_a[:, :BRANCH_W]
    y_pre, hc_re, hc_im = _s5_fwd(u, ssm["bb_re"], ssm["bb_im"], ssm["c_re"], ssm["c_im"], ssm["dsk"], tabs,
                                  name=tg("s5_fwd"))
    yg = _ew(_gelu, [y_pre], [(BRANCH_W, F32)], name=tg("gelu"))
    y_ssm, z_glu = _mm(yg, w["w_glu"], name=tg("glu"), extras=[(yg, 0)],
                       epilogue=lambda acc, ygv: (ygv * _sigmoid(acc), acc), out_dtypes=(F32, F32))

    q_lat = seg_a[:, 1024:1536]
    kv_lat = seg_a[:, 1536:2048]
    qn = _rmsnorm(q_lat, sm["mla_g_q"], name=tg("q_norm"))
    kvn = _rmsnorm(kv_lat, sm["mla_g_kv"], name=tg("kv_norm"))
    q_raw = _mm(qn, w["w_uq"], name=tg("uq"))
    kv_b = _mm(kvn, w["w_ukv"], name=tg("ukv"), out_dtypes=(BF16,))

    def rope_q(qv, c_, p_, m_):
        outs = [qv[:, :1024]]
        for hh in range(N_HEADS):
            outs.append(_rope_group(qv[:, 1024 + 128 * hh:1152 + 128 * hh], c_, p_, m_))
        return jnp.concatenate(outs, axis=1)

    q_b = _ew(rope_q, [q_raw, ct, sp_, sm_], [(2048, BF16)], name=tg("rope_q"))
    kr_b = _ew(_rope_group, [seg_e, ct, sp_, sm_], [(128, BF16)], name=tg("rope_k"))
    o_mla, lse = _mla_fwd(q_b, kv_b, kr_b, name=tg("mla_fwd"))

    def gate_fn(a, b, c_, gp):
        ys = jnp.concatenate([a, b, c_], axis=1)
        return ys * (gp * _sigmoid(gp))

    ys = _ew(gate_fn, [y_ssm, o_mla, o_sb, seg_c], [(3 * BRANCH_W, BF16)], name=tg("gate"))
    merged = None
    bos = []
    for b in range(3):
        if b == 0:
            ep = lambda acc, ml: (_sigmoid(ml) * acc, acc)
            ex = [(seg_d, 0)]
        else:
            ep = lambda acc, ml, prev: (prev + _sigmoid(ml) * acc, acc)
            ex = [(seg_d, b * D_MODEL), (merged, 0)]
        merged, bo = _mm(ys[:, b * BRANCH_W:(b + 1) * BRANCH_W], w["w_branch"][b], name=tg(f"branch{b}"),
                         extras=ex, epilogue=ep, out_dtypes=(F32, BF16))
        bos.append(bo)
    merged_b = merged.astype(BF16)
    x1 = _mm(merged_b, w["w_out"], name=tg("out"), extras=[(x_in, 0)], epilogue=lambda acc, xv: (xv + acc,))

    hn2 = _rmsnorm(x1, sm["ple_g"], name=tg("ple_norm"))
    e = _mm(p_l, w["w_ple_proj"], name=tg("ple_proj"))
    x2, gl = _mm(hn2, w["w_ple_gate"], name=tg("ple_gate"), extras=[(x1, 0), (e, 0)],
                 epilogue=lambda acc, xv, ev: (xv + _sigmoid(acc) * ev, acc), out_dtypes=(F32, F32))
    saved = dict(x_in=x_in, h=h, seg_a=seg_a, seg_b=seg_b, seg_c=seg_c, seg_d=seg_d, u=u, y_pre=y_pre, hc_re=hc_re,
                 hc_im=hc_im, yg=yg, y_ssm=y_ssm, z_glu=z_glu, q_lat=q_lat, kv_lat=kv_lat, qn=qn, kvn=kvn, q_b=q_b,
                 kv_b=kv_b, kr_b=kr_b, o_mla=o_mla, lse=lse, o_sb=o_sb, tot=tot, ys=ys, bos=bos, merged_b=merged_b,
                 x1=x1, hn2=hn2, e=e, gl=gl, p_l=p_l, w=w)
    return x2, saved


def _layer_bwd(dx2, sv, hooks, ssm, tabs, sm, rope, li):
    tg = lambda nm: nm
    w = sv["w"]
    ct, sp_, sm_ = rope
    gw, gs = {}, {}

    def ple_fn(d, ev, glv):
        gt = _sigmoid(glv)
        return d * ev * gt * (1.0 - gt), d * gt

    dgl, de = _ew(ple_fn, [dx2, sv["e"], sv["gl"]], [(D_MODEL, BF16), (D_MODEL, BF16)], name=tg("ple_bwd"))
    gw["w_ple_proj"] = _mm(sv["p_l"], de, ta=True, out_dtypes=(BF16,), name=tg("d_ple_proj"))
    gw["w_ple_gate"] = _mm(sv["hn2"], dgl, ta=True, out_dtypes=(BF16,), name=tg("d_ple_gate"))
    dhn2 = _mm(dgl, w["w_ple_gate"], tb=True, name=tg("ple_gate_t"))
    dx1, gs["ple_g"] = _rmsnorm_bwd(sv["x1"], sm["ple_g"], dhn2, dx2, name=tg("ple_norm_bwd"))

    dx1_b = dx1.astype(BF16)
    gw["w_out"] = _mm(sv["merged_b"], dx1_b, ta=True, out_dtypes=(BF16,), name=tg("d_out"))
    dmerged = _mm(dx1_b, w["w_out"], tb=True, name=tg("out_t"))

    def merge_fn(dm, ml, b0, b1, b2):
        dbo, dml = [], []
        for b, bo in enumerate((b0, b1, b2)):
            gt = _sigmoid(ml[:, b * D_MODEL:(b + 1) * D_MODEL])
            dbo.append(dm * gt)
            dml.append(dm * bo.astype(F32) * gt * (1.0 - gt))
        return jnp.concatenate(dbo, axis=1), jnp.concatenate(dml, axis=1)

    dbo, dseg_d = _ew(merge_fn, [dmerged, sv["seg_d"]] + sv["bos"], [(3 * D_MODEL, BF16), (3 * D_MODEL, BF16)],
                      name=tg("merge_bwd"))
    dys, dwb = [], []
    for b in range(3):
        dbo_b = dbo[:, b * D_MODEL:(b + 1) * D_MODEL]
        dwb.append(_mm(sv["ys"][:, b * BRANCH_W:(b + 1) * BRANCH_W], dbo_b, ta=True, out_dtypes=(BF16,), name=tg(f"d_branch{b}")))
        dys.append(_mm(dbo_b, w["w_branch"][b], tb=True, name=tg(f"branch{b}_t")))
    gw["w_branch"] = jnp.stack(dwb)

    def gate_bwd(d0, d1, d2, a, b, c_, gp):
        dy_all = jnp.concatenate([d0, d1, d2], axis=1)
        y_all = jnp.concatenate([a, b, c_], axis=1)
        sg = _sigmoid(gp)
        silu = gp * sg
        dsilu = sg * (1.0 + gp * (1.0 - sg))
        dyv = dy_all * silu
        return (dyv[:, :BRANCH_W], dyv[:, BRANCH_W:2 * BRANCH_W], dyv[:, 2 * BRANCH_W:], dy_all * y_all * dsilu)

    dy_ssm, do_mla, do_sb, dseg_c = _ew(
        gate_bwd, dys + [sv["y_ssm"], sv["o_mla"], sv["o_sb"], sv["seg_c"]],
        [(BRANCH_W, F32), (BRANCH_W, F32), (BRANCH_W, F32), (3 * BRANCH_W, BF16)], name=tg("gate_bwd"))

    dq_sb, dk_sb, dv_sb = _sb_bwd(sv["seg_b"], do_sb, sv["tot"], name=tg("sb_bwd"))
    dseg_b = jnp.concatenate([dq_sb, dk_sb, dv_sb], axis=1).astype(BF16)

    dqn_, dqr_, dkn_, dv_, dkr_ = _mla_bwd(sv["q_b"], sv["kv_b"], sv["kr_b"], sv["o_mla"], do_mla, sv["lse"],
                                           name=tg("mla_bwd"))

    def unrope_q(dn, dr, c_, p_, m_):
        outs = [dn]
        for hh in range(N_HEADS):
            outs.append(_rope_group_t(dr[:, 128 * hh:128 * hh + 128], c_, p_, m_))
        return jnp.concatenate(outs, axis=1)

    dq_raw = _ew(unrope_q, [dqn_, dqr_, ct, sp_, sm_], [(2048, BF16)], name=tg("unrope_q"))

    def unrope_k(dr, c_, p_, m_):
        tot_ = dr[:, :128]
        for hh in range(1, N_HEADS):
            tot_ = tot_ + dr[:, 128 * hh:128 * hh + 128]
        return _rope_group_t(tot_, c_, p_, m_)

    dseg_e = _ew(unrope_k, [dkr_, ct, sp_, sm_], [(128, BF16)], name=tg("unrope_k"))
    dkv_raw = jnp.concatenate([dkn_, dv_], axis=1).astype(BF16)
    dw_uq = _mm(sv["qn"], dq_raw, ta=True, out_dtypes=(BF16,), name=tg("d_uq"))
    dw_ukv = _mm(sv["kvn"], dkv_raw, ta=True, out_dtypes=(BF16,), name=tg("d_ukv"))
    dqn = _mm(dq_raw, w["w_uq"], tb=True, name=tg("uq_t"))
    dkvn = _mm(dkv_raw, w["w_ukv"], tb=True, name=tg("ukv_t"))
    dq_lat, gs["mla_g_q"] = _rmsnorm_bwd(sv["q_lat"], sm["mla_g_q"], dqn, None, name=tg("q_norm_bwd"))
    dkv_lat, gs["mla_g_kv"] = _rmsnorm_bwd(sv["kv_lat"], sm["mla_g_kv"], dkvn, None, name=tg("kv_norm_bwd"))
    dwq = dw_uq[:, :1024].reshape(512, N_HEADS, MLA_NOPE)
    dwr = dw_uq[:, 1024:].reshape(512, N_HEADS, 128)[:, :, :MLA_ROPE]
    gw["mla_w_uq"] = jnp.concatenate([dwq, dwr], axis=2).reshape(512, N_HEADS * (MLA_NOPE + MLA_ROPE))
    dwk = dw_ukv[:, :1024].reshape(512, N_HEADS, HEAD_D)
    dwv = dw_ukv[:, 1024:].reshape(512, N_HEADS, HEAD_D)
    gw["mla_w_ukv"] = jnp.concatenate([dwk, dwv], axis=2).reshape(512, 2 * N_HEADS * HEAD_D)

    def glu_bwd(d, ygv, zv):
        sg = _sigmoid(zv)
        return d * ygv * sg * (1.0 - sg), d * sg

    dz, dyg0 = _ew(glu_bwd, [dy_ssm, sv["yg"], sv["z_glu"]], [(BRANCH_W, BF16), (BRANCH_W, F32)], name=tg("glu_bwd"))
    gw["ssm_w_glu"] = _mm(sv["yg"], dz, ta=True, out_dtypes=(BF16,), name=tg("d_glu"))
    dy_pre = _mm(dz, w["w_glu"], tb=True, name=tg("glu_t"), extras=[(dyg0, 0), (sv["y_pre"], 0)],
                 epilogue=lambda acc, d0, yp: ((acc + d0) * _gelu_grad(yp),))
    token = hooks.grads_rest(li, gw)
    dsk = ssm["dsk"] if token is None else ssm["dsk"] + token[0, 0]
    du, dbb_re, dbb_im, dc_re, dc_im, da_re, da_im, dd = _s5_bwd(
        sv["u"], dy_pre, sv["hc_re"], sv["hc_im"], ssm["bb_re"], ssm["bb_im"], ssm["bbt_re"], ssm["bbt_im"],
        ssm["c_re"], ssm["c_im"], ssm["ct_re"], ssm["ct_im"], dsk, tabs, name=tg("s5_bwd"))
    gs["ssm_d"] = dd.reshape(BRANCH_W)
    gs["ssm_c_re"] = jnp.swapaxes(_blockdiag_extract(dc_re, SSM_STATE, SSM_GROUP), 2, 3).reshape(
        SSM_GROUPS, SSM_GROUP, SSM_STATE)
    gs["ssm_c_im"] = jnp.swapaxes(_blockdiag_extract(dc_im, SSM_STATE, SSM_GROUP), 2, 3).reshape(
        SSM_GROUPS, SSM_GROUP, SSM_STATE)
    gs["_dbb_re"] = jnp.swapaxes(_blockdiag_extract(dbb_re, SSM_GROUP, SSM_STATE), 2, 3).reshape(
        SSM_GROUPS, SSM_STATE, SSM_GROUP)
    gs["_dbb_im"] = jnp.swapaxes(_blockdiag_extract(dbb_im, SSM_GROUP, SSM_STATE), 2, 3).reshape(
        SSM_GROUPS, SSM_STATE, SSM_GROUP)
    gs["_da_re"] = da_re.reshape(SSM_GROUPS, SSM_STATE)
    gs["_da_im"] = da_im.reshape(SSM_GROUPS, SSM_STATE)

    dseg_a = jnp.concatenate([du, dq_lat, dkv_lat], axis=1).astype(BF16)
    hb = sv["h"]
    dwa = _mm(hb, dseg_a, ta=True, out_dtypes=(BF16,), name=tg("d_in_a"))
    dwb_ = _mm(hb, dseg_b, ta=True, out_dtypes=(BF16,), name=tg("d_in_b"))
    dwc = _mm(hb, dseg_c, ta=True, out_dtypes=(BF16,), name=tg("d_in_c"))
    dwd = _mm(hb, dseg_d, ta=True, out_dtypes=(BF16,), name=tg("d_in_d"))
    dwe = _mm(hb, dseg_e, ta=True, out_dtypes=(BF16,), name=tg("d_in_e"))
    gw["w_in"] = jnp.concatenate([dwa, dwe[:, :MLA_ROPE], dwb_, dwc, dwd], axis=1)
    add = lambda acc, prev: (acc + prev,)
    dh = _mm(dseg_a, w["wa"], tb=True, name=tg("in_a_t"))
    dh = _mm(dseg_b, w["wb"], tb=True, name=tg("in_b_t"), extras=[(dh, 0)], epilogue=add)
    dh = _mm(dseg_c, w["wc"], tb=True, name=tg("in_c_t"), extras=[(dh, 0)], epilogue=add)
    dh = _mm(dseg_d, w["wd"], tb=True, name=tg("in_d_t"), extras=[(dh, 0)], epilogue=add)
    dh = _mm(dseg_e, w["we"], tb=True, name=tg("in_e_t"), extras=[(dh, 0)], epilogue=add)
    dx_in, gs["ln_g"] = _rmsnorm_bwd(sv["x_in"], sm["ln_g"], dh, dx1, name=tg("ln_bwd"))
    return dx_in, gw, gs


class _LocalWeights:
    def __init__(self, fulls):
        self.fulls = fulls

    def weights_in(self, li, after):
        return _prep_in(self.fulls[li]["w_in"])

    def weights_rest(self, li, after):
        return _prep_rest(self.fulls[li])

    def grads_rest(self, li, gw):
        return None

    def grads_layer(self, li, gw):
        return None


def _local_step(x, p, target, hooks, small):
    s = x.shape[0]
    depth = small["ln_g"].shape[0]
    rope = _rope_tables(s)
    ssms, tabss, sms, saved = [], [], [], []
    xc = x
    for li in range(depth):
        sp = dict(lam_re=small["ssm_lam_re"][li], lam_im=small["ssm_lam_im"][li], log_dt=small["ssm_log_dt"][li],
                  b_re=small["ssm_b_re"][li], b_im=small["ssm_b_im"][li], c_re=small["ssm_c_re"][li],
                  c_im=small["ssm_c_im"][li], d=small["ssm_d"][li])
        ssm = _prep_ssm(sp)
        tabs = _s5_tables(sp["lam_re"], sp["lam_im"], sp["log_dt"])
        sm = dict(ln_g=small["ln_g"][li], mla_g_q=small["mla_g_q"][li], mla_g_kv=small["mla_g_kv"][li],
                  ple_g=small["ple_g"][li])
        xc, sv = _layer_fwd(xc, p[li], hooks, ssm, tabs, sm, rope, li)
        ssms.append(ssm), tabss.append(tabs), sms.append(sm), saved.append(sv)
    dx, dgf, loss = _loss_head(xc, small["final_g"], target, name="loss_head")
    gws, gss = [None] * depth, [None] * depth
    token = None
    for li in reversed(range(depth)):
        sm = sms[li] if token is None else dict(sms[li], ple_g=sms[li]["ple_g"] + token[0, 0])
        dx, gw, gs = _layer_bwd(dx, saved[li], hooks, ssms[li], tabss[li], sm, rope, li)
        token = hooks.grads_layer(li, gw)
        args = (small["ssm_lam_re"][li], small["ssm_lam_im"][li], small["ssm_log_dt"][li], small["ssm_b_re"][li],
                small["ssm_b_im"][li])
        _, vjp = jax.vjp(_s5_disc, *args)
        g_lr, g_li, g_dt, g_br, g_bi = vjp((gs.pop("_da_re"), gs.pop("_da_im"), gs.pop("_dbb_re"), gs.pop("_dbb_im")))
        gs.update(ssm_lam_re=g_lr, ssm_lam_im=g_li, ssm_log_dt=g_dt, ssm_b_re=g_br, ssm_b_im=g_bi)
        gws[li], gss[li] = gw, gs
    gsmall = {nm: jnp.stack([gss[li][nm].reshape(small[nm].shape[1:]) for li in range(depth)])
              for nm in SMALL if nm != "final_g"}
    gsmall["final_g"] = dgf.reshape(-1)
    return loss[0, 0], dx, gws, gsmall


REST = tuple(t for t in SHARDED if t[0] != "w_in")


def _to_full(spec, got):
    _, fshape, ax = spec
    return jnp.moveaxis(got, 0, ax).reshape(fshape)


def _to_slabs(spec, g):
    _, fshape, ax = spec
    g = g.reshape(tuple(fshape[:ax]) + (N_DEV, fshape[ax] // N_DEV) + tuple(fshape[ax + 1:]))
    return jnp.moveaxis(g, ax, 0)


class _Exchanged:
    def __init__(self, shards):
        bf = lambda nm, li: shards[nm][li].astype(BF16)
        w_in0 = _exchange([bf("w_in", 0)], all_gather=True, name="gather_w_in")[0]
        self.w_in0 = _to_full(SHARDED[0], w_in0)
        first = bf(REST[-1][0], 0) + (w_in0[0, 0, 0] * 0).astype(BF16)
        rest0 = [bf(nm, 0) for nm, _, _ in REST[:-1]] + [first]
        self.rest0, tok_a = _exchange_start(rest0, all_gather=True, name="gather_rest_start")
        all1 = [bf(nm, 1) for nm, _, _ in SHARDED[:-1]] + [bf(SHARDED[-1][0], 1) + tok_a[0, 0].astype(BF16)]
        self.all1, tok_b = _exchange_start(all1, all_gather=True, name="gather_next_start")
        self.token = tok_a[0, 0] + tok_b[0, 0]
        self.full1 = None
        self.sent = {}

    def weights_in(self, li, after):
        if li == 0:
            return _prep_in(self.w_in0)
        got = _exchange_finish(self.all1, after, all_gather=True, name="gather_next_finish")
        self.full1 = {t[0]: _to_full(t, g) for t, g in zip(SHARDED, got)}
        return _prep_in(self.full1["w_in"])

    def weights_rest(self, li, after):
        if li == 1:
            return _prep_rest(self.full1)
        got = _exchange_finish(self.rest0, after, all_gather=True, name="gather_rest_finish")
        return _prep_rest({t[0]: _to_full(t, g) for t, g in zip(REST, got)})

    def grads_rest(self, li, gw):
        if li != 0:
            return None
        self.sent["rest0"], token = _exchange_start([_to_slabs(t, gw[t[0]]) for t in REST], all_gather=False,
                                                    name="scatter_rest_start")
        return token

    def grads_layer(self, li, gw):
        if li == 0:
            self.w_in_grad0 = _to_slabs(SHARDED[0], gw["w_in"])
            return None
        self.sent["all1"], token = _exchange_start([_to_slabs(t, gw[t[0]]) for t in SHARDED], all_gather=False,
                                                   name="scatter_prev_start")
        return token

    def partial_grads(self, after):
        got1 = _exchange_finish(self.sent["all1"], after, all_gather=False, name="scatter_prev_finish")
        got0 = _exchange_finish(self.sent["rest0"], after, all_gather=False, name="scatter_rest_finish")
        w_in0 = _exchange([self.w_in_grad0], all_gather=False, name="scatter_w_in")[0]
        p0 = {t[0]: g for t, g in zip(REST, got0)}
        p0["w_in"] = w_in0
        return [p0, {t[0]: g for t, g in zip(SHARDED, got1)}]


def kernel(x, p, ln_g, w_in, ssm_lam_re, ssm_lam_im, ssm_log_dt, ssm_b_re, ssm_b_im, ssm_c_re, ssm_c_im, ssm_d, ssm_w_glu, mla_g_q, mla_g_kv, mla_w_uq, mla_w_ukv, w_branch, w_out, ple_g, w_ple_gate, w_ple_proj, final_g, loss_target, m_ln_g, m_w_in, m_ssm_lam_re, m_ssm_lam_im, m_ssm_log_dt, m_ssm_b_re, m_ssm_b_im, m_ssm_c_re, m_ssm_c_im, m_ssm_d, m_ssm_w_glu, m_mla_g_q, m_mla_g_kv, m_mla_w_uq, m_mla_w_ukv, m_w_branch, m_w_out, m_ple_g, m_w_ple_gate, m_w_ple_proj, m_final_g, v_ln_g, v_w_in, v_ssm_lam_re, v_ssm_lam_im, v_ssm_log_dt, v_ssm_b_re, v_ssm_b_im, v_ssm_c_re, v_ssm_c_im, v_ssm_d, v_ssm_w_glu, v_mla_g_q, v_mla_g_kv, v_mla_w_uq, v_mla_w_ukv, v_w_branch, v_w_out, v_ple_g, v_w_ple_gate, v_w_ple_proj, v_final_g):
    weights = dict(ln_g=ln_g, w_in=w_in, ssm_lam_re=ssm_lam_re, ssm_lam_im=ssm_lam_im, ssm_log_dt=ssm_log_dt,
                   ssm_b_re=ssm_b_re, ssm_b_im=ssm_b_im, ssm_c_re=ssm_c_re, ssm_c_im=ssm_c_im, ssm_d=ssm_d,
                   ssm_w_glu=ssm_w_glu, mla_g_q=mla_g_q, mla_g_kv=mla_g_kv, mla_w_uq=mla_w_uq, mla_w_ukv=mla_w_ukv,
                   w_branch=w_branch, w_out=w_out, ple_g=ple_g, w_ple_gate=w_ple_gate, w_ple_proj=w_ple_proj,
                   final_g=final_g)
    mom_m = dict(ln_g=m_ln_g, w_in=m_w_in, ssm_lam_re=m_ssm_lam_re, ssm_lam_im=m_ssm_lam_im, ssm_log_dt=m_ssm_log_dt,
                 ssm_b_re=m_ssm_b_re, ssm_b_im=m_ssm_b_im, ssm_c_re=m_ssm_c_re, ssm_c_im=m_ssm_c_im, ssm_d=m_ssm_d,
                 ssm_w_glu=m_ssm_w_glu, mla_g_q=m_mla_g_q, mla_g_kv=m_mla_g_kv, mla_w_uq=m_mla_w_uq,
                 mla_w_ukv=m_mla_w_ukv, w_branch=m_w_branch, w_out=m_w_out, ple_g=m_ple_g, w_ple_gate=m_w_ple_gate,
                 w_ple_proj=m_w_ple_proj, final_g=m_final_g)
    mom_v = dict(ln_g=v_ln_g, w_in=v_w_in, ssm_lam_re=v_ssm_lam_re, ssm_lam_im=v_ssm_lam_im, ssm_log_dt=v_ssm_log_dt,
                 ssm_b_re=v_ssm_b_re, ssm_b_im=v_ssm_b_im, ssm_c_re=v_ssm_c_re, ssm_c_im=v_ssm_c_im, ssm_d=v_ssm_d,
                 ssm_w_glu=v_ssm_w_glu, mla_g_q=v_mla_g_q, mla_g_kv=v_mla_g_kv, mla_w_uq=v_mla_w_uq,
                 mla_w_ukv=v_mla_w_ukv, w_branch=v_w_branch, w_out=v_w_out, ple_g=v_ple_g, w_ple_gate=v_w_ple_gate,
                 w_ple_proj=v_w_ple_proj, final_g=v_final_g)
    depth = ln_g.shape[0]
    assert depth == 2
    hooks = _Exchanged(weights)
    small = {nm: weights[nm] for nm in SMALL}
    small["ln_g"] = small["ln_g"].at[0, 0].add(hooks.token)
    loss_local, dx, gws, gsmall = _local_step(x[0], p[:, 0], loss_target[0], hooks, small)
    loss = lax.psum(loss_local, ("x", "y", "c"))

    grads, delta, new_m, new_v = {}, {}, {}, {}
    flat = jnp.concatenate([gsmall[nm].reshape(-1) for nm in SMALL])
    n_small = flat.shape[0]
    got = _exchange([_pack_rows(flat, 8)], all_gather=True, name="gather_small_grads")[0]
    gsum = _sum8(got, name="sum_small_grads").reshape(-1)[:n_small]
    off = 0
    for nm in SMALL:
        cnt = weights[nm].size
        grads[nm] = gsum[off:off + cnt].reshape(weights[nm].shape)
        off += cnt
    pk = lambda d: _pack_rows(jnp.concatenate([d[nm].reshape(-1) for nm in SMALL]), 8)
    d_s, m_s, v_s = _adamw(pk(weights), pk(grads), pk(mom_m), pk(mom_v), name="adamw_small")
    off = 0
    for nm in SMALL:
        cnt = weights[nm].size
        shp = weights[nm].shape
        delta[nm] = d_s.reshape(-1)[off:off + cnt].reshape(shp)
        new_m[nm] = m_s.reshape(-1)[off:off + cnt].reshape(shp)
        new_v[nm] = v_s.reshape(-1)[off:off + cnt].reshape(shp)
        off += cnt

    parts = hooks.partial_grads(dx)
    for nm, _, _ in SHARDED:
        grads[nm], delta[nm], new_m[nm], new_v[nm] = _adamw_sharded(
            weights[nm], [parts[li][nm] for li in range(depth)], mom_m[nm], mom_v[nm], name=f"adamw_{nm}")
    return (loss, dx[None], *[grads[nm] for nm in WEIGHT_ORDER], *[delta[nm] for nm in WEIGHT_ORDER],
            *[new_m[nm] for nm in WEIGHT_ORDER], *[new_v[nm] for nm in WEIGHT_ORDER])
```

```python
import functools
import math

import jax
import jax.numpy as jnp
from jax import lax
from jax.experimental import pallas as pl
from jax.experimental.pallas import tpu as pltpu

F32 = jnp.float32
BF16 = jnp.bfloat16

N_DEV = 8
D_MODEL = 2048
BRANCH_W = 1024
N_HEADS = 8
HEAD_D = 128
SSM_GROUPS = 64
SSM_GROUP = 16
SSM_STATE = 64
SSM_GB = 8
SSM_CHUNK = 256
MLA_ROPE = 64
MLA_NOPE = 128
ROPE_THETA = 10000.0
NORM_EPS = 1e-6
DT_MIN = 1e-3
DT_MAX = 1e-1
PACK_COLS = 1024

ADAM_LR = 0.001
ADAM_B1 = 0.9
ADAM_B2 = 0.999
ADAM_EPS = 1e-08
ADAM_WD = 0.01
ADAM_STEP = 10

VMEM_LIMIT = 56 * 2 ** 20
NEG_BIG = -1e30
MM_VMEM_BUDGET = 36 * 2 ** 20

SHARDED = (
    ("w_in", (2048, 14400), 1),
    ("ssm_w_glu", (1024, 1024), 0),
    ("mla_w_uq", (512, 1536), 1),
    ("mla_w_ukv", (512, 2048), 1),
    ("w_branch", (3, 1024, 2048), 2),
    ("w_out", (2048, 2048), 0),
    ("w_ple_gate", (2048, 2048), 0),
    ("w_ple_proj", (256, 2048), 1),
)
SMALL = ("ln_g", "ssm_lam_re", "ssm_lam_im", "ssm_log_dt", "ssm_b_re", "ssm_b_im", "ssm_c_re", "ssm_c_im",
         "ssm_d", "mla_g_q", "mla_g_kv", "ple_g", "final_g")
WEIGHT_ORDER = ("ln_g", "w_in", "ssm_lam_re", "ssm_lam_im", "ssm_log_dt", "ssm_b_re", "ssm_b_im", "ssm_c_re",
                "ssm_c_im", "ssm_d", "ssm_w_glu", "mla_g_q", "mla_g_kv", "mla_w_uq", "mla_w_ukv", "w_branch",
                "w_out", "ple_g", "w_ple_gate", "w_ple_proj", "final_g")


def _cparams(sem=None):
    return pltpu.CompilerParams(dimension_semantics=sem, vmem_limit_bytes=VMEM_LIMIT)


def _pick(n, prefs):
    for t in prefs:
        if n % t == 0:
            return t
    return n


def _dot(a, b):
    return lax.dot_general(a, b, (((1,), (0,)), ((), ())), preferred_element_type=F32)


def _dot_nt(a, b):
    return lax.dot_general(a, b, (((1,), (1,)), ((), ())), preferred_element_type=F32)


def _dot_tn(a, b):
    return lax.dot_general(a, b, (((0,), (0,)), ((), ())), preferred_element_type=F32)


def _sigmoid(x):
    return 1.0 / (1.0 + jnp.exp(-x))


def _mm(a, b, *, name, ta=False, tb=False, extras=(), epilogue=None, out_dtypes=(F32,)):
    m, k = (a.shape[1], a.shape[0]) if ta else a.shape
    n = b.shape[0] if tb else b.shape[1]
    assert (b.shape[1] if tb else b.shape[0]) == k
    n_ex = len(extras)
    n_out = len(out_dtypes)
    tn = _pick(n, (1024, 512, 256, 128))
    tk = _pick(k, (2048, 1024, 512, 256, 128))
    for tm in (1024, 512, 256, 128):
        if m % tm:
            continue
        est = 2 * (tm * tk * a.dtype.itemsize + tk * tn * b.dtype.itemsize) + tm * tn * 4
        est += 2 * tm * tn * (sum(e[0].dtype.itemsize for e in extras) + sum(jnp.dtype(d).itemsize for d in out_dtypes))
        if est <= MM_VMEM_BUDGET:
            break
    nk = k // tk

    def body(*refs):
        a_ref, b_ref = refs[0], refs[1]
        ex_refs = refs[2:2 + n_ex]
        out_refs = refs[2 + n_ex:2 + n_ex + n_out]
        acc_ref = refs[-1]
        kk = pl.program_id(2)

        @pl.when(kk == 0)
        def _():
            acc_ref[...] = jnp.zeros_like(acc_ref)

        av = a_ref[...].astype(BF16)
        bv = b_ref[...].astype(BF16)
        dims = (((0 if ta else 1,), (1 if tb else 0,)), ((), ()))
        acc_ref[...] += lax.dot_general(av, bv, dims, preferred_element_type=F32)

        @pl.when(kk == nk - 1)
        def _():
            acc = acc_ref[...]
            res = epilogue(acc, *[r[...] for r in ex_refs]) if epilogue is not None else (acc,)
            for r, v in zip(out_refs, res):
                r[...] = v.astype(r.dtype)

    a_spec = pl.BlockSpec((tk, tm), lambda i, j, q: (q, i)) if ta else pl.BlockSpec((tm, tk), lambda i, j, q: (i, q))
    b_spec = pl.BlockSpec((tn, tk), lambda i, j, q: (j, q)) if tb else pl.BlockSpec((tk, tn), lambda i, j, q: (q, j))
    ex_specs = []
    for arr, off in extras:
        assert off % tn == 0 and arr.shape[0] == m
        ex_specs.append(pl.BlockSpec((tm, tn), functools.partial(lambda i, j, q, o: (i, j + o), o=off // tn)))
    outs = pl.pallas_call(
        body, name=name,
        grid=(m // tm, n // tn, nk),
        in_specs=[a_spec, b_spec] + ex_specs,
        out_specs=[pl.BlockSpec((tm, tn), lambda i, j, q: (i, j)) for _ in out_dtypes],
        out_shape=[jax.ShapeDtypeStruct((m, n), dt) for dt in out_dtypes],
        scratch_shapes=[pltpu.VMEM((tm, tn), F32)],
        compiler_params=_cparams(("parallel", "parallel", "arbitrary")),
    )(a, b, *[e[0] for e in extras])
    return outs if n_out > 1 else outs[0]


def _row_tile(m, bytes_per_row):
    for t in (1024, 512, 256, 128, 64, 32, 16):
        if m % t == 0 and 2 * t * bytes_per_row <= 20 * 2 ** 20:
            return t
    return 16 if m % 16 == 0 else m


def _ew(fn, ins, outs, *, name):
    m = max(x.shape[0] for x in ins)
    bpr = sum(x.shape[1] * x.dtype.itemsize for x in ins if x.shape[0] == m)
    bpr += sum(no * jnp.dtype(dt).itemsize for no, dt in outs)
    tm = _row_tile(m, bpr)
    n_in = len(ins)

    def body(*refs):
        res = fn(*[r[...] for r in refs[:n_in]])
        if not isinstance(res, (tuple, list)):
            res = (res,)
        for r, v in zip(refs[n_in:], res):
            r[...] = v.astype(r.dtype)

    in_specs = []
    for x in ins:
        if x.shape[0] == m:
            in_specs.append(pl.BlockSpec((tm, x.shape[1]), lambda i: (i, 0)))
        else:
            in_specs.append(pl.BlockSpec(x.shape, lambda i: (0, 0)))
    res = pl.pallas_call(
        body, name=name,
        grid=(m // tm,),
        in_specs=in_specs,
        out_specs=[pl.BlockSpec((tm, no), lambda i: (i, 0)) for no, _ in outs],
        out_shape=[jax.ShapeDtypeStruct((m, no), dt) for no, dt in outs],
        compiler_params=_cparams(("parallel",)),
    )(*ins)
    return res if len(outs) > 1 else res[0]


def _rmsnorm(x, g, *, name, out_dtype=BF16):
    d = x.shape[1]

    def fn(xv, gv):
        r = lax.rsqrt(jnp.mean(xv * xv, axis=-1, keepdims=True) + NORM_EPS)
        return xv * r * gv

    return _ew(fn, [x, g.reshape(1, d)], [(d, out_dtype)], name=name)


def _rmsnorm_bwd(x, g, dy, res, *, name):
    m, d = x.shape
    has_res = res is not None
    tm = _row_tile(m, d * 4 * (4 if has_res else 3))

    def body(*refs):
        x_ref, g_ref, dy_ref = refs[:3]
        res_ref = refs[3] if has_res else None
        dx_ref, dg_ref = refs[-2], refs[-1]

        @pl.when(pl.program_id(0) == 0)
        def _():
            dg_ref[...] = jnp.zeros_like(dg_ref)

        xv = x_ref[...]
        dyv = dy_ref[...].astype(F32)
        r = lax.rsqrt(jnp.mean(xv * xv, axis=-1, keepdims=True) + NORM_EPS)
        xh = xv * r
        dg_ref[...] += jnp.sum(dyv * xh, axis=0, keepdims=True)
        dyg = dyv * g_ref[...]
        dx = r * (dyg - xh * jnp.mean(dyg * xh, axis=-1, keepdims=True))
        if has_res:
            dx = dx + res_ref[...]
        dx_ref[...] = dx

    row = pl.BlockSpec((tm, d), lambda i: (i, 0))
    vec = pl.BlockSpec((1, d), lambda i: (0, 0))
    ins = [x, g.reshape(1, d), dy] + ([res] if has_res else [])
    return pl.pallas_call(
        body, name=name,
        grid=(m // tm,),
        in_specs=[row, vec, row] + ([row] if has_res else []),
        out_specs=[row, vec],
        out_shape=[jax.ShapeDtypeStruct((m, d), F32), jax.ShapeDtypeStruct((1, d), F32)],
        compiler_params=_cparams(("arbitrary",)),
    )(*ins)


def _loss_head(x, g, target, *, name):
    m, d = x.shape
    tm = _row_tile(m, d * 4 * 3)

    def body(x_ref, g_ref, t_ref, dx_ref, dg_ref, loss_ref):
        @pl.when(pl.program_id(0) == 0)
        def _():
            dg_ref[...] = jnp.zeros_like(dg_ref)
            loss_ref[...] = jnp.zeros_like(loss_ref)

        xv = x_ref[...]
        gv = g_ref[...]
        r = lax.rsqrt(jnp.mean(xv * xv, axis=-1, keepdims=True) + NORM_EPS)
        xh = xv * r
        diff = xh * gv - t_ref[...]
        loss_ref[...] += 0.5 * jnp.sum(jnp.mean(diff * diff, axis=-1, keepdims=True), axis=0, keepdims=True)
        dyv = diff * (1.0 / d)
        dg_ref[...] += jnp.sum(dyv * xh, axis=0, keepdims=True)
        dyg = dyv * gv
        dx_ref[...] = r * (dyg - xh * jnp.mean(dyg * xh, axis=-1, keepdims=True))

    row = pl.BlockSpec((tm, d), lambda i: (i, 0))
    vec = pl.BlockSpec((1, d), lambda i: (0, 0))
    return pl.pallas_call(
        body, name=name,
        grid=(m // tm,),
        in_specs=[row, vec, row],
        out_specs=[row, vec, pl.BlockSpec((1, 128), lambda i: (0, 0))],
        out_shape=[jax.ShapeDtypeStruct((m, d), F32), jax.ShapeDtypeStruct((1, d), F32),
                   jax.ShapeDtypeStruct((1, 128), F32)],
        compiler_params=_cparams(("arbitrary",)),
    )(x, g.reshape(1, d), target)


def _rope_tables(seqlen):
    pos = jnp.arange(seqlen, dtype=F32)
    inv_freq = ROPE_THETA ** (-jnp.arange(0, MLA_ROPE, 2, dtype=F32) / MLA_ROPE)
    ang = pos[:, None] * inv_freq[None, :]
    c, s = jnp.cos(ang), jnp.sin(ang)
    z = jnp.zeros_like(c)
    return (jnp.concatenate([c, c, z, z], axis=1), jnp.concatenate([z, s, z, z], axis=1),
            jnp.concatenate([-s, z, z, z], axis=1))


def _rope_group(xg, ct, sp, sm):
    return xg * ct + pltpu.roll(xg, 32, 1) * sp + pltpu.roll(xg, 96, 1) * sm


def _rope_group_t(dg, ct, sp, sm):
    return dg * ct + pltpu.roll(dg * sp, 96, 1) + pltpu.roll(dg * sm, 32, 1)


def _s5_disc(lam_re, lam_im, log_dt, b_re, b_im):
    dt = jnp.exp(log_dt)[:, None]
    mag = jnp.exp(lam_re * dt)
    ab_re = mag * jnp.cos(lam_im * dt)
    ab_im = mag * jnp.sin(lam_im * dt)
    den = lam_re * lam_re + lam_im * lam_im
    nr = ab_re - 1.0
    ni = ab_im
    coef_re = (nr * lam_re + ni * lam_im) / den
    coef_im = (ni * lam_re - nr * lam_im) / den
    bb_re = coef_re[..., None] * b_re - coef_im[..., None] * b_im
    bb_im = coef_re[..., None] * b_im + coef_im[..., None] * b_re
    return ab_re, ab_im, bb_re, bb_im


def _blockdiag(x):
    nb, ng, r, c = x.shape
    eye = jnp.eye(ng, dtype=x.dtype)
    return (x[:, :, :, None, :] * eye[None, :, None, :, None]).reshape(nb, ng * r, ng * c)


def _blockdiag_extract(x, r, c):
    nb = x.shape[0]
    x5 = x.reshape(nb, SSM_GB, r, SSM_GB, c)
    eye = jnp.eye(SSM_GB, dtype=x.dtype)
    return jnp.sum(x5 * eye[None, :, None, :, None], axis=3)


def _tile_scan(h_re, h_im, a_re_ref, a_im_ref, t_chunk, reverse):
    n = h_re.shape[1]
    h_re = h_re.reshape(t_chunk // 8, 8, n)
    h_im = h_im.reshape(t_chunk // 8, 8, n)
    sub = lax.broadcasted_iota(jnp.int32, (t_chunk // 8, 8, n), 1)
    for k in range(3):
        sh = 1 << k
        ar = a_re_ref[pl.ds(k, 1), :]
        ai = a_im_ref[pl.ds(k, 1), :]
        if reverse:
            s_re = pltpu.roll(h_re, 8 - sh, 1)
            s_im = pltpu.roll(h_im, 8 - sh, 1)
            keep = sub < 8 - sh
            ai = -ai
        else:
            s_re = pltpu.roll(h_re, sh, 1)
            s_im = pltpu.roll(h_im, sh, 1)
            keep = sub >= sh
        s_re = jnp.where(keep, s_re, 0.0)
        s_im = jnp.where(keep, s_im, 0.0)
        h_re, h_im = h_re + ar * s_re - ai * s_im, h_im + ar * s_im + ai * s_re
    return h_re.reshape(t_chunk, n), h_im.reshape(t_chunk, n)


def _tile_chain(h_re, h_im, w_re, w_im, c_re, c_im, t_chunk, reverse):
    nt = t_chunk // 8
    out_re, out_im = [None] * nt, [None] * nt
    edge = 0 if reverse else 7
    for j in (reversed(range(nt)) if reverse else range(nt)):
        tr = h_re[8 * j:8 * j + 8, :] + w_re * c_re - w_im * c_im
        ti = h_im[8 * j:8 * j + 8, :] + w_re * c_im + w_im * c_re
        c_re = tr[edge:edge + 1, :]
        c_im = ti[edge:edge + 1, :]
        out_re[j], out_im[j] = tr, ti
    return jnp.concatenate(out_re, axis=0), jnp.concatenate(out_im, axis=0), c_re, c_im


def _s5_tables(lam_re, lam_im, log_dt):
    dt = jnp.exp(log_dt)[:, None]
    lr = (lam_re * dt).reshape(1, -1)
    li = (lam_im * dt).reshape(1, -1)

    def powers(n):
        mag = jnp.exp(n * lr)
        return mag * jnp.cos(n * li), mag * jnp.sin(n * li)

    p_re, p_im = powers(jnp.arange(1, 9, dtype=F32)[:, None])
    pad = jnp.zeros((5, lr.shape[1]), F32)
    a_re = jnp.concatenate([p_re[0:1], p_re[1:2], p_re[3:4], pad], axis=0)
    a_im = jnp.concatenate([p_im[0:1], p_im[1:2], p_im[3:4], pad], axis=0)
    return p_re, p_im, p_re[::-1], p_im[::-1], a_re, a_im


def _s5_fwd(u, bb_re, bb_im, c_re, c_im, dsk, tabs, *, name):
    s = u.shape[0]
    t = min(SSM_CHUNK, s)
    nt = s // t
    ns = SSM_GB * SSM_STATE
    p_re, p_im, _, _, a_re, a_im = tabs

    def body(u_ref, bbr_ref, bbi_ref, cr_ref, ci_ref, d_ref, pr_ref, pi_ref, ar_ref, ai_ref,
             y_ref, hcr_ref, hci_ref, car_re, car_im):
        @pl.when(pl.program_id(1) == 0)
        def _():
            car_re[...] = jnp.zeros_like(car_re)
            car_im[...] = jnp.zeros_like(car_im)

        cin_re = car_re[...]
        cin_im = car_im[...]
        hcr_ref[0] = cin_re
        hci_ref[0] = cin_im
        uv = u_ref[...]
        ub = uv.astype(BF16)
        h_re = _dot(ub, bbr_ref[0])
        h_im = _dot(ub, bbi_ref[0])
        rows = lax.broadcasted_iota(jnp.int32, (t, ns), 0)
        h_re, h_im = _tile_scan(h_re, h_im, ar_ref, ai_ref, t, False)
        h_re, h_im, c_re_, c_im_ = _tile_chain(h_re, h_im, pr_ref[...], pi_ref[...], cin_re, cin_im, t, False)
        car_re[...] = c_re_
        car_im[...] = c_im_
        y = _dot(h_re.astype(BF16), cr_ref[0]) - _dot(h_im.astype(BF16), ci_ref[0]) + d_ref[...] * uv
        y_ref[...] = y

    blk = lambda r, c: pl.BlockSpec((1, r, c), lambda g, i: (g, 0, 0))
    tab = pl.BlockSpec((8, ns), lambda g, i: (0, g))
    return pl.pallas_call(
        body, name=name,
        grid=(SSM_GB, nt),
        in_specs=[pl.BlockSpec((t, 128), lambda g, i: (i, g)), blk(128, ns), blk(128, ns), blk(ns, 128), blk(ns, 128),
                  pl.BlockSpec((1, 128), lambda g, i: (0, g)), tab, tab, tab, tab],
        out_specs=[pl.BlockSpec((t, 128), lambda g, i: (i, g)),
                   pl.BlockSpec((1, 1, ns), lambda g, i: (i, 0, g)), pl.BlockSpec((1, 1, ns), lambda g, i: (i, 0, g))],
        out_shape=[jax.ShapeDtypeStruct((s, BRANCH_W), F32),
                   jax.ShapeDtypeStruct((nt, 1, SSM_GB * ns), F32), jax.ShapeDtypeStruct((nt, 1, SSM_GB * ns), F32)],
        scratch_shapes=[pltpu.VMEM((1, ns), F32), pltpu.VMEM((1, ns), F32)],
        compiler_params=_cparams(("parallel", "arbitrary")),
    )(u, bb_re, bb_im, c_re, c_im, dsk, p_re, p_im, a_re, a_im)


def _s5_bwd(u, dy, hc_re, hc_im, bb_re, bb_im, bbt_re, bbt_im, c_re, c_im, ct_re, ct_im, dsk, tabs, *, name):
    s = u.shape[0]
    t = min(SSM_CHUNK, s)
    nt = s // t
    ns = SSM_GB * SSM_STATE
    p_re, p_im, q_re, q_im, a_re, a_im = tabs

    def body(u_ref, dy_ref, hcr_ref, hci_ref, bbr_ref, bbi_ref, bbtr_ref, bbti_ref, cr_ref, ci_ref, ctr_ref, cti_ref,
             d_ref, pr_ref, pi_ref, qr_ref, qi_ref, ar_ref, ai_ref,
             du_ref, dbbr_ref, dbbi_ref, dcr_ref, dci_ref, dar_ref, dai_ref, dd_ref, lam_re_c, lam_im_c):
        @pl.when(pl.program_id(1) == 0)
        def _():
            lam_re_c[...] = jnp.zeros_like(lam_re_c)
            lam_im_c[...] = jnp.zeros_like(lam_im_c)
            dbbr_ref[...] = jnp.zeros_like(dbbr_ref)
            dbbi_ref[...] = jnp.zeros_like(dbbi_ref)
            dcr_ref[...] = jnp.zeros_like(dcr_ref)
            dci_ref[...] = jnp.zeros_like(dci_ref)
            dar_ref[...] = jnp.zeros_like(dar_ref)
            dai_ref[...] = jnp.zeros_like(dai_ref)
            dd_ref[...] = jnp.zeros_like(dd_ref)

        cin_re = hcr_ref[0]
        cin_im = hci_ref[0]
        uv = u_ref[...]
        ub = uv.astype(BF16)
        dyv = dy_ref[...]
        dyb = dyv.astype(BF16)
        rows = lax.broadcasted_iota(jnp.int32, (t, ns), 0)
        h_re = _dot(ub, bbr_ref[0])
        h_im = _dot(ub, bbi_ref[0])
        h_re, h_im = _tile_scan(h_re, h_im, ar_ref, ai_ref, t, False)
        h_re, h_im, _, _ = _tile_chain(h_re, h_im, pr_ref[...], pi_ref[...], cin_re, cin_im, t, False)
        dcr_ref[0] += _dot_tn(h_re.astype(BF16), dyb)
        dci_ref[0] -= _dot_tn(h_im.astype(BF16), dyb)
        first = rows == 0
        hp_re = jnp.where(first, cin_re, pltpu.roll(h_re, 1, 0))
        hp_im = jnp.where(first, cin_im, pltpu.roll(h_im, 1, 0))
        l_re = _dot(dyb, ctr_ref[0])
        l_im = -_dot(dyb, cti_ref[0])
        l_re, l_im = _tile_scan(l_re, l_im, ar_ref, ai_ref, t, True)
        l_re, l_im, n_re, n_im = _tile_chain(l_re, l_im, qr_ref[...], -qi_ref[...], lam_re_c[...], lam_im_c[...],
                                             t, True)
        lam_re_c[...] = n_re
        lam_im_c[...] = n_im
        lrb = l_re.astype(BF16)
        lib = l_im.astype(BF16)
        du_ref[...] = _dot(lrb, bbtr_ref[0]) + _dot(lib, bbti_ref[0]) + d_ref[...] * dyv
        dbbr_ref[0] += _dot_tn(ub, lrb)
        dbbi_ref[0] += _dot_tn(ub, lib)
        dar_ref[0] += jnp.sum(l_re * hp_re + l_im * hp_im, axis=0, keepdims=True)
        dai_ref[0] += jnp.sum(l_im * hp_re - l_re * hp_im, axis=0, keepdims=True)
        dd_ref[0] += jnp.sum(dyv * uv, axis=0, keepdims=True)

    rev = lambda g, i: (nt - 1 - i, g)
    blk = lambda r, c: pl.BlockSpec((1, r, c), lambda g, i: (g, 0, 0))
    tab = pl.BlockSpec((8, ns), lambda g, i: (0, g))
    car = pl.BlockSpec((1, 1, ns), lambda g, i: (nt - 1 - i, 0, g))
    return pl.pallas_call(
        body, name=name,
        grid=(SSM_GB, nt),
        in_specs=[pl.BlockSpec((t, 128), rev), pl.BlockSpec((t, 128), rev), car, car,
                  blk(128, ns), blk(128, ns), blk(ns, 128), blk(ns, 128), blk(ns, 128), blk(ns, 128),
                  blk(128, ns), blk(128, ns), pl.BlockSpec((1, 128), lambda g, i: (0, g)),
                  tab, tab, tab, tab, tab, tab],
        out_specs=[pl.BlockSpec((t, 128), rev), blk(128, ns), blk(128, ns), blk(ns, 128), blk(ns, 128),
                   blk(1, ns), blk(1, ns), blk(1, 128)],
        out_shape=[jax.ShapeDtypeStruct((s, BRANCH_W), F32),
                   jax.ShapeDtypeStruct((SSM_GB, 128, ns), F32), jax.ShapeDtypeStruct((SSM_GB, 128, ns), F32),
                   jax.ShapeDtypeStruct((SSM_GB, ns, 128), F32), jax.ShapeDtypeStruct((SSM_GB, ns, 128), F32),
                   jax.ShapeDtypeStruct((SSM_GB, 1, ns), F32), jax.ShapeDtypeStruct((SSM_GB, 1, ns), F32),
                   jax.ShapeDtypeStruct((SSM_GB, 1, 128), F32)],
        scratch_shapes=[pltpu.VMEM((1, ns), F32), pltpu.VMEM((1, ns), F32)],
        compiler_params=_cparams(("parallel", "arbitrary")),
    )(u, dy, hc_re, hc_im, bb_re, bb_im, bbt_re, bbt_im, c_re, c_im, ct_re, ct_im, dsk,
      p_re, p_im, q_re, q_im, a_re, a_im)


ATT_BQ = 512
ATT_BK = 512


def _attn_blocks(s):
    return min(ATT_BQ, s), min(ATT_BK, s)


def _rel_index(bq, bk):
    return lax.broadcasted_iota(jnp.int32, (bq, bk), 0) - lax.broadcasted_iota(jnp.int32, (bq, bk), 1)


def _mla_fwd(q, k, kv, *, name):
    s = q.shape[0]
    bq, bk = _attn_blocks(s)
    r = bq // bk
    scale = float((MLA_NOPE + MLA_ROPE) ** -0.5)

    def body(q_ref, k_ref, v_ref, o_ref, lse_ref):
        qi = pl.program_id(1)
        qv = q_ref[...]
        nfull = qi * r
        rel = _rel_index(bq, bk)

        def step(j, carry, d):
            m, l, acc = carry
            ks = pl.ds(pl.multiple_of(j * bk, bk), bk)
            sc = _dot_nt(qv, k_ref[ks, :]) * scale
            if d is not None:
                sc = jnp.where(rel >= d * bk, sc, NEG_BIG)
            m_new = jnp.maximum(m, jnp.max(sc, axis=1, keepdims=True))
            alpha = jnp.exp(m - m_new)
            p = jnp.exp(sc - m_new)
            l = alpha * l + jnp.sum(p, axis=1, keepdims=True)
            acc = alpha * acc + _dot(p.astype(BF16), v_ref[ks, :])
            return m_new, l, acc

        carry = (jnp.full((bq, 1), NEG_BIG, F32), jnp.zeros((bq, 1), F32), jnp.zeros((bq, HEAD_D), F32))
        carry = lax.fori_loop(0, nfull, lambda j, c: step(j, c, None), carry)
        for d in range(r):
            carry = step(nfull + d, carry, d)
        m, l, acc = carry
        o_ref[...] = acc / l
        lse_ref[0] = m + jnp.log(l)

    return pl.pallas_call(
        body, name=name,
        grid=(N_HEADS, s // bq),
        in_specs=[pl.BlockSpec((bq, 2 * HEAD_D), lambda h, i: (i, h)),
                  pl.BlockSpec((s, 2 * HEAD_D), lambda h, i: (0, h)),
                  pl.BlockSpec((s, HEAD_D), lambda h, i: (0, N_HEADS + h))],
        out_specs=[pl.BlockSpec((bq, HEAD_D), lambda h, i: (i, h)), pl.BlockSpec((1, bq, 1), lambda h, i: (h, i, 0))],
        out_shape=[jax.ShapeDtypeStruct((s, N_HEADS * HEAD_D), F32), jax.ShapeDtypeStruct((N_HEADS, s, 1), F32)],
        compiler_params=_cparams(("parallel", "arbitrary")),
    )(q, k, kv)


def _mla_bwd(q, k, kv, o, do, lse, *, name):
    s = q.shape[0]
    bq, bk = _attn_blocks(s)
    r = bq // bk
    scale = float((MLA_NOPE + MLA_ROPE) ** -0.5)

    def body(q_ref, k_ref, v_ref, o_ref, do_ref, lse_ref, dq_ref, dk_ref, dv_ref):
        qi = pl.program_id(1)

        @pl.when(qi == 0)
        def _():
            dk_ref[...] = jnp.zeros_like(dk_ref)
            dv_ref[...] = jnp.zeros_like(dv_ref)

        qv = q_ref[...]
        dov = do_ref[...]
        dob = dov.astype(BF16)
        delta = jnp.sum(dov * o_ref[...], axis=1, keepdims=True)
        lse = lse_ref[0]
        nfull = qi * r
        rel = _rel_index(bq, bk)

        def step(j, dq, d):
            ks = pl.ds(pl.multiple_of(j * bk, bk), bk)
            kt = k_ref[ks, :]
            sc = _dot_nt(qv, kt) * scale
            p = jnp.exp(sc - lse)
            if d is not None:
                p = jnp.where(rel >= d * bk, p, 0.0)
            dp = _dot_nt(dob, v_ref[ks, :])
            ds = (p * (dp - delta) * scale).astype(BF16)
            pb = p.astype(BF16)
            dk_ref[ks, :] += _dot_tn(ds, qv)
            dv_ref[ks, :] += _dot_tn(pb, dob)
            return dq + _dot(ds, kt)

        dq = lax.fori_loop(0, nfull, lambda j, c: step(j, c, None), jnp.zeros((bq, 2 * HEAD_D), F32))
        for d in range(r):
            dq = step(nfull + d, dq, d)
        dq_ref[...] = dq

    tile = pl.BlockSpec((bq, HEAD_D), lambda h, i: (i, h))
    tile2 = pl.BlockSpec((bq, 2 * HEAD_D), lambda h, i: (i, h))
    res2 = pl.BlockSpec((s, 2 * HEAD_D), lambda h, i: (0, h))
    return pl.pallas_call(
        body, name=name,
        grid=(N_HEADS, s // bq),
        in_specs=[tile2, res2, pl.BlockSpec((s, HEAD_D), lambda h, i: (0, N_HEADS + h)), tile, tile,
                  pl.BlockSpec((1, bq, 1), lambda h, i: (h, i, 0))],
        out_specs=[tile2, res2, pl.BlockSpec((s, HEAD_D), lambda h, i: (0, h))],
        out_shape=[jax.ShapeDtypeStruct((s, 2 * N_HEADS * HEAD_D), F32)] * 2
        + [jax.ShapeDtypeStruct((s, N_HEADS * HEAD_D), F32)],
        compiler_params=_cparams(("parallel", "arbitrary")),
    )(q, k, kv, o, do, lse)


def _split_dot(x, tri):
    hi = x.astype(BF16)
    lo = (x - hi.astype(F32)).astype(BF16)
    return _dot(hi, tri) + _dot(lo, tri)


TRI = 256


def _tri(kind):
    r0 = lax.broadcasted_iota(jnp.int32, (TRI, TRI), 0)
    c0 = lax.broadcasted_iota(jnp.int32, (TRI, TRI), 1)
    return {"after": r0 > c0, "upto": r0 <= c0, "before": r0 < c0}[kind].astype(BF16)


def _group_sums(x, tri, reverse, split):
    n = x.shape[1] // TRI
    parts = [x[:, g * TRI:(g + 1) * TRI] for g in range(n)]
    tots = [jnp.sum(pp, axis=1, keepdims=True) for pp in parts]
    outs = [None] * n
    run = None
    for g in (reversed(range(n)) if reverse else range(n)):
        inner = _split_dot(parts[g], tri) if split else _dot(parts[g].astype(BF16), tri)
        outs[g] = inner if run is None else inner + run
        run = tots[g] if run is None else run + tots[g]
    return (outs[0] if n == 1 else jnp.concatenate(outs, axis=1)), run


LOG2E = 1.4426950408889634


def _log2_sigmoid(z2):
    return jnp.minimum(z2, 0.0) - jnp.log(1.0 + jnp.exp2(-jnp.abs(z2))) * LOG2E


def _sb_fwd(qkv, *, name):
    s = qkv.shape[0]
    bq, bk = _attn_blocks(s)
    r = bq // bk
    scale = float(HEAD_D ** -0.5)

    def body(q_ref, k_ref, v_ref, o_ref, tot_ref):
        qi = pl.program_id(1)
        qv = q_ref[...]
        nfull = qi * r
        rel = _rel_index(bq, bk)
        after = _tri("after")

        def step(j, carry, d):
            car, acc = carry
            ks = pl.ds(pl.multiple_of(j * bk, bk), bk)
            z = _dot_nt(qv, k_ref[ks, :]) * (scale * LOG2E)
            lb = _log2_sigmoid(z)
            lr = lb - z
            if d is not None:
                mask = rel > d * bk
                lr = jnp.where(mask, lr, 0.0)
            later, tot_lr = _group_sums(lr, after, True, True)
            w = jnp.exp2(lb + later + car)
            if d is not None:
                w = jnp.where(mask, w, 0.0)
            acc = acc + _dot(w.astype(BF16), v_ref[ks, :])
            return car + tot_lr, acc

        carry = (jnp.zeros((bq, 1), F32), jnp.zeros((bq, HEAD_D), F32))
        for d in reversed(range(r)):
            carry = step(nfull + d, carry, d)
        car, acc = lax.fori_loop(0, nfull, lambda i, c: step(nfull - 1 - i, c, None), carry)
        o_ref[...] = acc
        tot_ref[0] = car

    res = lambda off: pl.BlockSpec((s, HEAD_D), functools.partial(lambda h, i, o: (0, h + o), o=off))
    return pl.pallas_call(
        body, name=name,
        grid=(N_HEADS, s // bq),
        in_specs=[pl.BlockSpec((bq, HEAD_D), lambda h, i: (i, h)), res(N_HEADS), res(2 * N_HEADS)],
        out_specs=[pl.BlockSpec((bq, HEAD_D), lambda h, i: (i, h)), pl.BlockSpec((1, bq, 1), lambda h, i: (h, i, 0))],
        out_shape=[jax.ShapeDtypeStruct((s, N_HEADS * HEAD_D), F32), jax.ShapeDtypeStruct((N_HEADS, s, 1), F32)],
        compiler_params=_cparams(("parallel", "arbitrary")),
    )(qkv, qkv, qkv)


def _sb_bwd(qkv, do, tot, *, name):
    s = qkv.shape[0]
    bq, bk = _attn_blocks(s)
    r = bq // bk
    scale = float(HEAD_D ** -0.5)

    def body(q_ref, k_ref, v_ref, do_ref, tot_ref, dq_ref, dk_ref, dv_ref):
        qi = pl.program_id(1)

        @pl.when(qi == 0)
        def _():
            dk_ref[...] = jnp.zeros_like(dk_ref)
            dv_ref[...] = jnp.zeros_like(dv_ref)

        qv = q_ref[...]
        dob = do_ref[...].astype(BF16)
        nfull = qi * r
        rel = _rel_index(bq, bk)
        upto = _tri("upto")
        before = _tri("before")

        def step(j, carry, d):
            rest, gsum, dq = carry
            ks = pl.ds(pl.multiple_of(j * bk, bk), bk)
            kv_ = k_ref[ks, :]
            z = _dot_nt(qv, kv_) * (scale * LOG2E)
            lb = _log2_sigmoid(z)
            lr = lb - z
            if d is not None:
                mask = rel > d * bk
                lr = jnp.where(mask, lr, 0.0)
            sofar, tot_lr = _group_sums(lr, upto, False, True)
            w = jnp.exp2(lb + rest - sofar)
            if d is not None:
                w = jnp.where(mask, w, 0.0)
            g = _dot_nt(dob, v_ref[ks, :]) * w
            gpre, tot_g = _group_sums(g, before, False, False)
            dlr = gsum + gpre
            dz = (g * jnp.exp2(lr) - dlr * jnp.exp2(lb)) * scale
            if d is not None:
                dz = jnp.where(mask, dz, 0.0)
            dzb = dz.astype(BF16)
            dk_ref[ks, :] += _dot_tn(dzb, qv)
            dv_ref[ks, :] += _dot_tn(w.astype(BF16), dob)
            return rest - tot_lr, gsum + tot_g, dq + _dot(dzb, kv_)

        carry = (tot_ref[0], jnp.zeros((bq, 1), F32), jnp.zeros((bq, HEAD_D), F32))
        carry = lax.fori_loop(0, nfull, lambda j, c: step(j, c, None), carry)
        for d in range(r):
            carry = step(nfull + d, carry, d)
        dq_ref[...] = carry[2]

    tile = pl.BlockSpec((bq, HEAD_D), lambda h, i: (i, h))
    res = lambda off: pl.BlockSpec((s, HEAD_D), functools.partial(lambda h, i, o: (0, h + o), o=off))
    wide = jax.ShapeDtypeStruct((s, N_HEADS * HEAD_D), F32)
    return pl.pallas_call(
        body, name=name,
        grid=(N_HEADS, s // bq),
        in_specs=[tile, res(N_HEADS), res(2 * N_HEADS), tile, pl.BlockSpec((1, bq, 1), lambda h, i: (h, i, 0))],
        out_specs=[tile, res(0), res(0)],
        out_shape=[wide] * 3,
        compiler_params=_cparams(("parallel", "arbitrary")),
    )(qkv, qkv, qkv, do, tot)


def _exchange(bufs, *, all_gather, name):
    nb = len(bufs)
    shapes = [((N_DEV,) + b.shape) if all_gather else b.shape for b in bufs]

    def body(*refs):
        srcs, outs = refs[:nb], refs[nb:2 * nb]
        send_sems, recv_sems, local_sems = refs[2 * nb:]
        x, y, c = lax.axis_index("x"), lax.axis_index("y"), lax.axis_index("c")
        me = 4 * x + 2 * y + c

        def src_for(b, idx):
            return srcs[b] if all_gather else srcs[b].at[idx]

        def copy(slot, b, dev, src_idx, dst_idx):
            return pltpu.make_async_remote_copy(
                src_ref=src_for(b, src_idx), dst_ref=outs[b].at[dst_idx], send_sem=send_sems.at[slot * nb + b],
                recv_sem=recv_sems.at[slot * nb + b], device_id=dev, device_id_type=pl.DeviceIdType.MESH)

        owns = [pltpu.make_async_copy(src_for(b, me), outs[b].at[me], local_sems.at[b]) for b in range(nb)]
        for own in owns:
            own.start()
        peers = []
        for k in range(1, N_DEV):
            px = (1 - x) if (k >> 2) & 1 else x
            py = (1 - y) if (k >> 1) & 1 else y
            pc = (1 - c) if k & 1 else c
            peers.append((k - 1, (px, py, pc), 4 * px + 2 * py + pc))
        sends = [copy(slot, b, dev, idx, me) for slot, dev, idx in peers for b in range(nb)]
        for cp in sends:
            cp.start()
        for slot, dev, idx in peers:
            for b in range(nb):
                copy(slot, b, dev, idx, idx).wait_recv()
        for cp in sends:
            cp.wait_send()
        for own in owns:
            own.wait()

    any_spec = pl.BlockSpec(memory_space=pl.ANY)
    return pl.pallas_call(
        body, name=name,
        in_specs=[any_spec] * nb,
        out_specs=[any_spec] * nb,
        out_shape=[jax.ShapeDtypeStruct(sh, b.dtype) for sh, b in zip(shapes, bufs)],
        scratch_shapes=[pltpu.SemaphoreType.DMA(((N_DEV - 1) * nb,)), pltpu.SemaphoreType.DMA(((N_DEV - 1) * nb,)),
                        pltpu.SemaphoreType.DMA((nb,))],
    )(*bufs)


_HBM = pl.BlockSpec(memory_space=pltpu.HBM)
_SEM = pl.BlockSpec(memory_space=pltpu.SEMAPHORE)
_EFFECT = pltpu.SideEffectType.DATAFLOW_SIDE_EFFECTING


def _peers():
    x, y, c = lax.axis_index("x"), lax.axis_index("y"), lax.axis_index("c")
    peers = []
    for k in range(1, N_DEV):
        px = (1 - x) if (k >> 2) & 1 else x
        py = (1 - y) if (k >> 1) & 1 else y
        pc = (1 - c) if k & 1 else c
        peers.append((k - 1, (px, py, pc), 4 * px + 2 * py + pc))
    return 4 * x + 2 * y + c, peers


def _exchange_start(bufs, *, all_gather, name):
    nb = len(bufs)
    shapes = [((N_DEV,) + b.shape) if all_gather else b.shape for b in bufs]
    ncp = (N_DEV - 1) * nb

    def body(*refs):
        srcs, lands = refs[:nb], refs[nb:2 * nb]
        send_sems, recv_sems = refs[2 * nb], refs[2 * nb + 1]
        token = refs[-1]
        me, peers = _peers()
        for slot, dev, idx in peers:
            for b in range(nb):
                pltpu.make_async_remote_copy(
                    src_ref=srcs[b] if all_gather else srcs[b].at[idx], dst_ref=lands[b].at[me],
                    send_sem=send_sems.at[slot * nb + b], recv_sem=recv_sems.at[slot * nb + b],
                    device_id=dev, device_id_type=pl.DeviceIdType.MESH).start()
        token[...] = jnp.zeros_like(token)

    ins = [pltpu.with_memory_space_constraint(b, pltpu.HBM) for b in bufs]
    ins += [pltpu.with_memory_space_constraint(lax.empty(sh, b.dtype), pltpu.HBM) for sh, b in zip(shapes, bufs)]
    outs = pl.pallas_call(
        body, name=name,
        out_shape=(pltpu.SemaphoreType.DMA((ncp,)), pltpu.SemaphoreType.DMA((ncp,)),
                   *[pltpu.HBM(b.shape, b.dtype) for b in bufs], *[pltpu.HBM(sh, b.dtype) for sh, b in zip(shapes, bufs)],
                   jax.ShapeDtypeStruct((8, 128), F32)),
        in_specs=[_HBM] * (2 * nb),
        out_specs=(_SEM, _SEM, *([_HBM] * (2 * nb)), pl.BlockSpec(memory_space=pltpu.VMEM)),
        input_output_aliases={i: 2 + i for i in range(2 * nb)},
        compiler_params=pltpu.CompilerParams(has_side_effects=_EFFECT),
    )(*ins)
    return outs[:-1], outs[-1]


def _exchange_finish(started, after, *, all_gather, name):
    nb = (len(started) - 2) // 2
    send_sems, recv_sems = started[0], started[1]
    thru = started[2:]

    def body(*refs):
        srcs, lands = refs[:nb], refs[nb:2 * nb]
        ssem, rsem = refs[2 * nb], refs[2 * nb + 1]
        me, peers = _peers()
        for slot, dev, idx in peers:
            for b in range(nb):
                cp = pltpu.make_async_remote_copy(
                    src_ref=srcs[b] if all_gather else srcs[b].at[idx], dst_ref=lands[b].at[idx],
                    send_sem=ssem.at[slot * nb + b], recv_sem=rsem.at[slot * nb + b],
                    device_id=dev, device_id_type=pl.DeviceIdType.MESH)
                cp.wait_send()
                cp.wait_recv()

    done = pl.pallas_call(
        body, name=name,
        out_shape=tuple(pltpu.HBM(t.shape, t.dtype) for t in thru),
        in_specs=[_HBM] * (2 * nb) + [_SEM, _SEM, pl.BlockSpec(memory_space=pl.ANY)],
        out_specs=[_HBM] * (2 * nb),
        input_output_aliases={i: i for i in range(2 * nb)},
        compiler_params=pltpu.CompilerParams(has_side_effects=_EFFECT),
    )(*thru, send_sems, recv_sems, after)
    srcs, lands = done[:nb], done[nb:]
    me = 4 * lax.axis_index("x") + 2 * lax.axis_index("y") + lax.axis_index("c")
    outs = []
    for s_, l_ in zip(srcs, lands):
        own = s_[None] if all_gather else lax.dynamic_slice_in_dim(s_, me, 1, axis=0)
        outs.append(lax.dynamic_update_slice_in_dim(l_, own, me, axis=0))
    return outs


def _sum8(buf, *, name):
    _, r, c = buf.shape
    tr = _pick(r, (256, 128, 64, 32, 16, 8))

    def body(b_ref, o_ref):
        acc = b_ref[0].astype(F32)
        for q in range(1, N_DEV):
            acc = acc + b_ref[q].astype(F32)
        o_ref[...] = acc

    return pl.pallas_call(
        body, name=name,
        grid=(r // tr,),
        in_specs=[pl.BlockSpec((N_DEV, tr, c), lambda i: (0, i, 0))],
        out_specs=pl.BlockSpec((tr, c), lambda i: (i, 0)),
        out_shape=jax.ShapeDtypeStruct((r, c), F32),
        compiler_params=_cparams(("parallel",)),
    )(buf)


def _pack_rows(flat, mult):
    n = flat.shape[0]
    chunk = mult * PACK_COLS
    tot = -(-n // chunk) * chunk
    return jnp.pad(flat, (0, tot - n)).reshape(tot // PACK_COLS, PACK_COLS)


_ADAM_BC1 = 1.0 - ADAM_B1 ** ADAM_STEP
_ADAM_BC2 = 1.0 - ADAM_B2 ** ADAM_STEP


def _adamw_math(wv, gv, mv, vv):
    mn = ADAM_B1 * mv + (1.0 - ADAM_B1) * gv
    vn = ADAM_B2 * vv + (1.0 - ADAM_B2) * (gv * gv)
    m_hat = mn / _ADAM_BC1
    v_hat = vn / _ADAM_BC2
    delta = -ADAM_LR * (m_hat / (jnp.sqrt(v_hat) + ADAM_EPS) + ADAM_WD * wv)
    return delta, mn, vn


def _adamw(w, g, m, v, *, name):
    shape = w.shape
    cols = shape[-1]
    to2d = lambda a: a.reshape(-1, cols)
    d, mn, vn = _ew(_adamw_math, [to2d(w), to2d(g), to2d(m), to2d(v)], [(cols, F32)] * 3, name=name)
    return d.reshape(shape), mn.reshape(shape), vn.reshape(shape)


def _adamw_sharded(w, parts, m, v, *, name):
    depth = w.shape[0]
    cols = w.shape[-1]
    rows = w[0].size // cols
    to3 = lambda a: a.reshape(depth, rows, cols)
    w3, m3, v3 = to3(w), to3(m), to3(v)
    bpr = cols * (3 * 4 + N_DEV * parts[0].dtype.itemsize + 4 * 4)
    tm = _row_tile(rows, bpr)
    outs = None
    for li in range(depth):
        carried = li > 0

        def body(*refs):
            w_ref, p_ref, m_ref, v_ref = refs[:4]
            g_ref, d_ref, mo_ref, vo_ref = refs[-4:]
            gv = p_ref[0].astype(F32)
            for q in range(1, N_DEV):
                gv = gv + p_ref[q].astype(F32)
            delta, mn, vn = _adamw_math(w_ref[0], gv, m_ref[0], v_ref[0])
            g_ref[0] = gv
            d_ref[0] = delta
            mo_ref[0] = mn
            vo_ref[0] = vn

        lay = pl.BlockSpec((1, tm, cols), functools.partial(lambda i, l: (l, i, 0), l=li))
        any_spec = pl.BlockSpec(memory_space=pl.ANY)
        ins = [w3, parts[li].reshape(N_DEV, rows, cols), m3, v3] + (list(outs) if carried else [])
        outs = pl.pallas_call(
            body, name=f"{name}_l{li}",
            grid=(rows // tm,),
            in_specs=[lay, pl.BlockSpec((N_DEV, tm, cols), lambda i: (0, i, 0)), lay, lay]
            + ([any_spec] * 4 if carried else []),
            out_specs=[lay] * 4,
            out_shape=[jax.ShapeDtypeStruct((depth, rows, cols), F32)] * 4,
            input_output_aliases={4 + q: q for q in range(4)} if carried else {},
            compiler_params=_cparams(("parallel",)),
        )(*ins)
    return [o.reshape(w.shape) for o in outs]


def _prep_in(w_in):
    rope_cols = jnp.pad(w_in[:, 2048:2048 + MLA_ROPE], ((0, 0), (0, 128 - MLA_ROPE)))
    return dict(wa=w_in[:, :2048], wb=w_in[:, 2112:5184], wc=w_in[:, 5184:8256], wd=w_in[:, 8256:14400], we=rope_cols)


def _prep_rest(full):
    wq = full["mla_w_uq"].reshape(512, N_HEADS, MLA_NOPE + MLA_ROPE)
    wkv = full["mla_w_ukv"].reshape(512, N_HEADS, 2 * HEAD_D)
    return dict(
        w_uq=jnp.pad(wq, ((0, 0), (0, 0), (0, 2 * HEAD_D - MLA_NOPE - MLA_ROPE))).reshape(512, 2 * N_HEADS * HEAD_D),
        w_ukv=jnp.concatenate([wkv[:, :, :HEAD_D].reshape(512, 1024), wkv[:, :, HEAD_D:].reshape(512, 1024)], axis=1),
        w_glu=full["ssm_w_glu"], w_branch=full["w_branch"], w_out=full["w_out"],
        w_ple_gate=full["w_ple_gate"], w_ple_proj=full["w_ple_proj"])


def _prep_ssm(sp):
    ab_re, ab_im, bb_re, bb_im = _s5_disc(sp["lam_re"], sp["lam_im"], sp["log_dt"], sp["b_re"], sp["b_im"])
    g4 = lambda a: a.reshape(SSM_GB, SSM_GB, a.shape[1], a.shape[2])
    bbd_re = _blockdiag(jnp.swapaxes(g4(bb_re), 2, 3)).astype(BF16)
    bbd_im = _blockdiag(jnp.swapaxes(g4(bb_im), 2, 3)).astype(BF16)
    cd_re = _blockdiag(jnp.swapaxes(g4(sp["c_re"]), 2, 3)).astype(BF16)
    cd_im = _blockdiag(jnp.swapaxes(g4(sp["c_im"]), 2, 3)).astype(BF16)
    return dict(bb_re=bbd_re, bb_im=bbd_im, bbt_re=jnp.swapaxes(bbd_re, 1, 2), bbt_im=jnp.swapaxes(bbd_im, 1, 2),
                c_re=cd_re, c_im=cd_im, ct_re=jnp.swapaxes(cd_re, 1, 2), ct_im=jnp.swapaxes(cd_im, 1, 2),
                dsk=sp["d"].reshape(1, BRANCH_W))


GELU_C = math.sqrt(2.0 / math.pi)


def _gelu(x):
    return 0.5 * x * (1.0 + jnp.tanh(GELU_C * (x + 0.044715 * x * x * x)))


def _gelu_grad(x):
    t = jnp.tanh(GELU_C * (x + 0.044715 * x * x * x))
    return 0.5 * (1.0 + t) + 0.5 * x * (1.0 - t * t) * GELU_C * (1.0 + 3.0 * 0.044715 * x * x)


def _layer_fwd(x_in, p_l, hooks, ssm, tabs, sm, rope, li):
    tg = lambda nm: nm
    ct, sp_, sm_ = rope
    h = _rmsnorm(x_in, sm["ln_g"], name=tg("ln"))
    w = dict(hooks.weights_in(li, x_in))
    seg_a = _mm(h, w["wa"], name=tg("in_a"))
    seg_b = _mm(h, w["wb"], name=tg("in_b"), out_dtypes=(BF16,))
    seg_c = _mm(h, w["wc"], name=tg("in_c"))
    seg_d = _mm(h, w["wd"], name=tg("in_d"))
    seg_e = _mm(h, w["we"], name=tg("in_e"))

    o_sb, tot = _sb_fwd(seg_b, name=tg("sb_fwd"))
    w.update(hooks.weights_rest(li, o_sb))

    u = seg_a[:, :BRANCH_W]
    y_pre, hc_re, hc_im = _s5_fwd(u, ssm["bb_re"], ssm["bb_im"], ssm["c_re"], ssm["c_im"], ssm["dsk"], tabs,
                                  name=tg("s5_fwd"))
    yg = _ew(_gelu, [y_pre], [(BRANCH_W, F32)], name=tg("gelu"))
    y_ssm, z_glu = _mm(yg, w["w_glu"], name=tg("glu"), extras=[(yg, 0)],
                       epilogue=lambda acc, ygv: (ygv * _sigmoid(acc), acc), out_dtypes=(F32, F32))

    q_lat = seg_a[:, 1024:1536]
    kv_lat = seg_a[:, 1536:2048]
    qn = _rmsnorm(q_lat, sm["mla_g_q"], name=tg("q_norm"))
    kvn = _rmsnorm(kv_lat, sm["mla_g_kv"], name=tg("kv_norm"))
    q_raw = _mm(qn, w["w_uq"], name=tg("uq"))
    kv_b = _mm(kvn, w["w_ukv"], name=tg("ukv"), out_dtypes=(BF16,))

    def rope_q(qv, c_, p_, m_):
        outs = []
        for hh in range(N_HEADS):
            outs.append(qv[:, 256 * hh:256 * hh + 128])
            outs.append(_rope_group(qv[:, 256 * hh + 128:256 * hh + 256], c_, p_, m_))
        return jnp.concatenate(outs, axis=1)

    def rope_k(kvv, kr, c_, p_, m_):
        kr_rot = _rope_group(kr, c_, p_, m_).astype(BF16)
        outs = []
        for hh in range(N_HEADS):
            outs.append(kvv[:, 128 * hh:128 * hh + 128])
            outs.append(kr_rot)
        return jnp.concatenate(outs, axis=1)

    q_b = _ew(rope_q, [q_raw, ct, sp_, sm_], [(2048, BF16)], name=tg("rope_q"))
    k_b = _ew(rope_k, [kv_b, seg_e, ct, sp_, sm_], [(2048, BF16)], name=tg("rope_k"))
    o_mla, lse = _mla_fwd(q_b, k_b, kv_b, name=tg("mla_fwd"))

    def gate_fn(a, b, c_, gp):
        ys = jnp.concatenate([a, b, c_], axis=1)
        return ys * (gp * _sigmoid(gp))

    ys = _ew(gate_fn, [y_ssm, o_mla, o_sb, seg_c], [(3 * BRANCH_W, BF16)], name=tg("gate"))
    merged = None
    bos = []
    for b in range(3):
        if b == 0:
            ep = lambda acc, ml: (_sigmoid(ml) * acc, acc)
            ex = [(seg_d, 0)]
        else:
            ep = lambda acc, ml, prev: (prev + _sigmoid(ml) * acc, acc)
            ex = [(seg_d, b * D_MODEL), (merged, 0)]
        merged, bo = _mm(ys[:, b * BRANCH_W:(b + 1) * BRANCH_W], w["w_branch"][b], name=tg(f"branch{b}"),
                         extras=ex, epilogue=ep, out_dtypes=(F32, BF16))
        bos.append(bo)
    merged_b = merged.astype(BF16)
    x1 = _mm(merged_b, w["w_out"], name=tg("out"), extras=[(x_in, 0)], epilogue=lambda acc, xv: (xv + acc,))

    hn2 = _rmsnorm(x1, sm["ple_g"], name=tg("ple_norm"))
    e = _mm(p_l, w["w_ple_proj"], name=tg("ple_proj"))
    x2, gl = _mm(hn2, w["w_ple_gate"], name=tg("ple_gate"), extras=[(x1, 0), (e, 0)],
                 epilogue=lambda acc, xv, ev: (xv + _sigmoid(acc) * ev, acc), out_dtypes=(F32, F32))
    saved = dict(x_in=x_in, h=h, seg_a=seg_a, seg_b=seg_b, seg_c=seg_c, seg_d=seg_d, u=u, y_pre=y_pre, hc_re=hc_re,
                 hc_im=hc_im, yg=yg, y_ssm=y_ssm, z_glu=z_glu, q_lat=q_lat, kv_lat=kv_lat, qn=qn, kvn=kvn, q_b=q_b,
                 kv_b=kv_b, k_b=k_b, o_mla=o_mla, lse=lse, o_sb=o_sb, tot=tot, ys=ys, bos=bos, merged_b=merged_b,
                 x1=x1, hn2=hn2, e=e, gl=gl, p_l=p_l, w=w)
    return x2, saved


def _layer_bwd(dx2, sv, hooks, ssm, tabs, sm, rope, li):
    tg = lambda nm: nm
    w = sv["w"]
    ct, sp_, sm_ = rope
    gw, gs = {}, {}

    def ple_fn(d, ev, glv):
        gt = _sigmoid(glv)
        return d * ev * gt * (1.0 - gt), d * gt

    dgl, de = _ew(ple_fn, [dx2, sv["e"], sv["gl"]], [(D_MODEL, BF16), (D_MODEL, BF16)], name=tg("ple_bwd"))
    gw["w_ple_proj"] = _mm(sv["p_l"], de, ta=True, out_dtypes=(BF16,), name=tg("d_ple_proj"))
    gw["w_ple_gate"] = _mm(sv["hn2"], dgl, ta=True, out_dtypes=(BF16,), name=tg("d_ple_gate"))
    dhn2 = _mm(dgl, w["w_ple_gate"], tb=True, name=tg("ple_gate_t"))
    dx1, gs["ple_g"] = _rmsnorm_bwd(sv["x1"], sm["ple_g"], dhn2, dx2, name=tg("ple_norm_bwd"))

    dx1_b = dx1.astype(BF16)
    gw["w_out"] = _mm(sv["merged_b"], dx1_b, ta=True, out_dtypes=(BF16,), name=tg("d_out"))
    dmerged = _mm(dx1_b, w["w_out"], tb=True, name=tg("out_t"))

    def merge_fn(dm, ml, b0, b1, b2):
        dbo, dml = [], []
        for b, bo in enumerate((b0, b1, b2)):
            gt = _sigmoid(ml[:, b * D_MODEL:(b + 1) * D_MODEL])
            dbo.append(dm * gt)
            dml.append(dm * bo.astype(F32) * gt * (1.0 - gt))
        return jnp.concatenate(dbo, axis=1), jnp.concatenate(dml, axis=1)

    dbo, dseg_d = _ew(merge_fn, [dmerged, sv["seg_d"]] + sv["bos"], [(3 * D_MODEL, BF16), (3 * D_MODEL, BF16)],
                      name=tg("merge_bwd"))
    dys, dwb = [], []
    for b in range(3):
        dbo_b = dbo[:, b * D_MODEL:(b + 1) * D_MODEL]
        dwb.append(_mm(sv["ys"][:, b * BRANCH_W:(b + 1) * BRANCH_W], dbo_b, ta=True, out_dtypes=(BF16,), name=tg(f"d_branch{b}")))
        dys.append(_mm(dbo_b, w["w_branch"][b], tb=True, name=tg(f"branch{b}_t")))
    gw["w_branch"] = jnp.stack(dwb)

    def gate_bwd(d0, d1, d2, a, b, c_, gp):
        dy_all = jnp.concatenate([d0, d1, d2], axis=1)
        y_all = jnp.concatenate([a, b, c_], axis=1)
        sg = _sigmoid(gp)
        silu = gp * sg
        dsilu = sg * (1.0 + gp * (1.0 - sg))
        dyv = dy_all * silu
        return (dyv[:, :BRANCH_W], dyv[:, BRANCH_W:2 * BRANCH_W], dyv[:, 2 * BRANCH_W:], dy_all * y_all * dsilu)

    dy_ssm, do_mla, do_sb, dseg_c = _ew(
        gate_bwd, dys + [sv["y_ssm"], sv["o_mla"], sv["o_sb"], sv["seg_c"]],
        [(BRANCH_W, F32), (BRANCH_W, F32), (BRANCH_W, F32), (3 * BRANCH_W, BF16)], name=tg("gate_bwd"))

    dq_sb, dk_sb, dv_sb = _sb_bwd(sv["seg_b"], do_sb, sv["tot"], name=tg("sb_bwd"))
    dseg_b = jnp.concatenate([dq_sb, dk_sb, dv_sb], axis=1).astype(BF16)

    dq_, dk_, dv_ = _mla_bwd(sv["q_b"], sv["k_b"], sv["kv_b"], sv["o_mla"], do_mla, sv["lse"], name=tg("mla_bwd"))

    def unrope_q(dq, c_, p_, m_):
        outs = []
        for hh in range(N_HEADS):
            outs.append(dq[:, 256 * hh:256 * hh + 128])
            outs.append(_rope_group_t(dq[:, 256 * hh + 128:256 * hh + 256], c_, p_, m_))
        return jnp.concatenate(outs, axis=1)

    dq_raw = _ew(unrope_q, [dq_, ct, sp_, sm_], [(2048, BF16)], name=tg("unrope_q"))

    def unrope_k(dk, dvv, c_, p_, m_):
        tot_ = dk[:, 128:256]
        for hh in range(1, N_HEADS):
            tot_ = tot_ + dk[:, 256 * hh + 128:256 * hh + 256]
        dkn = [dk[:, 256 * hh:256 * hh + 128] for hh in range(N_HEADS)]
        return jnp.concatenate(dkn + [dvv], axis=1), _rope_group_t(tot_, c_, p_, m_)

    dkv_raw, dseg_e = _ew(unrope_k, [dk_, dv_, ct, sp_, sm_], [(2048, BF16), (128, BF16)], name=tg("unrope_k"))
    dw_uq = _mm(sv["qn"], dq_raw, ta=True, out_dtypes=(BF16,), name=tg("d_uq"))
    dw_ukv = _mm(sv["kvn"], dkv_raw, ta=True, out_dtypes=(BF16,), name=tg("d_ukv"))
    dqn = _mm(dq_raw, w["w_uq"], tb=True, name=tg("uq_t"))
    dkvn = _mm(dkv_raw, w["w_ukv"], tb=True, name=tg("ukv_t"))
    dq_lat, gs["mla_g_q"] = _rmsnorm_bwd(sv["q_lat"], sm["mla_g_q"], dqn, None, name=tg("q_norm_bwd"))
    dkv_lat, gs["mla_g_kv"] = _rmsnorm_bwd(sv["kv_lat"], sm["mla_g_kv"], dkvn, None, name=tg("kv_norm_bwd"))
    gw["mla_w_uq"] = dw_uq.reshape(512, N_HEADS, 2 * HEAD_D)[:, :, :MLA_NOPE + MLA_ROPE].reshape(
        512, N_HEADS * (MLA_NOPE + MLA_ROPE))
    dwk = dw_ukv[:, :1024].reshape(512, N_HEADS, HEAD_D)
    dwv = dw_ukv[:, 1024:].reshape(512, N_HEADS, HEAD_D)
    gw["mla_w_ukv"] = jnp.concatenate([dwk, dwv], axis=2).reshape(512, 2 * N_HEADS * HEAD_D)

    def glu_bwd(d, ygv, zv):
        sg = _sigmoid(zv)
        return d * ygv * sg * (1.0 - sg), d * sg

    dz, dyg0 = _ew(glu_bwd, [dy_ssm, sv["yg"], sv["z_glu"]], [(BRANCH_W, BF16), (BRANCH_W, F32)], name=tg("glu_bwd"))
    gw["ssm_w_glu"] = _mm(sv["yg"], dz, ta=True, out_dtypes=(BF16,), name=tg("d_glu"))
    dy_pre = _mm(dz, w["w_glu"], tb=True, name=tg("glu_t"), extras=[(dyg0, 0), (sv["y_pre"], 0)],
                 epilogue=lambda acc, d0, yp: ((acc + d0) * _gelu_grad(yp),))
    token = hooks.grads_rest(li, gw)
    dsk = ssm["dsk"] if token is None else ssm["dsk"] + token[0, 0]
    du, dbb_re, dbb_im, dc_re, dc_im, da_re, da_im, dd = _s5_bwd(
        sv["u"], dy_pre, sv["hc_re"], sv["hc_im"], ssm["bb_re"], ssm["bb_im"], ssm["bbt_re"], ssm["bbt_im"],
        ssm["c_re"], ssm["c_im"], ssm["ct_re"], ssm["ct_im"], dsk, tabs, name=tg("s5_bwd"))
    gs["ssm_d"] = dd.reshape(BRANCH_W)
    gs["ssm_c_re"] = jnp.swapaxes(_blockdiag_extract(dc_re, SSM_STATE, SSM_GROUP), 2, 3).reshape(
        SSM_GROUPS, SSM_GROUP, SSM_STATE)
    gs["ssm_c_im"] = jnp.swapaxes(_blockdiag_extract(dc_im, SSM_STATE, SSM_GROUP), 2, 3).reshape(
        SSM_GROUPS, SSM_GROUP, SSM_STATE)
    gs["_dbb_re"] = jnp.swapaxes(_blockdiag_extract(dbb_re, SSM_GROUP, SSM_STATE), 2, 3).reshape(
        SSM_GROUPS, SSM_STATE, SSM_GROUP)
    gs["_dbb_im"] = jnp.swapaxes(_blockdiag_extract(dbb_im, SSM_GROUP, SSM_STATE), 2, 3).reshape(
        SSM_GROUPS, SSM_STATE, SSM_GROUP)
    gs["_da_re"] = da_re.reshape(SSM_GROUPS, SSM_STATE)
    gs["_da_im"] = da_im.reshape(SSM_GROUPS, SSM_STATE)

    dseg_a = jnp.concatenate([du, dq_lat, dkv_lat], axis=1).astype(BF16)
    hb = sv["h"]
    dwa = _mm(hb, dseg_a, ta=True, out_dtypes=(BF16,), name=tg("d_in_a"))
    dwb_ = _mm(hb, dseg_b, ta=True, out_dtypes=(BF16,), name=tg("d_in_b"))
    dwc = _mm(hb, dseg_c, ta=True, out_dtypes=(BF16,), name=tg("d_in_c"))
    dwd = _mm(hb, dseg_d, ta=True, out_dtypes=(BF16,), name=tg("d_in_d"))
    dwe = _mm(hb, dseg_e, ta=True, out_dtypes=(BF16,), name=tg("d_in_e"))
    gw["w_in"] = jnp.concatenate([dwa, dwe[:, :MLA_ROPE], dwb_, dwc, dwd], axis=1)
    token = hooks.grads_in(li, gw["w_in"])
    wa = w["wa"] if token is None else w["wa"] + token[0, 0].astype(BF16)
    add = lambda acc, prev: (acc + prev,)
    dh = _mm(dseg_a, wa, tb=True, name=tg("in_a_t"))
    dh = _mm(dseg_b, w["wb"], tb=True, name=tg("in_b_t"), extras=[(dh, 0)], epilogue=add)
    dh = _mm(dseg_c, w["wc"], tb=True, name=tg("in_c_t"), extras=[(dh, 0)], epilogue=add)
    dh = _mm(dseg_d, w["wd"], tb=True, name=tg("in_d_t"), extras=[(dh, 0)], epilogue=add)
    dh = _mm(dseg_e, w["we"], tb=True, name=tg("in_e_t"), extras=[(dh, 0)], epilogue=add)
    dx_in, gs["ln_g"] = _rmsnorm_bwd(sv["x_in"], sm["ln_g"], dh, dx1, name=tg("ln_bwd"))
    return dx_in, gw, gs


class _LocalWeights:
    def __init__(self, fulls):
        self.fulls = fulls

    def weights_in(self, li, after):
        return _prep_in(self.fulls[li]["w_in"])

    def weights_rest(self, li, after):
        return _prep_rest(self.fulls[li])

    def grads_rest(self, li, gw):
        return None

    def grads_in(self, li, g):
        return None

    def grads_layer(self, li, gw):
        return None


def _local_step(x, p, target, hooks, small):
    s = x.shape[0]
    depth = small["ln_g"].shape[0]
    rope = _rope_tables(s)
    ssms, tabss, sms, saved = [], [], [], []
    xc = x
    for li in range(depth):
        sp = dict(lam_re=small["ssm_lam_re"][li], lam_im=small["ssm_lam_im"][li], log_dt=small["ssm_log_dt"][li],
                  b_re=small["ssm_b_re"][li], b_im=small["ssm_b_im"][li], c_re=small["ssm_c_re"][li],
                  c_im=small["ssm_c_im"][li], d=small["ssm_d"][li])
        ssm = _prep_ssm(sp)
        tabs = _s5_tables(sp["lam_re"], sp["lam_im"], sp["log_dt"])
        sm = dict(ln_g=small["ln_g"][li], mla_g_q=small["mla_g_q"][li], mla_g_kv=small["mla_g_kv"][li],
                  ple_g=small["ple_g"][li])
        xc, sv = _layer_fwd(xc, p[li], hooks, ssm, tabs, sm, rope, li)
        ssms.append(ssm), tabss.append(tabs), sms.append(sm), saved.append(sv)
    dx, dgf, loss = _loss_head(xc, small["final_g"], target, name="loss_head")
    gws, gss = [None] * depth, [None] * depth
    token = None
    for li in reversed(range(depth)):
        sm = sms[li] if token is None else dict(sms[li], ple_g=sms[li]["ple_g"] + token[0, 0])
        dx, gw, gs = _layer_bwd(dx, saved[li], hooks, ssms[li], tabss[li], sm, rope, li)
        token = hooks.grads_layer(li, gw)
        args = (small["ssm_lam_re"][li], small["ssm_lam_im"][li], small["ssm_log_dt"][li], small["ssm_b_re"][li],
                small["ssm_b_im"][li])
        _, vjp = jax.vjp(_s5_disc, *args)
        g_lr, g_li, g_dt, g_br, g_bi = vjp((gs.pop("_da_re"), gs.pop("_da_im"), gs.pop("_dbb_re"), gs.pop("_dbb_im")))
        gs.update(ssm_lam_re=g_lr, ssm_lam_im=g_li, ssm_log_dt=g_dt, ssm_b_re=g_br, ssm_b_im=g_bi)
        gws[li], gss[li] = gw, gs
    gsmall = {nm: jnp.stack([gss[li][nm].reshape(small[nm].shape[1:]) for li in range(depth)])
              for nm in SMALL if nm != "final_g"}
    gsmall["final_g"] = dgf.reshape(-1)
    return loss[0, 0], dx, gws, gsmall


REST = tuple(t for t in SHARDED if t[0] != "w_in")


def _to_full(spec, got):
    _, fshape, ax = spec
    return jnp.moveaxis(got, 0, ax).reshape(fshape)


def _to_slabs(spec, g):
    _, fshape, ax = spec
    g = g.reshape(tuple(fshape[:ax]) + (N_DEV, fshape[ax] // N_DEV) + tuple(fshape[ax + 1:]))
    return jnp.moveaxis(g, ax, 0)


class _Exchanged:
    def __init__(self, shards):
        bf = lambda nm, li: shards[nm][li].astype(BF16)
        w_in0 = _exchange([bf("w_in", 0)], all_gather=True, name="gather_w_in")[0]
        self.w_in0 = _to_full(SHARDED[0], w_in0)
        first = bf(REST[-1][0], 0) + (w_in0[0, 0, 0] * 0).astype(BF16)
        rest0 = [bf(nm, 0) for nm, _, _ in REST[:-1]] + [first]
        self.rest0, tok_a = _exchange_start(rest0, all_gather=True, name="gather_rest_start")
        all1 = [bf(nm, 1) for nm, _, _ in SHARDED[:-1]] + [bf(SHARDED[-1][0], 1) + tok_a[0, 0].astype(BF16)]
        self.all1, tok_b = _exchange_start(all1, all_gather=True, name="gather_next_start")
        self.token = tok_a[0, 0] + tok_b[0, 0]
        self.full1 = None
        self.sent = {}

    def weights_in(self, li, after):
        if li == 0:
            return _prep_in(self.w_in0)
        got = _exchange_finish(self.all1, after, all_gather=True, name="gather_next_finish")
        self.full1 = {t[0]: _to_full(t, g) for t, g in zip(SHARDED, got)}
        return _prep_in(self.full1["w_in"])

    def weights_rest(self, li, after):
        if li == 1:
            return _prep_rest(self.full1)
        got = _exchange_finish(self.rest0, after, all_gather=True, name="gather_rest_finish")
        return _prep_rest({t[0]: _to_full(t, g) for t, g in zip(REST, got)})

    def grads_rest(self, li, gw):
        if li != 0:
            return None
        self.sent["rest0"], token = _exchange_start([_to_slabs(t, gw[t[0]]) for t in REST], all_gather=False,
                                                    name="scatter_rest_start")
        return token

    def grads_in(self, li, g):
        if li != 0:
            return None
        self.sent["w_in0"], token = _exchange_start([_to_slabs(SHARDED[0], g)], all_gather=False,
                                                    name="scatter_w_in_start")
        return token

    def grads_layer(self, li, gw):
        if li == 0:
            return None
        self.sent["all1"], token = _exchange_start([_to_slabs(t, gw[t[0]]) for t in SHARDED], all_gather=False,
                                                   name="scatter_prev_start")
        return token

    def partial_grads(self, after):
        got1 = _exchange_finish(self.sent["all1"], after, all_gather=False, name="scatter_prev_finish")
        got0 = _exchange_finish(self.sent["rest0"], after, all_gather=False, name="scatter_rest_finish")
        w_in0 = _exchange_finish(self.sent["w_in0"], after, all_gather=False, name="scatter_w_in_finish")[0]
        p0 = {t[0]: g for t, g in zip(REST, got0)}
        p0["w_in"] = w_in0
        return [p0, {t[0]: g for t, g in zip(SHARDED, got1)}]


def kernel(x, p, ln_g, w_in, ssm_lam_re, ssm_lam_im, ssm_log_dt, ssm_b_re, ssm_b_im, ssm_c_re, ssm_c_im, ssm_d, ssm_w_glu, mla_g_q, mla_g_kv, mla_w_uq, mla_w_ukv, w_branch, w_out, ple_g, w_ple_gate, w_ple_proj, final_g, loss_target, m_ln_g, m_w_in, m_ssm_lam_re, m_ssm_lam_im, m_ssm_log_dt, m_ssm_b_re, m_ssm_b_im, m_ssm_c_re, m_ssm_c_im, m_ssm_d, m_ssm_w_glu, m_mla_g_q, m_mla_g_kv, m_mla_w_uq, m_mla_w_ukv, m_w_branch, m_w_out, m_ple_g, m_w_ple_gate, m_w_ple_proj, m_final_g, v_ln_g, v_w_in, v_ssm_lam_re, v_ssm_lam_im, v_ssm_log_dt, v_ssm_b_re, v_ssm_b_im, v_ssm_c_re, v_ssm_c_im, v_ssm_d, v_ssm_w_glu, v_mla_g_q, v_mla_g_kv, v_mla_w_uq, v_mla_w_ukv, v_w_branch, v_w_out, v_ple_g, v_w_ple_gate, v_w_ple_proj, v_final_g):
    weights = dict(ln_g=ln_g, w_in=w_in, ssm_lam_re=ssm_lam_re, ssm_lam_im=ssm_lam_im, ssm_log_dt=ssm_log_dt,
                   ssm_b_re=ssm_b_re, ssm_b_im=ssm_b_im, ssm_c_re=ssm_c_re, ssm_c_im=ssm_c_im, ssm_d=ssm_d,
                   ssm_w_glu=ssm_w_glu, mla_g_q=mla_g_q, mla_g_kv=mla_g_kv, mla_w_uq=mla_w_uq, mla_w_ukv=mla_w_ukv,
                   w_branch=w_branch, w_out=w_out, ple_g=ple_g, w_ple_gate=w_ple_gate, w_ple_proj=w_ple_proj,
                   final_g=final_g)
    mom_m = dict(ln_g=m_ln_g, w_in=m_w_in, ssm_lam_re=m_ssm_lam_re, ssm_lam_im=m_ssm_lam_im, ssm_log_dt=m_ssm_log_dt,
                 ssm_b_re=m_ssm_b_re, ssm_b_im=m_ssm_b_im, ssm_c_re=m_ssm_c_re, ssm_c_im=m_ssm_c_im, ssm_d=m_ssm_d,
                 ssm_w_glu=m_ssm_w_glu, mla_g_q=m_mla_g_q, mla_g_kv=m_mla_g_kv, mla_w_uq=m_mla_w_uq,
                 mla_w_ukv=m_mla_w_ukv, w_branch=m_w_branch, w_out=m_w_out, ple_g=m_ple_g, w_ple_gate=m_w_ple_gate,
                 w_ple_proj=m_w_ple_proj, final_g=m_final_g)
    mom_v = dict(ln_g=v_ln_g, w_in=v_w_in, ssm_lam_re=v_ssm_lam_re, ssm_lam_im=v_ssm_lam_im, ssm_log_dt=v_ssm_log_dt,
                 ssm_b_re=v_ssm_b_re, ssm_b_im=v_ssm_b_im, ssm_c_re=v_ssm_c_re, ssm_c_im=v_ssm_c_im, ssm_d=v_ssm_d,
                 ssm_w_glu=v_ssm_w_glu, mla_g_q=v_mla_g_q, mla_g_kv=v_mla_g_kv, mla_w_uq=v_mla_w_uq,
                 mla_w_ukv=v_mla_w_ukv, w_branch=v_w_branch, w_out=v_w_out, ple_g=v_ple_g, w_ple_gate=v_w_ple_gate,
                 w_ple_proj=v_w_ple_proj, final_g=v_final_g)
    depth = ln_g.shape[0]
    assert depth == 2
    hooks = _Exchanged(weights)
    small = {nm: weights[nm] for nm in SMALL}
    small["ln_g"] = small["ln_g"].at[0, 0].add(hooks.token)
    loss_local, dx, gws, gsmall = _local_step(x[0], p[:, 0], loss_target[0], hooks, small)
    loss = lax.psum(loss_local, ("x", "y", "c"))

    grads, delta, new_m, new_v = {}, {}, {}, {}
    flat = jnp.concatenate([gsmall[nm].reshape(-1) for nm in SMALL])
    n_small = flat.shape[0]
    got = _exchange([_pack_rows(flat, 8)], all_gather=True, name="gather_small_grads")[0]
    gsum = _sum8(got, name="sum_small_grads").reshape(-1)[:n_small]
    off = 0
    for nm in SMALL:
        cnt = weights[nm].size
        grads[nm] = gsum[off:off + cnt].reshape(weights[nm].shape)
        off += cnt
    pk = lambda d: _pack_rows(jnp.concatenate([d[nm].reshape(-1) for nm in SMALL]), 8)
    d_s, m_s, v_s = _adamw(pk(weights), pk(grads), pk(mom_m), pk(mom_v), name="adamw_small")
    off = 0
    for nm in SMALL:
        cnt = weights[nm].size
        shp = weights[nm].shape
        delta[nm] = d_s.reshape(-1)[off:off + cnt].reshape(shp)
        new_m[nm] = m_s.reshape(-1)[off:off + cnt].reshape(shp)
        new_v[nm] = v_s.reshape(-1)[off:off + cnt].reshape(shp)
        off += cnt

    parts = hooks.partial_grads(dx)
    for nm, _, _ in SHARDED:
        grads[nm], delta[nm], new_m[nm], new_v[nm] = _adamw_sharded(
            weights[nm], [parts[li][nm] for li in range(depth)], mom_m[nm], mom_v[nm], name=f"adamw_{nm}")
    return (loss, dx[None], *[grads[nm] for nm in WEIGHT_ORDER], *[delta[nm] for nm in WEIGHT_ORDER],
            *[new_m[nm] for nm in WEIGHT_ORDER], *[new_v[nm] for nm in WEIGHT_ORDER])
```

```python
import functools
import math

import jax
import jax.numpy as jnp
from jax import lax
from jax.experimental import pallas as pl
from jax.experimental.pallas import tpu as pltpu

F32 = jnp.float32
BF16 = jnp.bfloat16

N_DEV = 8
D_MODEL = 2048
BRANCH_W = 1024
N_HEADS = 8
HEAD_D = 128
SSM_GROUPS = 64
SSM_GROUP = 16
SSM_STATE = 64
SSM_GB = 8
SSM_CHUNK = 256
MLA_ROPE = 64
MLA_NOPE = 128
ROPE_THETA = 10000.0
NORM_EPS = 1e-6
DT_MIN = 1e-3
DT_MAX = 1e-1
PACK_COLS = 1024

ADAM_LR = 0.001
ADAM_B1 = 0.9
ADAM_B2 = 0.999
ADAM_EPS = 1e-08
ADAM_WD = 0.01
ADAM_STEP = 10

VMEM_LIMIT = 56 * 2 ** 20
NEG_BIG = -1e30
MM_VMEM_BUDGET = 36 * 2 ** 20

SHARDED = (
    ("w_in", (2048, 14400), 1),
    ("ssm_w_glu", (1024, 1024), 0),
    ("mla_w_uq", (512, 1536), 1),
    ("mla_w_ukv", (512, 2048), 1),
    ("w_branch", (3, 1024, 2048), 2),
    ("w_out", (2048, 2048), 0),
    ("w_ple_gate", (2048, 2048), 0),
    ("w_ple_proj", (256, 2048), 1),
)
SMALL = ("ln_g", "ssm_lam_re", "ssm_lam_im", "ssm_log_dt", "ssm_b_re", "ssm_b_im", "ssm_c_re", "ssm_c_im",
         "ssm_d", "mla_g_q", "mla_g_kv", "ple_g", "final_g")
WEIGHT_ORDER = ("ln_g", "w_in", "ssm_lam_re", "ssm_lam_im", "ssm_log_dt", "ssm_b_re", "ssm_b_im", "ssm_c_re",
                "ssm_c_im", "ssm_d", "ssm_w_glu", "mla_g_q", "mla_g_kv", "mla_w_uq", "mla_w_ukv", "w_branch",
                "w_out", "ple_g", "w_ple_gate", "w_ple_proj", "final_g")


def _cparams(sem=None):
    return pltpu.CompilerParams(dimension_semantics=sem, vmem_limit_bytes=VMEM_LIMIT)


def _pick(n, prefs):
    for t in prefs:
        if n % t == 0:
            return t
    return n


def _dot(a, b):
    return lax.dot_general(a, b, (((1,), (0,)), ((), ())), preferred_element_type=F32)


def _dot_nt(a, b):
    return lax.dot_general(a, b, (((1,), (1,)), ((), ())), preferred_element_type=F32)


def _dot_tn(a, b):
    return lax.dot_general(a, b, (((0,), (0,)), ((), ())), preferred_element_type=F32)


def _sigmoid(x):
    return 1.0 / (1.0 + jnp.exp(-x))


def _mm(a, b, *, name, ta=False, tb=False, extras=(), epilogue=None, out_dtypes=(F32,)):
    m, k = (a.shape[1], a.shape[0]) if ta else a.shape
    n = b.shape[0] if tb else b.shape[1]
    assert (b.shape[1] if tb else b.shape[0]) == k
    n_ex = len(extras)
    n_out = len(out_dtypes)
    tn = _pick(n, (1024, 512, 256, 128))
    tk = _pick(k, (2048, 1024, 512, 256, 128))
    for tm in (1024, 512, 256, 128):
        if m % tm:
            continue
        est = 2 * (tm * tk * a.dtype.itemsize + tk * tn * b.dtype.itemsize) + tm * tn * 4
        est += 2 * tm * tn * (sum(e[0].dtype.itemsize for e in extras) + sum(jnp.dtype(d).itemsize for d in out_dtypes))
        if est <= MM_VMEM_BUDGET:
            break
    nk = k // tk

    def body(*refs):
        a_ref, b_ref = refs[0], refs[1]
        ex_refs = refs[2:2 + n_ex]
        out_refs = refs[2 + n_ex:2 + n_ex + n_out]
        acc_ref = refs[-1]
        kk = pl.program_id(2)

        @pl.when(kk == 0)
        def _():
            acc_ref[...] = jnp.zeros_like(acc_ref)

        av = a_ref[...].astype(BF16)
        bv = b_ref[...].astype(BF16)
        dims = (((0 if ta else 1,), (1 if tb else 0,)), ((), ()))
        acc_ref[...] += lax.dot_general(av, bv, dims, preferred_element_type=F32)

        @pl.when(kk == nk - 1)
        def _():
            acc = acc_ref[...]
            res = epilogue(acc, *[r[...] for r in ex_refs]) if epilogue is not None else (acc,)
            for r, v in zip(out_refs, res):
                r[...] = v.astype(r.dtype)

    a_spec = pl.BlockSpec((tk, tm), lambda i, j, q: (q, i)) if ta else pl.BlockSpec((tm, tk), lambda i, j, q: (i, q))
    b_spec = pl.BlockSpec((tn, tk), lambda i, j, q: (j, q)) if tb else pl.BlockSpec((tk, tn), lambda i, j, q: (q, j))
    ex_specs = []
    for arr, off in extras:
        assert off % tn == 0 and arr.shape[0] == m
        ex_specs.append(pl.BlockSpec((tm, tn), functools.partial(lambda i, j, q, o: (i, j + o), o=off // tn)))
    outs = pl.pallas_call(
        body, name=name,
        grid=(m // tm, n // tn, nk),
        in_specs=[a_spec, b_spec] + ex_specs,
        out_specs=[pl.BlockSpec((tm, tn), lambda i, j, q: (i, j)) for _ in out_dtypes],
        out_shape=[jax.ShapeDtypeStruct((m, n), dt) for dt in out_dtypes],
        scratch_shapes=[pltpu.VMEM((tm, tn), F32)],
        compiler_params=_cparams(("parallel", "parallel", "arbitrary")),
    )(a, b, *[e[0] for e in extras])
    return outs if n_out > 1 else outs[0]


def _row_tile(m, bytes_per_row):
    for t in (1024, 512, 256, 128, 64, 32, 16):
        if m % t == 0 and 2 * t * bytes_per_row <= 20 * 2 ** 20:
            return t
    return 16 if m % 16 == 0 else m


def _ew(fn, ins, outs, *, name):
    m = max(x.shape[0] for x in ins)
    bpr = sum(x.shape[1] * x.dtype.itemsize for x in ins if x.shape[0] == m)
    bpr += sum(no * jnp.dtype(dt).itemsize for no, dt in outs)
    tm = _row_tile(m, bpr)
    n_in = len(ins)

    def body(*refs):
        res = fn(*[r[...] for r in refs[:n_in]])
        if not isinstance(res, (tuple, list)):
            res = (res,)
        for r, v in zip(refs[n_in:], res):
            r[...] = v.astype(r.dtype)

    in_specs = []
    for x in ins:
        if x.shape[0] == m:
            in_specs.append(pl.BlockSpec((tm, x.shape[1]), lambda i: (i, 0)))
        else:
            in_specs.append(pl.BlockSpec(x.shape, lambda i: (0, 0)))
    res = pl.pallas_call(
        body, name=name,
        grid=(m // tm,),
        in_specs=in_specs,
        out_specs=[pl.BlockSpec((tm, no), lambda i: (i, 0)) for no, _ in outs],
        out_shape=[jax.ShapeDtypeStruct((m, no), dt) for no, dt in outs],
        compiler_params=_cparams(("parallel",)),
    )(*ins)
    return res if len(outs) > 1 else res[0]


def _rmsnorm(x, g, *, name, out_dtype=BF16):
    d = x.shape[1]

    def fn(xv, gv):
        r = lax.rsqrt(jnp.mean(xv * xv, axis=-1, keepdims=True) + NORM_EPS)
        return xv * r * gv

    return _ew(fn, [x, g.reshape(1, d)], [(d, out_dtype)], name=name)


def _rmsnorm_bwd(x, g, dy, res, *, name):
    m, d = x.shape
    has_res = res is not None
    tm = _row_tile(m, d * 4 * (4 if has_res else 3))

    def body(*refs):
        x_ref, g_ref, dy_ref = refs[:3]
        res_ref = refs[3] if has_res else None
        dx_ref, dg_ref = refs[-2], refs[-1]

        @pl.when(pl.program_id(0) == 0)
        def _():
            dg_ref[...] = jnp.zeros_like(dg_ref)

        xv = x_ref[...]
        dyv = dy_ref[...].astype(F32)
        r = lax.rsqrt(jnp.mean(xv * xv, axis=-1, keepdims=True) + NORM_EPS)
        xh = xv * r
        dg_ref[...] += jnp.sum(dyv * xh, axis=0, keepdims=True)
        dyg = dyv * g_ref[...]
        dx = r * (dyg - xh * jnp.mean(dyg * xh, axis=-1, keepdims=True))
        if has_res:
            dx = dx + res_ref[...]
        dx_ref[...] = dx

    row = pl.BlockSpec((tm, d), lambda i: (i, 0))
    vec = pl.BlockSpec((1, d), lambda i: (0, 0))
    ins = [x, g.reshape(1, d), dy] + ([res] if has_res else [])
    return pl.pallas_call(
        body, name=name,
        grid=(m // tm,),
        in_specs=[row, vec, row] + ([row] if has_res else []),
        out_specs=[row, vec],
        out_shape=[jax.ShapeDtypeStruct((m, d), F32), jax.ShapeDtypeStruct((1, d), F32)],
        compiler_params=_cparams(("arbitrary",)),
    )(*ins)


def _loss_head(x, g, target, *, name):
    m, d = x.shape
    tm = _row_tile(m, d * 4 * 3)

    def body(x_ref, g_ref, t_ref, dx_ref, dg_ref, loss_ref):
        @pl.when(pl.program_id(0) == 0)
        def _():
            dg_ref[...] = jnp.zeros_like(dg_ref)
            loss_ref[...] = jnp.zeros_like(loss_ref)

        xv = x_ref[...]
        gv = g_ref[...]
        r = lax.rsqrt(jnp.mean(xv * xv, axis=-1, keepdims=True) + NORM_EPS)
        xh = xv * r
        diff = xh * gv - t_ref[...]
        loss_ref[...] += 0.5 * jnp.sum(jnp.mean(diff * diff, axis=-1, keepdims=True), axis=0, keepdims=True)
        dyv = diff * (1.0 / d)
        dg_ref[...] += jnp.sum(dyv * xh, axis=0, keepdims=True)
        dyg = dyv * gv
        dx_ref[...] = r * (dyg - xh * jnp.mean(dyg * xh, axis=-1, keepdims=True))

    row = pl.BlockSpec((tm, d), lambda i: (i, 0))
    vec = pl.BlockSpec((1, d), lambda i: (0, 0))
    return pl.pallas_call(
        body, name=name,
        grid=(m // tm,),
        in_specs=[row, vec, row],
        out_specs=[row, vec, pl.BlockSpec((1, 128), lambda i: (0, 0))],
        out_shape=[jax.ShapeDtypeStruct((m, d), F32), jax.ShapeDtypeStruct((1, d), F32),
                   jax.ShapeDtypeStruct((1, 128), F32)],
        compiler_params=_cparams(("arbitrary",)),
    )(x, g.reshape(1, d), target)


def _rope_tables(seqlen):
    pos = jnp.arange(seqlen, dtype=F32)
    inv_freq = ROPE_THETA ** (-jnp.arange(0, MLA_ROPE, 2, dtype=F32) / MLA_ROPE)
    ang = pos[:, None] * inv_freq[None, :]
    c, s = jnp.cos(ang), jnp.sin(ang)
    z = jnp.zeros_like(c)
    return (jnp.concatenate([c, c, z, z], axis=1), jnp.concatenate([z, s, z, z], axis=1),
            jnp.concatenate([-s, z, z, z], axis=1))


def _rope_group(xg, ct, sp, sm):
    return xg * ct + pltpu.roll(xg, 32, 1) * sp + pltpu.roll(xg, 96, 1) * sm


def _rope_group_t(dg, ct, sp, sm):
    return dg * ct + pltpu.roll(dg * sp, 96, 1) + pltpu.roll(dg * sm, 32, 1)


def _s5_disc(lam_re, lam_im, log_dt, b_re, b_im):
    dt = jnp.exp(log_dt)[:, None]
    mag = jnp.exp(lam_re * dt)
    ab_re = mag * jnp.cos(lam_im * dt)
    ab_im = mag * jnp.sin(lam_im * dt)
    den = lam_re * lam_re + lam_im * lam_im
    nr = ab_re - 1.0
    ni = ab_im
    coef_re = (nr * lam_re + ni * lam_im) / den
    coef_im = (ni * lam_re - nr * lam_im) / den
    bb_re = coef_re[..., None] * b_re - coef_im[..., None] * b_im
    bb_im = coef_re[..., None] * b_im + coef_im[..., None] * b_re
    return ab_re, ab_im, bb_re, bb_im


def _blockdiag(x):
    nb, ng, r, c = x.shape
    eye = jnp.eye(ng, dtype=x.dtype)
    return (x[:, :, :, None, :] * eye[None, :, None, :, None]).reshape(nb, ng * r, ng * c)


def _blockdiag_extract(x, r, c):
    nb = x.shape[0]
    x5 = x.reshape(nb, SSM_GB, r, SSM_GB, c)
    eye = jnp.eye(SSM_GB, dtype=x.dtype)
    return jnp.sum(x5 * eye[None, :, None, :, None], axis=3)


def _tile_scan(h_re, h_im, a_re_ref, a_im_ref, t_chunk, reverse):
    n = h_re.shape[1]
    h_re = h_re.reshape(t_chunk // 8, 8, n)
    h_im = h_im.reshape(t_chunk // 8, 8, n)
    sub = lax.broadcasted_iota(jnp.int32, (t_chunk // 8, 8, n), 1)
    for k in range(3):
        sh = 1 << k
        ar = a_re_ref[pl.ds(k, 1), :]
        ai = a_im_ref[pl.ds(k, 1), :]
        if reverse:
            s_re = pltpu.roll(h_re, 8 - sh, 1)
            s_im = pltpu.roll(h_im, 8 - sh, 1)
            keep = sub < 8 - sh
            ai = -ai
        else:
            s_re = pltpu.roll(h_re, sh, 1)
            s_im = pltpu.roll(h_im, sh, 1)
            keep = sub >= sh
        s_re = jnp.where(keep, s_re, 0.0)
        s_im = jnp.where(keep, s_im, 0.0)
        h_re, h_im = h_re + ar * s_re - ai * s_im, h_im + ar * s_im + ai * s_re
    return h_re.reshape(t_chunk, n), h_im.reshape(t_chunk, n)


def _tile_chain(h_re, h_im, w_re, w_im, c_re, c_im, t_chunk, reverse):
    nt = t_chunk // 8
    out_re, out_im = [None] * nt, [None] * nt
    edge = 0 if reverse else 7
    for j in (reversed(range(nt)) if reverse else range(nt)):
        tr = h_re[8 * j:8 * j + 8, :] + w_re * c_re - w_im * c_im
        ti = h_im[8 * j:8 * j + 8, :] + w_re * c_im + w_im * c_re
        c_re = tr[edge:edge + 1, :]
        c_im = ti[edge:edge + 1, :]
        out_re[j], out_im[j] = tr, ti
    return jnp.concatenate(out_re, axis=0), jnp.concatenate(out_im, axis=0), c_re, c_im


def _s5_tables(lam_re, lam_im, log_dt):
    dt = jnp.exp(log_dt)[:, None]
    lr = (lam_re * dt).reshape(1, -1)
    li = (lam_im * dt).reshape(1, -1)

    def powers(n):
        mag = jnp.exp(n * lr)
        return mag * jnp.cos(n * li), mag * jnp.sin(n * li)

    p_re, p_im = powers(jnp.arange(1, 9, dtype=F32)[:, None])
    pad = jnp.zeros((5, lr.shape[1]), F32)
    a_re = jnp.concatenate([p_re[0:1], p_re[1:2], p_re[3:4], pad], axis=0)
    a_im = jnp.concatenate([p_im[0:1], p_im[1:2], p_im[3:4], pad], axis=0)
    return p_re, p_im, p_re[::-1], p_im[::-1], a_re, a_im


def _s5_fwd(u, bb_re, bb_im, c_re, c_im, dsk, tabs, *, name):
    s = u.shape[0]
    t = min(SSM_CHUNK, s)
    nt = s // t
    ns = SSM_GB * SSM_STATE
    p_re, p_im, _, _, a_re, a_im = tabs

    def body(u_ref, bbr_ref, bbi_ref, cr_ref, ci_ref, d_ref, pr_ref, pi_ref, ar_ref, ai_ref,
             y_ref, hcr_ref, hci_ref, hr_ref, hi_ref, car_re, car_im):
        @pl.when(pl.program_id(1) == 0)
        def _():
            car_re[...] = jnp.zeros_like(car_re)
            car_im[...] = jnp.zeros_like(car_im)

        cin_re = car_re[...]
        cin_im = car_im[...]
        hcr_ref[0] = cin_re
        hci_ref[0] = cin_im
        uv = u_ref[...]
        ub = uv.astype(BF16)
        h_re = _dot(ub, bbr_ref[0])
        h_im = _dot(ub, bbi_ref[0])
        h_re, h_im = _tile_scan(h_re, h_im, ar_ref, ai_ref, t, False)
        h_re, h_im, c_re_, c_im_ = _tile_chain(h_re, h_im, pr_ref[...], pi_ref[...], cin_re, cin_im, t, False)
        car_re[...] = c_re_
        car_im[...] = c_im_
        hrb = h_re.astype(BF16)
        hib = h_im.astype(BF16)
        hr_ref[...] = hrb
        hi_ref[...] = hib
        y_ref[...] = _dot(hrb, cr_ref[0]) - _dot(hib, ci_ref[0]) + d_ref[...] * uv

    blk = lambda r, c: pl.BlockSpec((1, r, c), lambda g, i: (g, 0, 0))
    tab = pl.BlockSpec((8, ns), lambda g, i: (0, g))
    return pl.pallas_call(
        body, name=name,
        grid=(SSM_GB, nt),
        in_specs=[pl.BlockSpec((t, 128), lambda g, i: (i, g)), blk(128, ns), blk(128, ns), blk(ns, 128), blk(ns, 128),
                  pl.BlockSpec((1, 128), lambda g, i: (0, g)), tab, tab, tab, tab],
        out_specs=[pl.BlockSpec((t, 128), lambda g, i: (i, g)),
                   pl.BlockSpec((1, 1, ns), lambda g, i: (i, 0, g)), pl.BlockSpec((1, 1, ns), lambda g, i: (i, 0, g)),
                   pl.BlockSpec((t, ns), lambda g, i: (i, g)), pl.BlockSpec((t, ns), lambda g, i: (i, g))],
        out_shape=[jax.ShapeDtypeStruct((s, BRANCH_W), F32),
                   jax.ShapeDtypeStruct((nt, 1, SSM_GB * ns), F32), jax.ShapeDtypeStruct((nt, 1, SSM_GB * ns), F32),
                   jax.ShapeDtypeStruct((s, SSM_GB * ns), BF16), jax.ShapeDtypeStruct((s, SSM_GB * ns), BF16)],
        scratch_shapes=[pltpu.VMEM((1, ns), F32), pltpu.VMEM((1, ns), F32)],
        compiler_params=_cparams(("parallel", "arbitrary")),
    )(u, bb_re, bb_im, c_re, c_im, dsk, p_re, p_im, a_re, a_im)


def _s5_bwd(u, dy, hc_re, hc_im, h_re, h_im, bbt_re, bbt_im, ct_re, ct_im, dsk, tabs, *, name):
    s = u.shape[0]
    t = min(SSM_CHUNK, s)
    nt = s // t
    ns = SSM_GB * SSM_STATE
    _, _, q_re, q_im, a_re, a_im = tabs

    def body(u_ref, dy_ref, hcr_ref, hci_ref, hr_ref, hi_ref, bbtr_ref, bbti_ref, ctr_ref, cti_ref,
             d_ref, qr_ref, qi_ref, ar_ref, ai_ref,
             du_ref, dbbr_ref, dbbi_ref, dcr_ref, dci_ref, dar_ref, dai_ref, dd_ref, lam_re_c, lam_im_c):
        @pl.when(pl.program_id(1) == 0)
        def _():
            lam_re_c[...] = jnp.zeros_like(lam_re_c)
            lam_im_c[...] = jnp.zeros_like(lam_im_c)
            dbbr_ref[...] = jnp.zeros_like(dbbr_ref)
            dbbi_ref[...] = jnp.zeros_like(dbbi_ref)
            dcr_ref[...] = jnp.zeros_like(dcr_ref)
            dci_ref[...] = jnp.zeros_like(dci_ref)
            dar_ref[...] = jnp.zeros_like(dar_ref)
            dai_ref[...] = jnp.zeros_like(dai_ref)
            dd_ref[...] = jnp.zeros_like(dd_ref)

        cin_re = hcr_ref[0]
        cin_im = hci_ref[0]
        uv = u_ref[...]
        ub = uv.astype(BF16)
        dyv = dy_ref[...]
        dyb = dyv.astype(BF16)
        rows = lax.broadcasted_iota(jnp.int32, (t, ns), 0)
        hrb = hr_ref[...]
        hib = hi_ref[...]
        dcr_ref[0] += _dot_tn(hrb, dyb)
        dci_ref[0] -= _dot_tn(hib, dyb)
        h_re = hrb.astype(F32)
        h_im = hib.astype(F32)
        first = rows == 0
        hp_re = jnp.where(first, cin_re, pltpu.roll(h_re, 1, 0))
        hp_im = jnp.where(first, cin_im, pltpu.roll(h_im, 1, 0))
        l_re = _dot(dyb, ctr_ref[0])
        l_im = -_dot(dyb, cti_ref[0])
        l_re, l_im = _tile_scan(l_re, l_im, ar_ref, ai_ref, t, True)
        l_re, l_im, n_re, n_im = _tile_chain(l_re, l_im, qr_ref[...], -qi_ref[...], lam_re_c[...], lam_im_c[...],
                                             t, True)
        lam_re_c[...] = n_re
        lam_im_c[...] = n_im
        lrb = l_re.astype(BF16)
        lib = l_im.astype(BF16)
        du_ref[...] = _dot(lrb, bbtr_ref[0]) + _dot(lib, bbti_ref[0]) + d_ref[...] * dyv
        dbbr_ref[0] += _dot_tn(ub, lrb)
        dbbi_ref[0] += _dot_tn(ub, lib)
        dar_ref[0] += jnp.sum(l_re * hp_re + l_im * hp_im, axis=0, keepdims=True)
        dai_ref[0] += jnp.sum(l_im * hp_re - l_re * hp_im, axis=0, keepdims=True)
        dd_ref[0] += jnp.sum(dyv * uv, axis=0, keepdims=True)

    rev = lambda g, i: (nt - 1 - i, g)
    blk = lambda r, c: pl.BlockSpec((1, r, c), lambda g, i: (g, 0, 0))
    tab = pl.BlockSpec((8, ns), lambda g, i: (0, g))
    car = pl.BlockSpec((1, 1, ns), lambda g, i: (nt - 1 - i, 0, g))
    return pl.pallas_call(
        body, name=name,
        grid=(SSM_GB, nt),
        in_specs=[pl.BlockSpec((t, 128), rev), pl.BlockSpec((t, 128), rev), car, car,
                  pl.BlockSpec((t, ns), rev), pl.BlockSpec((t, ns), rev),
                  blk(ns, 128), blk(ns, 128), blk(128, ns), blk(128, ns), pl.BlockSpec((1, 128), lambda g, i: (0, g)),
                  tab, tab, tab, tab],
        out_specs=[pl.BlockSpec((t, 128), rev), blk(128, ns), blk(128, ns), blk(ns, 128), blk(ns, 128),
                   blk(1, ns), blk(1, ns), blk(1, 128)],
        out_shape=[jax.ShapeDtypeStruct((s, BRANCH_W), F32),
                   jax.ShapeDtypeStruct((SSM_GB, 128, ns), F32), jax.ShapeDtypeStruct((SSM_GB, 128, ns), F32),
                   jax.ShapeDtypeStruct((SSM_GB, ns, 128), F32), jax.ShapeDtypeStruct((SSM_GB, ns, 128), F32),
                   jax.ShapeDtypeStruct((SSM_GB, 1, ns), F32), jax.ShapeDtypeStruct((SSM_GB, 1, ns), F32),
                   jax.ShapeDtypeStruct((SSM_GB, 1, 128), F32)],
        scratch_shapes=[pltpu.VMEM((1, ns), F32), pltpu.VMEM((1, ns), F32)],
        compiler_params=_cparams(("parallel", "arbitrary")),
    )(u, dy, hc_re, hc_im, h_re, h_im, bbt_re, bbt_im, ct_re, ct_im, dsk, q_re, q_im, a_re, a_im)


ATT_BQ = 512
ATT_BK = 512


def _attn_blocks(s):
    return min(ATT_BQ, s), min(ATT_BK, s)


def _rel_index(bq, bk):
    return lax.broadcasted_iota(jnp.int32, (bq, bk), 0) - lax.broadcasted_iota(jnp.int32, (bq, bk), 1)


def _mla_fwd(q, k, kv, *, name):
    s = q.shape[0]
    bq, bk = _attn_blocks(s)
    r = bq // bk
    scale = float((MLA_NOPE + MLA_ROPE) ** -0.5)

    def body(q_ref, k_ref, v_ref, o_ref, lse_ref):
        qi = pl.program_id(1)
        qv = q_ref[...]
        nfull = qi * r
        rel = _rel_index(bq, bk)

        def step(j, carry, d):
            m, l, acc = carry
            ks = pl.ds(pl.multiple_of(j * bk, bk), bk)
            sc = _dot_nt(qv, k_ref[ks, :]) * scale
            if d is not None:
                sc = jnp.where(rel >= d * bk, sc, NEG_BIG)
            m_new = jnp.maximum(m, jnp.max(sc, axis=1, keepdims=True))
            alpha = jnp.exp(m - m_new)
            p = jnp.exp(sc - m_new)
            l = alpha * l + jnp.sum(p, axis=1, keepdims=True)
            acc = alpha * acc + _dot(p.astype(BF16), v_ref[ks, :])
            return m_new, l, acc

        carry = (jnp.full((bq, 1), NEG_BIG, F32), jnp.zeros((bq, 1), F32), jnp.zeros((bq, HEAD_D), F32))
        carry = lax.fori_loop(0, nfull, lambda j, c: step(j, c, None), carry)
        for d in range(r):
            carry = step(nfull + d, carry, d)
        m, l, acc = carry
        o_ref[...] = acc / l
        lse_ref[0] = m + jnp.log(l)

    return pl.pallas_call(
        body, name=name,
        grid=(N_HEADS, s // bq),
        in_specs=[pl.BlockSpec((bq, 2 * HEAD_D), lambda h, i: (i, h)),
                  pl.BlockSpec((s, 2 * HEAD_D), lambda h, i: (0, h)),
                  pl.BlockSpec((s, HEAD_D), lambda h, i: (0, N_HEADS + h))],
        out_specs=[pl.BlockSpec((bq, HEAD_D), lambda h, i: (i, h)), pl.BlockSpec((1, bq, 1), lambda h, i: (h, i, 0))],
        out_shape=[jax.ShapeDtypeStruct((s, N_HEADS * HEAD_D), F32), jax.ShapeDtypeStruct((N_HEADS, s, 1), F32)],
        compiler_params=_cparams(("parallel", "arbitrary")),
    )(q, k, kv)


def _mla_bwd(q, k, kv, o, do, lse, *, name):
    s = q.shape[0]
    bq, bk = _attn_blocks(s)
    r = bq // bk
    scale = float((MLA_NOPE + MLA_ROPE) ** -0.5)

    def body(q_ref, k_ref, v_ref, o_ref, do_ref, lse_ref, dq_ref, dk_ref, dv_ref):
        qi = pl.program_id(1)

        @pl.when(qi == 0)
        def _():
            dk_ref[...] = jnp.zeros_like(dk_ref)
            dv_ref[...] = jnp.zeros_like(dv_ref)

        qv = q_ref[...]
        dov = do_ref[...]
        dob = dov.astype(BF16)
        delta = jnp.sum(dov * o_ref[...], axis=1, keepdims=True)
        lse = lse_ref[0]
        nfull = qi * r
        rel = _rel_index(bq, bk)

        def step(j, dq, d):
            ks = pl.ds(pl.multiple_of(j * bk, bk), bk)
            kt = k_ref[ks, :]
            sc = _dot_nt(qv, kt) * scale
            p = jnp.exp(sc - lse)
            if d is not None:
                p = jnp.where(rel >= d * bk, p, 0.0)
            dp = _dot_nt(dob, v_ref[ks, :])
            ds = (p * (dp - delta) * scale).astype(BF16)
            pb = p.astype(BF16)
            dk_ref[ks, :] += _dot_tn(ds, qv)
            dv_ref[ks, :] += _dot_tn(pb, dob)
            return dq + _dot(ds, kt)

        dq = lax.fori_loop(0, nfull, lambda j, c: step(j, c, None), jnp.zeros((bq, 2 * HEAD_D), F32))
        for d in range(r):
            dq = step(nfull + d, dq, d)
        dq_ref[...] = dq

    tile = pl.BlockSpec((bq, HEAD_D), lambda h, i: (i, h))
    tile2 = pl.BlockSpec((bq, 2 * HEAD_D), lambda h, i: (i, h))
    res2 = pl.BlockSpec((s, 2 * HEAD_D), lambda h, i: (0, h))
    return pl.pallas_call(
        body, name=name,
        grid=(N_HEADS, s // bq),
        in_specs=[tile2, res2, pl.BlockSpec((s, HEAD_D), lambda h, i: (0, N_HEADS + h)), tile, tile,
                  pl.BlockSpec((1, bq, 1), lambda h, i: (h, i, 0))],
        out_specs=[tile2, res2, pl.BlockSpec((s, HEAD_D), lambda h, i: (0, h))],
        out_shape=[jax.ShapeDtypeStruct((s, 2 * N_HEADS * HEAD_D), F32)] * 2
        + [jax.ShapeDtypeStruct((s, N_HEADS * HEAD_D), F32)],
        compiler_params=_cparams(("parallel", "arbitrary")),
    )(q, k, kv, o, do, lse)


def _split_dot(x, tri):
    hi = x.astype(BF16)
    lo = (x - hi.astype(F32)).astype(BF16)
    return _dot(hi, tri) + _dot(lo, tri)


TRI = 256


def _tri(kind):
    r0 = lax.broadcasted_iota(jnp.int32, (TRI, TRI), 0)
    c0 = lax.broadcasted_iota(jnp.int32, (TRI, TRI), 1)
    return {"after": r0 > c0, "upto": r0 <= c0, "before": r0 < c0}[kind].astype(BF16)


def _group_sums(x, tri, reverse, split):
    n = x.shape[1] // TRI
    parts = [x[:, g * TRI:(g + 1) * TRI] for g in range(n)]
    tots = [jnp.sum(pp, axis=1, keepdims=True) for pp in parts]
    outs = [None] * n
    run = None
    for g in (reversed(range(n)) if reverse else range(n)):
        inner = _split_dot(parts[g], tri) if split else _dot(parts[g].astype(BF16), tri)
        outs[g] = inner if run is None else inner + run
        run = tots[g] if run is None else run + tots[g]
    return (outs[0] if n == 1 else jnp.concatenate(outs, axis=1)), run


LOG2E = 1.4426950408889634


def _log2_sigmoid(z2):
    return jnp.minimum(z2, 0.0) - jnp.log(1.0 + jnp.exp2(-jnp.abs(z2))) * LOG2E


def _sb_fwd(qkv, *, name):
    s = qkv.shape[0]
    bq, bk = _attn_blocks(s)
    r = bq // bk
    scale = float(HEAD_D ** -0.5)

    def body(q_ref, k_ref, v_ref, o_ref, tot_ref):
        qi = pl.program_id(1)
        qv = q_ref[...]
        nfull = qi * r
        rel = _rel_index(bq, bk)
        after = _tri("after")

        def step(j, carry, d):
            car, acc = carry
            ks = pl.ds(pl.multiple_of(j * bk, bk), bk)
            z = _dot_nt(qv, k_ref[ks, :]) * (scale * LOG2E)
            lb = _log2_sigmoid(z)
            lr = lb - z
            if d is not None:
                mask = rel > d * bk
                lr = jnp.where(mask, lr, 0.0)
            later, tot_lr = _group_sums(lr, after, True, True)
            w = jnp.exp2(lb + later + car)
            if d is not None:
                w = jnp.where(mask, w, 0.0)
            acc = acc + _dot(w.astype(BF16), v_ref[ks, :])
            return car + tot_lr, acc

        carry = (jnp.zeros((bq, 1), F32), jnp.zeros((bq, HEAD_D), F32))
        for d in reversed(range(r)):
            carry = step(nfull + d, carry, d)
        car, acc = lax.fori_loop(0, nfull, lambda i, c: step(nfull - 1 - i, c, None), carry)
        o_ref[...] = acc
        tot_ref[0] = car

    res = lambda off: pl.BlockSpec((s, HEAD_D), functools.partial(lambda h, i, o: (0, h + o), o=off))
    return pl.pallas_call(
        body, name=name,
        grid=(N_HEADS, s // bq),
        in_specs=[pl.BlockSpec((bq, HEAD_D), lambda h, i: (i, h)), res(N_HEADS), res(2 * N_HEADS)],
        out_specs=[pl.BlockSpec((bq, HEAD_D), lambda h, i: (i, h)), pl.BlockSpec((1, bq, 1), lambda h, i: (h, i, 0))],
        out_shape=[jax.ShapeDtypeStruct((s, N_HEADS * HEAD_D), F32), jax.ShapeDtypeStruct((N_HEADS, s, 1), F32)],
        compiler_params=_cparams(("parallel", "arbitrary")),
    )(qkv, qkv, qkv)


def _sb_bwd(qkv, do, tot, *, name):
    s = qkv.shape[0]
    bq, bk = _attn_blocks(s)
    r = bq // bk
    scale = float(HEAD_D ** -0.5)

    def body(q_ref, k_ref, v_ref, do_ref, tot_ref, dq_ref, dk_ref, dv_ref):
        qi = pl.program_id(1)

        @pl.when(qi == 0)
        def _():
            dk_ref[...] = jnp.zeros_like(dk_ref)
            dv_ref[...] = jnp.zeros_like(dv_ref)

        qv = q_ref[...]
        dob = do_ref[...].astype(BF16)
        nfull = qi * r
        rel = _rel_index(bq, bk)
        upto = _tri("upto")
        before = _tri("before")

        def step(j, carry, d):
            rest, gsum, dq = carry
            ks = pl.ds(pl.multiple_of(j * bk, bk), bk)
            kv_ = k_ref[ks, :]
            z = _dot_nt(qv, kv_) * (scale * LOG2E)
            lb = _log2_sigmoid(z)
            lr = lb - z
            if d is not None:
                mask = rel > d * bk
                lr = jnp.where(mask, lr, 0.0)
            sofar, tot_lr = _group_sums(lr, upto, False, True)
            w = jnp.exp2(lb + rest - sofar)
            if d is not None:
                w = jnp.where(mask, w, 0.0)
            g = _dot_nt(dob, v_ref[ks, :]) * w
            gpre, tot_g = _group_sums(g, before, False, False)
            dlr = gsum + gpre
            dz = (g * jnp.exp2(lr) - dlr * jnp.exp2(lb)) * scale
            if d is not None:
                dz = jnp.where(mask, dz, 0.0)
            dzb = dz.astype(BF16)
            dk_ref[ks, :] += _dot_tn(dzb, qv)
            dv_ref[ks, :] += _dot_tn(w.astype(BF16), dob)
            return rest - tot_lr, gsum + tot_g, dq + _dot(dzb, kv_)

        carry = (tot_ref[0], jnp.zeros((bq, 1), F32), jnp.zeros((bq, HEAD_D), F32))
        carry = lax.fori_loop(0, nfull, lambda j, c: step(j, c, None), carry)
        for d in range(r):
            carry = step(nfull + d, carry, d)
        dq_ref[...] = carry[2]

    tile = pl.BlockSpec((bq, HEAD_D), lambda h, i: (i, h))
    res = lambda off: pl.BlockSpec((s, HEAD_D), functools.partial(lambda h, i, o: (0, h + o), o=off))
    wide = jax.ShapeDtypeStruct((s, N_HEADS * HEAD_D), F32)
    return pl.pallas_call(
        body, name=name,
        grid=(N_HEADS, s // bq),
        in_specs=[tile, res(N_HEADS), res(2 * N_HEADS), tile, pl.BlockSpec((1, bq, 1), lambda h, i: (h, i, 0))],
        out_specs=[tile, res(0), res(0)],
        out_shape=[wide] * 3,
        compiler_params=_cparams(("parallel", "arbitrary")),
    )(qkv, qkv, qkv, do, tot)


def _exchange(bufs, *, all_gather, name):
    nb = len(bufs)
    shapes = [((N_DEV,) + b.shape) if all_gather else b.shape for b in bufs]

    def body(*refs):
        srcs, outs = refs[:nb], refs[nb:2 * nb]
        send_sems, recv_sems, local_sems = refs[2 * nb:]
        x, y, c = lax.axis_index("x"), lax.axis_index("y"), lax.axis_index("c")
        me = 4 * x + 2 * y + c

        def src_for(b, idx):
            return srcs[b] if all_gather else srcs[b].at[idx]

        def copy(slot, b, dev, src_idx, dst_idx):
            return pltpu.make_async_remote_copy(
                src_ref=src_for(b, src_idx), dst_ref=outs[b].at[dst_idx], send_sem=send_sems.at[slot * nb + b],
                recv_sem=recv_sems.at[slot * nb + b], device_id=dev, device_id_type=pl.DeviceIdType.MESH)

        owns = [pltpu.make_async_copy(src_for(b, me), outs[b].at[me], local_sems.at[b]) for b in range(nb)]
        for own in owns:
            own.start()
        peers = []
        for k in range(1, N_DEV):
            px = (1 - x) if (k >> 2) & 1 else x
            py = (1 - y) if (k >> 1) & 1 else y
            pc = (1 - c) if k & 1 else c
            peers.append((k - 1, (px, py, pc), 4 * px + 2 * py + pc))
        sends = [copy(slot, b, dev, idx, me) for slot, dev, idx in peers for b in range(nb)]
        for cp in sends:
            cp.start()
        for slot, dev, idx in peers:
            for b in range(nb):
                copy(slot, b, dev, idx, idx).wait_recv()
        for cp in sends:
            cp.wait_send()
        for own in owns:
            own.wait()

    any_spec = pl.BlockSpec(memory_space=pl.ANY)
    return pl.pallas_call(
        body, name=name,
        in_specs=[any_spec] * nb,
        out_specs=[any_spec] * nb,
        out_shape=[jax.ShapeDtypeStruct(sh, b.dtype) for sh, b in zip(shapes, bufs)],
        scratch_shapes=[pltpu.SemaphoreType.DMA(((N_DEV - 1) * nb,)), pltpu.SemaphoreType.DMA(((N_DEV - 1) * nb,)),
                        pltpu.SemaphoreType.DMA((nb,))],
    )(*bufs)


def _gather_via_sibling(buf, *, name):
    def body(x_ref, out_ref, send_sems, recv_sems, local_sem):
        x, y, c = lax.axis_index("x"), lax.axis_index("y"), lax.axis_index("c")
        me, sibling = (x, y, c), (x, y, 1 - c)
        chips = [(1 - x, y), (x, 1 - y), (1 - x, 1 - y)]

        def slab(px, py, pc):
            return out_ref.at[4 * px + 2 * py + pc]

        def copy(k, block, to, src=None):
            return pltpu.make_async_remote_copy(
                src_ref=slab(*block) if src is None else src, dst_ref=slab(*block), send_sem=send_sems.at[k],
                recv_sem=recv_sems.at[k], device_id=to, device_id_type=pl.DeviceIdType.MESH)

        mine = pltpu.make_async_copy(x_ref, slab(*me), local_sem)
        mine.start()
        first = [copy(0, me, sibling, src=x_ref)]
        first += [copy(1 + j, me, (*chip, c), src=x_ref) for j, chip in enumerate(chips)]
        for cp in first:
            cp.start()
        passed = [copy(4 + j, (*chip, c), sibling) for j, chip in enumerate(chips)]
        for j, chip in enumerate(chips):
            copy(1 + j, (*chip, c), me).wait_recv()
            passed[j].start()
        copy(0, sibling, me).wait_recv()
        for j, chip in enumerate(chips):
            copy(4 + j, (*chip, 1 - c), me).wait_recv()
        for cp in first + passed:
            cp.wait_send()
        mine.wait()

    return pl.pallas_call(
        body, name=name,
        in_specs=[pl.BlockSpec(memory_space=pl.ANY)],
        out_specs=pl.BlockSpec(memory_space=pl.ANY),
        out_shape=jax.ShapeDtypeStruct((N_DEV,) + buf.shape, buf.dtype),
        scratch_shapes=[pltpu.SemaphoreType.DMA((N_DEV - 1,)), pltpu.SemaphoreType.DMA((N_DEV - 1,)),
                        pltpu.SemaphoreType.DMA],
    )(buf)


_HBM = pl.BlockSpec(memory_space=pltpu.HBM)
_SEM = pl.BlockSpec(memory_space=pltpu.SEMAPHORE)
_EFFECT = pltpu.SideEffectType.DATAFLOW_SIDE_EFFECTING


def _peers():
    x, y, c = lax.axis_index("x"), lax.axis_index("y"), lax.axis_index("c")
    peers = []
    for k in range(1, N_DEV):
        px = (1 - x) if (k >> 2) & 1 else x
        py = (1 - y) if (k >> 1) & 1 else y
        pc = (1 - c) if k & 1 else c
        peers.append((k - 1, (px, py, pc), 4 * px + 2 * py + pc))
    return 4 * x + 2 * y + c, peers


def _exchange_start(bufs, *, all_gather, name):
    nb = len(bufs)
    shapes = [((N_DEV,) + b.shape) if all_gather else b.shape for b in bufs]
    ncp = (N_DEV - 1) * nb

    def body(*refs):
        srcs, lands = refs[:nb], refs[nb:2 * nb]
        send_sems, recv_sems = refs[2 * nb], refs[2 * nb + 1]
        token = refs[-1]
        me, peers = _peers()
        for slot, dev, idx in peers:
            for b in range(nb):
                pltpu.make_async_remote_copy(
                    src_ref=srcs[b] if all_gather else srcs[b].at[idx], dst_ref=lands[b].at[me],
                    send_sem=send_sems.at[slot * nb + b], recv_sem=recv_sems.at[slot * nb + b],
                    device_id=dev, device_id_type=pl.DeviceIdType.MESH).start()
        token[...] = jnp.zeros_like(token)

    ins = [pltpu.with_memory_space_constraint(b, pltpu.HBM) for b in bufs]
    ins += [pltpu.with_memory_space_constraint(lax.empty(sh, b.dtype), pltpu.HBM) for sh, b in zip(shapes, bufs)]
    outs = pl.pallas_call(
        body, name=name,
        out_shape=(pltpu.SemaphoreType.DMA((ncp,)), pltpu.SemaphoreType.DMA((ncp,)),
                   *[pltpu.HBM(b.shape, b.dtype) for b in bufs], *[pltpu.HBM(sh, b.dtype) for sh, b in zip(shapes, bufs)],
                   jax.ShapeDtypeStruct((8, 128), F32)),
        in_specs=[_HBM] * (2 * nb),
        out_specs=(_SEM, _SEM, *([_HBM] * (2 * nb)), pl.BlockSpec(memory_space=pltpu.VMEM)),
        input_output_aliases={i: 2 + i for i in range(2 * nb)},
        compiler_params=pltpu.CompilerParams(has_side_effects=_EFFECT),
    )(*ins)
    return outs[:-1], outs[-1]


def _exchange_finish(started, after, *, all_gather, name):
    nb = (len(started) - 2) // 2
    send_sems, recv_sems = started[0], started[1]
    thru = started[2:]

    def body(*refs):
        srcs, lands = refs[:nb], refs[nb:2 * nb]
        ssem, rsem = refs[2 * nb], refs[2 * nb + 1]
        me, peers = _peers()
        for slot, dev, idx in peers:
            for b in range(nb):
                cp = pltpu.make_async_remote_copy(
                    src_ref=srcs[b] if all_gather else srcs[b].at[idx], dst_ref=lands[b].at[idx],
                    send_sem=ssem.at[slot * nb + b], recv_sem=rsem.at[slot * nb + b],
                    device_id=dev, device_id_type=pl.DeviceIdType.MESH)
                cp.wait_send()
                cp.wait_recv()

    done = pl.pallas_call(
        body, name=name,
        out_shape=tuple(pltpu.HBM(t.shape, t.dtype) for t in thru),
        in_specs=[_HBM] * (2 * nb) + [_SEM, _SEM, pl.BlockSpec(memory_space=pl.ANY)],
        out_specs=[_HBM] * (2 * nb),
        input_output_aliases={i: i for i in range(2 * nb)},
        compiler_params=pltpu.CompilerParams(has_side_effects=_EFFECT),
    )(*thru, send_sems, recv_sems, after)
    srcs, lands = done[:nb], done[nb:]
    me = 4 * lax.axis_index("x") + 2 * lax.axis_index("y") + lax.axis_index("c")
    outs = []
    for s_, l_ in zip(srcs, lands):
        own = s_[None] if all_gather else lax.dynamic_slice_in_dim(s_, me, 1, axis=0)
        outs.append(lax.dynamic_update_slice_in_dim(l_, own, me, axis=0))
    return outs


def _sum8(buf, *, name):
    _, r, c = buf.shape
    tr = _pick(r, (256, 128, 64, 32, 16, 8))

    def body(b_ref, o_ref):
        acc = b_ref[0].astype(F32)
        for q in range(1, N_DEV):
            acc = acc + b_ref[q].astype(F32)
        o_ref[...] = acc

    return pl.pallas_call(
        body, name=name,
        grid=(r // tr,),
        in_specs=[pl.BlockSpec((N_DEV, tr, c), lambda i: (0, i, 0))],
        out_specs=pl.BlockSpec((tr, c), lambda i: (i, 0)),
        out_shape=jax.ShapeDtypeStruct((r, c), F32),
        compiler_params=_cparams(("parallel",)),
    )(buf)


def _pack_rows(flat, mult):
    n = flat.shape[0]
    chunk = mult * PACK_COLS
    tot = -(-n // chunk) * chunk
    return jnp.pad(flat, (0, tot - n)).reshape(tot // PACK_COLS, PACK_COLS)


_ADAM_BC1 = 1.0 - ADAM_B1 ** ADAM_STEP
_ADAM_BC2 = 1.0 - ADAM_B2 ** ADAM_STEP


def _adamw_math(wv, gv, mv, vv):
    mn = ADAM_B1 * mv + (1.0 - ADAM_B1) * gv
    vn = ADAM_B2 * vv + (1.0 - ADAM_B2) * (gv * gv)
    m_hat = mn / _ADAM_BC1
    v_hat = vn / _ADAM_BC2
    delta = -ADAM_LR * (m_hat / (jnp.sqrt(v_hat) + ADAM_EPS) + ADAM_WD * wv)
    return delta, mn, vn


def _adamw(w, g, m, v, *, name):
    shape = w.shape
    cols = shape[-1]
    to2d = lambda a: a.reshape(-1, cols)
    d, mn, vn = _ew(_adamw_math, [to2d(w), to2d(g), to2d(m), to2d(v)], [(cols, F32)] * 3, name=name)
    return d.reshape(shape), mn.reshape(shape), vn.reshape(shape)


def _adamw_sharded(w, parts, m, v, *, name):
    depth = w.shape[0]
    cols = w.shape[-1]
    rows = w[0].size // cols
    to3 = lambda a: a.reshape(depth, rows, cols)
    w3, m3, v3 = to3(w), to3(m), to3(v)
    bpr = cols * (3 * 4 + N_DEV * parts[0].dtype.itemsize + 4 * 4)
    tm = _row_tile(rows, bpr)
    outs = None
    for li in range(depth):
        carried = li > 0

        def body(*refs):
            w_ref, p_ref, m_ref, v_ref = refs[:4]
            g_ref, d_ref, mo_ref, vo_ref = refs[-4:]
            gv = p_ref[0].astype(F32)
            for q in range(1, N_DEV):
                gv = gv + p_ref[q].astype(F32)
            delta, mn, vn = _adamw_math(w_ref[0], gv, m_ref[0], v_ref[0])
            g_ref[0] = gv
            d_ref[0] = delta
            mo_ref[0] = mn
            vo_ref[0] = vn

        lay = pl.BlockSpec((1, tm, cols), functools.partial(lambda i, l: (l, i, 0), l=li))
        any_spec = pl.BlockSpec(memory_space=pl.ANY)
        ins = [w3, parts[li].reshape(N_DEV, rows, cols), m3, v3] + (list(outs) if carried else [])
        outs = pl.pallas_call(
            body, name=f"{name}_l{li}",
            grid=(rows // tm,),
            in_specs=[lay, pl.BlockSpec((N_DEV, tm, cols), lambda i: (0, i, 0)), lay, lay]
            + ([any_spec] * 4 if carried else []),
            out_specs=[lay] * 4,
            out_shape=[jax.ShapeDtypeStruct((depth, rows, cols), F32)] * 4,
            input_output_aliases={4 + q: q for q in range(4)} if carried else {},
            compiler_params=_cparams(("parallel",)),
        )(*ins)
    return [o.reshape(w.shape) for o in outs]


def _prep_in(w_in):
    rope_cols = jnp.pad(w_in[:, 2048:2048 + MLA_ROPE], ((0, 0), (0, 128 - MLA_ROPE)))
    return dict(wa=w_in[:, :2048], wb=w_in[:, 2112:5184], wc=w_in[:, 5184:8256], wd=w_in[:, 8256:14400], we=rope_cols)


def _prep_rest(full):
    wq = full["mla_w_uq"].reshape(512, N_HEADS, MLA_NOPE + MLA_ROPE)
    wkv = full["mla_w_ukv"].reshape(512, N_HEADS, 2 * HEAD_D)
    return dict(
        w_uq=jnp.pad(wq, ((0, 0), (0, 0), (0, 2 * HEAD_D - MLA_NOPE - MLA_ROPE))).reshape(512, 2 * N_HEADS * HEAD_D),
        w_ukv=jnp.concatenate([wkv[:, :, :HEAD_D].reshape(512, 1024), wkv[:, :, HEAD_D:].reshape(512, 1024)], axis=1),
        w_glu=full["ssm_w_glu"], w_branch=full["w_branch"], w_out=full["w_out"],
        w_ple_gate=full["w_ple_gate"], w_ple_proj=full["w_ple_proj"])


def _prep_ssm(sp):
    ab_re, ab_im, bb_re, bb_im = _s5_disc(sp["lam_re"], sp["lam_im"], sp["log_dt"], sp["b_re"], sp["b_im"])
    g4 = lambda a: a.reshape(SSM_GB, SSM_GB, a.shape[1], a.shape[2])
    bbd_re = _blockdiag(jnp.swapaxes(g4(bb_re), 2, 3)).astype(BF16)
    bbd_im = _blockdiag(jnp.swapaxes(g4(bb_im), 2, 3)).astype(BF16)
    cd_re = _blockdiag(jnp.swapaxes(g4(sp["c_re"]), 2, 3)).astype(BF16)
    cd_im = _blockdiag(jnp.swapaxes(g4(sp["c_im"]), 2, 3)).astype(BF16)
    return dict(bb_re=bbd_re, bb_im=bbd_im, bbt_re=jnp.swapaxes(bbd_re, 1, 2), bbt_im=jnp.swapaxes(bbd_im, 1, 2),
                c_re=cd_re, c_im=cd_im, ct_re=jnp.swapaxes(cd_re, 1, 2), ct_im=jnp.swapaxes(cd_im, 1, 2),
                dsk=sp["d"].reshape(1, BRANCH_W))


GELU_C = math.sqrt(2.0 / math.pi)


def _gelu(x):
    return 0.5 * x * (1.0 + jnp.tanh(GELU_C * (x + 0.044715 * x * x * x)))


def _gelu_grad(x):
    t = jnp.tanh(GELU_C * (x + 0.044715 * x * x * x))
    return 0.5 * (1.0 + t) + 0.5 * x * (1.0 - t * t) * GELU_C * (1.0 + 3.0 * 0.044715 * x * x)


def _layer_fwd(x_in, p_l, hooks, ssm, tabs, sm, rope, li):
    tg = lambda nm: nm
    ct, sp_, sm_ = rope
    h = _rmsnorm(x_in, sm["ln_g"], name=tg("ln"))
    w = dict(hooks.weights_in(li, x_in))
    seg_a = _mm(h, w["wa"], name=tg("in_a"))
    seg_b = _mm(h, w["wb"], name=tg("in_b"), out_dtypes=(BF16,))
    seg_c = _mm(h, w["wc"], name=tg("in_c"))
    seg_d = _mm(h, w["wd"], name=tg("in_d"))
    seg_e = _mm(h, w["we"], name=tg("in_e"))

    o_sb, tot = _sb_fwd(seg_b, name=tg("sb_fwd"))
    w.update(hooks.weights_rest(li, o_sb))

    y_pre, hc_re, hc_im, hs_re, hs_im = _s5_fwd(seg_a, ssm["bb_re"], ssm["bb_im"], ssm["c_re"], ssm["c_im"], ssm["dsk"], tabs,
                                  name=tg("s5_fwd"))
    yg = _ew(_gelu, [y_pre], [(BRANCH_W, F32)], name=tg("gelu"))
    y_ssm, z_glu = _mm(yg, w["w_glu"], name=tg("glu"), extras=[(yg, 0)],
                       epilogue=lambda acc, ygv: (ygv * _sigmoid(acc), acc), out_dtypes=(F32, F32))

    q_lat = seg_a[:, 1024:1536]
    kv_lat = seg_a[:, 1536:2048]
    qn = _rmsnorm(q_lat, sm["mla_g_q"], name=tg("q_norm"))
    kvn = _rmsnorm(kv_lat, sm["mla_g_kv"], name=tg("kv_norm"))
    q_raw = _mm(qn, w["w_uq"], name=tg("uq"))
    kv_b = _mm(kvn, w["w_ukv"], name=tg("ukv"), out_dtypes=(BF16,))

    def rope_q(qv, c_, p_, m_):
        outs = []
        for hh in range(N_HEADS):
            outs.append(qv[:, 256 * hh:256 * hh + 128])
            outs.append(_rope_group(qv[:, 256 * hh + 128:256 * hh + 256], c_, p_, m_))
        return jnp.concatenate(outs, axis=1)

    def rope_k(kvv, kr, c_, p_, m_):
        kr_rot = _rope_group(kr, c_, p_, m_).astype(BF16)
        outs = []
        for hh in range(N_HEADS):
            outs.append(kvv[:, 128 * hh:128 * hh + 128])
            outs.append(kr_rot)
        return jnp.concatenate(outs, axis=1)

    q_b = _ew(rope_q, [q_raw, ct, sp_, sm_], [(2048, BF16)], name=tg("rope_q"))
    k_b = _ew(rope_k, [kv_b, seg_e, ct, sp_, sm_], [(2048, BF16)], name=tg("rope_k"))
    o_mla, lse = _mla_fwd(q_b, k_b, kv_b, name=tg("mla_fwd"))

    def gate_fn(a, b, c_, gp):
        ys = jnp.concatenate([a, b, c_], axis=1)
        return ys * (gp * _sigmoid(gp))

    ys = _ew(gate_fn, [y_ssm, o_mla, o_sb, seg_c], [(3 * BRANCH_W, BF16)], name=tg("gate"))
    merged = None
    bos = []
    for b in range(3):
        if b == 0:
            ep = lambda acc, ml: (_sigmoid(ml) * acc, acc)
            ex = [(seg_d, 0)]
        else:
            ep = lambda acc, ml, prev: (prev + _sigmoid(ml) * acc, acc)
            ex = [(seg_d, b * D_MODEL), (merged, 0)]
        merged, bo = _mm(ys[:, b * BRANCH_W:(b + 1) * BRANCH_W], w["w_branch"][b], name=tg(f"branch{b}"),
                         extras=ex, epilogue=ep, out_dtypes=(F32 if b < 2 else BF16, BF16))
        bos.append(bo)
    merged_b = merged
    x1 = _mm(merged_b, w["w_out"], name=tg("out"), extras=[(x_in, 0)], epilogue=lambda acc, xv: (xv + acc,))

    hn2 = _rmsnorm(x1, sm["ple_g"], name=tg("ple_norm"))
    e = _mm(p_l, w["w_ple_proj"], name=tg("ple_proj"))
    x2, gl = _mm(hn2, w["w_ple_gate"], name=tg("ple_gate"), extras=[(x1, 0), (e, 0)],
                 epilogue=lambda acc, xv, ev: (xv + _sigmoid(acc) * ev, acc), out_dtypes=(F32, F32))
    saved = dict(x_in=x_in, h=h, seg_a=seg_a, seg_b=seg_b, seg_c=seg_c, seg_d=seg_d, y_pre=y_pre, hc_re=hc_re,
                 hc_im=hc_im, hs_re=hs_re, hs_im=hs_im, yg=yg, y_ssm=y_ssm, z_glu=z_glu, q_lat=q_lat, kv_lat=kv_lat, qn=qn, kvn=kvn, q_b=q_b,
                 kv_b=kv_b, k_b=k_b, o_mla=o_mla, lse=lse, o_sb=o_sb, tot=tot, ys=ys, bos=bos, merged_b=merged_b,
                 x1=x1, hn2=hn2, e=e, gl=gl, p_l=p_l, w=w)
    return x2, saved


def _layer_bwd(dx2, sv, hooks, ssm, tabs, sm, rope, li):
    tg = lambda nm: nm
    w = sv["w"]
    ct, sp_, sm_ = rope
    gw, gs = {}, {}

    def ple_fn(d, ev, glv):
        gt = _sigmoid(glv)
        return d * ev * gt * (1.0 - gt), d * gt

    dgl, de = _ew(ple_fn, [dx2, sv["e"], sv["gl"]], [(D_MODEL, BF16), (D_MODEL, BF16)], name=tg("ple_bwd"))
    gw["w_ple_proj"] = _mm(sv["p_l"], de, ta=True, out_dtypes=(BF16,), name=tg("d_ple_proj"))
    gw["w_ple_gate"] = _mm(sv["hn2"], dgl, ta=True, out_dtypes=(BF16,), name=tg("d_ple_gate"))
    dhn2 = _mm(dgl, w["w_ple_gate"], tb=True, name=tg("ple_gate_t"))
    dx1, gs["ple_g"] = _rmsnorm_bwd(sv["x1"], sm["ple_g"], dhn2, dx2, name=tg("ple_norm_bwd"))

    dx1_b = dx1.astype(BF16)
    gw["w_out"] = _mm(sv["merged_b"], dx1_b, ta=True, out_dtypes=(BF16,), name=tg("d_out"))
    dmerged = _mm(dx1_b, w["w_out"], tb=True, name=tg("out_t"))

    def merge_fn(dm, ml, b0, b1, b2):
        dbo, dml = [], []
        for b, bo in enumerate((b0, b1, b2)):
            gt = _sigmoid(ml[:, b * D_MODEL:(b + 1) * D_MODEL])
            dbo.append(dm * gt)
            dml.append(dm * bo.astype(F32) * gt * (1.0 - gt))
        return jnp.concatenate(dbo, axis=1), jnp.concatenate(dml, axis=1)

    dbo, dseg_d = _ew(merge_fn, [dmerged, sv["seg_d"]] + sv["bos"], [(3 * D_MODEL, BF16), (3 * D_MODEL, BF16)],
                      name=tg("merge_bwd"))
    dys, dwb = [], []
    for b in range(3):
        dbo_b = dbo[:, b * D_MODEL:(b + 1) * D_MODEL]
        dwb.append(_mm(sv["ys"][:, b * BRANCH_W:(b + 1) * BRANCH_W], dbo_b, ta=True, out_dtypes=(BF16,), name=tg(f"d_branch{b}")))
        dys.append(_mm(dbo_b, w["w_branch"][b], tb=True, name=tg(f"branch{b}_t")))
    gw["w_branch"] = jnp.stack(dwb)

    def gate_bwd(d0, d1, d2, a, b, c_, gp):
        dy_all = jnp.concatenate([d0, d1, d2], axis=1)
        y_all = jnp.concatenate([a, b, c_], axis=1)
        sg = _sigmoid(gp)
        silu = gp * sg
        dsilu = sg * (1.0 + gp * (1.0 - sg))
        dyv = dy_all * silu
        return (dyv[:, :BRANCH_W], dyv[:, BRANCH_W:2 * BRANCH_W], dyv[:, 2 * BRANCH_W:], dy_all * y_all * dsilu)

    dy_ssm, do_mla, do_sb, dseg_c = _ew(
        gate_bwd, dys + [sv["y_ssm"], sv["o_mla"], sv["o_sb"], sv["seg_c"]],
        [(BRANCH_W, F32), (BRANCH_W, F32), (BRANCH_W, F32), (3 * BRANCH_W, BF16)], name=tg("gate_bwd"))

    dq_sb, dk_sb, dv_sb = _sb_bwd(sv["seg_b"], do_sb, sv["tot"], name=tg("sb_bwd"))
    dseg_b = jnp.concatenate([dq_sb, dk_sb, dv_sb], axis=1).astype(BF16)

    dq_, dk_, dv_ = _mla_bwd(sv["q_b"], sv["k_b"], sv["kv_b"], sv["o_mla"], do_mla, sv["lse"], name=tg("mla_bwd"))

    def unrope_q(dq, c_, p_, m_):
        outs = []
        for hh in range(N_HEADS):
            outs.append(dq[:, 256 * hh:256 * hh + 128])
            outs.append(_rope_group_t(dq[:, 256 * hh + 128:256 * hh + 256], c_, p_, m_))
        return jnp.concatenate(outs, axis=1)

    dq_raw = _ew(unrope_q, [dq_, ct, sp_, sm_], [(2048, BF16)], name=tg("unrope_q"))

    def unrope_k(dk, dvv, c_, p_, m_):
        tot_ = dk[:, 128:256]
        for hh in range(1, N_HEADS):
            tot_ = tot_ + dk[:, 256 * hh + 128:256 * hh + 256]
        dkn = [dk[:, 256 * hh:256 * hh + 128] for hh in range(N_HEADS)]
        return jnp.concatenate(dkn + [dvv], axis=1), _rope_group_t(tot_, c_, p_, m_)

    dkv_raw, dseg_e = _ew(unrope_k, [dk_, dv_, ct, sp_, sm_], [(2048, BF16), (128, BF16)], name=tg("unrope_k"))
    dw_uq = _mm(sv["qn"], dq_raw, ta=True, out_dtypes=(BF16,), name=tg("d_uq"))
    dw_ukv = _mm(sv["kvn"], dkv_raw, ta=True, out_dtypes=(BF16,), name=tg("d_ukv"))
    dqn = _mm(dq_raw, w["w_uq"], tb=True, name=tg("uq_t"))
    dkvn = _mm(dkv_raw, w["w_ukv"], tb=True, name=tg("ukv_t"))
    dq_lat, gs["mla_g_q"] = _rmsnorm_bwd(sv["q_lat"], sm["mla_g_q"], dqn, None, name=tg("q_norm_bwd"))
    dkv_lat, gs["mla_g_kv"] = _rmsnorm_bwd(sv["kv_lat"], sm["mla_g_kv"], dkvn, None, name=tg("kv_norm_bwd"))
    gw["mla_w_uq"] = dw_uq.reshape(512, N_HEADS, 2 * HEAD_D)[:, :, :MLA_NOPE + MLA_ROPE].reshape(
        512, N_HEADS * (MLA_NOPE + MLA_ROPE))
    dwk = dw_ukv[:, :1024].reshape(512, N_HEADS, HEAD_D)
    dwv = dw_ukv[:, 1024:].reshape(512, N_HEADS, HEAD_D)
    gw["mla_w_ukv"] = jnp.concatenate([dwk, dwv], axis=2).reshape(512, 2 * N_HEADS * HEAD_D)

    def glu_bwd(d, ygv, zv):
        sg = _sigmoid(zv)
        return d * ygv * sg * (1.0 - sg), d * sg

    dz, dyg0 = _ew(glu_bwd, [dy_ssm, sv["yg"], sv["z_glu"]], [(BRANCH_W, BF16), (BRANCH_W, F32)], name=tg("glu_bwd"))
    gw["ssm_w_glu"] = _mm(sv["yg"], dz, ta=True, out_dtypes=(BF16,), name=tg("d_glu"))
    dy_pre = _mm(dz, w["w_glu"], tb=True, name=tg("glu_t"), extras=[(dyg0, 0), (sv["y_pre"], 0)],
                 epilogue=lambda acc, d0, yp: ((acc + d0) * _gelu_grad(yp),))
    token = hooks.grads_rest(li, gw)
    dsk = ssm["dsk"] if token is None else ssm["dsk"] + token[0, 0]
    du, dbb_re, dbb_im, dc_re, dc_im, da_re, da_im, dd = _s5_bwd(
        sv["seg_a"], dy_pre, sv["hc_re"], sv["hc_im"], sv["hs_re"], sv["hs_im"], ssm["bbt_re"], ssm["bbt_im"],
        ssm["ct_re"], ssm["ct_im"], dsk, tabs, name=tg("s5_bwd"))
    gs["ssm_d"] = dd.reshape(BRANCH_W)
    gs["ssm_c_re"] = jnp.swapaxes(_blockdiag_extract(dc_re, SSM_STATE, SSM_GROUP), 2, 3).reshape(
        SSM_GROUPS, SSM_GROUP, SSM_STATE)
    gs["ssm_c_im"] = jnp.swapaxes(_blockdiag_extract(dc_im, SSM_STATE, SSM_GROUP), 2, 3).reshape(
        SSM_GROUPS, SSM_GROUP, SSM_STATE)
    gs["_dbb_re"] = jnp.swapaxes(_blockdiag_extract(dbb_re, SSM_GROUP, SSM_STATE), 2, 3).reshape(
        SSM_GROUPS, SSM_STATE, SSM_GROUP)
    gs["_dbb_im"] = jnp.swapaxes(_blockdiag_extract(dbb_im, SSM_GROUP, SSM_STATE), 2, 3).reshape(
        SSM_GROUPS, SSM_STATE, SSM_GROUP)
    gs["_da_re"] = da_re.reshape(SSM_GROUPS, SSM_STATE)
    gs["_da_im"] = da_im.reshape(SSM_GROUPS, SSM_STATE)

    dseg_a = jnp.concatenate([du, dq_lat, dkv_lat], axis=1).astype(BF16)
    hb = sv["h"]
    dwa = _mm(hb, dseg_a, ta=True, out_dtypes=(BF16,), name=tg("d_in_a"))
    dwb_ = _mm(hb, dseg_b, ta=True, out_dtypes=(BF16,), name=tg("d_in_b"))
    dwc = _mm(hb, dseg_c, ta=True, out_dtypes=(BF16,), name=tg("d_in_c"))
    dwd = _mm(hb, dseg_d, ta=True, out_dtypes=(BF16,), name=tg("d_in_d"))
    dwe = _mm(hb, dseg_e, ta=True, out_dtypes=(BF16,), name=tg("d_in_e"))
    gw["w_in"] = jnp.concatenate([dwa, dwe[:, :MLA_ROPE], dwb_, dwc, dwd], axis=1)
    token = hooks.grads_in(li, gw["w_in"])
    wa = w["wa"] if token is None else w["wa"] + token[0, 0].astype(BF16)
    add = lambda acc, prev: (acc + prev,)
    dh = _mm(dseg_a, wa, tb=True, name=tg("in_a_t"))
    dh = _mm(dseg_b, w["wb"], tb=True, name=tg("in_b_t"), extras=[(dh, 0)], epilogue=add)
    dh = _mm(dseg_c, w["wc"], tb=True, name=tg("in_c_t"), extras=[(dh, 0)], epilogue=add)
    dh = _mm(dseg_d, w["wd"], tb=True, name=tg("in_d_t"), extras=[(dh, 0)], epilogue=add)
    dh = _mm(dseg_e, w["we"], tb=True, name=tg("in_e_t"), extras=[(dh, 0)], epilogue=add)
    dx_in, gs["ln_g"] = _rmsnorm_bwd(sv["x_in"], sm["ln_g"], dh, dx1, name=tg("ln_bwd"))
    return dx_in, gw, gs


class _LocalWeights:
    def __init__(self, fulls):
        self.fulls = fulls

    def weights_in(self, li, after):
        return _prep_in(self.fulls[li]["w_in"])

    def weights_rest(self, li, after):
        return _prep_rest(self.fulls[li])

    def grads_rest(self, li, gw):
        return None

    def grads_in(self, li, g):
        return None

    def grads_layer(self, li, gw):
        return None


def _local_step(x, p, target, hooks, small):
    s = x.shape[0]
    depth = small["ln_g"].shape[0]
    rope = _rope_tables(s)
    ssms, tabss, sms, saved = [], [], [], []
    xc = x
    for li in range(depth):
        sp = dict(lam_re=small["ssm_lam_re"][li], lam_im=small["ssm_lam_im"][li], log_dt=small["ssm_log_dt"][li],
                  b_re=small["ssm_b_re"][li], b_im=small["ssm_b_im"][li], c_re=small["ssm_c_re"][li],
                  c_im=small["ssm_c_im"][li], d=small["ssm_d"][li])
        ssm = _prep_ssm(sp)
        tabs = _s5_tables(sp["lam_re"], sp["lam_im"], sp["log_dt"])
        sm = dict(ln_g=small["ln_g"][li], mla_g_q=small["mla_g_q"][li], mla_g_kv=small["mla_g_kv"][li],
                  ple_g=small["ple_g"][li])
        xc, sv = _layer_fwd(xc, p[li], hooks, ssm, tabs, sm, rope, li)
        ssms.append(ssm), tabss.append(tabs), sms.append(sm), saved.append(sv)
    dx, dgf, loss = _loss_head(xc, small["final_g"], target, name="loss_head")
    gws, gss = [None] * depth, [None] * depth
    token = None
    for li in reversed(range(depth)):
        sm = sms[li] if token is None else dict(sms[li], ple_g=sms[li]["ple_g"] + token[0, 0])
        dx, gw, gs = _layer_bwd(dx, saved[li], hooks, ssms[li], tabss[li], sm, rope, li)
        token = hooks.grads_layer(li, gw)
        args = (small["ssm_lam_re"][li], small["ssm_lam_im"][li], small["ssm_log_dt"][li], small["ssm_b_re"][li],
                small["ssm_b_im"][li])
        _, vjp = jax.vjp(_s5_disc, *args)
        g_lr, g_li, g_dt, g_br, g_bi = vjp((gs.pop("_da_re"), gs.pop("_da_im"), gs.pop("_dbb_re"), gs.pop("_dbb_im")))
        gs.update(ssm_lam_re=g_lr, ssm_lam_im=g_li, ssm_log_dt=g_dt, ssm_b_re=g_br, ssm_b_im=g_bi)
        gws[li], gss[li] = gw, gs
    gsmall = {nm: jnp.stack([gss[li][nm].reshape(small[nm].shape[1:]) for li in range(depth)])
              for nm in SMALL if nm != "final_g"}
    gsmall["final_g"] = dgf.reshape(-1)
    return loss[0, 0], dx, gws, gsmall


REST = tuple(t for t in SHARDED if t[0] != "w_in")


def _to_full(spec, got):
    _, fshape, ax = spec
    return jnp.moveaxis(got, 0, ax).reshape(fshape)


def _to_slabs(spec, g):
    _, fshape, ax = spec
    g = g.reshape(tuple(fshape[:ax]) + (N_DEV, fshape[ax] // N_DEV) + tuple(fshape[ax + 1:]))
    return jnp.moveaxis(g, ax, 0)


class _Exchanged:
    def __init__(self, shards):
        bf = lambda nm, li: shards[nm][li].astype(BF16)
        w_in0 = _gather_via_sibling(bf("w_in", 0), name="gather_w_in")
        self.w_in0 = _to_full(SHARDED[0], w_in0)
        first = bf(REST[-1][0], 0) + (w_in0[0, 0, 0] * 0).astype(BF16)
        rest0 = [bf(nm, 0) for nm, _, _ in REST[:-1]] + [first]
        self.rest0, tok_a = _exchange_start(rest0, all_gather=True, name="gather_rest_start")
        all1 = [bf(nm, 1) for nm, _, _ in SHARDED[:-1]] + [bf(SHARDED[-1][0], 1) + tok_a[0, 0].astype(BF16)]
        self.all1, tok_b = _exchange_start(all1, all_gather=True, name="gather_next_start")
        self.token = tok_a[0, 0] + tok_b[0, 0]
        self.full1 = None
        self.sent = {}

    def weights_in(self, li, after):
        if li == 0:
            return _prep_in(self.w_in0)
        got = _exchange_finish(self.all1, after, all_gather=True, name="gather_next_finish")
        self.full1 = {t[0]: _to_full(t, g) for t, g in zip(SHARDED, got)}
        return _prep_in(self.full1["w_in"])

    def weights_rest(self, li, after):
        if li == 1:
            return _prep_rest(self.full1)
        got = _exchange_finish(self.rest0, after, all_gather=True, name="gather_rest_finish")
        return _prep_rest({t[0]: _to_full(t, g) for t, g in zip(REST, got)})

    def grads_rest(self, li, gw):
        if li != 0:
            return None
        self.sent["rest0"], token = _exchange_start([_to_slabs(t, gw[t[0]]) for t in REST], all_gather=False,
                                                    name="scatter_rest_start")
        return token

    def grads_in(self, li, g):
        if li != 0:
            return None
        self.sent["w_in0"], token = _exchange_start([_to_slabs(SHARDED[0], g)], all_gather=False,
                                                    name="scatter_w_in_start")
        return token

    def grads_layer(self, li, gw):
        if li == 0:
            return None
        self.sent["all1"], token = _exchange_start([_to_slabs(t, gw[t[0]]) for t in SHARDED], all_gather=False,
                                                   name="scatter_prev_start")
        return token

    def partial_grads(self, after):
        got1 = _exchange_finish(self.sent["all1"], after, all_gather=False, name="scatter_prev_finish")
        got0 = _exchange_finish(self.sent["rest0"], after, all_gather=False, name="scatter_rest_finish")
        w_in0 = _exchange_finish(self.sent["w_in0"], after, all_gather=False, name="scatter_w_in_finish")[0]
        p0 = {t[0]: g for t, g in zip(REST, got0)}
        p0["w_in"] = w_in0
        return [p0, {t[0]: g for t, g in zip(SHARDED, got1)}]


def kernel(x, p, ln_g, w_in, ssm_lam_re, ssm_lam_im, ssm_log_dt, ssm_b_re, ssm_b_im, ssm_c_re, ssm_c_im, ssm_d, ssm_w_glu, mla_g_q, mla_g_kv, mla_w_uq, mla_w_ukv, w_branch, w_out, ple_g, w_ple_gate, w_ple_proj, final_g, loss_target, m_ln_g, m_w_in, m_ssm_lam_re, m_ssm_lam_im, m_ssm_log_dt, m_ssm_b_re, m_ssm_b_im, m_ssm_c_re, m_ssm_c_im, m_ssm_d, m_ssm_w_glu, m_mla_g_q, m_mla_g_kv, m_mla_w_uq, m_mla_w_ukv, m_w_branch, m_w_out, m_ple_g, m_w_ple_gate, m_w_ple_proj, m_final_g, v_ln_g, v_w_in, v_ssm_lam_re, v_ssm_lam_im, v_ssm_log_dt, v_ssm_b_re, v_ssm_b_im, v_ssm_c_re, v_ssm_c_im, v_ssm_d, v_ssm_w_glu, v_mla_g_q, v_mla_g_kv, v_mla_w_uq, v_mla_w_ukv, v_w_branch, v_w_out, v_ple_g, v_w_ple_gate, v_w_ple_proj, v_final_g):
    weights = dict(ln_g=ln_g, w_in=w_in, ssm_lam_re=ssm_lam_re, ssm_lam_im=ssm_lam_im, ssm_log_dt=ssm_log_dt,
                   ssm_b_re=ssm_b_re, ssm_b_im=ssm_b_im, ssm_c_re=ssm_c_re, ssm_c_im=ssm_c_im, ssm_d=ssm_d,
                   ssm_w_glu=ssm_w_glu, mla_g_q=mla_g_q, mla_g_kv=mla_g_kv, mla_w_uq=mla_w_uq, mla_w_ukv=mla_w_ukv,
                   w_branch=w_branch, w_out=w_out, ple_g=ple_g, w_ple_gate=w_ple_gate, w_ple_proj=w_ple_proj,
                   final_g=final_g)
    mom_m = dict(ln_g=m_ln_g, w_in=m_w_in, ssm_lam_re=m_ssm_lam_re, ssm_lam_im=m_ssm_lam_im, ssm_log_dt=m_ssm_log_dt,
                 ssm_b_re=m_ssm_b_re, ssm_b_im=m_ssm_b_im, ssm_c_re=m_ssm_c_re, ssm_c_im=m_ssm_c_im, ssm_d=m_ssm_d,
                 ssm_w_glu=m_ssm_w_glu, mla_g_q=m_mla_g_q, mla_g_kv=m_mla_g_kv, mla_w_uq=m_mla_w_uq,
                 mla_w_ukv=m_mla_w_ukv, w_branch=m_w_branch, w_out=m_w_out, ple_g=m_ple_g, w_ple_gate=m_w_ple_gate,
                 w_ple_proj=m_w_ple_proj, final_g=m_final_g)
    mom_v = dict(ln_g=v_ln_g, w_in=v_w_in, ssm_lam_re=v_ssm_lam_re, ssm_lam_im=v_ssm_lam_im, ssm_log_dt=v_ssm_log_dt,
                 ssm_b_re=v_ssm_b_re, ssm_b_im=v_ssm_b_im, ssm_c_re=v_ssm_c_re, ssm_c_im=v_ssm_c_im, ssm_d=v_ssm_d,
                 ssm_w_glu=v_ssm_w_glu, mla_g_q=v_mla_g_q, mla_g_kv=v_mla_g_kv, mla_w_uq=v_mla_w_uq,
                 mla_w_ukv=v_mla_w_ukv, w_branch=v_w_branch, w_out=v_w_out, ple_g=v_ple_g, w_ple_gate=v_w_ple_gate,
                 w_ple_proj=v_w_ple_proj, final_g=v_final_g)
    depth = ln_g.shape[0]
    assert depth == 2
    hooks = _Exchanged(weights)
    small = {nm: weights[nm] for nm in SMALL}
    small["ln_g"] = small["ln_g"].at[0, 0].add(hooks.token)
    loss_local, dx, gws, gsmall = _local_step(x[0], p[:, 0], loss_target[0], hooks, small)
    loss = lax.psum(loss_local, ("x", "y", "c"))

    grads, delta, new_m, new_v = {}, {}, {}, {}
    flat = jnp.concatenate([gsmall[nm].reshape(-1) for nm in SMALL])
    n_small = flat.shape[0]
    got = _exchange([_pack_rows(flat, 8)], all_gather=True, name="gather_small_grads")[0]
    gsum = _sum8(got, name="sum_small_grads").reshape(-1)[:n_small]
    off = 0
    for nm in SMALL:
        cnt = weights[nm].size
        grads[nm] = gsum[off:off + cnt].reshape(weights[nm].shape)
        off += cnt
    pk = lambda d: _pack_rows(jnp.concatenate([d[nm].reshape(-1) for nm in SMALL]), 8)
    d_s, m_s, v_s = _adamw(pk(weights), pk(grads), pk(mom_m), pk(mom_v), name="adamw_small")
    off = 0
    for nm in SMALL:
        cnt = weights[nm].size
        shp = weights[nm].shape
        delta[nm] = d_s.reshape(-1)[off:off + cnt].reshape(shp)
        new_m[nm] = m_s.reshape(-1)[off:off + cnt].reshape(shp)
        new_v[nm] = v_s.reshape(-1)[off:off + cnt].reshape(shp)
        off += cnt

    parts = hooks.partial_grads(dx)
    for nm, _, _ in SHARDED:
        grads[nm], delta[nm], new_m[nm], new_v[nm] = _adamw_sharded(
            weights[nm], [parts[li][nm] for li in range(depth)], mom_m[nm], mom_v[nm], name=f"adamw_{nm}")
    return (loss, dx[None], *[grads[nm] for nm in WEIGHT_ORDER], *[delta[nm] for nm in WEIGHT_ORDER],
            *[new_m[nm] for nm in WEIGHT_ORDER], *[new_v[nm] for nm in WEIGHT_ORDER])
```

```python
import functools
import math

import jax
import jax.numpy as jnp
from jax import lax
from jax.experimental import pallas as pl
from jax.experimental.pallas import tpu as pltpu

F32 = jnp.float32
BF16 = jnp.bfloat16

N_DEV = 8
D_MODEL = 2048
BRANCH_W = 1024
N_HEADS = 8
N_BRANCH = 3
HEAD_D = 128
SSM_GROUPS = 64
SSM_GROUP = 16
SSM_STATE = 64
SSM_GB = 8
SSM_CHUNK = 512
MLA_ROPE = 64
MLA_NOPE = 128
ROPE_THETA = 10000.0
NORM_EPS = 1e-6
DT_MIN = 1e-3
DT_MAX = 1e-1
PACK_COLS = 1024

ADAM_LR = 0.001
ADAM_B1 = 0.9
ADAM_B2 = 0.999
ADAM_EPS = 1e-08
ADAM_WD = 0.01
ADAM_STEP = 10

VMEM_LIMIT = 56 * 2 ** 20
NEG_BIG = -1e30
MM_VMEM_BUDGET = 36 * 2 ** 20

SHARDED = (
    ("w_in", (2048, 14400), 1),
    ("ssm_w_glu", (1024, 1024), 0),
    ("mla_w_uq", (512, 1536), 1),
    ("mla_w_ukv", (512, 2048), 1),
    ("w_branch", (3, 1024, 2048), 2),
    ("w_out", (2048, 2048), 0),
    ("w_ple_gate", (2048, 2048), 0),
    ("w_ple_proj", (256, 2048), 1),
)
SMALL = ("ln_g", "ssm_lam_re", "ssm_lam_im", "ssm_log_dt", "ssm_b_re", "ssm_b_im", "ssm_c_re", "ssm_c_im",
         "ssm_d", "mla_g_q", "mla_g_kv", "ple_g", "final_g")
WEIGHT_ORDER = ("ln_g", "w_in", "ssm_lam_re", "ssm_lam_im", "ssm_log_dt", "ssm_b_re", "ssm_b_im", "ssm_c_re",
                "ssm_c_im", "ssm_d", "ssm_w_glu", "mla_g_q", "mla_g_kv", "mla_w_uq", "mla_w_ukv", "w_branch",
                "w_out", "ple_g", "w_ple_gate", "w_ple_proj", "final_g")


def _cparams(sem=None):
    return pltpu.CompilerParams(dimension_semantics=sem, vmem_limit_bytes=VMEM_LIMIT)


def _pick(n, prefs):
    for t in prefs:
        if n % t == 0:
            return t
    return n


def _dot(a, b):
    return lax.dot_general(a, b, (((1,), (0,)), ((), ())), preferred_element_type=F32)


def _dot_nt(a, b):
    return lax.dot_general(a, b, (((1,), (1,)), ((), ())), preferred_element_type=F32)


def _dot_tn(a, b):
    return lax.dot_general(a, b, (((0,), (0,)), ((), ())), preferred_element_type=F32)


def _sigmoid(x):
    return 1.0 / (1.0 + jnp.exp(-x))


def _mm(a, b, *, name, ta=False, tb=False, extras=(), epilogue=None, out_dtypes=(F32,)):
    m, k = (a.shape[1], a.shape[0]) if ta else a.shape
    n = b.shape[0] if tb else b.shape[1]
    assert (b.shape[1] if tb else b.shape[0]) == k
    n_ex = len(extras)
    n_out = len(out_dtypes)
    tn = _pick(n, (1024, 512, 256, 128))
    tk = _pick(k, (2048, 1024, 512, 256, 128))
    for tm in (1024, 512, 256, 128):
        if m % tm:
            continue
        est = 2 * (tm * tk * a.dtype.itemsize + tk * tn * b.dtype.itemsize) + tm * tn * 4
        est += 2 * tm * tn * (sum(e[0].dtype.itemsize for e in extras) + sum(jnp.dtype(d).itemsize for d in out_dtypes))
        if est <= MM_VMEM_BUDGET:
            break
    nk = k // tk

    def body(*refs):
        a_ref, b_ref = refs[0], refs[1]
        ex_refs = refs[2:2 + n_ex]
        out_refs = refs[2 + n_ex:2 + n_ex + n_out]
        acc_ref = refs[-1]
        kk = pl.program_id(2)

        @pl.when(kk == 0)
        def _():
            acc_ref[...] = jnp.zeros_like(acc_ref)

        av = a_ref[...].astype(BF16)
        bv = b_ref[...].astype(BF16)
        dims = (((0 if ta else 1,), (1 if tb else 0,)), ((), ()))
        acc_ref[...] += lax.dot_general(av, bv, dims, preferred_element_type=F32)

        @pl.when(kk == nk - 1)
        def _():
            acc = acc_ref[...]
            res = epilogue(acc, *[r[...] for r in ex_refs]) if epilogue is not None else (acc,)
            for r, v in zip(out_refs, res):
                r[...] = v.astype(r.dtype)

    a_spec = pl.BlockSpec((tk, tm), lambda i, j, q: (q, i)) if ta else pl.BlockSpec((tm, tk), lambda i, j, q: (i, q))
    b_spec = pl.BlockSpec((tn, tk), lambda i, j, q: (j, q)) if tb else pl.BlockSpec((tk, tn), lambda i, j, q: (q, j))
    ex_specs = []
    for arr, off in extras:
        assert off % tn == 0 and arr.shape[0] == m
        ex_specs.append(pl.BlockSpec((tm, tn), functools.partial(lambda i, j, q, o: (i, j + o), o=off // tn)))
    outs = pl.pallas_call(
        body, name=name,
        grid=(m // tm, n // tn, nk),
        in_specs=[a_spec, b_spec] + ex_specs,
        out_specs=[pl.BlockSpec((tm, tn), lambda i, j, q: (i, j)) for _ in out_dtypes],
        out_shape=[jax.ShapeDtypeStruct((m, n), dt) for dt in out_dtypes],
        scratch_shapes=[pltpu.VMEM((tm, tn), F32)],
        compiler_params=_cparams(("parallel", "parallel", "arbitrary")),
    )(a, b, *[e[0] for e in extras])
    return outs if n_out > 1 else outs[0]


def _row_tile(m, bytes_per_row):
    for t in (1024, 512, 256, 128, 64, 32, 16):
        if m % t == 0 and 2 * t * bytes_per_row <= 20 * 2 ** 20:
            return t
    return 16 if m % 16 == 0 else m


def _ew(fn, ins, outs, *, name):
    m = max(x.shape[0] for x in ins)
    bpr = sum(x.shape[1] * x.dtype.itemsize for x in ins if x.shape[0] == m)
    bpr += sum(no * jnp.dtype(dt).itemsize for no, dt in outs)
    tm = _row_tile(m, bpr)
    n_in = len(ins)

    def body(*refs):
        res = fn(*[r[...] for r in refs[:n_in]])
        if not isinstance(res, (tuple, list)):
            res = (res,)
        for r, v in zip(refs[n_in:], res):
            r[...] = v.astype(r.dtype)

    in_specs = []
    for x in ins:
        if x.shape[0] == m:
            in_specs.append(pl.BlockSpec((tm, x.shape[1]), lambda i: (i, 0)))
        else:
            in_specs.append(pl.BlockSpec(x.shape, lambda i: (0, 0)))
    res = pl.pallas_call(
        body, name=name,
        grid=(m // tm,),
        in_specs=in_specs,
        out_specs=[pl.BlockSpec((tm, no), lambda i: (i, 0)) for no, _ in outs],
        out_shape=[jax.ShapeDtypeStruct((m, no), dt) for no, dt in outs],
        compiler_params=_cparams(("parallel",)),
    )(*ins)
    return res if len(outs) > 1 else res[0]


def _rmsnorm(x, g, *, name, out_dtype=BF16):
    d = x.shape[1]

    def fn(xv, gv):
        r = lax.rsqrt(jnp.mean(xv * xv, axis=-1, keepdims=True) + NORM_EPS)
        return xv * r * gv

    return _ew(fn, [x, g.reshape(1, d)], [(d, out_dtype)], name=name)


def _rmsnorm_bwd(x, g, dy, res, *, name):
    m, d = x.shape
    has_res = res is not None
    tm = _row_tile(m, d * 4 * (4 if has_res else 3))

    def body(*refs):
        x_ref, g_ref, dy_ref = refs[:3]
        res_ref = refs[3] if has_res else None
        dx_ref, dg_ref = refs[-2], refs[-1]

        @pl.when(pl.program_id(0) == 0)
        def _():
            dg_ref[...] = jnp.zeros_like(dg_ref)

        xv = x_ref[...]
        dyv = dy_ref[...].astype(F32)
        r = lax.rsqrt(jnp.mean(xv * xv, axis=-1, keepdims=True) + NORM_EPS)
        xh = xv * r
        dg_ref[...] += jnp.sum(dyv * xh, axis=0, keepdims=True)
        dyg = dyv * g_ref[...]
        dx = r * (dyg - xh * jnp.mean(dyg * xh, axis=-1, keepdims=True))
        if has_res:
            dx = dx + res_ref[...]
        dx_ref[...] = dx

    row = pl.BlockSpec((tm, d), lambda i: (i, 0))
    vec = pl.BlockSpec((1, d), lambda i: (0, 0))
    ins = [x, g.reshape(1, d), dy] + ([res] if has_res else [])
    return pl.pallas_call(
        body, name=name,
        grid=(m // tm,),
        in_specs=[row, vec, row] + ([row] if has_res else []),
        out_specs=[row, vec],
        out_shape=[jax.ShapeDtypeStruct((m, d), F32), jax.ShapeDtypeStruct((1, d), F32)],
        compiler_params=_cparams(("arbitrary",)),
    )(*ins)


def _loss_head(x, g, target, *, name):
    m, d = x.shape
    tm = _row_tile(m, d * 4 * 3)

    def body(x_ref, g_ref, t_ref, dx_ref, dg_ref, loss_ref):
        @pl.when(pl.program_id(0) == 0)
        def _():
            dg_ref[...] = jnp.zeros_like(dg_ref)
            loss_ref[...] = jnp.zeros_like(loss_ref)

        xv = x_ref[...]
        gv = g_ref[...]
        r = lax.rsqrt(jnp.mean(xv * xv, axis=-1, keepdims=True) + NORM_EPS)
        xh = xv * r
        diff = xh * gv - t_ref[...]
        loss_ref[...] += 0.5 * jnp.sum(jnp.mean(diff * diff, axis=-1, keepdims=True), axis=0, keepdims=True)
        dyv = diff * (1.0 / d)
        dg_ref[...] += jnp.sum(dyv * xh, axis=0, keepdims=True)
        dyg = dyv * gv
        dx_ref[...] = r * (dyg - xh * jnp.mean(dyg * xh, axis=-1, keepdims=True))

    row = pl.BlockSpec((tm, d), lambda i: (i, 0))
    vec = pl.BlockSpec((1, d), lambda i: (0, 0))
    return pl.pallas_call(
        body, name=name,
        grid=(m // tm,),
        in_specs=[row, vec, row],
        out_specs=[row, vec, pl.BlockSpec((1, 128), lambda i: (0, 0))],
        out_shape=[jax.ShapeDtypeStruct((m, d), F32), jax.ShapeDtypeStruct((1, d), F32),
                   jax.ShapeDtypeStruct((1, 128), F32)],
        compiler_params=_cparams(("arbitrary",)),
    )(x, g.reshape(1, d), target)


def _rope_tables(seqlen):
    pos = jnp.arange(seqlen, dtype=F32)
    inv_freq = ROPE_THETA ** (-jnp.arange(0, MLA_ROPE, 2, dtype=F32) / MLA_ROPE)
    ang = pos[:, None] * inv_freq[None, :]
    c, s = jnp.cos(ang), jnp.sin(ang)
    z = jnp.zeros_like(c)
    return (jnp.concatenate([c, c, z, z], axis=1), jnp.concatenate([z, s, z, z], axis=1),
            jnp.concatenate([-s, z, z, z], axis=1))


def _rope_group(xg, ct, sp, sm):
    return xg * ct + pltpu.roll(xg, 32, 1) * sp + pltpu.roll(xg, 96, 1) * sm


def _rope_group_t(dg, ct, sp, sm):
    return dg * ct + pltpu.roll(dg * sp, 96, 1) + pltpu.roll(dg * sm, 32, 1)


def _s5_disc(lam_re, lam_im, log_dt, b_re, b_im):
    dt = jnp.exp(log_dt)[:, None]
    mag = jnp.exp(lam_re * dt)
    ab_re = mag * jnp.cos(lam_im * dt)
    ab_im = mag * jnp.sin(lam_im * dt)
    den = lam_re * lam_re + lam_im * lam_im
    nr = ab_re - 1.0
    ni = ab_im
    coef_re = (nr * lam_re + ni * lam_im) / den
    coef_im = (ni * lam_re - nr * lam_im) / den
    bb_re = coef_re[..., None] * b_re - coef_im[..., None] * b_im
    bb_im = coef_re[..., None] * b_im + coef_im[..., None] * b_re
    return ab_re, ab_im, bb_re, bb_im


def _blockdiag(x):
    nb, ng, r, c = x.shape
    eye = jnp.eye(ng, dtype=x.dtype)
    return (x[:, :, :, None, :] * eye[None, :, None, :, None]).reshape(nb, ng * r, ng * c)


def _blockdiag_extract(x, r, c):
    nb = x.shape[0]
    x5 = x.reshape(nb, SSM_GB, r, SSM_GB, c)
    eye = jnp.eye(SSM_GB, dtype=x.dtype)
    return jnp.sum(x5 * eye[None, :, None, :, None], axis=3)


def _tile_scan(h_re, h_im, a_re_ref, a_im_ref, t_chunk, reverse):
    n = h_re.shape[1]
    h_re = h_re.reshape(t_chunk // 8, 8, n)
    h_im = h_im.reshape(t_chunk // 8, 8, n)
    sub = lax.broadcasted_iota(jnp.int32, (t_chunk // 8, 8, n), 1)
    for k in range(3):
        sh = 1 << k
        ar = a_re_ref[pl.ds(k, 1), :]
        ai = a_im_ref[pl.ds(k, 1), :]
        if reverse:
            s_re = pltpu.roll(h_re, 8 - sh, 1)
            s_im = pltpu.roll(h_im, 8 - sh, 1)
            keep = sub < 8 - sh
            ai = -ai
        else:
            s_re = pltpu.roll(h_re, sh, 1)
            s_im = pltpu.roll(h_im, sh, 1)
            keep = sub >= sh
        s_re = jnp.where(keep, s_re, 0.0)
        s_im = jnp.where(keep, s_im, 0.0)
        h_re, h_im = h_re + ar * s_re - ai * s_im, h_im + ar * s_im + ai * s_re
    return h_re.reshape(t_chunk, n), h_im.reshape(t_chunk, n)


def _tile_chain(h_re, h_im, w_re, w_im, c_re, c_im, t_chunk, reverse):
    nt = t_chunk // 8
    out_re, out_im = [None] * nt, [None] * nt
    edge = 0 if reverse else 7
    for j in (reversed(range(nt)) if reverse else range(nt)):
        tr = h_re[8 * j:8 * j + 8, :] + w_re * c_re - w_im * c_im
        ti = h_im[8 * j:8 * j + 8, :] + w_re * c_im + w_im * c_re
        c_re = tr[edge:edge + 1, :]
        c_im = ti[edge:edge + 1, :]
        out_re[j], out_im[j] = tr, ti
    return jnp.concatenate(out_re, axis=0), jnp.concatenate(out_im, axis=0), c_re, c_im


def _s5_tables(lam_re, lam_im, log_dt):
    dt = jnp.exp(log_dt)[:, None]
    lr = (lam_re * dt).reshape(1, -1)
    li = (lam_im * dt).reshape(1, -1)

    def powers(n):
        mag = jnp.exp(n * lr)
        return mag * jnp.cos(n * li), mag * jnp.sin(n * li)

    p_re, p_im = powers(jnp.arange(1, 9, dtype=F32)[:, None])
    pad = jnp.zeros((5, lr.shape[1]), F32)
    a_re = jnp.concatenate([p_re[0:1], p_re[1:2], p_re[3:4], pad], axis=0)
    a_im = jnp.concatenate([p_im[0:1], p_im[1:2], p_im[3:4], pad], axis=0)
    return p_re, p_im, p_re[::-1], p_im[::-1], a_re, a_im


def _s5_fwd(u, bb_re, bb_im, c_re, c_im, dsk, tabs, *, name):
    s = u.shape[0]
    t = min(SSM_CHUNK, s)
    nt = s // t
    ns = SSM_GB * SSM_STATE
    p_re, p_im, _, _, a_re, a_im = tabs

    def body(u_ref, bbr_ref, bbi_ref, cr_ref, ci_ref, d_ref, pr_ref, pi_ref, ar_ref, ai_ref,
             y_ref, hcr_ref, hci_ref, hr_ref, hi_ref, car_re, car_im):
        @pl.when(pl.program_id(1) == 0)
        def _():
            car_re[...] = jnp.zeros_like(car_re)
            car_im[...] = jnp.zeros_like(car_im)

        cin_re = car_re[...]
        cin_im = car_im[...]
        hcr_ref[0] = cin_re
        hci_ref[0] = cin_im
        uv = u_ref[...]
        ub = uv.astype(BF16)
        h_re = _dot(ub, bbr_ref[0])
        h_im = _dot(ub, bbi_ref[0])
        h_re, h_im = _tile_scan(h_re, h_im, ar_ref, ai_ref, t, False)
        h_re, h_im, c_re_, c_im_ = _tile_chain(h_re, h_im, pr_ref[...], pi_ref[...], cin_re, cin_im, t, False)
        car_re[...] = c_re_
        car_im[...] = c_im_
        hrb = h_re.astype(BF16)
        hib = h_im.astype(BF16)
        hr_ref[...] = hrb
        hi_ref[...] = hib
        y_ref[...] = _dot(hrb, cr_ref[0]) - _dot(hib, ci_ref[0]) + d_ref[...] * uv

    blk = lambda r, c: pl.BlockSpec((1, r, c), lambda g, i: (g, 0, 0))
    tab = pl.BlockSpec((8, ns), lambda g, i: (0, g))
    return pl.pallas_call(
        body, name=name,
        grid=(SSM_GB, nt),
        in_specs=[pl.BlockSpec((t, 128), lambda g, i: (i, g)), blk(128, ns), blk(128, ns), blk(ns, 128), blk(ns, 128),
                  pl.BlockSpec((1, 128), lambda g, i: (0, g)), tab, tab, tab, tab],
        out_specs=[pl.BlockSpec((t, 128), lambda g, i: (i, g)),
                   pl.BlockSpec((1, 1, ns), lambda g, i: (i, 0, g)), pl.BlockSpec((1, 1, ns), lambda g, i: (i, 0, g)),
                   pl.BlockSpec((t, ns), lambda g, i: (i, g)), pl.BlockSpec((t, ns), lambda g, i: (i, g))],
        out_shape=[jax.ShapeDtypeStruct((s, BRANCH_W), F32),
                   jax.ShapeDtypeStruct((nt, 1, SSM_GB * ns), F32), jax.ShapeDtypeStruct((nt, 1, SSM_GB * ns), F32),
                   jax.ShapeDtypeStruct((s, SSM_GB * ns), BF16), jax.ShapeDtypeStruct((s, SSM_GB * ns), BF16)],
        scratch_shapes=[pltpu.VMEM((1, ns), F32), pltpu.VMEM((1, ns), F32)],
        compiler_params=_cparams(("parallel", "arbitrary")),
    )(u, bb_re, bb_im, c_re, c_im, dsk, p_re, p_im, a_re, a_im)


def _s5_bwd(u, dy, hc_re, hc_im, h_re, h_im, bbt_re, bbt_im, ct_re, ct_im, dsk, tabs, *, name):
    s = u.shape[0]
    t = min(SSM_CHUNK, s)
    nt = s // t
    ns = SSM_GB * SSM_STATE
    _, _, q_re, q_im, a_re, a_im = tabs

    def body(u_ref, dy_ref, hcr_ref, hci_ref, hr_ref, hi_ref, bbtr_ref, bbti_ref, ctr_ref, cti_ref,
             d_ref, qr_ref, qi_ref, ar_ref, ai_ref,
             du_ref, dbbr_ref, dbbi_ref, dcr_ref, dci_ref, dar_ref, dai_ref, dd_ref, lam_re_c, lam_im_c):
        @pl.when(pl.program_id(1) == 0)
        def _():
            lam_re_c[...] = jnp.zeros_like(lam_re_c)
            lam_im_c[...] = jnp.zeros_like(lam_im_c)
            dbbr_ref[...] = jnp.zeros_like(dbbr_ref)
            dbbi_ref[...] = jnp.zeros_like(dbbi_ref)
            dcr_ref[...] = jnp.zeros_like(dcr_ref)
            dci_ref[...] = jnp.zeros_like(dci_ref)
            dar_ref[...] = jnp.zeros_like(dar_ref)
            dai_ref[...] = jnp.zeros_like(dai_ref)
            dd_ref[...] = jnp.zeros_like(dd_ref)

        cin_re = hcr_ref[0]
        cin_im = hci_ref[0]
        uv = u_ref[...]
        ub = uv.astype(BF16)
        dyv = dy_ref[...]
        dyb = dyv.astype(BF16)
        rows = lax.broadcasted_iota(jnp.int32, (t, ns), 0)
        hrb = hr_ref[...]
        hib = hi_ref[...]
        dcr_ref[0] += _dot_tn(hrb, dyb)
        dci_ref[0] -= _dot_tn(hib, dyb)
        h_re = hrb.astype(F32)
        h_im = hib.astype(F32)
        first = rows == 0
        hp_re = jnp.where(first, cin_re, pltpu.roll(h_re, 1, 0))
        hp_im = jnp.where(first, cin_im, pltpu.roll(h_im, 1, 0))
        l_re = _dot(dyb, ctr_ref[0])
        l_im = -_dot(dyb, cti_ref[0])
        l_re, l_im = _tile_scan(l_re, l_im, ar_ref, ai_ref, t, True)
        l_re, l_im, n_re, n_im = _tile_chain(l_re, l_im, qr_ref[...], -qi_ref[...], lam_re_c[...], lam_im_c[...],
                                             t, True)
        lam_re_c[...] = n_re
        lam_im_c[...] = n_im
        lrb = l_re.astype(BF16)
        lib = l_im.astype(BF16)
        du_ref[...] = _dot(lrb, bbtr_ref[0]) + _dot(lib, bbti_ref[0]) + d_ref[...] * dyv
        dbbr_ref[0] += _dot_tn(ub, lrb)
        dbbi_ref[0] += _dot_tn(ub, lib)
        dar_ref[0] += jnp.sum(l_re * hp_re + l_im * hp_im, axis=0, keepdims=True)
        dai_ref[0] += jnp.sum(l_im * hp_re - l_re * hp_im, axis=0, keepdims=True)
        dd_ref[0] += jnp.sum(dyv * uv, axis=0, keepdims=True)

    rev = lambda g, i: (nt - 1 - i, g)
    blk = lambda r, c: pl.BlockSpec((1, r, c), lambda g, i: (g, 0, 0))
    tab = pl.BlockSpec((8, ns), lambda g, i: (0, g))
    car = pl.BlockSpec((1, 1, ns), lambda g, i: (nt - 1 - i, 0, g))
    return pl.pallas_call(
        body, name=name,
        grid=(SSM_GB, nt),
        in_specs=[pl.BlockSpec((t, 128), rev), pl.BlockSpec((t, 128), rev), car, car,
                  pl.BlockSpec((t, ns), rev), pl.BlockSpec((t, ns), rev),
                  blk(ns, 128), blk(ns, 128), blk(128, ns), blk(128, ns), pl.BlockSpec((1, 128), lambda g, i: (0, g)),
                  tab, tab, tab, tab],
        out_specs=[pl.BlockSpec((t, 128), rev), blk(128, ns), blk(128, ns), blk(ns, 128), blk(ns, 128),
                   blk(1, ns), blk(1, ns), blk(1, 128)],
        out_shape=[jax.ShapeDtypeStruct((s, BRANCH_W), F32),
                   jax.ShapeDtypeStruct((SSM_GB, 128, ns), F32), jax.ShapeDtypeStruct((SSM_GB, 128, ns), F32),
                   jax.ShapeDtypeStruct((SSM_GB, ns, 128), F32), jax.ShapeDtypeStruct((SSM_GB, ns, 128), F32),
                   jax.ShapeDtypeStruct((SSM_GB, 1, ns), F32), jax.ShapeDtypeStruct((SSM_GB, 1, ns), F32),
                   jax.ShapeDtypeStruct((SSM_GB, 1, 128), F32)],
        scratch_shapes=[pltpu.VMEM((1, ns), F32), pltpu.VMEM((1, ns), F32)],
        compiler_params=_cparams(("parallel", "arbitrary")),
    )(u, dy, hc_re, hc_im, h_re, h_im, bbt_re, bbt_im, ct_re, ct_im, dsk, q_re, q_im, a_re, a_im)


ATT_BQ = 512
ATT_BK = 512


def _attn_blocks(s):
    return min(ATT_BQ, s), min(ATT_BK, s)


def _rel_index(bq, bk):
    return lax.broadcasted_iota(jnp.int32, (bq, bk), 0) - lax.broadcasted_iota(jnp.int32, (bq, bk), 1)


def _mla_fwd(q, k, kv, *, name):
    s = q.shape[0]
    bq, bk = _attn_blocks(s)
    r = bq // bk
    scale = float((MLA_NOPE + MLA_ROPE) ** -0.5)

    def body(q_ref, k_ref, v_ref, o_ref, lse_ref):
        qi = pl.program_id(1)
        qv = q_ref[...]
        nfull = qi * r
        rel = _rel_index(bq, bk)

        def step(j, carry, d):
            m, l, acc = carry
            ks = pl.ds(pl.multiple_of(j * bk, bk), bk)
            sc = _dot_nt(qv, k_ref[ks, :]) * scale
            if d is not None:
                sc = jnp.where(rel >= d * bk, sc, NEG_BIG)
            m_new = jnp.maximum(m, jnp.max(sc, axis=1, keepdims=True))
            alpha = jnp.exp(m - m_new)
            p = jnp.exp(sc - m_new)
            l = alpha * l + jnp.sum(p, axis=1, keepdims=True)
            acc = alpha * acc + _dot(p.astype(BF16), v_ref[ks, :])
            return m_new, l, acc

        carry = (jnp.full((bq, 1), NEG_BIG, F32), jnp.zeros((bq, 1), F32), jnp.zeros((bq, HEAD_D), F32))
        carry = lax.fori_loop(0, nfull, lambda j, c: step(j, c, None), carry)
        for d in range(r):
            carry = step(nfull + d, carry, d)
        m, l, acc = carry
        o_ref[...] = acc / l
        lse_ref[0] = m + jnp.log(l)

    return pl.pallas_call(
        body, name=name,
        grid=(N_HEADS, s // bq),
        in_specs=[pl.BlockSpec((bq, 2 * HEAD_D), lambda h, i: (i, h)),
                  pl.BlockSpec((s, 2 * HEAD_D), lambda h, i: (0, h)),
                  pl.BlockSpec((s, HEAD_D), lambda h, i: (0, N_HEADS + h))],
        out_specs=[pl.BlockSpec((bq, HEAD_D), lambda h, i: (i, h)), pl.BlockSpec((1, bq, 1), lambda h, i: (h, i, 0))],
        out_shape=[jax.ShapeDtypeStruct((s, N_HEADS * HEAD_D), F32), jax.ShapeDtypeStruct((N_HEADS, s, 1), F32)],
        compiler_params=_cparams(("parallel", "arbitrary")),
    )(q, k, kv)


def _mla_bwd(q, k, kv, o, do, lse, *, name):
    s = q.shape[0]
    bq, bk = _attn_blocks(s)
    r = bq // bk
    scale = float((MLA_NOPE + MLA_ROPE) ** -0.5)

    def body(q_ref, k_ref, v_ref, o_ref, do_ref, lse_ref, dq_ref, dk_ref, dv_ref):
        qi = pl.program_id(1)

        @pl.when(qi == 0)
        def _():
            dk_ref[...] = jnp.zeros_like(dk_ref)
            dv_ref[...] = jnp.zeros_like(dv_ref)

        qv = q_ref[...]
        dov = do_ref[...]
        dob = dov.astype(BF16)
        delta = jnp.sum(dov * o_ref[...], axis=1, keepdims=True)
        lse = lse_ref[0]
        nfull = qi * r
        rel = _rel_index(bq, bk)

        def step(j, dq, d):
            ks = pl.ds(pl.multiple_of(j * bk, bk), bk)
            kt = k_ref[ks, :]
            sc = _dot_nt(qv, kt) * scale
            p = jnp.exp(sc - lse)
            if d is not None:
                p = jnp.where(rel >= d * bk, p, 0.0)
            dp = _dot_nt(dob, v_ref[ks, :])
            ds = (p * (dp - delta) * scale).astype(BF16)
            pb = p.astype(BF16)
            dk_ref[ks, :] += _dot_tn(ds, qv)
            dv_ref[ks, :] += _dot_tn(pb, dob)
            return dq + _dot(ds, kt)

        dq = lax.fori_loop(0, nfull, lambda j, c: step(j, c, None), jnp.zeros((bq, 2 * HEAD_D), F32))
        for d in range(r):
            dq = step(nfull + d, dq, d)
        dq_ref[...] = dq

    tile = pl.BlockSpec((bq, HEAD_D), lambda h, i: (i, h))
    tile2 = pl.BlockSpec((bq, 2 * HEAD_D), lambda h, i: (i, h))
    res2 = pl.BlockSpec((s, 2 * HEAD_D), lambda h, i: (0, h))
    return pl.pallas_call(
        body, name=name,
        grid=(N_HEADS, s // bq),
        in_specs=[tile2, res2, pl.BlockSpec((s, HEAD_D), lambda h, i: (0, N_HEADS + h)), tile, tile,
                  pl.BlockSpec((1, bq, 1), lambda h, i: (h, i, 0))],
        out_specs=[tile2, res2, pl.BlockSpec((s, HEAD_D), lambda h, i: (0, h))],
        out_shape=[jax.ShapeDtypeStruct((s, 2 * N_HEADS * HEAD_D), F32)] * 2
        + [jax.ShapeDtypeStruct((s, N_HEADS * HEAD_D), F32)],
        compiler_params=_cparams(("parallel", "arbitrary")),
    )(q, k, kv, o, do, lse)


def _split_dot(x, tri):
    hi = x.astype(BF16)
    lo = (x - hi.astype(F32)).astype(BF16)
    return _dot(hi, tri) + _dot(lo, tri)


TRI = 256


def _tri(kind):
    r0 = lax.broadcasted_iota(jnp.int32, (TRI, TRI), 0)
    c0 = lax.broadcasted_iota(jnp.int32, (TRI, TRI), 1)
    return {"after": r0 > c0, "upto": r0 <= c0, "before": r0 < c0}[kind].astype(BF16)


def _group_sums(x, tri, reverse, split):
    n = x.shape[1] // TRI
    parts = [x[:, g * TRI:(g + 1) * TRI] for g in range(n)]
    tots = [jnp.sum(pp, axis=1, keepdims=True) for pp in parts]
    outs = [None] * n
    run = None
    for g in (reversed(range(n)) if reverse else range(n)):
        inner = _split_dot(parts[g], tri) if split else _dot(parts[g].astype(BF16), tri)
        outs[g] = inner if run is None else inner + run
        run = tots[g] if run is None else run + tots[g]
    return (outs[0] if n == 1 else jnp.concatenate(outs, axis=1)), run


LOG2E = 1.4426950408889634


def _log2_sigmoid(z2):
    return jnp.minimum(z2, 0.0) - jnp.log(1.0 + jnp.exp2(-jnp.abs(z2))) * LOG2E


def _sb_fwd(qkv, *, name):
    s = qkv.shape[0]
    bq, bk = _attn_blocks(s)
    r = bq // bk
    scale = float(HEAD_D ** -0.5)

    def body(q_ref, k_ref, v_ref, o_ref, tot_ref):
        qi = pl.program_id(1)
        qv = q_ref[...]
        nfull = qi * r
        rel = _rel_index(bq, bk)
        after = _tri("after")

        def step(j, carry, d):
            car, acc = carry
            ks = pl.ds(pl.multiple_of(j * bk, bk), bk)
            z = _dot_nt(qv, k_ref[ks, :]) * (scale * LOG2E)
            lb = _log2_sigmoid(z)
            lr = lb - z
            if d is not None:
                mask = rel > d * bk
                lr = jnp.where(mask, lr, 0.0)
            later, tot_lr = _group_sums(lr, after, True, True)
            w = jnp.exp2(lb + later + car)
            if d is not None:
                w = jnp.where(mask, w, 0.0)
            acc = acc + _dot(w.astype(BF16), v_ref[ks, :])
            return car + tot_lr, acc

        carry = (jnp.zeros((bq, 1), F32), jnp.zeros((bq, HEAD_D), F32))
        for d in reversed(range(r)):
            carry = step(nfull + d, carry, d)
        car, acc = lax.fori_loop(0, nfull, lambda i, c: step(nfull - 1 - i, c, None), carry)
        o_ref[...] = acc
        tot_ref[0] = car

    res = lambda off: pl.BlockSpec((s, HEAD_D), functools.partial(lambda h, i, o: (0, h + o), o=off))
    return pl.pallas_call(
        body, name=name,
        grid=(N_HEADS, s // bq),
        in_specs=[pl.BlockSpec((bq, HEAD_D), lambda h, i: (i, h)), res(N_HEADS), res(2 * N_HEADS)],
        out_specs=[pl.BlockSpec((bq, HEAD_D), lambda h, i: (i, h)), pl.BlockSpec((1, bq, 1), lambda h, i: (h, i, 0))],
        out_shape=[jax.ShapeDtypeStruct((s, N_HEADS * HEAD_D), F32), jax.ShapeDtypeStruct((N_HEADS, s, 1), F32)],
        compiler_params=_cparams(("parallel", "arbitrary")),
    )(qkv, qkv, qkv)


def _sb_bwd(qkv, do, tot, *, name):
    s = qkv.shape[0]
    bq, bk = _attn_blocks(s)
    r = bq // bk
    scale = float(HEAD_D ** -0.5)

    def body(q_ref, k_ref, v_ref, do_ref, tot_ref, dq_ref, dk_ref, dv_ref):
        qi = pl.program_id(1)

        @pl.when(qi == 0)
        def _():
            dk_ref[...] = jnp.zeros_like(dk_ref)
            dv_ref[...] = jnp.zeros_like(dv_ref)

        qv = q_ref[...]
        dob = do_ref[...].astype(BF16)
        nfull = qi * r
        rel = _rel_index(bq, bk)
        upto = _tri("upto")
        before = _tri("before")

        def step(j, carry, d):
            rest, gsum, dq = carry
            ks = pl.ds(pl.multiple_of(j * bk, bk), bk)
            kv_ = k_ref[ks, :]
            z = _dot_nt(qv, kv_) * (scale * LOG2E)
            lb = _log2_sigmoid(z)
            lr = lb - z
            if d is not None:
                mask = rel > d * bk
                lr = jnp.where(mask, lr, 0.0)
            sofar, tot_lr = _group_sums(lr, upto, False, True)
            w = jnp.exp2(lb + rest - sofar)
            if d is not None:
                w = jnp.where(mask, w, 0.0)
            g = _dot_nt(dob, v_ref[ks, :]) * w
            gpre, tot_g = _group_sums(g, before, False, False)
            dlr = gsum + gpre
            dz = (g * jnp.exp2(lr) - dlr * jnp.exp2(lb)) * scale
            if d is not None:
                dz = jnp.where(mask, dz, 0.0)
            dzb = dz.astype(BF16)
            dk_ref[ks, :] += _dot_tn(dzb, qv)
            dv_ref[ks, :] += _dot_tn(w.astype(BF16), dob)
            return rest - tot_lr, gsum + tot_g, dq + _dot(dzb, kv_)

        carry = (tot_ref[0], jnp.zeros((bq, 1), F32), jnp.zeros((bq, HEAD_D), F32))
        carry = lax.fori_loop(0, nfull, lambda j, c: step(j, c, None), carry)
        for d in range(r):
            carry = step(nfull + d, carry, d)
        dq_ref[...] = carry[2]

    tile = pl.BlockSpec((bq, HEAD_D), lambda h, i: (i, h))
    res = lambda off: pl.BlockSpec((s, HEAD_D), functools.partial(lambda h, i, o: (0, h + o), o=off))
    wide = jax.ShapeDtypeStruct((s, N_HEADS * HEAD_D), F32)
    return pl.pallas_call(
        body, name=name,
        grid=(N_HEADS, s // bq),
        in_specs=[tile, res(N_HEADS), res(2 * N_HEADS), tile, pl.BlockSpec((1, bq, 1), lambda h, i: (h, i, 0))],
        out_specs=[tile, res(0), res(0)],
        out_shape=[wide] * 3,
        compiler_params=_cparams(("parallel", "arbitrary")),
    )(qkv, qkv, qkv, do, tot)


def _exchange(bufs, *, all_gather, name):
    nb = len(bufs)
    shapes = [((N_DEV,) + b.shape) if all_gather else b.shape for b in bufs]

    def body(*refs):
        srcs, outs = refs[:nb], refs[nb:2 * nb]
        send_sems, recv_sems, local_sems = refs[2 * nb:]
        x, y, c = lax.axis_index("x"), lax.axis_index("y"), lax.axis_index("c")
        me = 4 * x + 2 * y + c

        def src_for(b, idx):
            return srcs[b] if all_gather else srcs[b].at[idx]

        def copy(slot, b, dev, src_idx, dst_idx):
            return pltpu.make_async_remote_copy(
                src_ref=src_for(b, src_idx), dst_ref=outs[b].at[dst_idx], send_sem=send_sems.at[slot * nb + b],
                recv_sem=recv_sems.at[slot * nb + b], device_id=dev, device_id_type=pl.DeviceIdType.MESH)

        owns = [pltpu.make_async_copy(src_for(b, me), outs[b].at[me], local_sems.at[b]) for b in range(nb)]
        for own in owns:
            own.start()
        peers = []
        for k in range(1, N_DEV):
            px = (1 - x) if (k >> 2) & 1 else x
            py = (1 - y) if (k >> 1) & 1 else y
            pc = (1 - c) if k & 1 else c
            peers.append((k - 1, (px, py, pc), 4 * px + 2 * py + pc))
        sends = [copy(slot, b, dev, idx, me) for slot, dev, idx in peers for b in range(nb)]
        for cp in sends:
            cp.start()
        for slot, dev, idx in peers:
            for b in range(nb):
                copy(slot, b, dev, idx, idx).wait_recv()
        for cp in sends:
            cp.wait_send()
        for own in owns:
            own.wait()

    any_spec = pl.BlockSpec(memory_space=pl.ANY)
    return pl.pallas_call(
        body, name=name,
        in_specs=[any_spec] * nb,
        out_specs=[any_spec] * nb,
        out_shape=[jax.ShapeDtypeStruct(sh, b.dtype) for sh, b in zip(shapes, bufs)],
        scratch_shapes=[pltpu.SemaphoreType.DMA(((N_DEV - 1) * nb,)), pltpu.SemaphoreType.DMA(((N_DEV - 1) * nb,)),
                        pltpu.SemaphoreType.DMA((nb,))],
    )(*bufs)


def _gather_via_sibling(buf, *, name):
    def body(x_ref, out_ref, send_sems, recv_sems, local_sem):
        x, y, c = lax.axis_index("x"), lax.axis_index("y"), lax.axis_index("c")
        me, sibling = (x, y, c), (x, y, 1 - c)
        chips = [(1 - x, y), (x, 1 - y), (1 - x, 1 - y)]

        def slab(px, py, pc):
            return out_ref.at[4 * px + 2 * py + pc]

        def copy(k, block, to, src=None):
            return pltpu.make_async_remote_copy(
                src_ref=slab(*block) if src is None else src, dst_ref=slab(*block), send_sem=send_sems.at[k],
                recv_sem=recv_sems.at[k], device_id=to, device_id_type=pl.DeviceIdType.MESH)

        mine = pltpu.make_async_copy(x_ref, slab(*me), local_sem)
        mine.start()
        first = [copy(0, me, sibling, src=x_ref)]
        first += [copy(1 + j, me, (*chip, c), src=x_ref) for j, chip in enumerate(chips)]
        for cp in first:
            cp.start()
        passed = [copy(4 + j, (*chip, c), sibling) for j, chip in enumerate(chips)]
        for j, chip in enumerate(chips):
            copy(1 + j, (*chip, c), me).wait_recv()
            passed[j].start()
        copy(0, sibling, me).wait_recv()
        for j, chip in enumerate(chips):
            copy(4 + j, (*chip, 1 - c), me).wait_recv()
        for cp in first + passed:
            cp.wait_send()
        mine.wait()

    return pl.pallas_call(
        body, name=name,
        in_specs=[pl.BlockSpec(memory_space=pl.ANY)],
        out_specs=pl.BlockSpec(memory_space=pl.ANY),
        out_shape=jax.ShapeDtypeStruct((N_DEV,) + buf.shape, buf.dtype),
        scratch_shapes=[pltpu.SemaphoreType.DMA((N_DEV - 1,)), pltpu.SemaphoreType.DMA((N_DEV - 1,)),
                        pltpu.SemaphoreType.DMA],
    )(buf)


_HBM = pl.BlockSpec(memory_space=pltpu.HBM)
_SEM = pl.BlockSpec(memory_space=pltpu.SEMAPHORE)
_EFFECT = pltpu.SideEffectType.DATAFLOW_SIDE_EFFECTING


def _peers():
    x, y, c = lax.axis_index("x"), lax.axis_index("y"), lax.axis_index("c")
    peers = []
    for k in range(1, N_DEV):
        px = (1 - x) if (k >> 2) & 1 else x
        py = (1 - y) if (k >> 1) & 1 else y
        pc = (1 - c) if k & 1 else c
        peers.append((k - 1, (px, py, pc), 4 * px + 2 * py + pc))
    return 4 * x + 2 * y + c, peers


def _exchange_start(bufs, *, all_gather, name):
    nb = len(bufs)
    shapes = [((N_DEV,) + b.shape) if all_gather else b.shape for b in bufs]
    ncp = (N_DEV - 1) * nb

    def body(*refs):
        srcs, lands = refs[:nb], refs[nb:2 * nb]
        send_sems, recv_sems = refs[2 * nb], refs[2 * nb + 1]
        token = refs[-1]
        me, peers = _peers()
        for slot, dev, idx in peers:
            for b in range(nb):
                pltpu.make_async_remote_copy(
                    src_ref=srcs[b] if all_gather else srcs[b].at[idx], dst_ref=lands[b].at[me],
                    send_sem=send_sems.at[slot * nb + b], recv_sem=recv_sems.at[slot * nb + b],
                    device_id=dev, device_id_type=pl.DeviceIdType.MESH).start()
        token[...] = jnp.zeros_like(token)

    ins = [pltpu.with_memory_space_constraint(b, pltpu.HBM) for b in bufs]
    ins += [pltpu.with_memory_space_constraint(lax.empty(sh, b.dtype), pltpu.HBM) for sh, b in zip(shapes, bufs)]
    outs = pl.pallas_call(
        body, name=name,
        out_shape=(pltpu.SemaphoreType.DMA((ncp,)), pltpu.SemaphoreType.DMA((ncp,)),
                   *[pltpu.HBM(b.shape, b.dtype) for b in bufs], *[pltpu.HBM(sh, b.dtype) for sh, b in zip(shapes, bufs)],
                   jax.ShapeDtypeStruct((8, 128), F32)),
        in_specs=[_HBM] * (2 * nb),
        out_specs=(_SEM, _SEM, *([_HBM] * (2 * nb)), pl.BlockSpec(memory_space=pltpu.VMEM)),
        input_output_aliases={i: 2 + i for i in range(2 * nb)},
        compiler_params=pltpu.CompilerParams(has_side_effects=_EFFECT),
    )(*ins)
    return outs[:-1], outs[-1]


def _exchange_finish(started, after, *, all_gather, name):
    nb = (len(started) - 2) // 2
    send_sems, recv_sems = started[0], started[1]
    thru = started[2:]

    def body(*refs):
        srcs, lands = refs[:nb], refs[nb:2 * nb]
        ssem, rsem = refs[2 * nb], refs[2 * nb + 1]
        me, peers = _peers()
        for slot, dev, idx in peers:
            for b in range(nb):
                cp = pltpu.make_async_remote_copy(
                    src_ref=srcs[b] if all_gather else srcs[b].at[idx], dst_ref=lands[b].at[idx],
                    send_sem=ssem.at[slot * nb + b], recv_sem=rsem.at[slot * nb + b],
                    device_id=dev, device_id_type=pl.DeviceIdType.MESH)
                cp.wait_send()
                cp.wait_recv()

    done = pl.pallas_call(
        body, name=name,
        out_shape=tuple(pltpu.HBM(t.shape, t.dtype) for t in thru),
        in_specs=[_HBM] * (2 * nb) + [_SEM, _SEM, pl.BlockSpec(memory_space=pl.ANY)],
        out_specs=[_HBM] * (2 * nb),
        input_output_aliases={i: i for i in range(2 * nb)},
        compiler_params=pltpu.CompilerParams(has_side_effects=_EFFECT),
    )(*thru, send_sems, recv_sems, after)
    srcs, lands = done[:nb], done[nb:]
    me = 4 * lax.axis_index("x") + 2 * lax.axis_index("y") + lax.axis_index("c")
    outs = []
    for s_, l_ in zip(srcs, lands):
        own = s_[None] if all_gather else lax.dynamic_slice_in_dim(s_, me, 1, axis=0)
        outs.append(lax.dynamic_update_slice_in_dim(l_, own, me, axis=0))
    return outs


def _sum8(buf, *, name):
    _, r, c = buf.shape
    tr = _pick(r, (256, 128, 64, 32, 16, 8))

    def body(b_ref, o_ref):
        acc = b_ref[0].astype(F32)
        for q in range(1, N_DEV):
            acc = acc + b_ref[q].astype(F32)
        o_ref[...] = acc

    return pl.pallas_call(
        body, name=name,
        grid=(r // tr,),
        in_specs=[pl.BlockSpec((N_DEV, tr, c), lambda i: (0, i, 0))],
        out_specs=pl.BlockSpec((tr, c), lambda i: (i, 0)),
        out_shape=jax.ShapeDtypeStruct((r, c), F32),
        compiler_params=_cparams(("parallel",)),
    )(buf)


def _pack_rows(flat, mult):
    n = flat.shape[0]
    chunk = mult * PACK_COLS
    tot = -(-n // chunk) * chunk
    return jnp.pad(flat, (0, tot - n)).reshape(tot // PACK_COLS, PACK_COLS)


_ADAM_BC1 = 1.0 - ADAM_B1 ** ADAM_STEP
_ADAM_BC2 = 1.0 - ADAM_B2 ** ADAM_STEP


def _adamw_math(wv, gv, mv, vv):
    mn = ADAM_B1 * mv + (1.0 - ADAM_B1) * gv
    vn = ADAM_B2 * vv + (1.0 - ADAM_B2) * (gv * gv)
    m_hat = mn / _ADAM_BC1
    v_hat = vn / _ADAM_BC2
    delta = -ADAM_LR * (m_hat / (jnp.sqrt(v_hat) + ADAM_EPS) + ADAM_WD * wv)
    return delta, mn, vn


def _adamw(w, g, m, v, *, name):
    shape = w.shape
    cols = shape[-1]
    to2d = lambda a: a.reshape(-1, cols)
    d, mn, vn = _ew(_adamw_math, [to2d(w), to2d(g), to2d(m), to2d(v)], [(cols, F32)] * 3, name=name)
    return d.reshape(shape), mn.reshape(shape), vn.reshape(shape)


def _adamw_sharded(w, parts, m, v, *, name):
    depth = w.shape[0]
    cols = w.shape[-1]
    rows = w[0].size // cols
    to3 = lambda a: a.reshape(depth, rows, cols)
    w3, m3, v3 = to3(w), to3(m), to3(v)
    bpr = cols * (3 * 4 + N_DEV * parts[0].dtype.itemsize + 4 * 4)
    tm = _row_tile(rows, bpr)
    outs = None
    for li in range(depth):
        carried = li > 0

        def body(*refs):
            w_ref, p_ref, m_ref, v_ref = refs[:4]
            g_ref, d_ref, mo_ref, vo_ref = refs[-4:]
            gv = p_ref[0].astype(F32)
            for q in range(1, N_DEV):
                gv = gv + p_ref[q].astype(F32)
            delta, mn, vn = _adamw_math(w_ref[0], gv, m_ref[0], v_ref[0])
            g_ref[0] = gv
            d_ref[0] = delta
            mo_ref[0] = mn
            vo_ref[0] = vn

        lay = pl.BlockSpec((1, tm, cols), functools.partial(lambda i, l: (l, i, 0), l=li))
        any_spec = pl.BlockSpec(memory_space=pl.ANY)
        ins = [w3, parts[li].reshape(N_DEV, rows, cols), m3, v3] + (list(outs) if carried else [])
        outs = pl.pallas_call(
            body, name=f"{name}_l{li}",
            grid=(rows // tm,),
            in_specs=[lay, pl.BlockSpec((N_DEV, tm, cols), lambda i: (0, i, 0)), lay, lay]
            + ([any_spec] * 4 if carried else []),
            out_specs=[lay] * 4,
            out_shape=[jax.ShapeDtypeStruct((depth, rows, cols), F32)] * 4,
            input_output_aliases={4 + q: q for q in range(4)} if carried else {},
            compiler_params=_cparams(("parallel",)),
        )(*ins)
    return [o.reshape(w.shape) for o in outs]


def _branch_merge(ys, w_branch, logits, *, name):
    s = ys.shape[0]
    d = w_branch.shape[2]
    tm, tn = _pick(s, (512, 256, 128)), _pick(d, (1024, 512, 256, 128))

    def body(ys_ref, w_ref, l0_ref, l1_ref, l2_ref, m_ref, b0_ref, b1_ref, b2_ref):
        merged = None
        for b, (l_ref, b_ref) in enumerate(((l0_ref, b0_ref), (l1_ref, b1_ref), (l2_ref, b2_ref))):
            acc = _dot(ys_ref[:, b * BRANCH_W:(b + 1) * BRANCH_W], w_ref[b])
            b_ref[...] = acc.astype(b_ref.dtype)
            term = _sigmoid(l_ref[...]) * acc
            merged = term if merged is None else merged + term
        m_ref[...] = merged.astype(m_ref.dtype)

    nj = d // tn
    tile = pl.BlockSpec((tm, tn), lambda i, j: (i, j))
    logit = lambda b: pl.BlockSpec((tm, tn), functools.partial(lambda i, j, o: (i, j + o), o=b * nj))
    return pl.pallas_call(
        body, name=name,
        grid=(s // tm, nj),
        in_specs=[pl.BlockSpec((tm, N_BRANCH * BRANCH_W), lambda i, j: (i, 0)),
                  pl.BlockSpec((N_BRANCH, BRANCH_W, tn), lambda i, j: (0, 0, j)), logit(0), logit(1), logit(2)],
        out_specs=[tile] * 4,
        out_shape=[jax.ShapeDtypeStruct((s, d), BF16)] * 4,
        compiler_params=_cparams(("parallel", "parallel")),
    )(ys, w_branch, logits, logits, logits)


def _prep_in(w_in):
    rope_cols = jnp.pad(w_in[:, 2048:2048 + MLA_ROPE], ((0, 0), (0, 128 - MLA_ROPE)))
    return dict(wa=w_in[:, :2048], wb=w_in[:, 2112:5184], wc=w_in[:, 5184:8256], wd=w_in[:, 8256:14400], we=rope_cols)


def _prep_rest(full):
    wq = full["mla_w_uq"].reshape(512, N_HEADS, MLA_NOPE + MLA_ROPE)
    wkv = full["mla_w_ukv"].reshape(512, N_HEADS, 2 * HEAD_D)
    return dict(
        w_uq=jnp.pad(wq, ((0, 0), (0, 0), (0, 2 * HEAD_D - MLA_NOPE - MLA_ROPE))).reshape(512, 2 * N_HEADS * HEAD_D),
        w_ukv=jnp.concatenate([wkv[:, :, :HEAD_D].reshape(512, 1024), wkv[:, :, HEAD_D:].reshape(512, 1024)], axis=1),
        w_glu=full["ssm_w_glu"], w_branch=full["w_branch"], w_out=full["w_out"],
        w_ple_gate=full["w_ple_gate"], w_ple_proj=full["w_ple_proj"])


def _prep_ssm(sp):
    ab_re, ab_im, bb_re, bb_im = _s5_disc(sp["lam_re"], sp["lam_im"], sp["log_dt"], sp["b_re"], sp["b_im"])
    g4 = lambda a: a.reshape(SSM_GB, SSM_GB, a.shape[1], a.shape[2])
    bbd_re = _blockdiag(jnp.swapaxes(g4(bb_re), 2, 3)).astype(BF16)
    bbd_im = _blockdiag(jnp.swapaxes(g4(bb_im), 2, 3)).astype(BF16)
    cd_re = _blockdiag(jnp.swapaxes(g4(sp["c_re"]), 2, 3)).astype(BF16)
    cd_im = _blockdiag(jnp.swapaxes(g4(sp["c_im"]), 2, 3)).astype(BF16)
    return dict(bb_re=bbd_re, bb_im=bbd_im, bbt_re=jnp.swapaxes(bbd_re, 1, 2), bbt_im=jnp.swapaxes(bbd_im, 1, 2),
                c_re=cd_re, c_im=cd_im, ct_re=jnp.swapaxes(cd_re, 1, 2), ct_im=jnp.swapaxes(cd_im, 1, 2),
                dsk=sp["d"].reshape(1, BRANCH_W))


GELU_C = math.sqrt(2.0 / math.pi)


def _gelu(x):
    return 0.5 * x * (1.0 + jnp.tanh(GELU_C * (x + 0.044715 * x * x * x)))


def _gelu_grad(x):
    t = jnp.tanh(GELU_C * (x + 0.044715 * x * x * x))
    return 0.5 * (1.0 + t) + 0.5 * x * (1.0 - t * t) * GELU_C * (1.0 + 3.0 * 0.044715 * x * x)


def _layer_fwd(x_in, p_l, hooks, ssm, tabs, sm, rope, li):
    tg = lambda nm: nm
    ct, sp_, sm_ = rope
    h = _rmsnorm(x_in, sm["ln_g"], name=tg("ln"))
    w = dict(hooks.weights_in(li, x_in))
    seg_a = _mm(h, w["wa"], name=tg("in_a"))
    seg_b = _mm(h, w["wb"], name=tg("in_b"), out_dtypes=(BF16,))
    seg_c = _mm(h, w["wc"], name=tg("in_c"))
    seg_d = _mm(h, w["wd"], name=tg("in_d"))
    seg_e = _mm(h, w["we"], name=tg("in_e"))

    o_sb, tot = _sb_fwd(seg_b, name=tg("sb_fwd"))
    w.update(hooks.weights_rest(li, o_sb))

    y_pre, hc_re, hc_im, hs_re, hs_im = _s5_fwd(seg_a, ssm["bb_re"], ssm["bb_im"], ssm["c_re"], ssm["c_im"], ssm["dsk"], tabs,
                                  name=tg("s5_fwd"))
    yg = _ew(_gelu, [y_pre], [(BRANCH_W, F32)], name=tg("gelu"))
    y_ssm, z_glu = _mm(yg, w["w_glu"], name=tg("glu"), extras=[(yg, 0)],
                       epilogue=lambda acc, ygv: (ygv * _sigmoid(acc), acc), out_dtypes=(F32, F32))

    q_lat = seg_a[:, 1024:1536]
    kv_lat = seg_a[:, 1536:2048]
    qn = _rmsnorm(q_lat, sm["mla_g_q"], name=tg("q_norm"))
    kvn = _rmsnorm(kv_lat, sm["mla_g_kv"], name=tg("kv_norm"))
    q_raw = _mm(qn, w["w_uq"], name=tg("uq"))
    kv_b = _mm(kvn, w["w_ukv"], name=tg("ukv"), out_dtypes=(BF16,))

    def rope_q(qv, c_, p_, m_):
        outs = []
        for hh in range(N_HEADS):
            outs.append(qv[:, 256 * hh:256 * hh + 128])
            outs.append(_rope_group(qv[:, 256 * hh + 128:256 * hh + 256], c_, p_, m_))
        return jnp.concatenate(outs, axis=1)

    def rope_k(kvv, kr, c_, p_, m_):
        kr_rot = _rope_group(kr, c_, p_, m_).astype(BF16)
        outs = []
        for hh in range(N_HEADS):
            outs.append(kvv[:, 128 * hh:128 * hh + 128])
            outs.append(kr_rot)
        return jnp.concatenate(outs, axis=1)

    q_b = _ew(rope_q, [q_raw, ct, sp_, sm_], [(2048, BF16)], name=tg("rope_q"))
    k_b = _ew(rope_k, [kv_b, seg_e, ct, sp_, sm_], [(2048, BF16)], name=tg("rope_k"))
    o_mla, lse = _mla_fwd(q_b, k_b, kv_b, name=tg("mla_fwd"))

    def gate_fn(a, b, c_, gp):
        ys = jnp.concatenate([a, b, c_], axis=1)
        return ys * (gp * _sigmoid(gp))

    ys = _ew(gate_fn, [y_ssm, o_mla, o_sb, seg_c], [(3 * BRANCH_W, BF16)], name=tg("gate"))
    merged_b, *bos = _branch_merge(ys, w["w_branch"], seg_d, name=tg("branch"))
    x1 = _mm(merged_b, w["w_out"], name=tg("out"), extras=[(x_in, 0)], epilogue=lambda acc, xv: (xv + acc,))

    hn2 = _rmsnorm(x1, sm["ple_g"], name=tg("ple_norm"))
    e = _mm(p_l, w["w_ple_proj"], name=tg("ple_proj"))
    x2, gl = _mm(hn2, w["w_ple_gate"], name=tg("ple_gate"), extras=[(x1, 0), (e, 0)],
                 epilogue=lambda acc, xv, ev: (xv + _sigmoid(acc) * ev, acc), out_dtypes=(F32, F32))
    saved = dict(x_in=x_in, h=h, seg_a=seg_a, seg_b=seg_b, seg_c=seg_c, seg_d=seg_d, y_pre=y_pre, hc_re=hc_re,
                 hc_im=hc_im, hs_re=hs_re, hs_im=hs_im, yg=yg, y_ssm=y_ssm, z_glu=z_glu, q_lat=q_lat, kv_lat=kv_lat, qn=qn, kvn=kvn, q_b=q_b,
                 kv_b=kv_b, k_b=k_b, o_mla=o_mla, lse=lse, o_sb=o_sb, tot=tot, ys=ys, bos=bos, merged_b=merged_b,
                 x1=x1, hn2=hn2, e=e, gl=gl, p_l=p_l, w=w)
    return x2, saved


def _layer_bwd(dx2, sv, hooks, ssm, tabs, sm, rope, li):
    tg = lambda nm: nm
    w = sv["w"]
    ct, sp_, sm_ = rope
    gw, gs = {}, {}

    def ple_fn(d, ev, glv):
        gt = _sigmoid(glv)
        return d * ev * gt * (1.0 - gt), d * gt

    dgl, de = _ew(ple_fn, [dx2, sv["e"], sv["gl"]], [(D_MODEL, BF16), (D_MODEL, BF16)], name=tg("ple_bwd"))
    gw["w_ple_proj"] = _mm(sv["p_l"], de, ta=True, out_dtypes=(BF16,), name=tg("d_ple_proj"))
    gw["w_ple_gate"] = _mm(sv["hn2"], dgl, ta=True, out_dtypes=(BF16,), name=tg("d_ple_gate"))
    dhn2 = _mm(dgl, w["w_ple_gate"], tb=True, name=tg("ple_gate_t"))
    dx1, gs["ple_g"] = _rmsnorm_bwd(sv["x1"], sm["ple_g"], dhn2, dx2, name=tg("ple_norm_bwd"))

    dx1_b = dx1.astype(BF16)
    gw["w_out"] = _mm(sv["merged_b"], dx1_b, ta=True, out_dtypes=(BF16,), name=tg("d_out"))
    dmerged = _mm(dx1_b, w["w_out"], tb=True, name=tg("out_t"))

    def merge_fn(dm, ml, b0, b1, b2):
        dbo, dml = [], []
        for b, bo in enumerate((b0, b1, b2)):
            gt = _sigmoid(ml[:, b * D_MODEL:(b + 1) * D_MODEL])
            dbo.append(dm * gt)
            dml.append(dm * bo.astype(F32) * gt * (1.0 - gt))
        return jnp.concatenate(dbo, axis=1), jnp.concatenate(dml, axis=1)

    dbo, dseg_d = _ew(merge_fn, [dmerged, sv["seg_d"]] + sv["bos"], [(3 * D_MODEL, BF16), (3 * D_MODEL, BF16)],
                      name=tg("merge_bwd"))
    dys, dwb = [], []
    for b in range(3):
        dbo_b = dbo[:, b * D_MODEL:(b + 1) * D_MODEL]
        dwb.append(_mm(sv["ys"][:, b * BRANCH_W:(b + 1) * BRANCH_W], dbo_b, ta=True, out_dtypes=(BF16,), name=tg(f"d_branch{b}")))
        dys.append(_mm(dbo_b, w["w_branch"][b], tb=True, name=tg(f"branch{b}_t")))
    gw["w_branch"] = jnp.stack(dwb)

    def gate_bwd(d0, d1, d2, a, b, c_, gp):
        dy_all = jnp.concatenate([d0, d1, d2], axis=1)
        y_all = jnp.concatenate([a, b, c_], axis=1)
        sg = _sigmoid(gp)
        silu = gp * sg
        dsilu = sg * (1.0 + gp * (1.0 - sg))
        dyv = dy_all * silu
        return (dyv[:, :BRANCH_W], dyv[:, BRANCH_W:2 * BRANCH_W], dyv[:, 2 * BRANCH_W:], dy_all * y_all * dsilu)

    dy_ssm, do_mla, do_sb, dseg_c = _ew(
        gate_bwd, dys + [sv["y_ssm"], sv["o_mla"], sv["o_sb"], sv["seg_c"]],
        [(BRANCH_W, F32), (BRANCH_W, F32), (BRANCH_W, F32), (3 * BRANCH_W, BF16)], name=tg("gate_bwd"))

    dq_sb, dk_sb, dv_sb = _sb_bwd(sv["seg_b"], do_sb, sv["tot"], name=tg("sb_bwd"))
    dseg_b = jnp.concatenate([dq_sb, dk_sb, dv_sb], axis=1).astype(BF16)

    dq_, dk_, dv_ = _mla_bwd(sv["q_b"], sv["k_b"], sv["kv_b"], sv["o_mla"], do_mla, sv["lse"], name=tg("mla_bwd"))

    def unrope_q(dq, c_, p_, m_):
        outs = []
        for hh in range(N_HEADS):
            outs.append(dq[:, 256 * hh:256 * hh + 128])
            outs.append(_rope_group_t(dq[:, 256 * hh + 128:256 * hh + 256], c_, p_, m_))
        return jnp.concatenate(outs, axis=1)

    dq_raw = _ew(unrope_q, [dq_, ct, sp_, sm_], [(2048, BF16)], name=tg("unrope_q"))

    def unrope_k(dk, dvv, c_, p_, m_):
        tot_ = dk[:, 128:256]
        for hh in range(1, N_HEADS):
            tot_ = tot_ + dk[:, 256 * hh + 128:256 * hh + 256]
        dkn = [dk[:, 256 * hh:256 * hh + 128] for hh in range(N_HEADS)]
        return jnp.concatenate(dkn + [dvv], axis=1), _rope_group_t(tot_, c_, p_, m_)

    dkv_raw, dseg_e = _ew(unrope_k, [dk_, dv_, ct, sp_, sm_], [(2048, BF16), (128, BF16)], name=tg("unrope_k"))
    dw_uq = _mm(sv["qn"], dq_raw, ta=True, out_dtypes=(BF16,), name=tg("d_uq"))
    dw_ukv = _mm(sv["kvn"], dkv_raw, ta=True, out_dtypes=(BF16,), name=tg("d_ukv"))
    dqn = _mm(dq_raw, w["w_uq"], tb=True, name=tg("uq_t"))
    dkvn = _mm(dkv_raw, w["w_ukv"], tb=True, name=tg("ukv_t"))
    dq_lat, gs["mla_g_q"] = _rmsnorm_bwd(sv["q_lat"], sm["mla_g_q"], dqn, None, name=tg("q_norm_bwd"))
    dkv_lat, gs["mla_g_kv"] = _rmsnorm_bwd(sv["kv_lat"], sm["mla_g_kv"], dkvn, None, name=tg("kv_norm_bwd"))
    gw["mla_w_uq"] = dw_uq.reshape(512, N_HEADS, 2 * HEAD_D)[:, :, :MLA_NOPE + MLA_ROPE].reshape(
        512, N_HEADS * (MLA_NOPE + MLA_ROPE))
    dwk = dw_ukv[:, :1024].reshape(512, N_HEADS, HEAD_D)
    dwv = dw_ukv[:, 1024:].reshape(512, N_HEADS, HEAD_D)
    gw["mla_w_ukv"] = jnp.concatenate([dwk, dwv], axis=2).reshape(512, 2 * N_HEADS * HEAD_D)

    def glu_bwd(d, ygv, zv):
        sg = _sigmoid(zv)
        return d * ygv * sg * (1.0 - sg), d * sg

    dz, dyg0 = _ew(glu_bwd, [dy_ssm, sv["yg"], sv["z_glu"]], [(BRANCH_W, BF16), (BRANCH_W, F32)], name=tg("glu_bwd"))
    gw["ssm_w_glu"] = _mm(sv["yg"], dz, ta=True, out_dtypes=(BF16,), name=tg("d_glu"))
    dy_pre = _mm(dz, w["w_glu"], tb=True, name=tg("glu_t"), extras=[(dyg0, 0), (sv["y_pre"], 0)],
                 epilogue=lambda acc, d0, yp: ((acc + d0) * _gelu_grad(yp),))
    token = hooks.grads_rest(li, gw)
    dsk = ssm["dsk"] if token is None else ssm["dsk"] + token[0, 0]
    du, dbb_re, dbb_im, dc_re, dc_im, da_re, da_im, dd = _s5_bwd(
        sv["seg_a"], dy_pre, sv["hc_re"], sv["hc_im"], sv["hs_re"], sv["hs_im"], ssm["bbt_re"], ssm["bbt_im"],
        ssm["ct_re"], ssm["ct_im"], dsk, tabs, name=tg("s5_bwd"))
    gs["ssm_d"] = dd.reshape(BRANCH_W)
    gs["ssm_c_re"] = jnp.swapaxes(_blockdiag_extract(dc_re, SSM_STATE, SSM_GROUP), 2, 3).reshape(
        SSM_GROUPS, SSM_GROUP, SSM_STATE)
    gs["ssm_c_im"] = jnp.swapaxes(_blockdiag_extract(dc_im, SSM_STATE, SSM_GROUP), 2, 3).reshape(
        SSM_GROUPS, SSM_GROUP, SSM_STATE)
    gs["_dbb_re"] = jnp.swapaxes(_blockdiag_extract(dbb_re, SSM_GROUP, SSM_STATE), 2, 3).reshape(
        SSM_GROUPS, SSM_STATE, SSM_GROUP)
    gs["_dbb_im"] = jnp.swapaxes(_blockdiag_extract(dbb_im, SSM_GROUP, SSM_STATE), 2, 3).reshape(
        SSM_GROUPS, SSM_STATE, SSM_GROUP)
    gs["_da_re"] = da_re.reshape(SSM_GROUPS, SSM_STATE)
    gs["_da_im"] = da_im.reshape(SSM_GROUPS, SSM_STATE)

    dseg_a = jnp.concatenate([du, dq_lat, dkv_lat], axis=1).astype(BF16)
    hb = sv["h"]
    dwa = _mm(hb, dseg_a, ta=True, out_dtypes=(BF16,), name=tg("d_in_a"))
    dwb_ = _mm(hb, dseg_b, ta=True, out_dtypes=(BF16,), name=tg("d_in_b"))
    dwc = _mm(hb, dseg_c, ta=True, out_dtypes=(BF16,), name=tg("d_in_c"))
    dwd = _mm(hb, dseg_d, ta=True, out_dtypes=(BF16,), name=tg("d_in_d"))
    dwe = _mm(hb, dseg_e, ta=True, out_dtypes=(BF16,), name=tg("d_in_e"))
    gw["w_in"] = jnp.concatenate([dwa, dwe[:, :MLA_ROPE], dwb_, dwc, dwd], axis=1)
    token = hooks.grads_in(li, gw["w_in"])
    wa = w["wa"] if token is None else w["wa"] + token[0, 0].astype(BF16)
    add = lambda acc, prev: (acc + prev,)
    dh = _mm(dseg_a, wa, tb=True, name=tg("in_a_t"))
    dh = _mm(dseg_b, w["wb"], tb=True, name=tg("in_b_t"), extras=[(dh, 0)], epilogue=add)
    dh = _mm(dseg_c, w["wc"], tb=True, name=tg("in_c_t"), extras=[(dh, 0)], epilogue=add)
    dh = _mm(dseg_d, w["wd"], tb=True, name=tg("in_d_t"), extras=[(dh, 0)], epilogue=add)
    dh = _mm(dseg_e, w["we"], tb=True, name=tg("in_e_t"), extras=[(dh, 0)], epilogue=add)
    dx_in, gs["ln_g"] = _rmsnorm_bwd(sv["x_in"], sm["ln_g"], dh, dx1, name=tg("ln_bwd"))
    return dx_in, gw, gs


class _LocalWeights:
    def __init__(self, fulls):
        self.fulls = fulls

    def weights_in(self, li, after):
        return _prep_in(self.fulls[li]["w_in"])

    def weights_rest(self, li, after):
        return _prep_rest(self.fulls[li])

    def grads_rest(self, li, gw):
        return None

    def grads_in(self, li, g):
        return None

    def grads_layer(self, li, gw):
        return None


def _local_step(x, p, target, hooks, small):
    s = x.shape[0]
    depth = small["ln_g"].shape[0]
    rope = _rope_tables(s)
    ssms, tabss, sms, saved = [], [], [], []
    xc = x
    for li in range(depth):
        sp = dict(lam_re=small["ssm_lam_re"][li], lam_im=small["ssm_lam_im"][li], log_dt=small["ssm_log_dt"][li],
                  b_re=small["ssm_b_re"][li], b_im=small["ssm_b_im"][li], c_re=small["ssm_c_re"][li],
                  c_im=small["ssm_c_im"][li], d=small["ssm_d"][li])
        ssm = _prep_ssm(sp)
        tabs = _s5_tables(sp["lam_re"], sp["lam_im"], sp["log_dt"])
        sm = dict(ln_g=small["ln_g"][li], mla_g_q=small["mla_g_q"][li], mla_g_kv=small["mla_g_kv"][li],
                  ple_g=small["ple_g"][li])
        xc, sv = _layer_fwd(xc, p[li], hooks, ssm, tabs, sm, rope, li)
        ssms.append(ssm), tabss.append(tabs), sms.append(sm), saved.append(sv)
    dx, dgf, loss = _loss_head(xc, small["final_g"], target, name="loss_head")
    gws, gss = [None] * depth, [None] * depth
    token = None
    for li in reversed(range(depth)):
        sm = sms[li] if token is None else dict(sms[li], ple_g=sms[li]["ple_g"] + token[0, 0])
        dx, gw, gs = _layer_bwd(dx, saved[li], hooks, ssms[li], tabss[li], sm, rope, li)
        token = hooks.grads_layer(li, gw)
        args = (small["ssm_lam_re"][li], small["ssm_lam_im"][li], small["ssm_log_dt"][li], small["ssm_b_re"][li],
                small["ssm_b_im"][li])
        _, vjp = jax.vjp(_s5_disc, *args)
        g_lr, g_li, g_dt, g_br, g_bi = vjp((gs.pop("_da_re"), gs.pop("_da_im"), gs.pop("_dbb_re"), gs.pop("_dbb_im")))
        gs.update(ssm_lam_re=g_lr, ssm_lam_im=g_li, ssm_log_dt=g_dt, ssm_b_re=g_br, ssm_b_im=g_bi)
        gws[li], gss[li] = gw, gs
    gsmall = {nm: jnp.stack([gss[li][nm].reshape(small[nm].shape[1:]) for li in range(depth)])
              for nm in SMALL if nm != "final_g"}
    gsmall["final_g"] = dgf.reshape(-1)
    return loss[0, 0], dx, gws, gsmall


REST = tuple(t for t in SHARDED if t[0] != "w_in")


def _to_full(spec, got):
    _, fshape, ax = spec
    return jnp.moveaxis(got, 0, ax).reshape(fshape)


def _to_slabs(spec, g):
    _, fshape, ax = spec
    g = g.reshape(tuple(fshape[:ax]) + (N_DEV, fshape[ax] // N_DEV) + tuple(fshape[ax + 1:]))
    return jnp.moveaxis(g, ax, 0)


class _Exchanged:
    def __init__(self, shards):
        bf = lambda nm, li: shards[nm][li].astype(BF16)
        w_in0 = _gather_via_sibling(bf("w_in", 0), name="gather_w_in")
        self.w_in0 = _to_full(SHARDED[0], w_in0)
        first = bf(REST[-1][0], 0) + (w_in0[0, 0, 0] * 0).astype(BF16)
        rest0 = [bf(nm, 0) for nm, _, _ in REST[:-1]] + [first]
        self.rest0, tok_a = _exchange_start(rest0, all_gather=True, name="gather_rest_start")
        all1 = [bf(nm, 1) for nm, _, _ in SHARDED[:-1]] + [bf(SHARDED[-1][0], 1) + tok_a[0, 0].astype(BF16)]
        self.all1, tok_b = _exchange_start(all1, all_gather=True, name="gather_next_start")
        self.token = tok_a[0, 0] + tok_b[0, 0]
        self.full1 = None
        self.sent = {}

    def weights_in(self, li, after):
        if li == 0:
            return _prep_in(self.w_in0)
        got = _exchange_finish(self.all1, after, all_gather=True, name="gather_next_finish")
        self.full1 = {t[0]: _to_full(t, g) for t, g in zip(SHARDED, got)}
        return _prep_in(self.full1["w_in"])

    def weights_rest(self, li, after):
        if li == 1:
            return _prep_rest(self.full1)
        got = _exchange_finish(self.rest0, after, all_gather=True, name="gather_rest_finish")
        return _prep_rest({t[0]: _to_full(t, g) for t, g in zip(REST, got)})

    def grads_rest(self, li, gw):
        if li != 0:
            return None
        self.sent["rest0"], token = _exchange_start([_to_slabs(t, gw[t[0]]) for t in REST], all_gather=False,
                                                    name="scatter_rest_start")
        return token

    def grads_in(self, li, g):
        if li != 0:
            return None
        self.sent["w_in0"], token = _exchange_start([_to_slabs(SHARDED[0], g)], all_gather=False,
                                                    name="scatter_w_in_start")
        return token

    def grads_layer(self, li, gw):
        if li == 0:
            return None
        self.sent["all1"], token = _exchange_start([_to_slabs(t, gw[t[0]]) for t in SHARDED], all_gather=False,
                                                   name="scatter_prev_start")
        return token

    def partial_grads(self, after):
        got1 = _exchange_finish(self.sent["all1"], after, all_gather=False, name="scatter_prev_finish")
        got0 = _exchange_finish(self.sent["rest0"], after, all_gather=False, name="scatter_rest_finish")
        w_in0 = _exchange_finish(self.sent["w_in0"], after, all_gather=False, name="scatter_w_in_finish")[0]
        p0 = {t[0]: g for t, g in zip(REST, got0)}
        p0["w_in"] = w_in0
        return [p0, {t[0]: g for t, g in zip(SHARDED, got1)}]


def kernel(x, p, ln_g, w_in, ssm_lam_re, ssm_lam_im, ssm_log_dt, ssm_b_re, ssm_b_im, ssm_c_re, ssm_c_im, ssm_d, ssm_w_glu, mla_g_q, mla_g_kv, mla_w_uq, mla_w_ukv, w_branch, w_out, ple_g, w_ple_gate, w_ple_proj, final_g, loss_target, m_ln_g, m_w_in, m_ssm_lam_re, m_ssm_lam_im, m_ssm_log_dt, m_ssm_b_re, m_ssm_b_im, m_ssm_c_re, m_ssm_c_im, m_ssm_d, m_ssm_w_glu, m_mla_g_q, m_mla_g_kv, m_mla_w_uq, m_mla_w_ukv, m_w_branch, m_w_out, m_ple_g, m_w_ple_gate, m_w_ple_proj, m_final_g, v_ln_g, v_w_in, v_ssm_lam_re, v_ssm_lam_im, v_ssm_log_dt, v_ssm_b_re, v_ssm_b_im, v_ssm_c_re, v_ssm_c_im, v_ssm_d, v_ssm_w_glu, v_mla_g_q, v_mla_g_kv, v_mla_w_uq, v_mla_w_ukv, v_w_branch, v_w_out, v_ple_g, v_w_ple_gate, v_w_ple_proj, v_final_g):
    weights = dict(ln_g=ln_g, w_in=w_in, ssm_lam_re=ssm_lam_re, ssm_lam_im=ssm_lam_im, ssm_log_dt=ssm_log_dt,
                   ssm_b_re=ssm_b_re, ssm_b_im=ssm_b_im, ssm_c_re=ssm_c_re, ssm_c_im=ssm_c_im, ssm_d=ssm_d,
                   ssm_w_glu=ssm_w_glu, mla_g_q=mla_g_q, mla_g_kv=mla_g_kv, mla_w_uq=mla_w_uq, mla_w_ukv=mla_w_ukv,
                   w_branch=w_branch, w_out=w_out, ple_g=ple_g, w_ple_gate=w_ple_gate, w_ple_proj=w_ple_proj,
                   final_g=final_g)
    mom_m = dict(ln_g=m_ln_g, w_in=m_w_in, ssm_lam_re=m_ssm_lam_re, ssm_lam_im=m_ssm_lam_im, ssm_log_dt=m_ssm_log_dt,
                 ssm_b_re=m_ssm_b_re, ssm_b_im=m_ssm_b_im, ssm_c_re=m_ssm_c_re, ssm_c_im=m_ssm_c_im, ssm_d=m_ssm_d,
                 ssm_w_glu=m_ssm_w_glu, mla_g_q=m_mla_g_q, mla_g_kv=m_mla_g_kv, mla_w_uq=m_mla_w_uq,
                 mla_w_ukv=m_mla_w_ukv, w_branch=m_w_branch, w_out=m_w_out, ple_g=m_ple_g, w_ple_gate=m_w_ple_gate,
                 w_ple_proj=m_w_ple_proj, final_g=m_final_g)
    mom_v = dict(ln_g=v_ln_g, w_in=v_w_in, ssm_lam_re=v_ssm_lam_re, ssm_lam_im=v_ssm_lam_im, ssm_log_dt=v_ssm_log_dt,
                 ssm_b_re=v_ssm_b_re, ssm_b_im=v_ssm_b_im, ssm_c_re=v_ssm_c_re, ssm_c_im=v_ssm_c_im, ssm_d=v_ssm_d,
                 ssm_w_glu=v_ssm_w_glu, mla_g_q=v_mla_g_q, mla_g_kv=v_mla_g_kv, mla_w_uq=v_mla_w_uq,
                 mla_w_ukv=v_mla_w_ukv, w_branch=v_w_branch, w_out=v_w_out, ple_g=v_ple_g, w_ple_gate=v_w_ple_gate,
                 w_ple_proj=v_w_ple_proj, final_g=v_final_g)
    depth = ln_g.shape[0]
    assert depth == 2
    hooks = _Exchanged(weights)
    small = {nm: weights[nm] for nm in SMALL}
    small["ln_g"] = small["ln_g"].at[0, 0].add(hooks.token)
    loss_local, dx, gws, gsmall = _local_step(x[0], p[:, 0], loss_target[0], hooks, small)
    loss = lax.psum(loss_local, ("x", "y", "c"))

    grads, delta, new_m, new_v = {}, {}, {}, {}
    flat = jnp.concatenate([gsmall[nm].reshape(-1) for nm in SMALL])
    n_small = flat.shape[0]
    got = _exchange([_pack_rows(flat, 8)], all_gather=True, name="gather_small_grads")[0]
    gsum = _sum8(got, name="sum_small_grads").reshape(-1)[:n_small]
    off = 0
    for nm in SMALL:
        cnt = weights[nm].size
        grads[nm] = gsum[off:off + cnt].reshape(weights[nm].shape)
        off += cnt
    pk = lambda d: _pack_rows(jnp.concatenate([d[nm].reshape(-1) for nm in SMALL]), 8)
    d_s, m_s, v_s = _adamw(pk(weights), pk(grads), pk(mom_m), pk(mom_v), name="adamw_small")
    off = 0
    for nm in SMALL:
        cnt = weights[nm].size
        shp = weights[nm].shape
        delta[nm] = d_s.reshape(-1)[off:off + cnt].reshape(shp)
        new_m[nm] = m_s.reshape(-1)[off:off + cnt].reshape(shp)
        new_v[nm] = v_s.reshape(-1)[off:off + cnt].reshape(shp)
        off += cnt

    parts = hooks.partial_grads(dx)
    for nm, _, _ in SHARDED:
        grads[nm], delta[nm], new_m[nm], new_v[nm] = _adamw_sharded(
            weights[nm], [parts[li][nm] for li in range(depth)], mom_m[nm], mom_v[nm], name=f"adamw_{nm}")
    return (loss, dx[None], *[grads[nm] for nm in WEIGHT_ORDER], *[delta[nm] for nm in WEIGHT_ORDER],
            *[new_m[nm] for nm in WEIGHT_ORDER], *[new_v[nm] for nm in WEIGHT_ORDER])
```

```python
import functools
import math

import jax
import jax.numpy as jnp
from jax import lax
from jax.experimental import pallas as pl
from jax.experimental.pallas import tpu as pltpu

F32 = jnp.float32
BF16 = jnp.bfloat16

N_DEV = 8
D_MODEL = 2048
BRANCH_W = 1024
N_HEADS = 8
N_BRANCH = 3
HEAD_D = 128
SSM_GROUPS = 64
SSM_GROUP = 16
SSM_STATE = 64
SSM_GB = 8
SSM_CHUNK = 512
MLA_ROPE = 64
MLA_NOPE = 128
ROPE_THETA = 10000.0
NORM_EPS = 1e-6
DT_MIN = 1e-3
DT_MAX = 1e-1
PACK_COLS = 1024

ADAM_LR = 0.001
ADAM_B1 = 0.9
ADAM_B2 = 0.999
ADAM_EPS = 1e-08
ADAM_WD = 0.01
ADAM_STEP = 10

VMEM_LIMIT = 56 * 2 ** 20
NEG_BIG = -1e30
MM_VMEM_BUDGET = 36 * 2 ** 20

SHARDED = (
    ("w_in", (2048, 14400), 1),
    ("ssm_w_glu", (1024, 1024), 0),
    ("mla_w_uq", (512, 1536), 1),
    ("mla_w_ukv", (512, 2048), 1),
    ("w_branch", (3, 1024, 2048), 2),
    ("w_out", (2048, 2048), 0),
    ("w_ple_gate", (2048, 2048), 0),
    ("w_ple_proj", (256, 2048), 1),
)
SMALL = ("ln_g", "ssm_lam_re", "ssm_lam_im", "ssm_log_dt", "ssm_b_re", "ssm_b_im", "ssm_c_re", "ssm_c_im",
         "ssm_d", "mla_g_q", "mla_g_kv", "ple_g", "final_g")
WEIGHT_ORDER = ("ln_g", "w_in", "ssm_lam_re", "ssm_lam_im", "ssm_log_dt", "ssm_b_re", "ssm_b_im", "ssm_c_re",
                "ssm_c_im", "ssm_d", "ssm_w_glu", "mla_g_q", "mla_g_kv", "mla_w_uq", "mla_w_ukv", "w_branch",
                "w_out", "ple_g", "w_ple_gate", "w_ple_proj", "final_g")


def _cparams(sem=None):
    return pltpu.CompilerParams(dimension_semantics=sem, vmem_limit_bytes=VMEM_LIMIT)


def _pick(n, prefs):
    for t in prefs:
        if n % t == 0:
            return t
    return n


def _dot(a, b):
    return lax.dot_general(a, b, (((1,), (0,)), ((), ())), preferred_element_type=F32)


def _dot_nt(a, b):
    return lax.dot_general(a, b, (((1,), (1,)), ((), ())), preferred_element_type=F32)


def _dot_tn(a, b):
    return lax.dot_general(a, b, (((0,), (0,)), ((), ())), preferred_element_type=F32)


def _sigmoid(x):
    return 1.0 / (1.0 + jnp.exp(-x))


def _mm(a, b, *, name, ta=False, tb=False, extras=(), epilogue=None, out_dtypes=(F32,)):
    m, k = (a.shape[1], a.shape[0]) if ta else a.shape
    n = b.shape[0] if tb else b.shape[1]
    assert (b.shape[1] if tb else b.shape[0]) == k
    n_ex = len(extras)
    n_out = len(out_dtypes)
    tn = _pick(n, (1024, 512, 256, 128))
    tk = _pick(k, (2048, 1024, 512, 256, 128))
    for tm in (1024, 512, 256, 128):
        if m % tm:
            continue
        est = 2 * (tm * tk * a.dtype.itemsize + tk * tn * b.dtype.itemsize) + tm * tn * 4
        est += 2 * tm * tn * (sum(e[0].dtype.itemsize for e in extras) + sum(jnp.dtype(d).itemsize for d in out_dtypes))
        if est <= MM_VMEM_BUDGET:
            break
    nk = k // tk

    def body(*refs):
        a_ref, b_ref = refs[0], refs[1]
        ex_refs = refs[2:2 + n_ex]
        out_refs = refs[2 + n_ex:2 + n_ex + n_out]
        acc_ref = refs[-1]
        kk = pl.program_id(2)

        @pl.when(kk == 0)
        def _():
            acc_ref[...] = jnp.zeros_like(acc_ref)

        av = a_ref[...].astype(BF16)
        bv = b_ref[...].astype(BF16)
        dims = (((0 if ta else 1,), (1 if tb else 0,)), ((), ()))
        acc_ref[...] += lax.dot_general(av, bv, dims, preferred_element_type=F32)

        @pl.when(kk == nk - 1)
        def _():
            acc = acc_ref[...]
            res = epilogue(acc, *[r[...] for r in ex_refs]) if epilogue is not None else (acc,)
            for r, v in zip(out_refs, res):
                r[...] = v.astype(r.dtype)

    a_spec = pl.BlockSpec((tk, tm), lambda i, j, q: (q, i)) if ta else pl.BlockSpec((tm, tk), lambda i, j, q: (i, q))
    b_spec = pl.BlockSpec((tn, tk), lambda i, j, q: (j, q)) if tb else pl.BlockSpec((tk, tn), lambda i, j, q: (q, j))
    ex_specs = []
    for arr, off in extras:
        assert off % tn == 0 and arr.shape[0] == m
        ex_specs.append(pl.BlockSpec((tm, tn), functools.partial(lambda i, j, q, o: (i, j + o), o=off // tn)))
    outs = pl.pallas_call(
        body, name=name,
        grid=(m // tm, n // tn, nk),
        in_specs=[a_spec, b_spec] + ex_specs,
        out_specs=[pl.BlockSpec((tm, tn), lambda i, j, q: (i, j)) for _ in out_dtypes],
        out_shape=[jax.ShapeDtypeStruct((m, n), dt) for dt in out_dtypes],
        scratch_shapes=[pltpu.VMEM((tm, tn), F32)],
        compiler_params=_cparams(("parallel", "parallel", "arbitrary")),
    )(a, b, *[e[0] for e in extras])
    return outs if n_out > 1 else outs[0]


def _mm_sum_nt(pairs, *, name):
    m, n = pairs[0][0].shape[0], pairs[0][1].shape[0]
    tk = 1024
    counts = [a.shape[1] // tk for a, _ in pairs]
    starts = [sum(counts[:i]) for i in range(len(pairs))]
    nk = sum(counts)
    tm, tn = _pick(m, (1024, 512, 256, 128)), _pick(n, (1024, 512, 256, 128))
    ns = len(pairs)

    def body(*refs):
        o_ref, acc_ref = refs[2 * ns], refs[2 * ns + 1]
        kk = pl.program_id(2)

        @pl.when(kk == 0)
        def _():
            acc_ref[...] = jnp.zeros_like(acc_ref)

        for i in range(ns):
            @pl.when((kk >= starts[i]) & (kk < starts[i] + counts[i]))
            def _(i=i):
                acc_ref[...] += _dot_nt(refs[2 * i][...], refs[2 * i + 1][...])

        @pl.when(kk == nk - 1)
        def _():
            o_ref[...] = acc_ref[...]

    in_specs, args = [], []
    for (a, b), st, cnt in zip(pairs, starts, counts):
        assert a.shape[1] == cnt * tk and b.shape == (n, a.shape[1]) and a.shape[0] == m
        kmap = functools.partial(lambda q, st, cnt: jnp.clip(q - st, 0, cnt - 1), st=st, cnt=cnt)
        in_specs.append(pl.BlockSpec((tm, tk), functools.partial(lambda i, j, q, f: (i, f(q)), f=kmap)))
        in_specs.append(pl.BlockSpec((tn, tk), functools.partial(lambda i, j, q, f: (j, f(q)), f=kmap)))
        args += [a, b]
    return pl.pallas_call(
        body, name=name,
        grid=(m // tm, n // tn, nk),
        in_specs=in_specs,
        out_specs=pl.BlockSpec((tm, tn), lambda i, j, q: (i, j)),
        out_shape=jax.ShapeDtypeStruct((m, n), F32),
        scratch_shapes=[pltpu.VMEM((tm, tn), F32)],
        compiler_params=_cparams(("parallel", "parallel", "arbitrary")),
    )(*args)


def _row_tile(m, bytes_per_row):
    for t in (1024, 512, 256, 128, 64, 32, 16):
        if m % t == 0 and 2 * t * bytes_per_row <= 20 * 2 ** 20:
            return t
    return 16 if m % 16 == 0 else m


def _ew(fn, ins, outs, *, name):
    m = max(x.shape[0] for x in ins)
    bpr = sum(x.shape[1] * x.dtype.itemsize for x in ins if x.shape[0] == m)
    bpr += sum(no * jnp.dtype(dt).itemsize for no, dt in outs)
    tm = _row_tile(m, bpr)
    n_in = len(ins)

    def body(*refs):
        res = fn(*[r[...] for r in refs[:n_in]])
        if not isinstance(res, (tuple, list)):
            res = (res,)
        for r, v in zip(refs[n_in:], res):
            r[...] = v.astype(r.dtype)

    in_specs = []
    for x in ins:
        if x.shape[0] == m:
            in_specs.append(pl.BlockSpec((tm, x.shape[1]), lambda i: (i, 0)))
        else:
            in_specs.append(pl.BlockSpec(x.shape, lambda i: (0, 0)))
    res = pl.pallas_call(
        body, name=name,
        grid=(m // tm,),
        in_specs=in_specs,
        out_specs=[pl.BlockSpec((tm, no), lambda i: (i, 0)) for no, _ in outs],
        out_shape=[jax.ShapeDtypeStruct((m, no), dt) for no, dt in outs],
        compiler_params=_cparams(("parallel",)),
    )(*ins)
    return res if len(outs) > 1 else res[0]


def _rmsnorm(x, g, *, name, out_dtype=BF16):
    d = x.shape[1]

    def fn(xv, gv):
        r = lax.rsqrt(jnp.mean(xv * xv, axis=-1, keepdims=True) + NORM_EPS)
        return xv * r * gv

    return _ew(fn, [x, g.reshape(1, d)], [(d, out_dtype)], name=name)


def _rmsnorm_bwd(x, g, dy, res, *, name):
    m, d = x.shape
    has_res = res is not None
    tm = _row_tile(m, d * 4 * (4 if has_res else 3))

    def body(*refs):
        x_ref, g_ref, dy_ref = refs[:3]
        res_ref = refs[3] if has_res else None
        dx_ref, dg_ref = refs[-2], refs[-1]

        @pl.when(pl.program_id(0) == 0)
        def _():
            dg_ref[...] = jnp.zeros_like(dg_ref)

        xv = x_ref[...]
        dyv = dy_ref[...].astype(F32)
        r = lax.rsqrt(jnp.mean(xv * xv, axis=-1, keepdims=True) + NORM_EPS)
        xh = xv * r
        dg_ref[...] += jnp.sum(dyv * xh, axis=0, keepdims=True)
        dyg = dyv * g_ref[...]
        dx = r * (dyg - xh * jnp.mean(dyg * xh, axis=-1, keepdims=True))
        if has_res:
            dx = dx + res_ref[...]
        dx_ref[...] = dx

    row = pl.BlockSpec((tm, d), lambda i: (i, 0))
    vec = pl.BlockSpec((1, d), lambda i: (0, 0))
    ins = [x, g.reshape(1, d), dy] + ([res] if has_res else [])
    return pl.pallas_call(
        body, name=name,
        grid=(m // tm,),
        in_specs=[row, vec, row] + ([row] if has_res else []),
        out_specs=[row, vec],
        out_shape=[jax.ShapeDtypeStruct((m, d), F32), jax.ShapeDtypeStruct((1, d), F32)],
        compiler_params=_cparams(("arbitrary",)),
    )(*ins)


def _loss_head(x, g, target, *, name):
    m, d = x.shape
    tm = _row_tile(m, d * 4 * 3)

    def body(x_ref, g_ref, t_ref, dx_ref, dg_ref, loss_ref):
        @pl.when(pl.program_id(0) == 0)
        def _():
            dg_ref[...] = jnp.zeros_like(dg_ref)
            loss_ref[...] = jnp.zeros_like(loss_ref)

        xv = x_ref[...]
        gv = g_ref[...]
        r = lax.rsqrt(jnp.mean(xv * xv, axis=-1, keepdims=True) + NORM_EPS)
        xh = xv * r
        diff = xh * gv - t_ref[...]
        loss_ref[...] += 0.5 * jnp.sum(jnp.mean(diff * diff, axis=-1, keepdims=True), axis=0, keepdims=True)
        dyv = diff * (1.0 / d)
        dg_ref[...] += jnp.sum(dyv * xh, axis=0, keepdims=True)
        dyg = dyv * gv
        dx_ref[...] = r * (dyg - xh * jnp.mean(dyg * xh, axis=-1, keepdims=True))

    row = pl.BlockSpec((tm, d), lambda i: (i, 0))
    vec = pl.BlockSpec((1, d), lambda i: (0, 0))
    return pl.pallas_call(
        body, name=name,
        grid=(m // tm,),
        in_specs=[row, vec, row],
        out_specs=[row, vec, pl.BlockSpec((1, 128), lambda i: (0, 0))],
        out_shape=[jax.ShapeDtypeStruct((m, d), F32), jax.ShapeDtypeStruct((1, d), F32),
                   jax.ShapeDtypeStruct((1, 128), F32)],
        compiler_params=_cparams(("arbitrary",)),
    )(x, g.reshape(1, d), target)


def _rope_tables(seqlen):
    pos = jnp.arange(seqlen, dtype=F32)
    inv_freq = ROPE_THETA ** (-jnp.arange(0, MLA_ROPE, 2, dtype=F32) / MLA_ROPE)
    ang = pos[:, None] * inv_freq[None, :]
    c, s = jnp.cos(ang), jnp.sin(ang)
    z = jnp.zeros_like(c)
    return (jnp.concatenate([c, c, z, z], axis=1), jnp.concatenate([z, s, z, z], axis=1),
            jnp.concatenate([-s, z, z, z], axis=1))


def _rope_group(xg, ct, sp, sm):
    return xg * ct + pltpu.roll(xg, 32, 1) * sp + pltpu.roll(xg, 96, 1) * sm


def _rope_group_t(dg, ct, sp, sm):
    return dg * ct + pltpu.roll(dg * sp, 96, 1) + pltpu.roll(dg * sm, 32, 1)


def _s5_disc(lam_re, lam_im, log_dt, b_re, b_im):
    dt = jnp.exp(log_dt)[:, None]
    mag = jnp.exp(lam_re * dt)
    ab_re = mag * jnp.cos(lam_im * dt)
    ab_im = mag * jnp.sin(lam_im * dt)
    den = lam_re * lam_re + lam_im * lam_im
    nr = ab_re - 1.0
    ni = ab_im
    coef_re = (nr * lam_re + ni * lam_im) / den
    coef_im = (ni * lam_re - nr * lam_im) / den
    bb_re = coef_re[..., None] * b_re - coef_im[..., None] * b_im
    bb_im = coef_re[..., None] * b_im + coef_im[..., None] * b_re
    return ab_re, ab_im, bb_re, bb_im


def _blockdiag(x):
    nb, ng, r, c = x.shape
    eye = jnp.eye(ng, dtype=x.dtype)
    return (x[:, :, :, None, :] * eye[None, :, None, :, None]).reshape(nb, ng * r, ng * c)


def _blockdiag_extract(x, r, c):
    nb = x.shape[0]
    x5 = x.reshape(nb, SSM_GB, r, SSM_GB, c)
    eye = jnp.eye(SSM_GB, dtype=x.dtype)
    return jnp.sum(x5 * eye[None, :, None, :, None], axis=3)


def _tile_scan(h_re, h_im, a_re_ref, a_im_ref, t_chunk, reverse):
    n = h_re.shape[1]
    h_re = h_re.reshape(t_chunk // 8, 8, n)
    h_im = h_im.reshape(t_chunk // 8, 8, n)
    sub = lax.broadcasted_iota(jnp.int32, (t_chunk // 8, 8, n), 1)
    for k in range(3):
        sh = 1 << k
        ar = a_re_ref[pl.ds(k, 1), :]
        ai = a_im_ref[pl.ds(k, 1), :]
        if reverse:
            s_re = pltpu.roll(h_re, 8 - sh, 1)
            s_im = pltpu.roll(h_im, 8 - sh, 1)
            keep = sub < 8 - sh
            ai = -ai
        else:
            s_re = pltpu.roll(h_re, sh, 1)
            s_im = pltpu.roll(h_im, sh, 1)
            keep = sub >= sh
        s_re = jnp.where(keep, s_re, 0.0)
        s_im = jnp.where(keep, s_im, 0.0)
        h_re, h_im = h_re + ar * s_re - ai * s_im, h_im + ar * s_im + ai * s_re
    return h_re.reshape(t_chunk, n), h_im.reshape(t_chunk, n)


def _tile_chain(h_re, h_im, w_re, w_im, c_re, c_im, t_chunk, reverse):
    nt = t_chunk // 8
    out_re, out_im = [None] * nt, [None] * nt
    edge = 0 if reverse else 7
    for j in (reversed(range(nt)) if reverse else range(nt)):
        tr = h_re[8 * j:8 * j + 8, :] + w_re * c_re - w_im * c_im
        ti = h_im[8 * j:8 * j + 8, :] + w_re * c_im + w_im * c_re
        c_re = tr[edge:edge + 1, :]
        c_im = ti[edge:edge + 1, :]
        out_re[j], out_im[j] = tr, ti
    return jnp.concatenate(out_re, axis=0), jnp.concatenate(out_im, axis=0), c_re, c_im


def _s5_tables(lam_re, lam_im, log_dt):
    dt = jnp.exp(log_dt)[:, None]
    lr = (lam_re * dt).reshape(1, -1)
    li = (lam_im * dt).reshape(1, -1)

    def powers(n):
        mag = jnp.exp(n * lr)
        return mag * jnp.cos(n * li), mag * jnp.sin(n * li)

    p_re, p_im = powers(jnp.arange(1, 9, dtype=F32)[:, None])
    pad = jnp.zeros((5, lr.shape[1]), F32)
    a_re = jnp.concatenate([p_re[0:1], p_re[1:2], p_re[3:4], pad], axis=0)
    a_im = jnp.concatenate([p_im[0:1], p_im[1:2], p_im[3:4], pad], axis=0)
    return p_re, p_im, p_re[::-1], p_im[::-1], a_re, a_im


def _s5_fwd(u, bb_re, bb_im, c_re, c_im, dsk, tabs, *, name):
    s = u.shape[0]
    t = min(SSM_CHUNK, s)
    nt = s // t
    ns = SSM_GB * SSM_STATE
    p_re, p_im, _, _, a_re, a_im = tabs

    def body(u_ref, bbr_ref, bbi_ref, cr_ref, ci_ref, d_ref, pr_ref, pi_ref, ar_ref, ai_ref,
             y_ref, hcr_ref, hci_ref, hr_ref, hi_ref, car_re, car_im):
        @pl.when(pl.program_id(1) == 0)
        def _():
            car_re[...] = jnp.zeros_like(car_re)
            car_im[...] = jnp.zeros_like(car_im)

        cin_re = car_re[...]
        cin_im = car_im[...]
        hcr_ref[0] = cin_re
        hci_ref[0] = cin_im
        uv = u_ref[...]
        ub = uv.astype(BF16)
        h_re = _dot(ub, bbr_ref[0])
        h_im = _dot(ub, bbi_ref[0])
        h_re, h_im = _tile_scan(h_re, h_im, ar_ref, ai_ref, t, False)
        h_re, h_im, c_re_, c_im_ = _tile_chain(h_re, h_im, pr_ref[...], pi_ref[...], cin_re, cin_im, t, False)
        car_re[...] = c_re_
        car_im[...] = c_im_
        hrb = h_re.astype(BF16)
        hib = h_im.astype(BF16)
        hr_ref[...] = hrb
        hi_ref[...] = hib
        y_ref[...] = _dot(hrb, cr_ref[0]) - _dot(hib, ci_ref[0]) + d_ref[...] * uv

    blk = lambda r, c: pl.BlockSpec((1, r, c), lambda g, i: (g, 0, 0))
    tab = pl.BlockSpec((8, ns), lambda g, i: (0, g))
    return pl.pallas_call(
        body, name=name,
        grid=(SSM_GB, nt),
        in_specs=[pl.BlockSpec((t, 128), lambda g, i: (i, g)), blk(128, ns), blk(128, ns), blk(ns, 128), blk(ns, 128),
                  pl.BlockSpec((1, 128), lambda g, i: (0, g)), tab, tab, tab, tab],
        out_specs=[pl.BlockSpec((t, 128), lambda g, i: (i, g)),
                   pl.BlockSpec((1, 1, ns), lambda g, i: (i, 0, g)), pl.BlockSpec((1, 1, ns), lambda g, i: (i, 0, g)),
                   pl.BlockSpec((t, ns), lambda g, i: (i, g)), pl.BlockSpec((t, ns), lambda g, i: (i, g))],
        out_shape=[jax.ShapeDtypeStruct((s, BRANCH_W), F32),
                   jax.ShapeDtypeStruct((nt, 1, SSM_GB * ns), F32), jax.ShapeDtypeStruct((nt, 1, SSM_GB * ns), F32),
                   jax.ShapeDtypeStruct((s, SSM_GB * ns), BF16), jax.ShapeDtypeStruct((s, SSM_GB * ns), BF16)],
        scratch_shapes=[pltpu.VMEM((1, ns), F32), pltpu.VMEM((1, ns), F32)],
        compiler_params=_cparams(("parallel", "arbitrary")),
    )(u, bb_re, bb_im, c_re, c_im, dsk, p_re, p_im, a_re, a_im)


def _s5_bwd(u, dy, hc_re, hc_im, h_re, h_im, bbt_re, bbt_im, ct_re, ct_im, dsk, tabs, *, name):
    s = u.shape[0]
    t = min(SSM_CHUNK, s)
    nt = s // t
    ns = SSM_GB * SSM_STATE
    _, _, q_re, q_im, a_re, a_im = tabs

    def body(u_ref, dy_ref, hcr_ref, hci_ref, hr_ref, hi_ref, bbtr_ref, bbti_ref, ctr_ref, cti_ref,
             d_ref, qr_ref, qi_ref, ar_ref, ai_ref,
             du_ref, dbbr_ref, dbbi_ref, dcr_ref, dci_ref, dar_ref, dai_ref, dd_ref, lam_re_c, lam_im_c):
        @pl.when(pl.program_id(1) == 0)
        def _():
            lam_re_c[...] = jnp.zeros_like(lam_re_c)
            lam_im_c[...] = jnp.zeros_like(lam_im_c)
            dbbr_ref[...] = jnp.zeros_like(dbbr_ref)
            dbbi_ref[...] = jnp.zeros_like(dbbi_ref)
            dcr_ref[...] = jnp.zeros_like(dcr_ref)
            dci_ref[...] = jnp.zeros_like(dci_ref)
            dar_ref[...] = jnp.zeros_like(dar_ref)
            dai_ref[...] = jnp.zeros_like(dai_ref)
            dd_ref[...] = jnp.zeros_like(dd_ref)

        cin_re = hcr_ref[0]
        cin_im = hci_ref[0]
        uv = u_ref[...]
        ub = uv.astype(BF16)
        dyv = dy_ref[...]
        dyb = dyv.astype(BF16)
        rows = lax.broadcasted_iota(jnp.int32, (t, ns), 0)
        hrb = hr_ref[...]
        hib = hi_ref[...]
        dcr_ref[0] += _dot_tn(hrb, dyb)
        dci_ref[0] -= _dot_tn(hib, dyb)
        h_re = hrb.astype(F32)
        h_im = hib.astype(F32)
        first = rows == 0
        hp_re = jnp.where(first, cin_re, pltpu.roll(h_re, 1, 0))
        hp_im = jnp.where(first, cin_im, pltpu.roll(h_im, 1, 0))
        l_re = _dot(dyb, ctr_ref[0])
        l_im = -_dot(dyb, cti_ref[0])
        l_re, l_im = _tile_scan(l_re, l_im, ar_ref, ai_ref, t, True)
        l_re, l_im, n_re, n_im = _tile_chain(l_re, l_im, qr_ref[...], -qi_ref[...], lam_re_c[...], lam_im_c[...],
                                             t, True)
        lam_re_c[...] = n_re
        lam_im_c[...] = n_im
        lrb = l_re.astype(BF16)
        lib = l_im.astype(BF16)
        du_ref[...] = _dot(lrb, bbtr_ref[0]) + _dot(lib, bbti_ref[0]) + d_ref[...] * dyv
        dbbr_ref[0] += _dot_tn(ub, lrb)
        dbbi_ref[0] += _dot_tn(ub, lib)
        dar_ref[0] += jnp.sum(l_re * hp_re + l_im * hp_im, axis=0, keepdims=True)
        dai_ref[0] += jnp.sum(l_im * hp_re - l_re * hp_im, axis=0, keepdims=True)
        dd_ref[0] += jnp.sum(dyv * uv, axis=0, keepdims=True)

    rev = lambda g, i: (nt - 1 - i, g)
    blk = lambda r, c: pl.BlockSpec((1, r, c), lambda g, i: (g, 0, 0))
    tab = pl.BlockSpec((8, ns), lambda g, i: (0, g))
    car = pl.BlockSpec((1, 1, ns), lambda g, i: (nt - 1 - i, 0, g))
    return pl.pallas_call(
        body, name=name,
        grid=(SSM_GB, nt),
        in_specs=[pl.BlockSpec((t, 128), rev), pl.BlockSpec((t, 128), rev), car, car,
                  pl.BlockSpec((t, ns), rev), pl.BlockSpec((t, ns), rev),
                  blk(ns, 128), blk(ns, 128), blk(128, ns), blk(128, ns), pl.BlockSpec((1, 128), lambda g, i: (0, g)),
                  tab, tab, tab, tab],
        out_specs=[pl.BlockSpec((t, 128), rev), blk(128, ns), blk(128, ns), blk(ns, 128), blk(ns, 128),
                   blk(1, ns), blk(1, ns), blk(1, 128)],
        out_shape=[jax.ShapeDtypeStruct((s, BRANCH_W), F32),
                   jax.ShapeDtypeStruct((SSM_GB, 128, ns), F32), jax.ShapeDtypeStruct((SSM_GB, 128, ns), F32),
                   jax.ShapeDtypeStruct((SSM_GB, ns, 128), F32), jax.ShapeDtypeStruct((SSM_GB, ns, 128), F32),
                   jax.ShapeDtypeStruct((SSM_GB, 1, ns), F32), jax.ShapeDtypeStruct((SSM_GB, 1, ns), F32),
                   jax.ShapeDtypeStruct((SSM_GB, 1, 128), F32)],
        scratch_shapes=[pltpu.VMEM((1, ns), F32), pltpu.VMEM((1, ns), F32)],
        compiler_params=_cparams(("parallel", "arbitrary")),
    )(u, dy, hc_re, hc_im, h_re, h_im, bbt_re, bbt_im, ct_re, ct_im, dsk, q_re, q_im, a_re, a_im)


ATT_BQ = 512
ATT_BK = 512


def _attn_blocks(s):
    return min(ATT_BQ, s), min(ATT_BK, s)


def _rel_index(bq, bk):
    return lax.broadcasted_iota(jnp.int32, (bq, bk), 0) - lax.broadcasted_iota(jnp.int32, (bq, bk), 1)


def _mla_fwd(q, k, kv, *, name):
    s = q.shape[0]
    bq, bk = _attn_blocks(s)
    r = bq // bk
    scale = float((MLA_NOPE + MLA_ROPE) ** -0.5)

    def body(q_ref, k_ref, v_ref, o_ref, lse_ref):
        qi = pl.program_id(1)
        qv = q_ref[...]
        nfull = qi * r
        rel = _rel_index(bq, bk)

        def step(j, carry, d):
            m, l, acc = carry
            ks = pl.ds(pl.multiple_of(j * bk, bk), bk)
            sc = _dot_nt(qv, k_ref[ks, :]) * scale
            if d is not None:
                sc = jnp.where(rel >= d * bk, sc, NEG_BIG)
            m_new = jnp.maximum(m, jnp.max(sc, axis=1, keepdims=True))
            alpha = jnp.exp(m - m_new)
            p = jnp.exp(sc - m_new)
            l = alpha * l + jnp.sum(p, axis=1, keepdims=True)
            acc = alpha * acc + _dot(p.astype(BF16), v_ref[ks, :])
            return m_new, l, acc

        carry = (jnp.full((bq, 1), NEG_BIG, F32), jnp.zeros((bq, 1), F32), jnp.zeros((bq, HEAD_D), F32))
        carry = lax.fori_loop(0, nfull, lambda j, c: step(j, c, None), carry)
        for d in range(r):
            carry = step(nfull + d, carry, d)
        m, l, acc = carry
        o_ref[...] = acc / l
        lse_ref[0] = m + jnp.log(l)

    return pl.pallas_call(
        body, name=name,
        grid=(N_HEADS, s // bq),
        in_specs=[pl.BlockSpec((bq, 2 * HEAD_D), lambda h, i: (i, h)),
                  pl.BlockSpec((s, 2 * HEAD_D), lambda h, i: (0, h)),
                  pl.BlockSpec((s, HEAD_D), lambda h, i: (0, N_HEADS + h))],
        out_specs=[pl.BlockSpec((bq, HEAD_D), lambda h, i: (i, h)), pl.BlockSpec((1, bq, 1), lambda h, i: (h, i, 0))],
        out_shape=[jax.ShapeDtypeStruct((s, N_HEADS * HEAD_D), F32), jax.ShapeDtypeStruct((N_HEADS, s, 1), F32)],
        compiler_params=_cparams(("parallel", "arbitrary")),
    )(q, k, kv)


def _mla_bwd(q, k, kv, o, do, lse, *, name):
    s = q.shape[0]
    bq, bk = _attn_blocks(s)
    r = bq // bk
    scale = float((MLA_NOPE + MLA_ROPE) ** -0.5)

    def body(q_ref, k_ref, v_ref, o_ref, do_ref, lse_ref, dq_ref, dk_ref, dv_ref):
        qi = pl.program_id(1)

        @pl.when(qi == 0)
        def _():
            dk_ref[...] = jnp.zeros_like(dk_ref)
            dv_ref[...] = jnp.zeros_like(dv_ref)

        qv = q_ref[...]
        dov = do_ref[...]
        dob = dov.astype(BF16)
        delta = jnp.sum(dov * o_ref[...], axis=1, keepdims=True)
        lse = lse_ref[0]
        nfull = qi * r
        rel = _rel_index(bq, bk)

        def step(j, dq, d):
            ks = pl.ds(pl.multiple_of(j * bk, bk), bk)
            kt = k_ref[ks, :]
            sc = _dot_nt(qv, kt) * scale
            p = jnp.exp(sc - lse)
            if d is not None:
                p = jnp.where(rel >= d * bk, p, 0.0)
            dp = _dot_nt(dob, v_ref[ks, :])
            ds = (p * (dp - delta) * scale).astype(BF16)
            pb = p.astype(BF16)
            dk_ref[ks, :] += _dot_tn(ds, qv)
            dv_ref[ks, :] += _dot_tn(pb, dob)
            return dq + _dot(ds, kt)

        dq = lax.fori_loop(0, nfull, lambda j, c: step(j, c, None), jnp.zeros((bq, 2 * HEAD_D), F32))
        for d in range(r):
            dq = step(nfull + d, dq, d)
        dq_ref[...] = dq

    tile = pl.BlockSpec((bq, HEAD_D), lambda h, i: (i, h))
    tile2 = pl.BlockSpec((bq, 2 * HEAD_D), lambda h, i: (i, h))
    res2 = pl.BlockSpec((s, 2 * HEAD_D), lambda h, i: (0, h))
    return pl.pallas_call(
        body, name=name,
        grid=(N_HEADS, s // bq),
        in_specs=[tile2, res2, pl.BlockSpec((s, HEAD_D), lambda h, i: (0, N_HEADS + h)), tile, tile,
                  pl.BlockSpec((1, bq, 1), lambda h, i: (h, i, 0))],
        out_specs=[tile2, res2, pl.BlockSpec((s, HEAD_D), lambda h, i: (0, h))],
        out_shape=[jax.ShapeDtypeStruct((s, 2 * N_HEADS * HEAD_D), F32)] * 2
        + [jax.ShapeDtypeStruct((s, N_HEADS * HEAD_D), F32)],
        compiler_params=_cparams(("parallel", "arbitrary")),
    )(q, k, kv, o, do, lse)


def _split_dot(x, tri):
    hi = x.astype(BF16)
    lo = (x - hi.astype(F32)).astype(BF16)
    return _dot(hi, tri) + _dot(lo, tri)


TRI = 256


def _tri(kind):
    r0 = lax.broadcasted_iota(jnp.int32, (TRI, TRI), 0)
    c0 = lax.broadcasted_iota(jnp.int32, (TRI, TRI), 1)
    return {"after": r0 > c0, "upto": r0 <= c0, "before": r0 < c0}[kind].astype(BF16)


def _group_sums(x, tri, reverse, split):
    n = x.shape[1] // TRI
    parts = [x[:, g * TRI:(g + 1) * TRI] for g in range(n)]
    tots = [jnp.sum(pp, axis=1, keepdims=True) for pp in parts]
    outs = [None] * n
    run = None
    for g in (reversed(range(n)) if reverse else range(n)):
        inner = _split_dot(parts[g], tri) if split else _dot(parts[g].astype(BF16), tri)
        outs[g] = inner if run is None else inner + run
        run = tots[g] if run is None else run + tots[g]
    return (outs[0] if n == 1 else jnp.concatenate(outs, axis=1)), run


LOG2E = 1.4426950408889634


def _log2_sigmoid(z2):
    return jnp.minimum(z2, 0.0) - jnp.log(1.0 + jnp.exp2(-jnp.abs(z2))) * LOG2E


def _sb_fwd(qkv, *, name):
    s = qkv.shape[0]
    bq, bk = _attn_blocks(s)
    r = bq // bk
    scale = float(HEAD_D ** -0.5)

    def body(q_ref, k_ref, v_ref, o_ref, tot_ref):
        qi = pl.program_id(1)
        qv = q_ref[...]
        nfull = qi * r
        rel = _rel_index(bq, bk)
        after = _tri("after")

        def step(j, carry, d):
            car, acc = carry
            ks = pl.ds(pl.multiple_of(j * bk, bk), bk)
            z = _dot_nt(qv, k_ref[ks, :]) * (scale * LOG2E)
            lb = _log2_sigmoid(z)
            lr = lb - z
            if d is not None:
                mask = rel > d * bk
                lr = jnp.where(mask, lr, 0.0)
            later, tot_lr = _group_sums(lr, after, True, True)
            w = jnp.exp2(lb + later + car)
            if d is not None:
                w = jnp.where(mask, w, 0.0)
            acc = acc + _dot(w.astype(BF16), v_ref[ks, :])
            return car + tot_lr, acc

        carry = (jnp.zeros((bq, 1), F32), jnp.zeros((bq, HEAD_D), F32))
        for d in reversed(range(r)):
            carry = step(nfull + d, carry, d)
        car, acc = lax.fori_loop(0, nfull, lambda i, c: step(nfull - 1 - i, c, None), carry)
        o_ref[...] = acc
        tot_ref[0] = car

    res = lambda off: pl.BlockSpec((s, HEAD_D), functools.partial(lambda h, i, o: (0, h + o), o=off))
    return pl.pallas_call(
        body, name=name,
        grid=(N_HEADS, s // bq),
        in_specs=[pl.BlockSpec((bq, HEAD_D), lambda h, i: (i, h)), res(N_HEADS), res(2 * N_HEADS)],
        out_specs=[pl.BlockSpec((bq, HEAD_D), lambda h, i: (i, h)), pl.BlockSpec((1, bq, 1), lambda h, i: (h, i, 0))],
        out_shape=[jax.ShapeDtypeStruct((s, N_HEADS * HEAD_D), F32), jax.ShapeDtypeStruct((N_HEADS, s, 1), F32)],
        compiler_params=_cparams(("parallel", "arbitrary")),
    )(qkv, qkv, qkv)


def _sb_bwd(qkv, do, tot, *, name):
    s = qkv.shape[0]
    bq, bk = _attn_blocks(s)
    r = bq // bk
    scale = float(HEAD_D ** -0.5)

    def body(q_ref, k_ref, v_ref, do_ref, tot_ref, dq_ref, dk_ref, dv_ref):
        qi = pl.program_id(1)

        @pl.when(qi == 0)
        def _():
            dk_ref[...] = jnp.zeros_like(dk_ref)
            dv_ref[...] = jnp.zeros_like(dv_ref)

        qv = q_ref[...]
        dob = do_ref[...].astype(BF16)
        nfull = qi * r
        rel = _rel_index(bq, bk)
        upto = _tri("upto")
        before = _tri("before")

        def step(j, carry, d):
            rest, gsum, dq = carry
            ks = pl.ds(pl.multiple_of(j * bk, bk), bk)
            kv_ = k_ref[ks, :]
            z = _dot_nt(qv, kv_) * (scale * LOG2E)
            lb = _log2_sigmoid(z)
            lr = lb - z
            if d is not None:
                mask = rel > d * bk
                lr = jnp.where(mask, lr, 0.0)
            sofar, tot_lr = _group_sums(lr, upto, False, True)
            w = jnp.exp2(lb + rest - sofar)
            if d is not None:
                w = jnp.where(mask, w, 0.0)
            g = _dot_nt(dob, v_ref[ks, :]) * w
            gpre, tot_g = _group_sums(g, before, False, False)
            dlr = gsum + gpre
            dz = (g * jnp.exp2(lr) - dlr * jnp.exp2(lb)) * scale
            if d is not None:
                dz = jnp.where(mask, dz, 0.0)
            dzb = dz.astype(BF16)
            dk_ref[ks, :] += _dot_tn(dzb, qv)
            dv_ref[ks, :] += _dot_tn(w.astype(BF16), dob)
            return rest - tot_lr, gsum + tot_g, dq + _dot(dzb, kv_)

        carry = (tot_ref[0], jnp.zeros((bq, 1), F32), jnp.zeros((bq, HEAD_D), F32))
        carry = lax.fori_loop(0, nfull, lambda j, c: step(j, c, None), carry)
        for d in range(r):
            carry = step(nfull + d, carry, d)
        dq_ref[...] = carry[2]

    tile = pl.BlockSpec((bq, HEAD_D), lambda h, i: (i, h))
    res = lambda off: pl.BlockSpec((s, HEAD_D), functools.partial(lambda h, i, o: (0, h + o), o=off))
    wide = jax.ShapeDtypeStruct((s, N_HEADS * HEAD_D), F32)
    return pl.pallas_call(
        body, name=name,
        grid=(N_HEADS, s // bq),
        in_specs=[tile, res(N_HEADS), res(2 * N_HEADS), tile, pl.BlockSpec((1, bq, 1), lambda h, i: (h, i, 0))],
        out_specs=[tile, res(0), res(0)],
        out_shape=[wide] * 3,
        compiler_params=_cparams(("parallel", "arbitrary")),
    )(qkv, qkv, qkv, do, tot)


def _exchange(bufs, *, all_gather, name):
    nb = len(bufs)
    shapes = [((N_DEV,) + b.shape) if all_gather else b.shape for b in bufs]

    def body(*refs):
        srcs, outs = refs[:nb], refs[nb:2 * nb]
        send_sems, recv_sems, local_sems = refs[2 * nb:]
        x, y, c = lax.axis_index("x"), lax.axis_index("y"), lax.axis_index("c")
        me = 4 * x + 2 * y + c

        def src_for(b, idx):
            return srcs[b] if all_gather else srcs[b].at[idx]

        def copy(slot, b, dev, src_idx, dst_idx):
            return pltpu.make_async_remote_copy(
                src_ref=src_for(b, src_idx), dst_ref=outs[b].at[dst_idx], send_sem=send_sems.at[slot * nb + b],
                recv_sem=recv_sems.at[slot * nb + b], device_id=dev, device_id_type=pl.DeviceIdType.MESH)

        owns = [pltpu.make_async_copy(src_for(b, me), outs[b].at[me], local_sems.at[b]) for b in range(nb)]
        for own in owns:
            own.start()
        peers = []
        for k in range(1, N_DEV):
            px = (1 - x) if (k >> 2) & 1 else x
            py = (1 - y) if (k >> 1) & 1 else y
            pc = (1 - c) if k & 1 else c
            peers.append((k - 1, (px, py, pc), 4 * px + 2 * py + pc))
        sends = [copy(slot, b, dev, idx, me) for slot, dev, idx in peers for b in range(nb)]
        for cp in sends:
            cp.start()
        for slot, dev, idx in peers:
            for b in range(nb):
                copy(slot, b, dev, idx, idx).wait_recv()
        for cp in sends:
            cp.wait_send()
        for own in owns:
            own.wait()

    any_spec = pl.BlockSpec(memory_space=pl.ANY)
    return pl.pallas_call(
        body, name=name,
        in_specs=[any_spec] * nb,
        out_specs=[any_spec] * nb,
        out_shape=[jax.ShapeDtypeStruct(sh, b.dtype) for sh, b in zip(shapes, bufs)],
        scratch_shapes=[pltpu.SemaphoreType.DMA(((N_DEV - 1) * nb,)), pltpu.SemaphoreType.DMA(((N_DEV - 1) * nb,)),
                        pltpu.SemaphoreType.DMA((nb,))],
    )(*bufs)


def _gather_via_sibling(buf, *, name):
    def body(x_ref, out_ref, send_sems, recv_sems, local_sem):
        x, y, c = lax.axis_index("x"), lax.axis_index("y"), lax.axis_index("c")
        me, sibling = (x, y, c), (x, y, 1 - c)
        chips = [(1 - x, y), (x, 1 - y), (1 - x, 1 - y)]

        def slab(px, py, pc):
            return out_ref.at[4 * px + 2 * py + pc]

        def copy(k, block, to, src=None):
            return pltpu.make_async_remote_copy(
                src_ref=slab(*block) if src is None else src, dst_ref=slab(*block), send_sem=send_sems.at[k],
                recv_sem=recv_sems.at[k], device_id=to, device_id_type=pl.DeviceIdType.MESH)

        mine = pltpu.make_async_copy(x_ref, slab(*me), local_sem)
        mine.start()
        first = [copy(0, me, sibling, src=x_ref)]
        first += [copy(1 + j, me, (*chip, c), src=x_ref) for j, chip in enumerate(chips)]
        for cp in first:
            cp.start()
        passed = [copy(4 + j, (*chip, c), sibling) for j, chip in enumerate(chips)]
        for j, chip in enumerate(chips):
            copy(1 + j, (*chip, c), me).wait_recv()
            passed[j].start()
        copy(0, sibling, me).wait_recv()
        for j, chip in enumerate(chips):
            copy(4 + j, (*chip, 1 - c), me).wait_recv()
        for cp in first + passed:
            cp.wait_send()
        mine.wait()

    return pl.pallas_call(
        body, name=name,
        in_specs=[pl.BlockSpec(memory_space=pl.ANY)],
        out_specs=pl.BlockSpec(memory_space=pl.ANY),
        out_shape=jax.ShapeDtypeStruct((N_DEV,) + buf.shape, buf.dtype),
        scratch_shapes=[pltpu.SemaphoreType.DMA((N_DEV - 1,)), pltpu.SemaphoreType.DMA((N_DEV - 1,)),
                        pltpu.SemaphoreType.DMA],
    )(buf)


_HBM = pl.BlockSpec(memory_space=pltpu.HBM)
_SEM = pl.BlockSpec(memory_space=pltpu.SEMAPHORE)
_EFFECT = pltpu.SideEffectType.DATAFLOW_SIDE_EFFECTING


def _peers():
    x, y, c = lax.axis_index("x"), lax.axis_index("y"), lax.axis_index("c")
    peers = []
    for k in range(1, N_DEV):
        px = (1 - x) if (k >> 2) & 1 else x
        py = (1 - y) if (k >> 1) & 1 else y
        pc = (1 - c) if k & 1 else c
        peers.append((k - 1, (px, py, pc), 4 * px + 2 * py + pc))
    return 4 * x + 2 * y + c, peers


def _exchange_start(bufs, *, all_gather, name):
    nb = len(bufs)
    shapes = [((N_DEV,) + b.shape) if all_gather else b.shape for b in bufs]
    ncp = (N_DEV - 1) * nb

    def body(*refs):
        srcs, lands = refs[:nb], refs[nb:2 * nb]
        send_sems, recv_sems = refs[2 * nb], refs[2 * nb + 1]
        token = refs[-1]
        me, peers = _peers()
        for slot, dev, idx in peers:
            for b in range(nb):
                pltpu.make_async_remote_copy(
                    src_ref=srcs[b] if all_gather else srcs[b].at[idx], dst_ref=lands[b].at[me],
                    send_sem=send_sems.at[slot * nb + b], recv_sem=recv_sems.at[slot * nb + b],
                    device_id=dev, device_id_type=pl.DeviceIdType.MESH).start()
        token[...] = jnp.zeros_like(token)

    ins = [pltpu.with_memory_space_constraint(b, pltpu.HBM) for b in bufs]
    ins += [pltpu.with_memory_space_constraint(lax.empty(sh, b.dtype), pltpu.HBM) for sh, b in zip(shapes, bufs)]
    outs = pl.pallas_call(
        body, name=name,
        out_shape=(pltpu.SemaphoreType.DMA((ncp,)), pltpu.SemaphoreType.DMA((ncp,)),
                   *[pltpu.HBM(b.shape, b.dtype) for b in bufs], *[pltpu.HBM(sh, b.dtype) for sh, b in zip(shapes, bufs)],
                   jax.ShapeDtypeStruct((8, 128), F32)),
        in_specs=[_HBM] * (2 * nb),
        out_specs=(_SEM, _SEM, *([_HBM] * (2 * nb)), pl.BlockSpec(memory_space=pltpu.VMEM)),
        input_output_aliases={i: 2 + i for i in range(2 * nb)},
        compiler_params=pltpu.CompilerParams(has_side_effects=_EFFECT),
    )(*ins)
    return outs[:-1], outs[-1]


def _exchange_finish(started, after, *, all_gather, name):
    nb = (len(started) - 2) // 2
    send_sems, recv_sems = started[0], started[1]
    thru = started[2:]

    def body(*refs):
        srcs, lands = refs[:nb], refs[nb:2 * nb]
        ssem, rsem = refs[2 * nb], refs[2 * nb + 1]
        me, peers = _peers()
        for slot, dev, idx in peers:
            for b in range(nb):
                cp = pltpu.make_async_remote_copy(
                    src_ref=srcs[b] if all_gather else srcs[b].at[idx], dst_ref=lands[b].at[idx],
                    send_sem=ssem.at[slot * nb + b], recv_sem=rsem.at[slot * nb + b],
                    device_id=dev, device_id_type=pl.DeviceIdType.MESH)
                cp.wait_send()
                cp.wait_recv()

    done = pl.pallas_call(
        body, name=name,
        out_shape=tuple(pltpu.HBM(t.shape, t.dtype) for t in thru),
        in_specs=[_HBM] * (2 * nb) + [_SEM, _SEM, pl.BlockSpec(memory_space=pl.ANY)],
        out_specs=[_HBM] * (2 * nb),
        input_output_aliases={i: i for i in range(2 * nb)},
        compiler_params=pltpu.CompilerParams(has_side_effects=_EFFECT),
    )(*thru, send_sems, recv_sems, after)
    srcs, lands = done[:nb], done[nb:]
    me = 4 * lax.axis_index("x") + 2 * lax.axis_index("y") + lax.axis_index("c")
    outs = []
    for s_, l_ in zip(srcs, lands):
        own = s_[None] if all_gather else lax.dynamic_slice_in_dim(s_, me, 1, axis=0)
        outs.append(lax.dynamic_update_slice_in_dim(l_, own, me, axis=0))
    return outs


def _sum8(buf, *, name):
    _, r, c = buf.shape
    tr = _pick(r, (256, 128, 64, 32, 16, 8))

    def body(b_ref, o_ref):
        acc = b_ref[0].astype(F32)
        for q in range(1, N_DEV):
            acc = acc + b_ref[q].astype(F32)
        o_ref[...] = acc

    return pl.pallas_call(
        body, name=name,
        grid=(r // tr,),
        in_specs=[pl.BlockSpec((N_DEV, tr, c), lambda i: (0, i, 0))],
        out_specs=pl.BlockSpec((tr, c), lambda i: (i, 0)),
        out_shape=jax.ShapeDtypeStruct((r, c), F32),
        compiler_params=_cparams(("parallel",)),
    )(buf)


def _pack_rows(flat, mult):
    n = flat.shape[0]
    chunk = mult * PACK_COLS
    tot = -(-n // chunk) * chunk
    return jnp.pad(flat, (0, tot - n)).reshape(tot // PACK_COLS, PACK_COLS)


_ADAM_BC1 = 1.0 - ADAM_B1 ** ADAM_STEP
_ADAM_BC2 = 1.0 - ADAM_B2 ** ADAM_STEP


def _adamw_math(wv, gv, mv, vv):
    mn = ADAM_B1 * mv + (1.0 - ADAM_B1) * gv
    vn = ADAM_B2 * vv + (1.0 - ADAM_B2) * (gv * gv)
    m_hat = mn / _ADAM_BC1
    v_hat = vn / _ADAM_BC2
    delta = -ADAM_LR * (m_hat / (jnp.sqrt(v_hat) + ADAM_EPS) + ADAM_WD * wv)
    return delta, mn, vn


def _adamw(w, g, m, v, *, name):
    shape = w.shape
    cols = shape[-1]
    to2d = lambda a: a.reshape(-1, cols)
    d, mn, vn = _ew(_adamw_math, [to2d(w), to2d(g), to2d(m), to2d(v)], [(cols, F32)] * 3, name=name)
    return d.reshape(shape), mn.reshape(shape), vn.reshape(shape)


def _adamw_sharded(w, parts, m, v, *, name):
    depth = w.shape[0]
    cols = w.shape[-1]
    rows = w[0].size // cols
    to3 = lambda a: a.reshape(depth, rows, cols)
    w3, m3, v3 = to3(w), to3(m), to3(v)
    bpr = cols * (3 * 4 + N_DEV * parts[0].dtype.itemsize + 4 * 4)
    tm = _row_tile(rows, bpr)
    outs = None
    for li in range(depth):
        carried = li > 0

        def body(*refs):
            w_ref, p_ref, m_ref, v_ref = refs[:4]
            g_ref, d_ref, mo_ref, vo_ref = refs[-4:]
            gv = p_ref[0].astype(F32)
            for q in range(1, N_DEV):
                gv = gv + p_ref[q].astype(F32)
            delta, mn, vn = _adamw_math(w_ref[0], gv, m_ref[0], v_ref[0])
            g_ref[0] = gv
            d_ref[0] = delta
            mo_ref[0] = mn
            vo_ref[0] = vn

        lay = pl.BlockSpec((1, tm, cols), functools.partial(lambda i, l: (l, i, 0), l=li))
        any_spec = pl.BlockSpec(memory_space=pl.ANY)
        ins = [w3, parts[li].reshape(N_DEV, rows, cols), m3, v3] + (list(outs) if carried else [])
        outs = pl.pallas_call(
            body, name=f"{name}_l{li}",
            grid=(rows // tm,),
            in_specs=[lay, pl.BlockSpec((N_DEV, tm, cols), lambda i: (0, i, 0)), lay, lay]
            + ([any_spec] * 4 if carried else []),
            out_specs=[lay] * 4,
            out_shape=[jax.ShapeDtypeStruct((depth, rows, cols), F32)] * 4,
            input_output_aliases={4 + q: q for q in range(4)} if carried else {},
            compiler_params=_cparams(("parallel",)),
        )(*ins)
    return [o.reshape(w.shape) for o in outs]


def _branch_merge(ys, w_branch, logits, *, name):
    s = ys.shape[0]
    d = w_branch.shape[2]
    tm, tn = _pick(s, (512, 256, 128)), _pick(d, (1024, 512, 256, 128))

    def body(ys_ref, w_ref, l0_ref, l1_ref, l2_ref, m_ref, b0_ref, b1_ref, b2_ref):
        merged = None
        for b, (l_ref, b_ref) in enumerate(((l0_ref, b0_ref), (l1_ref, b1_ref), (l2_ref, b2_ref))):
            acc = _dot(ys_ref[:, b * BRANCH_W:(b + 1) * BRANCH_W], w_ref[b])
            b_ref[...] = acc.astype(b_ref.dtype)
            term = _sigmoid(l_ref[...]) * acc
            merged = term if merged is None else merged + term
        m_ref[...] = merged.astype(m_ref.dtype)

    nj = d // tn
    tile = pl.BlockSpec((tm, tn), lambda i, j: (i, j))
    logit = lambda b: pl.BlockSpec((tm, tn), functools.partial(lambda i, j, o: (i, j + o), o=b * nj))
    return pl.pallas_call(
        body, name=name,
        grid=(s // tm, nj),
        in_specs=[pl.BlockSpec((tm, N_BRANCH * BRANCH_W), lambda i, j: (i, 0)),
                  pl.BlockSpec((N_BRANCH, BRANCH_W, tn), lambda i, j: (0, 0, j)), logit(0), logit(1), logit(2)],
        out_specs=[tile] * 4,
        out_shape=[jax.ShapeDtypeStruct((s, d), BF16)] * 4,
        compiler_params=_cparams(("parallel", "parallel")),
    )(ys, w_branch, logits, logits, logits)


def _prep_in(w_in):
    rope_cols = jnp.pad(w_in[:, 2048:2048 + MLA_ROPE], ((0, 0), (0, 128 - MLA_ROPE)))
    return dict(wa=w_in[:, :2048], wb=w_in[:, 2112:5184], wc=w_in[:, 5184:8256], wd=w_in[:, 8256:14400], we=rope_cols)


def _prep_rest(full):
    wq = full["mla_w_uq"].reshape(512, N_HEADS, MLA_NOPE + MLA_ROPE)
    wkv = full["mla_w_ukv"].reshape(512, N_HEADS, 2 * HEAD_D)
    return dict(
        w_uq=jnp.pad(wq, ((0, 0), (0, 0), (0, 2 * HEAD_D - MLA_NOPE - MLA_ROPE))).reshape(512, 2 * N_HEADS * HEAD_D),
        w_ukv=jnp.concatenate([wkv[:, :, :HEAD_D].reshape(512, 1024), wkv[:, :, HEAD_D:].reshape(512, 1024)], axis=1),
        w_glu=full["ssm_w_glu"], w_branch=full["w_branch"], w_out=full["w_out"],
        w_ple_gate=full["w_ple_gate"], w_ple_proj=full["w_ple_proj"])


def _prep_ssm(sp):
    ab_re, ab_im, bb_re, bb_im = _s5_disc(sp["lam_re"], sp["lam_im"], sp["log_dt"], sp["b_re"], sp["b_im"])
    g4 = lambda a: a.reshape(SSM_GB, SSM_GB, a.shape[1], a.shape[2])
    bbd_re = _blockdiag(jnp.swapaxes(g4(bb_re), 2, 3)).astype(BF16)
    bbd_im = _blockdiag(jnp.swapaxes(g4(bb_im), 2, 3)).astype(BF16)
    cd_re = _blockdiag(jnp.swapaxes(g4(sp["c_re"]), 2, 3)).astype(BF16)
    cd_im = _blockdiag(jnp.swapaxes(g4(sp["c_im"]), 2, 3)).astype(BF16)
    return dict(bb_re=bbd_re, bb_im=bbd_im, bbt_re=jnp.swapaxes(bbd_re, 1, 2), bbt_im=jnp.swapaxes(bbd_im, 1, 2),
                c_re=cd_re, c_im=cd_im, ct_re=jnp.swapaxes(cd_re, 1, 2), ct_im=jnp.swapaxes(cd_im, 1, 2),
                dsk=sp["d"].reshape(1, BRANCH_W))


GELU_C = math.sqrt(2.0 / math.pi)


def _gelu(x):
    return 0.5 * x * (1.0 + jnp.tanh(GELU_C * (x + 0.044715 * x * x * x)))


def _gelu_grad(x):
    t = jnp.tanh(GELU_C * (x + 0.044715 * x * x * x))
    return 0.5 * (1.0 + t) + 0.5 * x * (1.0 - t * t) * GELU_C * (1.0 + 3.0 * 0.044715 * x * x)


def _layer_fwd(x_in, p_l, hooks, ssm, tabs, sm, rope, li):
    tg = lambda nm: nm
    ct, sp_, sm_ = rope
    h = _rmsnorm(x_in, sm["ln_g"], name=tg("ln"))
    w = dict(hooks.weights_in(li, x_in))
    seg_a = _mm(h, w["wa"], name=tg("in_a"))
    seg_b = _mm(h, w["wb"], name=tg("in_b"), out_dtypes=(BF16,))
    seg_c = _mm(h, w["wc"], name=tg("in_c"))
    seg_d = _mm(h, w["wd"], name=tg("in_d"))
    seg_e = _mm(h, w["we"], name=tg("in_e"))

    o_sb, tot = _sb_fwd(seg_b, name=tg("sb_fwd"))
    w.update(hooks.weights_rest(li, o_sb))

    y_pre, hc_re, hc_im, hs_re, hs_im = _s5_fwd(seg_a, ssm["bb_re"], ssm["bb_im"], ssm["c_re"], ssm["c_im"], ssm["dsk"], tabs,
                                  name=tg("s5_fwd"))
    yg = _ew(_gelu, [y_pre], [(BRANCH_W, F32)], name=tg("gelu"))
    y_ssm, z_glu = _mm(yg, w["w_glu"], name=tg("glu"), extras=[(yg, 0)],
                       epilogue=lambda acc, ygv: (ygv * _sigmoid(acc), acc), out_dtypes=(F32, F32))

    q_lat = seg_a[:, 1024:1536]
    kv_lat = seg_a[:, 1536:2048]
    qn = _rmsnorm(q_lat, sm["mla_g_q"], name=tg("q_norm"))
    kvn = _rmsnorm(kv_lat, sm["mla_g_kv"], name=tg("kv_norm"))
    q_raw = _mm(qn, w["w_uq"], name=tg("uq"))
    kv_b = _mm(kvn, w["w_ukv"], name=tg("ukv"), out_dtypes=(BF16,))

    def rope_q(qv, c_, p_, m_):
        outs = []
        for hh in range(N_HEADS):
            outs.append(qv[:, 256 * hh:256 * hh + 128])
            outs.append(_rope_group(qv[:, 256 * hh + 128:256 * hh + 256], c_, p_, m_))
        return jnp.concatenate(outs, axis=1)

    def rope_k(kvv, kr, c_, p_, m_):
        kr_rot = _rope_group(kr, c_, p_, m_).astype(BF16)
        outs = []
        for hh in range(N_HEADS):
            outs.append(kvv[:, 128 * hh:128 * hh + 128])
            outs.append(kr_rot)
        return jnp.concatenate(outs, axis=1)

    q_b = _ew(rope_q, [q_raw, ct, sp_, sm_], [(2048, BF16)], name=tg("rope_q"))
    k_b = _ew(rope_k, [kv_b, seg_e, ct, sp_, sm_], [(2048, BF16)], name=tg("rope_k"))
    o_mla, lse = _mla_fwd(q_b, k_b, kv_b, name=tg("mla_fwd"))

    def gate_fn(a, b, c_, gp):
        ys = jnp.concatenate([a, b, c_], axis=1)
        return ys * (gp * _sigmoid(gp))

    ys = _ew(gate_fn, [y_ssm, o_mla, o_sb, seg_c], [(3 * BRANCH_W, BF16)], name=tg("gate"))
    merged_b, *bos = _branch_merge(ys, w["w_branch"], seg_d, name=tg("branch"))
    x1 = _mm(merged_b, w["w_out"], name=tg("out"), extras=[(x_in, 0)], epilogue=lambda acc, xv: (xv + acc,))

    hn2 = _rmsnorm(x1, sm["ple_g"], name=tg("ple_norm"))
    e = _mm(p_l, w["w_ple_proj"], name=tg("ple_proj"))
    x2, gl = _mm(hn2, w["w_ple_gate"], name=tg("ple_gate"), extras=[(x1, 0), (e, 0)],
                 epilogue=lambda acc, xv, ev: (xv + _sigmoid(acc) * ev, acc), out_dtypes=(F32, F32))
    saved = dict(x_in=x_in, h=h, seg_a=seg_a, seg_b=seg_b, seg_c=seg_c, seg_d=seg_d, y_pre=y_pre, hc_re=hc_re,
                 hc_im=hc_im, hs_re=hs_re, hs_im=hs_im, yg=yg, y_ssm=y_ssm, z_glu=z_glu, q_lat=q_lat, kv_lat=kv_lat, qn=qn, kvn=kvn, q_b=q_b,
                 kv_b=kv_b, k_b=k_b, o_mla=o_mla, lse=lse, o_sb=o_sb, tot=tot, ys=ys, bos=bos, merged_b=merged_b,
                 x1=x1, hn2=hn2, e=e, gl=gl, p_l=p_l, w=w)
    return x2, saved


def _layer_bwd(dx2, sv, hooks, ssm, tabs, sm, rope, li):
    tg = lambda nm: nm
    w = sv["w"]
    ct, sp_, sm_ = rope
    gw, gs = {}, {}

    def ple_fn(d, ev, glv):
        gt = _sigmoid(glv)
        return d * ev * gt * (1.0 - gt), d * gt

    dgl, de = _ew(ple_fn, [dx2, sv["e"], sv["gl"]], [(D_MODEL, BF16), (D_MODEL, BF16)], name=tg("ple_bwd"))
    gw["w_ple_proj"] = _mm(sv["p_l"], de, ta=True, out_dtypes=(BF16,), name=tg("d_ple_proj"))
    gw["w_ple_gate"] = _mm(sv["hn2"], dgl, ta=True, out_dtypes=(BF16,), name=tg("d_ple_gate"))
    dhn2 = _mm(dgl, w["w_ple_gate"], tb=True, name=tg("ple_gate_t"))
    dx1, gs["ple_g"] = _rmsnorm_bwd(sv["x1"], sm["ple_g"], dhn2, dx2, name=tg("ple_norm_bwd"))

    dx1_b = dx1.astype(BF16)
    gw["w_out"] = _mm(sv["merged_b"], dx1_b, ta=True, out_dtypes=(BF16,), name=tg("d_out"))
    dmerged = _mm(dx1_b, w["w_out"], tb=True, name=tg("out_t"))

    def merge_fn(dm, ml, b0, b1, b2):
        dbo, dml = [], []
        for b, bo in enumerate((b0, b1, b2)):
            gt = _sigmoid(ml[:, b * D_MODEL:(b + 1) * D_MODEL])
            dbo.append(dm * gt)
            dml.append(dm * bo.astype(F32) * gt * (1.0 - gt))
        return jnp.concatenate(dbo, axis=1), jnp.concatenate(dml, axis=1)

    dbo, dseg_d = _ew(merge_fn, [dmerged, sv["seg_d"]] + sv["bos"], [(3 * D_MODEL, BF16), (3 * D_MODEL, BF16)],
                      name=tg("merge_bwd"))
    dys, dwb = [], []
    for b in range(3):
        dbo_b = dbo[:, b * D_MODEL:(b + 1) * D_MODEL]
        dwb.append(_mm(sv["ys"][:, b * BRANCH_W:(b + 1) * BRANCH_W], dbo_b, ta=True, out_dtypes=(BF16,), name=tg(f"d_branch{b}")))
        dys.append(_mm(dbo_b, w["w_branch"][b], tb=True, name=tg(f"branch{b}_t")))
    gw["w_branch"] = jnp.stack(dwb)

    def gate_bwd(d0, d1, d2, a, b, c_, gp):
        dy_all = jnp.concatenate([d0, d1, d2], axis=1)
        y_all = jnp.concatenate([a, b, c_], axis=1)
        sg = _sigmoid(gp)
        silu = gp * sg
        dsilu = sg * (1.0 + gp * (1.0 - sg))
        dyv = dy_all * silu
        return (dyv[:, :BRANCH_W], dyv[:, BRANCH_W:2 * BRANCH_W], dyv[:, 2 * BRANCH_W:], dy_all * y_all * dsilu)

    dy_ssm, do_mla, do_sb, dseg_c = _ew(
        gate_bwd, dys + [sv["y_ssm"], sv["o_mla"], sv["o_sb"], sv["seg_c"]],
        [(BRANCH_W, F32), (BRANCH_W, F32), (BRANCH_W, F32), (3 * BRANCH_W, BF16)], name=tg("gate_bwd"))

    dq_sb, dk_sb, dv_sb = _sb_bwd(sv["seg_b"], do_sb, sv["tot"], name=tg("sb_bwd"))
    dseg_b = jnp.concatenate([dq_sb, dk_sb, dv_sb], axis=1).astype(BF16)

    dq_, dk_, dv_ = _mla_bwd(sv["q_b"], sv["k_b"], sv["kv_b"], sv["o_mla"], do_mla, sv["lse"], name=tg("mla_bwd"))

    def unrope_q(dq, c_, p_, m_):
        outs = []
        for hh in range(N_HEADS):
            outs.append(dq[:, 256 * hh:256 * hh + 128])
            outs.append(_rope_group_t(dq[:, 256 * hh + 128:256 * hh + 256], c_, p_, m_))
        return jnp.concatenate(outs, axis=1)

    dq_raw = _ew(unrope_q, [dq_, ct, sp_, sm_], [(2048, BF16)], name=tg("unrope_q"))

    def unrope_k(dk, dvv, c_, p_, m_):
        tot_ = dk[:, 128:256]
        for hh in range(1, N_HEADS):
            tot_ = tot_ + dk[:, 256 * hh + 128:256 * hh + 256]
        dkn = [dk[:, 256 * hh:256 * hh + 128] for hh in range(N_HEADS)]
        return jnp.concatenate(dkn + [dvv], axis=1), _rope_group_t(tot_, c_, p_, m_)

    dkv_raw, dseg_e = _ew(unrope_k, [dk_, dv_, ct, sp_, sm_], [(2048, BF16), (128, BF16)], name=tg("unrope_k"))
    dw_uq = _mm(sv["qn"], dq_raw, ta=True, out_dtypes=(BF16,), name=tg("d_uq"))
    dw_ukv = _mm(sv["kvn"], dkv_raw, ta=True, out_dtypes=(BF16,), name=tg("d_ukv"))
    dqn = _mm(dq_raw, w["w_uq"], tb=True, name=tg("uq_t"))
    dkvn = _mm(dkv_raw, w["w_ukv"], tb=True, name=tg("ukv_t"))
    dq_lat, gs["mla_g_q"] = _rmsnorm_bwd(sv["q_lat"], sm["mla_g_q"], dqn, None, name=tg("q_norm_bwd"))
    dkv_lat, gs["mla_g_kv"] = _rmsnorm_bwd(sv["kv_lat"], sm["mla_g_kv"], dkvn, None, name=tg("kv_norm_bwd"))
    gw["mla_w_uq"] = dw_uq.reshape(512, N_HEADS, 2 * HEAD_D)[:, :, :MLA_NOPE + MLA_ROPE].reshape(
        512, N_HEADS * (MLA_NOPE + MLA_ROPE))
    dwk = dw_ukv[:, :1024].reshape(512, N_HEADS, HEAD_D)
    dwv = dw_ukv[:, 1024:].reshape(512, N_HEADS, HEAD_D)
    gw["mla_w_ukv"] = jnp.concatenate([dwk, dwv], axis=2).reshape(512, 2 * N_HEADS * HEAD_D)

    def glu_bwd(d, ygv, zv):
        sg = _sigmoid(zv)
        return d * ygv * sg * (1.0 - sg), d * sg

    dz, dyg0 = _ew(glu_bwd, [dy_ssm, sv["yg"], sv["z_glu"]], [(BRANCH_W, BF16), (BRANCH_W, F32)], name=tg("glu_bwd"))
    gw["ssm_w_glu"] = _mm(sv["yg"], dz, ta=True, out_dtypes=(BF16,), name=tg("d_glu"))
    dy_pre = _mm(dz, w["w_glu"], tb=True, name=tg("glu_t"), extras=[(dyg0, 0), (sv["y_pre"], 0)],
                 epilogue=lambda acc, d0, yp: ((acc + d0) * _gelu_grad(yp),))
    token = hooks.grads_rest(li, gw)
    dsk = ssm["dsk"] if token is None else ssm["dsk"] + token[0, 0]
    du, dbb_re, dbb_im, dc_re, dc_im, da_re, da_im, dd = _s5_bwd(
        sv["seg_a"], dy_pre, sv["hc_re"], sv["hc_im"], sv["hs_re"], sv["hs_im"], ssm["bbt_re"], ssm["bbt_im"],
        ssm["ct_re"], ssm["ct_im"], dsk, tabs, name=tg("s5_bwd"))
    gs["ssm_d"] = dd.reshape(BRANCH_W)
    gs["ssm_c_re"] = jnp.swapaxes(_blockdiag_extract(dc_re, SSM_STATE, SSM_GROUP), 2, 3).reshape(
        SSM_GROUPS, SSM_GROUP, SSM_STATE)
    gs["ssm_c_im"] = jnp.swapaxes(_blockdiag_extract(dc_im, SSM_STATE, SSM_GROUP), 2, 3).reshape(
        SSM_GROUPS, SSM_GROUP, SSM_STATE)
    gs["_dbb_re"] = jnp.swapaxes(_blockdiag_extract(dbb_re, SSM_GROUP, SSM_STATE), 2, 3).reshape(
        SSM_GROUPS, SSM_STATE, SSM_GROUP)
    gs["_dbb_im"] = jnp.swapaxes(_blockdiag_extract(dbb_im, SSM_GROUP, SSM_STATE), 2, 3).reshape(
        SSM_GROUPS, SSM_STATE, SSM_GROUP)
    gs["_da_re"] = da_re.reshape(SSM_GROUPS, SSM_STATE)
    gs["_da_im"] = da_im.reshape(SSM_GROUPS, SSM_STATE)

    dseg_a = jnp.concatenate([du, dq_lat, dkv_lat], axis=1).astype(BF16)
    hb = sv["h"]
    dwa = _mm(hb, dseg_a, ta=True, out_dtypes=(BF16,), name=tg("d_in_a"))
    dwb_ = _mm(hb, dseg_b, ta=True, out_dtypes=(BF16,), name=tg("d_in_b"))
    dwc = _mm(hb, dseg_c, ta=True, out_dtypes=(BF16,), name=tg("d_in_c"))
    dwd = _mm(hb, dseg_d, ta=True, out_dtypes=(BF16,), name=tg("d_in_d"))
    dwe = _mm(hb, dseg_e, ta=True, out_dtypes=(BF16,), name=tg("d_in_e"))
    gw["w_in"] = jnp.concatenate([dwa, dwe[:, :MLA_ROPE], dwb_, dwc, dwd], axis=1)
    token = hooks.grads_in(li, gw["w_in"])
    wa = w["wa"] if token is None else w["wa"] + token[0, 0].astype(BF16)
    dh = _mm_sum_nt([(dseg_a, wa), (dseg_b, w["wb"]), (dseg_c, w["wc"]), (dseg_d, w["wd"])], name=tg("in_t"))
    dh = _mm(dseg_e, w["we"], tb=True, name=tg("in_e_t"), extras=[(dh, 0)], epilogue=lambda acc, prev: (acc + prev,))
    dx_in, gs["ln_g"] = _rmsnorm_bwd(sv["x_in"], sm["ln_g"], dh, dx1, name=tg("ln_bwd"))
    return dx_in, gw, gs


class _LocalWeights:
    def __init__(self, fulls):
        self.fulls = fulls

    def weights_in(self, li, after):
        return _prep_in(self.fulls[li]["w_in"])

    def weights_rest(self, li, after):
        return _prep_rest(self.fulls[li])

    def grads_rest(self, li, gw):
        return None

    def grads_in(self, li, g):
        return None

    def grads_layer(self, li, gw):
        return None


def _local_step(x, p, target, hooks, small):
    s = x.shape[0]
    depth = small["ln_g"].shape[0]
    rope = _rope_tables(s)
    ssms, tabss, sms, saved = [], [], [], []
    xc = x
    for li in range(depth):
        sp = dict(lam_re=small["ssm_lam_re"][li], lam_im=small["ssm_lam_im"][li], log_dt=small["ssm_log_dt"][li],
                  b_re=small["ssm_b_re"][li], b_im=small["ssm_b_im"][li], c_re=small["ssm_c_re"][li],
                  c_im=small["ssm_c_im"][li], d=small["ssm_d"][li])
        ssm = _prep_ssm(sp)
        tabs = _s5_tables(sp["lam_re"], sp["lam_im"], sp["log_dt"])
        sm = dict(ln_g=small["ln_g"][li], mla_g_q=small["mla_g_q"][li], mla_g_kv=small["mla_g_kv"][li],
                  ple_g=small["ple_g"][li])
        xc, sv = _layer_fwd(xc, p[li], hooks, ssm, tabs, sm, rope, li)
        ssms.append(ssm), tabss.append(tabs), sms.append(sm), saved.append(sv)
    dx, dgf, loss = _loss_head(xc, small["final_g"], target, name="loss_head")
    gws, gss = [None] * depth, [None] * depth
    token = None
    for li in reversed(range(depth)):
        sm = sms[li] if token is None else dict(sms[li], ple_g=sms[li]["ple_g"] + token[0, 0])
        dx, gw, gs = _layer_bwd(dx, saved[li], hooks, ssms[li], tabss[li], sm, rope, li)
        token = hooks.grads_layer(li, gw)
        args = (small["ssm_lam_re"][li], small["ssm_lam_im"][li], small["ssm_log_dt"][li], small["ssm_b_re"][li],
                small["ssm_b_im"][li])
        _, vjp = jax.vjp(_s5_disc, *args)
        g_lr, g_li, g_dt, g_br, g_bi = vjp((gs.pop("_da_re"), gs.pop("_da_im"), gs.pop("_dbb_re"), gs.pop("_dbb_im")))
        gs.update(ssm_lam_re=g_lr, ssm_lam_im=g_li, ssm_log_dt=g_dt, ssm_b_re=g_br, ssm_b_im=g_bi)
        gws[li], gss[li] = gw, gs
    gsmall = {nm: jnp.stack([gss[li][nm].reshape(small[nm].shape[1:]) for li in range(depth)])
              for nm in SMALL if nm != "final_g"}
    gsmall["final_g"] = dgf.reshape(-1)
    return loss[0, 0], dx, gws, gsmall


REST = tuple(t for t in SHARDED if t[0] != "w_in")


def _to_full(spec, got):
    _, fshape, ax = spec
    return jnp.moveaxis(got, 0, ax).reshape(fshape)


def _to_slabs(spec, g):
    _, fshape, ax = spec
    g = g.reshape(tuple(fshape[:ax]) + (N_DEV, fshape[ax] // N_DEV) + tuple(fshape[ax + 1:]))
    return jnp.moveaxis(g, ax, 0)


class _Exchanged:
    def __init__(self, shards):
        bf = lambda nm, li: shards[nm][li].astype(BF16)
        w_in0 = _gather_via_sibling(bf("w_in", 0), name="gather_w_in")
        self.w_in0 = _to_full(SHARDED[0], w_in0)
        first = bf(REST[-1][0], 0) + (w_in0[0, 0, 0] * 0).astype(BF16)
        rest0 = [bf(nm, 0) for nm, _, _ in REST[:-1]] + [first]
        self.rest0, tok_a = _exchange_start(rest0, all_gather=True, name="gather_rest_start")
        all1 = [bf(nm, 1) for nm, _, _ in SHARDED[:-1]] + [bf(SHARDED[-1][0], 1) + tok_a[0, 0].astype(BF16)]
        self.all1, tok_b = _exchange_start(all1, all_gather=True, name="gather_next_start")
        self.token = tok_a[0, 0] + tok_b[0, 0]
        self.full1 = None
        self.sent = {}

    def weights_in(self, li, after):
        if li == 0:
            return _prep_in(self.w_in0)
        got = _exchange_finish(self.all1, after, all_gather=True, name="gather_next_finish")
        self.full1 = {t[0]: _to_full(t, g) for t, g in zip(SHARDED, got)}
        return _prep_in(self.full1["w_in"])

    def weights_rest(self, li, after):
        if li == 1:
            return _prep_rest(self.full1)
        got = _exchange_finish(self.rest0, after, all_gather=True, name="gather_rest_finish")
        return _prep_rest({t[0]: _to_full(t, g) for t, g in zip(REST, got)})

    def grads_rest(self, li, gw):
        if li != 0:
            return None
        self.sent["rest0"], token = _exchange_start([_to_slabs(t, gw[t[0]]) for t in REST], all_gather=False,
                                                    name="scatter_rest_start")
        return token

    def grads_in(self, li, g):
        if li != 0:
            return None
        self.sent["w_in0"], token = _exchange_start([_to_slabs(SHARDED[0], g)], all_gather=False,
                                                    name="scatter_w_in_start")
        return token

    def grads_layer(self, li, gw):
        if li == 0:
            return None
        self.sent["all1"], token = _exchange_start([_to_slabs(t, gw[t[0]]) for t in SHARDED], all_gather=False,
                                                   name="scatter_prev_start")
        return token

    def partial_grads(self, after):
        got1 = _exchange_finish(self.sent["all1"], after, all_gather=False, name="scatter_prev_finish")
        got0 = _exchange_finish(self.sent["rest0"], after, all_gather=False, name="scatter_rest_finish")
        w_in0 = _exchange_finish(self.sent["w_in0"], after, all_gather=False, name="scatter_w_in_finish")[0]
        p0 = {t[0]: g for t, g in zip(REST, got0)}
        p0["w_in"] = w_in0
        return [p0, {t[0]: g for t, g in zip(SHARDED, got1)}]


def kernel(x, p, ln_g, w_in, ssm_lam_re, ssm_lam_im, ssm_log_dt, ssm_b_re, ssm_b_im, ssm_c_re, ssm_c_im, ssm_d, ssm_w_glu, mla_g_q, mla_g_kv, mla_w_uq, mla_w_ukv, w_branch, w_out, ple_g, w_ple_gate, w_ple_proj, final_g, loss_target, m_ln_g, m_w_in, m_ssm_lam_re, m_ssm_lam_im, m_ssm_log_dt, m_ssm_b_re, m_ssm_b_im, m_ssm_c_re, m_ssm_c_im, m_ssm_d, m_ssm_w_glu, m_mla_g_q, m_mla_g_kv, m_mla_w_uq, m_mla_w_ukv, m_w_branch, m_w_out, m_ple_g, m_w_ple_gate, m_w_ple_proj, m_final_g, v_ln_g, v_w_in, v_ssm_lam_re, v_ssm_lam_im, v_ssm_log_dt, v_ssm_b_re, v_ssm_b_im, v_ssm_c_re, v_ssm_c_im, v_ssm_d, v_ssm_w_glu, v_mla_g_q, v_mla_g_kv, v_mla_w_uq, v_mla_w_ukv, v_w_branch, v_w_out, v_ple_g, v_w_ple_gate, v_w_ple_proj, v_final_g):
    weights = dict(ln_g=ln_g, w_in=w_in, ssm_lam_re=ssm_lam_re, ssm_lam_im=ssm_lam_im, ssm_log_dt=ssm_log_dt,
                   ssm_b_re=ssm_b_re, ssm_b_im=ssm_b_im, ssm_c_re=ssm_c_re, ssm_c_im=ssm_c_im, ssm_d=ssm_d,
                   ssm_w_glu=ssm_w_glu, mla_g_q=mla_g_q, mla_g_kv=mla_g_kv, mla_w_uq=mla_w_uq, mla_w_ukv=mla_w_ukv,
                   w_branch=w_branch, w_out=w_out, ple_g=ple_g, w_ple_gate=w_ple_gate, w_ple_proj=w_ple_proj,
                   final_g=final_g)
    mom_m = dict(ln_g=m_ln_g, w_in=m_w_in, ssm_lam_re=m_ssm_lam_re, ssm_lam_im=m_ssm_lam_im, ssm_log_dt=m_ssm_log_dt,
                 ssm_b_re=m_ssm_b_re, ssm_b_im=m_ssm_b_im, ssm_c_re=m_ssm_c_re, ssm_c_im=m_ssm_c_im, ssm_d=m_ssm_d,
                 ssm_w_glu=m_ssm_w_glu, mla_g_q=m_mla_g_q, mla_g_kv=m_mla_g_kv, mla_w_uq=m_mla_w_uq,
                 mla_w_ukv=m_mla_w_ukv, w_branch=m_w_branch, w_out=m_w_out, ple_g=m_ple_g, w_ple_gate=m_w_ple_gate,
                 w_ple_proj=m_w_ple_proj, final_g=m_final_g)
    mom_v = dict(ln_g=v_ln_g, w_in=v_w_in, ssm_lam_re=v_ssm_lam_re, ssm_lam_im=v_ssm_lam_im, ssm_log_dt=v_ssm_log_dt,
                 ssm_b_re=v_ssm_b_re, ssm_b_im=v_ssm_b_im, ssm_c_re=v_ssm_c_re, ssm_c_im=v_ssm_c_im, ssm_d=v_ssm_d,
                 ssm_w_glu=v_ssm_w_glu, mla_g_q=v_mla_g_q, mla_g_kv=v_mla_g_kv, mla_w_uq=v_mla_w_uq,
                 mla_w_ukv=v_mla_w_ukv, w_branch=v_w_branch, w_out=v_w_out, ple_g=v_ple_g, w_ple_gate=v_w_ple_gate,
                 w_ple_proj=v_w_ple_proj, final_g=v_final_g)
    depth = ln_g.shape[0]
    assert depth == 2
    hooks = _Exchanged(weights)
    small = {nm: weights[nm] for nm in SMALL}
    small["ln_g"] = small["ln_g"].at[0, 0].add(hooks.token)
    loss_local, dx, gws, gsmall = _local_step(x[0], p[:, 0], loss_target[0], hooks, small)
    loss = lax.psum(loss_local, ("x", "y", "c"))

    grads, delta, new_m, new_v = {}, {}, {}, {}
    flat = jnp.concatenate([gsmall[nm].reshape(-1) for nm in SMALL])
    n_small = flat.shape[0]
    small_sent, token = _exchange_start([_pack_rows(flat, 8)], all_gather=True, name="gather_small_start")

    parts = hooks.partial_grads(dx)
    for nm, _, _ in SHARDED:
        w_nm = weights[nm] + token[0, 0] if nm == "w_ple_proj" else weights[nm]
        grads[nm], delta[nm], new_m[nm], new_v[nm] = _adamw_sharded(
            w_nm, [parts[li][nm] for li in range(depth)], mom_m[nm], mom_v[nm], name=f"adamw_{nm}")
    done = sum(new_v[nm][(0,) * new_v[nm].ndim] for nm, _, _ in SHARDED).reshape(1)
    got = _exchange_finish(small_sent, done, all_gather=True, name="gather_small_finish")[0]
    gsum = _sum8(got, name="sum_small_grads").reshape(-1)[:n_small]
    off = 0
    for nm in SMALL:
        cnt = weights[nm].size
        grads[nm] = gsum[off:off + cnt].reshape(weights[nm].shape)
        off += cnt
    pk = lambda d: _pack_rows(jnp.concatenate([d[nm].reshape(-1) for nm in SMALL]), 8)
    d_s, m_s, v_s = _adamw(pk(weights), pk(grads), pk(mom_m), pk(mom_v), name="adamw_small")
    off = 0
    for nm in SMALL:
        cnt = weights[nm].size
        shp = weights[nm].shape
        delta[nm] = d_s.reshape(-1)[off:off + cnt].reshape(shp)
        new_m[nm] = m_s.reshape(-1)[off:off + cnt].reshape(shp)
        new_v[nm] = v_s.reshape(-1)[off:off + cnt].reshape(shp)
        off += cnt
    return (loss, dx[None], *[grads[nm] for nm in WEIGHT_ORDER], *[delta[nm] for nm in WEIGHT_ORDER],
            *[new_m[nm] for nm in WEIGHT_ORDER], *[new_v[nm] for nm in WEIGHT_ORDER])
```

```python
import functools
import math

import jax
import jax.numpy as jnp
from jax import lax
from jax.experimental import pallas as pl
from jax.experimental.pallas import tpu as pltpu

F32 = jnp.float32
BF16 = jnp.bfloat16

N_DEV = 8
D_MODEL = 2048
BRANCH_W = 1024
N_HEADS = 8
N_BRANCH = 3
HEAD_D = 128
SSM_GROUPS = 64
SSM_GROUP = 16
SSM_STATE = 64
SSM_GB = 8
SSM_CHUNK = 512
MLA_ROPE = 64
MLA_NOPE = 128
ROPE_THETA = 10000.0
NORM_EPS = 1e-6
DT_MIN = 1e-3
DT_MAX = 1e-1
PACK_COLS = 1024

ADAM_LR = 0.001
ADAM_B1 = 0.9
ADAM_B2 = 0.999
ADAM_EPS = 1e-08
ADAM_WD = 0.01
ADAM_STEP = 10

VMEM_LIMIT = 56 * 2 ** 20
NEG_BIG = -1e30
MM_VMEM_BUDGET = 36 * 2 ** 20

SHARDED = (
    ("w_in", (2048, 14400), 1),
    ("ssm_w_glu", (1024, 1024), 0),
    ("mla_w_uq", (512, 1536), 1),
    ("mla_w_ukv", (512, 2048), 1),
    ("w_branch", (3, 1024, 2048), 2),
    ("w_out", (2048, 2048), 0),
    ("w_ple_gate", (2048, 2048), 0),
    ("w_ple_proj", (256, 2048), 1),
)
SMALL = ("ln_g", "ssm_lam_re", "ssm_lam_im", "ssm_log_dt", "ssm_b_re", "ssm_b_im", "ssm_c_re", "ssm_c_im",
         "ssm_d", "mla_g_q", "mla_g_kv", "ple_g", "final_g")
WEIGHT_ORDER = ("ln_g", "w_in", "ssm_lam_re", "ssm_lam_im", "ssm_log_dt", "ssm_b_re", "ssm_b_im", "ssm_c_re",
                "ssm_c_im", "ssm_d", "ssm_w_glu", "mla_g_q", "mla_g_kv", "mla_w_uq", "mla_w_ukv", "w_branch",
                "w_out", "ple_g", "w_ple_gate", "w_ple_proj", "final_g")


def _cparams(sem=None):
    return pltpu.CompilerParams(dimension_semantics=sem, vmem_limit_bytes=VMEM_LIMIT)


def _pick(n, prefs):
    for t in prefs:
        if n % t == 0:
            return t
    return n


def _dot(a, b):
    return lax.dot_general(a, b, (((1,), (0,)), ((), ())), preferred_element_type=F32)


def _dot_nt(a, b):
    return lax.dot_general(a, b, (((1,), (1,)), ((), ())), preferred_element_type=F32)


def _dot_tn(a, b):
    return lax.dot_general(a, b, (((0,), (0,)), ((), ())), preferred_element_type=F32)


def _sigmoid(x):
    return 1.0 / (1.0 + jnp.exp(-x))


def _mm(a, b, *, name, ta=False, tb=False, extras=(), epilogue=None, out_dtypes=(F32,)):
    m, k = (a.shape[1], a.shape[0]) if ta else a.shape
    n = b.shape[0] if tb else b.shape[1]
    assert (b.shape[1] if tb else b.shape[0]) == k
    n_ex = len(extras)
    n_out = len(out_dtypes)
    tn = _pick(n, (1024, 512, 256, 128))
    tk = _pick(k, (2048, 1024, 512, 256, 128))
    for tm in (1024, 512, 256, 128):
        if m % tm:
            continue
        est = 2 * (tm * tk * a.dtype.itemsize + tk * tn * b.dtype.itemsize) + tm * tn * 4
        est += 2 * tm * tn * (sum(e[0].dtype.itemsize for e in extras) + sum(jnp.dtype(d).itemsize for d in out_dtypes))
        if est <= MM_VMEM_BUDGET:
            break
    nk = k // tk

    def body(*refs):
        a_ref, b_ref = refs[0], refs[1]
        ex_refs = refs[2:2 + n_ex]
        out_refs = refs[2 + n_ex:2 + n_ex + n_out]
        acc_ref = refs[-1]
        kk = pl.program_id(2)

        @pl.when(kk == 0)
        def _():
            acc_ref[...] = jnp.zeros_like(acc_ref)

        av = a_ref[...].astype(BF16)
        bv = b_ref[...].astype(BF16)
        dims = (((0 if ta else 1,), (1 if tb else 0,)), ((), ()))
        acc_ref[...] += lax.dot_general(av, bv, dims, preferred_element_type=F32)

        @pl.when(kk == nk - 1)
        def _():
            acc = acc_ref[...]
            res = epilogue(acc, *[r[...] for r in ex_refs]) if epilogue is not None else (acc,)
            for r, v in zip(out_refs, res):
                r[...] = v.astype(r.dtype)

    a_spec = pl.BlockSpec((tk, tm), lambda i, j, q: (q, i)) if ta else pl.BlockSpec((tm, tk), lambda i, j, q: (i, q))
    b_spec = pl.BlockSpec((tn, tk), lambda i, j, q: (j, q)) if tb else pl.BlockSpec((tk, tn), lambda i, j, q: (q, j))
    ex_specs = []
    for arr, off in extras:
        assert off % tn == 0 and arr.shape[0] == m
        ex_specs.append(pl.BlockSpec((tm, tn), functools.partial(lambda i, j, q, o: (i, j + o), o=off // tn)))
    outs = pl.pallas_call(
        body, name=name,
        grid=(m // tm, n // tn, nk),
        in_specs=[a_spec, b_spec] + ex_specs,
        out_specs=[pl.BlockSpec((tm, tn), lambda i, j, q: (i, j)) for _ in out_dtypes],
        out_shape=[jax.ShapeDtypeStruct((m, n), dt) for dt in out_dtypes],
        scratch_shapes=[pltpu.VMEM((tm, tn), F32)],
        compiler_params=_cparams(("parallel", "parallel", "arbitrary")),
    )(a, b, *[e[0] for e in extras])
    return outs if n_out > 1 else outs[0]


def _row_tile(m, bytes_per_row):
    for t in (1024, 512, 256, 128, 64, 32, 16):
        if m % t == 0 and 2 * t * bytes_per_row <= 20 * 2 ** 20:
            return t
    return 16 if m % 16 == 0 else m


def _ew(fn, ins, outs, *, name):
    m = max(x.shape[0] for x in ins)
    bpr = sum(x.shape[1] * x.dtype.itemsize for x in ins if x.shape[0] == m)
    bpr += sum(no * jnp.dtype(dt).itemsize for no, dt in outs)
    tm = _row_tile(m, bpr)
    n_in = len(ins)

    def body(*refs):
        res = fn(*[r[...] for r in refs[:n_in]])
        if not isinstance(res, (tuple, list)):
            res = (res,)
        for r, v in zip(refs[n_in:], res):
            r[...] = v.astype(r.dtype)

    in_specs = []
    for x in ins:
        if x.shape[0] == m:
            in_specs.append(pl.BlockSpec((tm, x.shape[1]), lambda i: (i, 0)))
        else:
            in_specs.append(pl.BlockSpec(x.shape, lambda i: (0, 0)))
    res = pl.pallas_call(
        body, name=name,
        grid=(m // tm,),
        in_specs=in_specs,
        out_specs=[pl.BlockSpec((tm, no), lambda i: (i, 0)) for no, _ in outs],
        out_shape=[jax.ShapeDtypeStruct((m, no), dt) for no, dt in outs],
        compiler_params=_cparams(("parallel",)),
    )(*ins)
    return res if len(outs) > 1 else res[0]


def _rmsnorm(x, g, *, name, out_dtype=BF16):
    d = x.shape[1]

    def fn(xv, gv):
        r = lax.rsqrt(jnp.mean(xv * xv, axis=-1, keepdims=True) + NORM_EPS)
        return xv * r * gv

    return _ew(fn, [x, g.reshape(1, d)], [(d, out_dtype)], name=name)


def _rmsnorm_bwd(x, g, dy, res, *, name):
    m, d = x.shape
    has_res = res is not None
    tm = _row_tile(m, d * 4 * (4 if has_res else 3))

    def body(*refs):
        x_ref, g_ref, dy_ref = refs[:3]
        res_ref = refs[3] if has_res else None
        dx_ref, dg_ref = refs[-2], refs[-1]

        @pl.when(pl.program_id(0) == 0)
        def _():
            dg_ref[...] = jnp.zeros_like(dg_ref)

        xv = x_ref[...]
        dyv = dy_ref[...].astype(F32)
        r = lax.rsqrt(jnp.mean(xv * xv, axis=-1, keepdims=True) + NORM_EPS)
        xh = xv * r
        dg_ref[...] += jnp.sum(dyv * xh, axis=0, keepdims=True)
        dyg = dyv * g_ref[...]
        dx = r * (dyg - xh * jnp.mean(dyg * xh, axis=-1, keepdims=True))
        if has_res:
            dx = dx + res_ref[...]
        dx_ref[...] = dx

    row = pl.BlockSpec((tm, d), lambda i: (i, 0))
    vec = pl.BlockSpec((1, d), lambda i: (0, 0))
    ins = [x, g.reshape(1, d), dy] + ([res] if has_res else [])
    return pl.pallas_call(
        body, name=name,
        grid=(m // tm,),
        in_specs=[row, vec, row] + ([row] if has_res else []),
        out_specs=[row, vec],
        out_shape=[jax.ShapeDtypeStruct((m, d), F32), jax.ShapeDtypeStruct((1, d), F32)],
        compiler_params=_cparams(("arbitrary",)),
    )(*ins)


def _loss_head(x, g, target, *, name):
    m, d = x.shape
    tm = _row_tile(m, d * 4 * 3)

    def body(x_ref, g_ref, t_ref, dx_ref, dg_ref, loss_ref):
        @pl.when(pl.program_id(0) == 0)
        def _():
            dg_ref[...] = jnp.zeros_like(dg_ref)
            loss_ref[...] = jnp.zeros_like(loss_ref)

        xv = x_ref[...]
        gv = g_ref[...]
        r = lax.rsqrt(jnp.mean(xv * xv, axis=-1, keepdims=True) + NORM_EPS)
        xh = xv * r
        diff = xh * gv - t_ref[...]
        loss_ref[...] += 0.5 * jnp.sum(jnp.mean(diff * diff, axis=-1, keepdims=True), axis=0, keepdims=True)
        dyv = diff * (1.0 / d)
        dg_ref[...] += jnp.sum(dyv * xh, axis=0, keepdims=True)
        dyg = dyv * gv
        dx_ref[...] = r * (dyg - xh * jnp.mean(dyg * xh, axis=-1, keepdims=True))

    row = pl.BlockSpec((tm, d), lambda i: (i, 0))
    vec = pl.BlockSpec((1, d), lambda i: (0, 0))
    return pl.pallas_call(
        body, name=name,
        grid=(m // tm,),
        in_specs=[row, vec, row],
        out_specs=[row, vec, pl.BlockSpec((1, 128), lambda i: (0, 0))],
        out_shape=[jax.ShapeDtypeStruct((m, d), F32), jax.ShapeDtypeStruct((1, d), F32),
                   jax.ShapeDtypeStruct((1, 128), F32)],
        compiler_params=_cparams(("arbitrary",)),
    )(x, g.reshape(1, d), target)


def _rope_tables(seqlen):
    pos = jnp.arange(seqlen, dtype=F32)
    inv_freq = ROPE_THETA ** (-jnp.arange(0, MLA_ROPE, 2, dtype=F32) / MLA_ROPE)
    ang = pos[:, None] * inv_freq[None, :]
    c, s = jnp.cos(ang), jnp.sin(ang)
    z = jnp.zeros_like(c)
    return (jnp.concatenate([c, c, z, z], axis=1), jnp.concatenate([z, s, z, z], axis=1),
            jnp.concatenate([-s, z, z, z], axis=1))


def _rope_group(xg, ct, sp, sm):
    return xg * ct + pltpu.roll(xg, 32, 1) * sp + pltpu.roll(xg, 96, 1) * sm


def _rope_group_t(dg, ct, sp, sm):
    return dg * ct + pltpu.roll(dg * sp, 96, 1) + pltpu.roll(dg * sm, 32, 1)


def _s5_disc(lam_re, lam_im, log_dt, b_re, b_im):
    dt = jnp.exp(log_dt)[:, None]
    mag = jnp.exp(lam_re * dt)
    ab_re = mag * jnp.cos(lam_im * dt)
    ab_im = mag * jnp.sin(lam_im * dt)
    den = lam_re * lam_re + lam_im * lam_im
    nr = ab_re - 1.0
    ni = ab_im
    coef_re = (nr * lam_re + ni * lam_im) / den
    coef_im = (ni * lam_re - nr * lam_im) / den
    bb_re = coef_re[..., None] * b_re - coef_im[..., None] * b_im
    bb_im = coef_re[..., None] * b_im + coef_im[..., None] * b_re
    return ab_re, ab_im, bb_re, bb_im


def _blockdiag(x):
    nb, ng, r, c = x.shape
    eye = jnp.eye(ng, dtype=x.dtype)
    return (x[:, :, :, None, :] * eye[None, :, None, :, None]).reshape(nb, ng * r, ng * c)


def _blockdiag_extract(x, r, c):
    nb = x.shape[0]
    x5 = x.reshape(nb, SSM_GB, r, SSM_GB, c)
    eye = jnp.eye(SSM_GB, dtype=x.dtype)
    return jnp.sum(x5 * eye[None, :, None, :, None], axis=3)


def _tile_scan(h_re, h_im, a_re_ref, a_im_ref, t_chunk, reverse):
    n = h_re.shape[1]
    h_re = h_re.reshape(t_chunk // 8, 8, n)
    h_im = h_im.reshape(t_chunk // 8, 8, n)
    sub = lax.broadcasted_iota(jnp.int32, (t_chunk // 8, 8, n), 1)
    for k in range(3):
        sh = 1 << k
        ar = a_re_ref[pl.ds(k, 1), :]
        ai = a_im_ref[pl.ds(k, 1), :]
        if reverse:
            s_re = pltpu.roll(h_re, 8 - sh, 1)
            s_im = pltpu.roll(h_im, 8 - sh, 1)
            keep = sub < 8 - sh
            ai = -ai
        else:
            s_re = pltpu.roll(h_re, sh, 1)
            s_im = pltpu.roll(h_im, sh, 1)
            keep = sub >= sh
        s_re = jnp.where(keep, s_re, 0.0)
        s_im = jnp.where(keep, s_im, 0.0)
        h_re, h_im = h_re + ar * s_re - ai * s_im, h_im + ar * s_im + ai * s_re
    return h_re.reshape(t_chunk, n), h_im.reshape(t_chunk, n)


def _tile_chain(h_re, h_im, w_re, w_im, c_re, c_im, t_chunk, reverse):
    nt = t_chunk // 8
    out_re, out_im = [None] * nt, [None] * nt
    edge = 0 if reverse else 7
    for j in (reversed(range(nt)) if reverse else range(nt)):
        tr = h_re[8 * j:8 * j + 8, :] + w_re * c_re - w_im * c_im
        ti = h_im[8 * j:8 * j + 8, :] + w_re * c_im + w_im * c_re
        c_re = tr[edge:edge + 1, :]
        c_im = ti[edge:edge + 1, :]
        out_re[j], out_im[j] = tr, ti
    return jnp.concatenate(out_re, axis=0), jnp.concatenate(out_im, axis=0), c_re, c_im


def _s5_tables(lam_re, lam_im, log_dt):
    dt = jnp.exp(log_dt)[:, None]
    lr = (lam_re * dt).reshape(1, -1)
    li = (lam_im * dt).reshape(1, -1)

    def powers(n):
        mag = jnp.exp(n * lr)
        return mag * jnp.cos(n * li), mag * jnp.sin(n * li)

    p_re, p_im = powers(jnp.arange(1, 9, dtype=F32)[:, None])
    pad = jnp.zeros((5, lr.shape[1]), F32)
    a_re = jnp.concatenate([p_re[0:1], p_re[1:2], p_re[3:4], pad], axis=0)
    a_im = jnp.concatenate([p_im[0:1], p_im[1:2], p_im[3:4], pad], axis=0)
    return p_re, p_im, p_re[::-1], p_im[::-1], a_re, a_im


def _s5_fwd(u, bb_re, bb_im, c_re, c_im, dsk, tabs, *, name):
    s = u.shape[0]
    t = min(SSM_CHUNK, s)
    nt = s // t
    ns = SSM_GB * SSM_STATE
    p_re, p_im, _, _, a_re, a_im = tabs

    def body(u_ref, bbr_ref, bbi_ref, cr_ref, ci_ref, d_ref, pr_ref, pi_ref, ar_ref, ai_ref,
             y_ref, hcr_ref, hci_ref, hr_ref, hi_ref, car_re, car_im):
        @pl.when(pl.program_id(1) == 0)
        def _():
            car_re[...] = jnp.zeros_like(car_re)
            car_im[...] = jnp.zeros_like(car_im)

        cin_re = car_re[...]
        cin_im = car_im[...]
        hcr_ref[0] = cin_re
        hci_ref[0] = cin_im
        uv = u_ref[...]
        ub = uv.astype(BF16)
        h_re = _dot(ub, bbr_ref[0])
        h_im = _dot(ub, bbi_ref[0])
        h_re, h_im = _tile_scan(h_re, h_im, ar_ref, ai_ref, t, False)
        h_re, h_im, c_re_, c_im_ = _tile_chain(h_re, h_im, pr_ref[...], pi_ref[...], cin_re, cin_im, t, False)
        car_re[...] = c_re_
        car_im[...] = c_im_
        hrb = h_re.astype(BF16)
        hib = h_im.astype(BF16)
        hr_ref[...] = hrb
        hi_ref[...] = hib
        y_ref[...] = _dot(hrb, cr_ref[0]) - _dot(hib, ci_ref[0]) + d_ref[...] * uv

    blk = lambda r, c: pl.BlockSpec((1, r, c), lambda g, i: (g, 0, 0))
    tab = pl.BlockSpec((8, ns), lambda g, i: (0, g))
    return pl.pallas_call(
        body, name=name,
        grid=(SSM_GB, nt),
        in_specs=[pl.BlockSpec((t, 128), lambda g, i: (i, g)), blk(128, ns), blk(128, ns), blk(ns, 128), blk(ns, 128),
                  pl.BlockSpec((1, 128), lambda g, i: (0, g)), tab, tab, tab, tab],
        out_specs=[pl.BlockSpec((t, 128), lambda g, i: (i, g)),
                   pl.BlockSpec((1, 1, ns), lambda g, i: (i, 0, g)), pl.BlockSpec((1, 1, ns), lambda g, i: (i, 0, g)),
                   pl.BlockSpec((t, ns), lambda g, i: (i, g)), pl.BlockSpec((t, ns), lambda g, i: (i, g))],
        out_shape=[jax.ShapeDtypeStruct((s, BRANCH_W), F32),
                   jax.ShapeDtypeStruct((nt, 1, SSM_GB * ns), F32), jax.ShapeDtypeStruct((nt, 1, SSM_GB * ns), F32),
                   jax.ShapeDtypeStruct((s, SSM_GB * ns), BF16), jax.ShapeDtypeStruct((s, SSM_GB * ns), BF16)],
        scratch_shapes=[pltpu.VMEM((1, ns), F32), pltpu.VMEM((1, ns), F32)],
        compiler_params=_cparams(("parallel", "arbitrary")),
    )(u, bb_re, bb_im, c_re, c_im, dsk, p_re, p_im, a_re, a_im)


def _s5_bwd(u, dy, hc_re, hc_im, h_re, h_im, bbt_re, bbt_im, ct_re, ct_im, dsk, tabs, *, name):
    s = u.shape[0]
    t = min(SSM_CHUNK, s)
    nt = s // t
    ns = SSM_GB * SSM_STATE
    _, _, q_re, q_im, a_re, a_im = tabs

    def body(u_ref, dy_ref, hcr_ref, hci_ref, hr_ref, hi_ref, bbtr_ref, bbti_ref, ctr_ref, cti_ref,
             d_ref, qr_ref, qi_ref, ar_ref, ai_ref,
             du_ref, dbbr_ref, dbbi_ref, dcr_ref, dci_ref, dar_ref, dai_ref, dd_ref, lam_re_c, lam_im_c):
        @pl.when(pl.program_id(1) == 0)
        def _():
            lam_re_c[...] = jnp.zeros_like(lam_re_c)
            lam_im_c[...] = jnp.zeros_like(lam_im_c)
            dbbr_ref[...] = jnp.zeros_like(dbbr_ref)
            dbbi_ref[...] = jnp.zeros_like(dbbi_ref)
            dcr_ref[...] = jnp.zeros_like(dcr_ref)
            dci_ref[...] = jnp.zeros_like(dci_ref)
            dar_ref[...] = jnp.zeros_like(dar_ref)
            dai_ref[...] = jnp.zeros_like(dai_ref)
            dd_ref[...] = jnp.zeros_like(dd_ref)

        cin_re = hcr_ref[0]
        cin_im = hci_ref[0]
        uv = u_ref[...]
        ub = uv.astype(BF16)
        dyv = dy_ref[...]
        dyb = dyv.astype(BF16)
        rows = lax.broadcasted_iota(jnp.int32, (t, ns), 0)
        hrb = hr_ref[...]
        hib = hi_ref[...]
        dcr_ref[0] += _dot_tn(hrb, dyb)
        dci_ref[0] -= _dot_tn(hib, dyb)
        h_re = hrb.astype(F32)
        h_im = hib.astype(F32)
        first = rows == 0
        hp_re = jnp.where(first, cin_re, pltpu.roll(h_re, 1, 0))
        hp_im = jnp.where(first, cin_im, pltpu.roll(h_im, 1, 0))
        l_re = _dot(dyb, ctr_ref[0])
        l_im = -_dot(dyb, cti_ref[0])
        l_re, l_im = _tile_scan(l_re, l_im, ar_ref, ai_ref, t, True)
        l_re, l_im, n_re, n_im = _tile_chain(l_re, l_im, qr_ref[...], -qi_ref[...], lam_re_c[...], lam_im_c[...],
                                             t, True)
        lam_re_c[...] = n_re
        lam_im_c[...] = n_im
        lrb = l_re.astype(BF16)
        lib = l_im.astype(BF16)
        du_ref[...] = _dot(lrb, bbtr_ref[0]) + _dot(lib, bbti_ref[0]) + d_ref[...] * dyv
        dbbr_ref[0] += _dot_tn(ub, lrb)
        dbbi_ref[0] += _dot_tn(ub, lib)
        dar_ref[0] += jnp.sum(l_re * hp_re + l_im * hp_im, axis=0, keepdims=True)
        dai_ref[0] += jnp.sum(l_im * hp_re - l_re * hp_im, axis=0, keepdims=True)
        dd_ref[0] += jnp.sum(dyv * uv, axis=0, keepdims=True)

    rev = lambda g, i: (nt - 1 - i, g)
    blk = lambda r, c: pl.BlockSpec((1, r, c), lambda g, i: (g, 0, 0))
    tab = pl.BlockSpec((8, ns), lambda g, i: (0, g))
    car = pl.BlockSpec((1, 1, ns), lambda g, i: (nt - 1 - i, 0, g))
    return pl.pallas_call(
        body, name=name,
        grid=(SSM_GB, nt),
        in_specs=[pl.BlockSpec((t, 128), rev), pl.BlockSpec((t, 128), rev), car, car,
                  pl.BlockSpec((t, ns), rev), pl.BlockSpec((t, ns), rev),
                  blk(ns, 128), blk(ns, 128), blk(128, ns), blk(128, ns), pl.BlockSpec((1, 128), lambda g, i: (0, g)),
                  tab, tab, tab, tab],
        out_specs=[pl.BlockSpec((t, 128), rev), blk(128, ns), blk(128, ns), blk(ns, 128), blk(ns, 128),
                   blk(1, ns), blk(1, ns), blk(1, 128)],
        out_shape=[jax.ShapeDtypeStruct((s, BRANCH_W), F32),
                   jax.ShapeDtypeStruct((SSM_GB, 128, ns), F32), jax.ShapeDtypeStruct((SSM_GB, 128, ns), F32),
                   jax.ShapeDtypeStruct((SSM_GB, ns, 128), F32), jax.ShapeDtypeStruct((SSM_GB, ns, 128), F32),
                   jax.ShapeDtypeStruct((SSM_GB, 1, ns), F32), jax.ShapeDtypeStruct((SSM_GB, 1, ns), F32),
                   jax.ShapeDtypeStruct((SSM_GB, 1, 128), F32)],
        scratch_shapes=[pltpu.VMEM((1, ns), F32), pltpu.VMEM((1, ns), F32)],
        compiler_params=_cparams(("parallel", "arbitrary")),
    )(u, dy, hc_re, hc_im, h_re, h_im, bbt_re, bbt_im, ct_re, ct_im, dsk, q_re, q_im, a_re, a_im)


ATT_BQ = 512
ATT_BK = 512


def _attn_blocks(s):
    return min(ATT_BQ, s), min(ATT_BK, s)


def _rel_index(bq, bk):
    return lax.broadcasted_iota(jnp.int32, (bq, bk), 0) - lax.broadcasted_iota(jnp.int32, (bq, bk), 1)


def _mla_fwd(q, k, kv, *, name):
    s = q.shape[0]
    bq, bk = _attn_blocks(s)
    r = bq // bk
    scale = float((MLA_NOPE + MLA_ROPE) ** -0.5)

    def body(q_ref, k_ref, v_ref, o_ref, lse_ref):
        qi = pl.program_id(1)
        qv = q_ref[...]
        nfull = qi * r
        rel = _rel_index(bq, bk)

        def step(j, carry, d):
            m, l, acc = carry
            ks = pl.ds(pl.multiple_of(j * bk, bk), bk)
            sc = _dot_nt(qv, k_ref[ks, :]) * scale
            if d is not None:
                sc = jnp.where(rel >= d * bk, sc, NEG_BIG)
            m_new = jnp.maximum(m, jnp.max(sc, axis=1, keepdims=True))
            alpha = jnp.exp(m - m_new)
            p = jnp.exp(sc - m_new)
            l = alpha * l + jnp.sum(p, axis=1, keepdims=True)
            acc = alpha * acc + _dot(p.astype(BF16), v_ref[ks, :])
            return m_new, l, acc

        carry = (jnp.full((bq, 1), NEG_BIG, F32), jnp.zeros((bq, 1), F32), jnp.zeros((bq, HEAD_D), F32))
        carry = lax.fori_loop(0, nfull, lambda j, c: step(j, c, None), carry)
        for d in range(r):
            carry = step(nfull + d, carry, d)
        m, l, acc = carry
        o_ref[...] = acc / l
        lse_ref[0] = m + jnp.log(l)

    return pl.pallas_call(
        body, name=name,
        grid=(N_HEADS, s // bq),
        in_specs=[pl.BlockSpec((bq, 2 * HEAD_D), lambda h, i: (i, h)),
                  pl.BlockSpec((s, 2 * HEAD_D), lambda h, i: (0, h)),
                  pl.BlockSpec((s, HEAD_D), lambda h, i: (0, N_HEADS + h))],
        out_specs=[pl.BlockSpec((bq, HEAD_D), lambda h, i: (i, h)), pl.BlockSpec((1, bq, 1), lambda h, i: (h, i, 0))],
        out_shape=[jax.ShapeDtypeStruct((s, N_HEADS * HEAD_D), F32), jax.ShapeDtypeStruct((N_HEADS, s, 1), F32)],
        compiler_params=_cparams(("parallel", "arbitrary")),
    )(q, k, kv)


def _mla_bwd(q, k, kv, o, do, lse, *, name):
    s = q.shape[0]
    bq, bk = _attn_blocks(s)
    r = bq // bk
    scale = float((MLA_NOPE + MLA_ROPE) ** -0.5)

    def body(q_ref, k_ref, v_ref, o_ref, do_ref, lse_ref, dq_ref, dk_ref, dv_ref):
        qi = pl.program_id(1)

        @pl.when(qi == 0)
        def _():
            dk_ref[...] = jnp.zeros_like(dk_ref)
            dv_ref[...] = jnp.zeros_like(dv_ref)

        qv = q_ref[...]
        dov = do_ref[...]
        dob = dov.astype(BF16)
        delta = jnp.sum(dov * o_ref[...], axis=1, keepdims=True)
        lse = lse_ref[0]
        nfull = qi * r
        rel = _rel_index(bq, bk)

        def step(j, dq, d):
            ks = pl.ds(pl.multiple_of(j * bk, bk), bk)
            kt = k_ref[ks, :]
            sc = _dot_nt(qv, kt) * scale
            p = jnp.exp(sc - lse)
            if d is not None:
                p = jnp.where(rel >= d * bk, p, 0.0)
            dp = _dot_nt(dob, v_ref[ks, :])
            ds = (p * (dp - delta) * scale).astype(BF16)
            pb = p.astype(BF16)
            dk_ref[ks, :] += _dot_tn(ds, qv)
            dv_ref[ks, :] += _dot_tn(pb, dob)
            return dq + _dot(ds, kt)

        dq = lax.fori_loop(0, nfull, lambda j, c: step(j, c, None), jnp.zeros((bq, 2 * HEAD_D), F32))
        for d in range(r):
            dq = step(nfull + d, dq, d)
        dq_ref[...] = dq

    tile = pl.BlockSpec((bq, HEAD_D), lambda h, i: (i, h))
    tile2 = pl.BlockSpec((bq, 2 * HEAD_D), lambda h, i: (i, h))
    res2 = pl.BlockSpec((s, 2 * HEAD_D), lambda h, i: (0, h))
    return pl.pallas_call(
        body, name=name,
        grid=(N_HEADS, s // bq),
        in_specs=[tile2, res2, pl.BlockSpec((s, HEAD_D), lambda h, i: (0, N_HEADS + h)), tile, tile,
                  pl.BlockSpec((1, bq, 1), lambda h, i: (h, i, 0))],
        out_specs=[tile2, res2, pl.BlockSpec((s, HEAD_D), lambda h, i: (0, h))],
        out_shape=[jax.ShapeDtypeStruct((s, 2 * N_HEADS * HEAD_D), F32)] * 2
        + [jax.ShapeDtypeStruct((s, N_HEADS * HEAD_D), F32)],
        compiler_params=_cparams(("parallel", "arbitrary")),
    )(q, k, kv, o, do, lse)


def _split_dot(x, tri):
    hi = x.astype(BF16)
    lo = (x - hi.astype(F32)).astype(BF16)
    return _dot(hi, tri) + _dot(lo, tri)


TRI = 256


def _tri(kind):
    r0 = lax.broadcasted_iota(jnp.int32, (TRI, TRI), 0)
    c0 = lax.broadcasted_iota(jnp.int32, (TRI, TRI), 1)
    return {"after": r0 > c0, "upto": r0 <= c0, "before": r0 < c0}[kind].astype(BF16)


def _group_sums(x, tri, reverse, split):
    n = x.shape[1] // TRI
    parts = [x[:, g * TRI:(g + 1) * TRI] for g in range(n)]
    tots = [jnp.sum(pp, axis=1, keepdims=True) for pp in parts]
    outs = [None] * n
    run = None
    for g in (reversed(range(n)) if reverse else range(n)):
        inner = _split_dot(parts[g], tri) if split else _dot(parts[g].astype(BF16), tri)
        outs[g] = inner if run is None else inner + run
        run = tots[g] if run is None else run + tots[g]
    return (outs[0] if n == 1 else jnp.concatenate(outs, axis=1)), run


LOG2E = 1.4426950408889634


def _log2_sigmoid(z2):
    return jnp.minimum(z2, 0.0) - jnp.log(1.0 + jnp.exp2(-jnp.abs(z2))) * LOG2E


def _sb_fwd(qkv, *, name):
    s = qkv.shape[0]
    bq, bk = _attn_blocks(s)
    r = bq // bk
    scale = float(HEAD_D ** -0.5)

    def body(q_ref, k_ref, v_ref, o_ref, tot_ref):
        qi = pl.program_id(1)
        qv = q_ref[...]
        nfull = qi * r
        rel = _rel_index(bq, bk)
        after = _tri("after")

        def step(j, carry, d):
            car, acc = carry
            ks = pl.ds(pl.multiple_of(j * bk, bk), bk)
            z = _dot_nt(qv, k_ref[ks, :]) * (scale * LOG2E)
            lb = _log2_sigmoid(z)
            lr = lb - z
            if d is not None:
                mask = rel > d * bk
                lr = jnp.where(mask, lr, 0.0)
            later, tot_lr = _group_sums(lr, after, True, True)
            w = jnp.exp2(lb + later + car)
            if d is not None:
                w = jnp.where(mask, w, 0.0)
            acc = acc + _dot(w.astype(BF16), v_ref[ks, :])
            return car + tot_lr, acc

        carry = (jnp.zeros((bq, 1), F32), jnp.zeros((bq, HEAD_D), F32))
        for d in reversed(range(r)):
            carry = step(nfull + d, carry, d)
        car, acc = lax.fori_loop(0, nfull, lambda i, c: step(nfull - 1 - i, c, None), carry)
        o_ref[...] = acc
        tot_ref[0] = car

    res = lambda off: pl.BlockSpec((s, HEAD_D), functools.partial(lambda h, i, o: (0, h + o), o=off))
    return pl.pallas_call(
        body, name=name,
        grid=(N_HEADS, s // bq),
        in_specs=[pl.BlockSpec((bq, HEAD_D), lambda h, i: (i, h)), res(N_HEADS), res(2 * N_HEADS)],
        out_specs=[pl.BlockSpec((bq, HEAD_D), lambda h, i: (i, h)), pl.BlockSpec((1, bq, 1), lambda h, i: (h, i, 0))],
        out_shape=[jax.ShapeDtypeStruct((s, N_HEADS * HEAD_D), F32), jax.ShapeDtypeStruct((N_HEADS, s, 1), F32)],
        compiler_params=_cparams(("parallel", "arbitrary")),
    )(qkv, qkv, qkv)


def _sb_bwd(qkv, do, tot, *, name):
    s = qkv.shape[0]
    bq, bk = _attn_blocks(s)
    r = bq // bk
    scale = float(HEAD_D ** -0.5)

    def body(q_ref, k_ref, v_ref, do_ref, tot_ref, dq_ref, dk_ref, dv_ref):
        qi = pl.program_id(1)

        @pl.when(qi == 0)
        def _():
            dk_ref[...] = jnp.zeros_like(dk_ref)
            dv_ref[...] = jnp.zeros_like(dv_ref)

        qv = q_ref[...]
        dob = do_ref[...].astype(BF16)
        nfull = qi * r
        rel = _rel_index(bq, bk)
        upto = _tri("upto")
        before = _tri("before")

        def step(j, carry, d):
            rest, gsum, dq = carry
            ks = pl.ds(pl.multiple_of(j * bk, bk), bk)
            kv_ = k_ref[ks, :]
            z = _dot_nt(qv, kv_) * (scale * LOG2E)
            lb = _log2_sigmoid(z)
            lr = lb - z
            if d is not None:
                mask = rel > d * bk
                lr = jnp.where(mask, lr, 0.0)
            sofar, tot_lr = _group_sums(lr, upto, False, True)
            w = jnp.exp2(lb + rest - sofar)
            if d is not None:
                w = jnp.where(mask, w, 0.0)
            g = _dot_nt(dob, v_ref[ks, :]) * w
            gpre, tot_g = _group_sums(g, before, False, False)
            dlr = gsum + gpre
            dz = (g * jnp.exp2(lr) - dlr * jnp.exp2(lb)) * scale
            if d is not None:
                dz = jnp.where(mask, dz, 0.0)
            dzb = dz.astype(BF16)
            dk_ref[ks, :] += _dot_tn(dzb, qv)
            dv_ref[ks, :] += _dot_tn(w.astype(BF16), dob)
            return rest - tot_lr, gsum + tot_g, dq + _dot(dzb, kv_)

        carry = (tot_ref[0], jnp.zeros((bq, 1), F32), jnp.zeros((bq, HEAD_D), F32))
        carry = lax.fori_loop(0, nfull, lambda j, c: step(j, c, None), carry)
        for d in range(r):
            carry = step(nfull + d, carry, d)
        dq_ref[...] = carry[2]

    tile = pl.BlockSpec((bq, HEAD_D), lambda h, i: (i, h))
    res = lambda off: pl.BlockSpec((s, HEAD_D), functools.partial(lambda h, i, o: (0, h + o), o=off))
    wide = jax.ShapeDtypeStruct((s, N_HEADS * HEAD_D), F32)
    return pl.pallas_call(
        body, name=name,
        grid=(N_HEADS, s // bq),
        in_specs=[tile, res(N_HEADS), res(2 * N_HEADS), tile, pl.BlockSpec((1, bq, 1), lambda h, i: (h, i, 0))],
        out_specs=[tile, res(0), res(0)],
        out_shape=[wide] * 3,
        compiler_params=_cparams(("parallel", "arbitrary")),
    )(qkv, qkv, qkv, do, tot)


def _exchange(bufs, *, all_gather, name):
    nb = len(bufs)
    shapes = [((N_DEV,) + b.shape) if all_gather else b.shape for b in bufs]

    def body(*refs):
        srcs, outs = refs[:nb], refs[nb:2 * nb]
        send_sems, recv_sems, local_sems = refs[2 * nb:]
        x, y, c = lax.axis_index("x"), lax.axis_index("y"), lax.axis_index("c")
        me = 4 * x + 2 * y + c

        def src_for(b, idx):
            return srcs[b] if all_gather else srcs[b].at[idx]

        def copy(slot, b, dev, src_idx, dst_idx):
            return pltpu.make_async_remote_copy(
                src_ref=src_for(b, src_idx), dst_ref=outs[b].at[dst_idx], send_sem=send_sems.at[slot * nb + b],
                recv_sem=recv_sems.at[slot * nb + b], device_id=dev, device_id_type=pl.DeviceIdType.MESH)

        owns = [pltpu.make_async_copy(src_for(b, me), outs[b].at[me], local_sems.at[b]) for b in range(nb)]
        for own in owns:
            own.start()
        peers = []
        for k in range(1, N_DEV):
            px = (1 - x) if (k >> 2) & 1 else x
            py = (1 - y) if (k >> 1) & 1 else y
            pc = (1 - c) if k & 1 else c
            peers.append((k - 1, (px, py, pc), 4 * px + 2 * py + pc))
        sends = [copy(slot, b, dev, idx, me) for slot, dev, idx in peers for b in range(nb)]
        for cp in sends:
            cp.start()
        for slot, dev, idx in peers:
            for b in range(nb):
                copy(slot, b, dev, idx, idx).wait_recv()
        for cp in sends:
            cp.wait_send()
        for own in owns:
            own.wait()

    any_spec = pl.BlockSpec(memory_space=pl.ANY)
    return pl.pallas_call(
        body, name=name,
        in_specs=[any_spec] * nb,
        out_specs=[any_spec] * nb,
        out_shape=[jax.ShapeDtypeStruct(sh, b.dtype) for sh, b in zip(shapes, bufs)],
        scratch_shapes=[pltpu.SemaphoreType.DMA(((N_DEV - 1) * nb,)), pltpu.SemaphoreType.DMA(((N_DEV - 1) * nb,)),
                        pltpu.SemaphoreType.DMA((nb,))],
    )(*bufs)


def _gather_via_sibling(buf, *, name):
    def body(x_ref, out_ref, send_sems, recv_sems, local_sem):
        x, y, c = lax.axis_index("x"), lax.axis_index("y"), lax.axis_index("c")
        me, sibling = (x, y, c), (x, y, 1 - c)
        chips = [(1 - x, y), (x, 1 - y), (1 - x, 1 - y)]

        def slab(px, py, pc):
            return out_ref.at[4 * px + 2 * py + pc]

        def copy(k, block, to, src=None):
            return pltpu.make_async_remote_copy(
                src_ref=slab(*block) if src is None else src, dst_ref=slab(*block), send_sem=send_sems.at[k],
                recv_sem=recv_sems.at[k], device_id=to, device_id_type=pl.DeviceIdType.MESH)

        mine = pltpu.make_async_copy(x_ref, slab(*me), local_sem)
        mine.start()
        first = [copy(0, me, sibling, src=x_ref)]
        first += [copy(1 + j, me, (*chip, c), src=x_ref) for j, chip in enumerate(chips)]
        for cp in first:
            cp.start()
        passed = [copy(4 + j, (*chip, c), sibling) for j, chip in enumerate(chips)]
        for j, chip in enumerate(chips):
            copy(1 + j, (*chip, c), me).wait_recv()
            passed[j].start()
        copy(0, sibling, me).wait_recv()
        for j, chip in enumerate(chips):
            copy(4 + j, (*chip, 1 - c), me).wait_recv()
        for cp in first + passed:
            cp.wait_send()
        mine.wait()

    return pl.pallas_call(
        body, name=name,
        in_specs=[pl.BlockSpec(memory_space=pl.ANY)],
        out_specs=pl.BlockSpec(memory_space=pl.ANY),
        out_shape=jax.ShapeDtypeStruct((N_DEV,) + buf.shape, buf.dtype),
        scratch_shapes=[pltpu.SemaphoreType.DMA((N_DEV - 1,)), pltpu.SemaphoreType.DMA((N_DEV - 1,)),
                        pltpu.SemaphoreType.DMA],
    )(buf)


_HBM = pl.BlockSpec(memory_space=pltpu.HBM)
_SEM = pl.BlockSpec(memory_space=pltpu.SEMAPHORE)
_EFFECT = pltpu.SideEffectType.DATAFLOW_SIDE_EFFECTING


def _peers():
    x, y, c = lax.axis_index("x"), lax.axis_index("y"), lax.axis_index("c")
    peers = []
    for k in range(1, N_DEV):
        px = (1 - x) if (k >> 2) & 1 else x
        py = (1 - y) if (k >> 1) & 1 else y
        pc = (1 - c) if k & 1 else c
        peers.append((k - 1, (px, py, pc), 4 * px + 2 * py + pc))
    return 4 * x + 2 * y + c, peers


def _exchange_start(bufs, *, all_gather, name):
    nb = len(bufs)
    shapes = [((N_DEV,) + b.shape) if all_gather else b.shape for b in bufs]
    ncp = (N_DEV - 1) * nb

    def body(*refs):
        srcs, lands = refs[:nb], refs[nb:2 * nb]
        send_sems, recv_sems = refs[2 * nb], refs[2 * nb + 1]
        token = refs[-1]
        me, peers = _peers()
        for slot, dev, idx in peers:
            for b in range(nb):
                pltpu.make_async_remote_copy(
                    src_ref=srcs[b] if all_gather else srcs[b].at[idx], dst_ref=lands[b].at[me],
                    send_sem=send_sems.at[slot * nb + b], recv_sem=recv_sems.at[slot * nb + b],
                    device_id=dev, device_id_type=pl.DeviceIdType.MESH).start()
        token[...] = jnp.zeros_like(token)

    ins = [pltpu.with_memory_space_constraint(b, pltpu.HBM) for b in bufs]
    ins += [pltpu.with_memory_space_constraint(lax.empty(sh, b.dtype), pltpu.HBM) for sh, b in zip(shapes, bufs)]
    outs = pl.pallas_call(
        body, name=name,
        out_shape=(pltpu.SemaphoreType.DMA((ncp,)), pltpu.SemaphoreType.DMA((ncp,)),
                   *[pltpu.HBM(b.shape, b.dtype) for b in bufs], *[pltpu.HBM(sh, b.dtype) for sh, b in zip(shapes, bufs)],
                   jax.ShapeDtypeStruct((8, 128), F32)),
        in_specs=[_HBM] * (2 * nb),
        out_specs=(_SEM, _SEM, *([_HBM] * (2 * nb)), pl.BlockSpec(memory_space=pltpu.VMEM)),
        input_output_aliases={i: 2 + i for i in range(2 * nb)},
        compiler_params=pltpu.CompilerParams(has_side_effects=_EFFECT),
    )(*ins)
    return outs[:-1], outs[-1]


def _exchange_finish(started, after, *, all_gather, name):
    nb = (len(started) - 2) // 2
    send_sems, recv_sems = started[0], started[1]
    thru = started[2:]

    def body(*refs):
        srcs, lands = refs[:nb], refs[nb:2 * nb]
        ssem, rsem = refs[2 * nb], refs[2 * nb + 1]
        me, peers = _peers()
        for slot, dev, idx in peers:
            for b in range(nb):
                cp = pltpu.make_async_remote_copy(
                    src_ref=srcs[b] if all_gather else srcs[b].at[idx], dst_ref=lands[b].at[idx],
                    send_sem=ssem.at[slot * nb + b], recv_sem=rsem.at[slot * nb + b],
                    device_id=dev, device_id_type=pl.DeviceIdType.MESH)
                cp.wait_send()
                cp.wait_recv()

    done = pl.pallas_call(
        body, name=name,
        out_shape=tuple(pltpu.HBM(t.shape, t.dtype) for t in thru),
        in_specs=[_HBM] * (2 * nb) + [_SEM, _SEM, pl.BlockSpec(memory_space=pl.ANY)],
        out_specs=[_HBM] * (2 * nb),
        input_output_aliases={i: i for i in range(2 * nb)},
        compiler_params=pltpu.CompilerParams(has_side_effects=_EFFECT),
    )(*thru, send_sems, recv_sems, after)
    srcs, lands = done[:nb], done[nb:]
    me = 4 * lax.axis_index("x") + 2 * lax.axis_index("y") + lax.axis_index("c")
    outs = []
    for s_, l_ in zip(srcs, lands):
        own = s_[None] if all_gather else lax.dynamic_slice_in_dim(s_, me, 1, axis=0)
        outs.append(lax.dynamic_update_slice_in_dim(l_, own, me, axis=0))
    return outs


def _sum8(buf, *, name):
    _, r, c = buf.shape
    tr = _pick(r, (256, 128, 64, 32, 16, 8))

    def body(b_ref, o_ref):
        acc = b_ref[0].astype(F32)
        for q in range(1, N_DEV):
            acc = acc + b_ref[q].astype(F32)
        o_ref[...] = acc

    return pl.pallas_call(
        body, name=name,
        grid=(r // tr,),
        in_specs=[pl.BlockSpec((N_DEV, tr, c), lambda i: (0, i, 0))],
        out_specs=pl.BlockSpec((tr, c), lambda i: (i, 0)),
        out_shape=jax.ShapeDtypeStruct((r, c), F32),
        compiler_params=_cparams(("parallel",)),
    )(buf)


def _pack_rows(flat, mult):
    n = flat.shape[0]
    chunk = mult * PACK_COLS
    tot = -(-n // chunk) * chunk
    return jnp.pad(flat, (0, tot - n)).reshape(tot // PACK_COLS, PACK_COLS)


_ADAM_BC1 = 1.0 - ADAM_B1 ** ADAM_STEP
_ADAM_BC2 = 1.0 - ADAM_B2 ** ADAM_STEP


def _adamw_math(wv, gv, mv, vv):
    mn = ADAM_B1 * mv + (1.0 - ADAM_B1) * gv
    vn = ADAM_B2 * vv + (1.0 - ADAM_B2) * (gv * gv)
    m_hat = mn / _ADAM_BC1
    v_hat = vn / _ADAM_BC2
    delta = -ADAM_LR * (m_hat / (jnp.sqrt(v_hat) + ADAM_EPS) + ADAM_WD * wv)
    return delta, mn, vn


def _adamw(w, g, m, v, *, name):
    shape = w.shape
    cols = shape[-1]
    to2d = lambda a: a.reshape(-1, cols)
    d, mn, vn = _ew(_adamw_math, [to2d(w), to2d(g), to2d(m), to2d(v)], [(cols, F32)] * 3, name=name)
    return d.reshape(shape), mn.reshape(shape), vn.reshape(shape)


def _adamw_sharded(w, parts, m, v, *, name):
    depth = w.shape[0]
    cols = w.shape[-1]
    rows = w[0].size // cols
    to3 = lambda a: a.reshape(depth, rows, cols)
    w3, m3, v3 = to3(w), to3(m), to3(v)
    bpr = cols * (3 * 4 + N_DEV * parts[0].dtype.itemsize + 4 * 4)
    tm = _row_tile(rows, bpr)
    outs = None
    for li in range(depth):
        carried = li > 0

        def body(*refs):
            w_ref, p_ref, m_ref, v_ref = refs[:4]
            g_ref, d_ref, mo_ref, vo_ref = refs[-4:]
            gv = p_ref[0].astype(F32)
            for q in range(1, N_DEV):
                gv = gv + p_ref[q].astype(F32)
            delta, mn, vn = _adamw_math(w_ref[0], gv, m_ref[0], v_ref[0])
            g_ref[0] = gv
            d_ref[0] = delta
            mo_ref[0] = mn
            vo_ref[0] = vn

        lay = pl.BlockSpec((1, tm, cols), functools.partial(lambda i, l: (l, i, 0), l=li))
        any_spec = pl.BlockSpec(memory_space=pl.ANY)
        ins = [w3, parts[li].reshape(N_DEV, rows, cols), m3, v3] + (list(outs) if carried else [])
        outs = pl.pallas_call(
            body, name=f"{name}_l{li}",
            grid=(rows // tm,),
            in_specs=[lay, pl.BlockSpec((N_DEV, tm, cols), lambda i: (0, i, 0)), lay, lay]
            + ([any_spec] * 4 if carried else []),
            out_specs=[lay] * 4,
            out_shape=[jax.ShapeDtypeStruct((depth, rows, cols), F32)] * 4,
            input_output_aliases={4 + q: q for q in range(4)} if carried else {},
            compiler_params=_cparams(("parallel",)),
        )(*ins)
    return [o.reshape(w.shape) for o in outs]


def _branch_merge(ys, w_branch, logits, *, name):
    s = ys.shape[0]
    d = w_branch.shape[2]
    tm, tn = _pick(s, (512, 256, 128)), _pick(d, (1024, 512, 256, 128))

    def body(ys_ref, w_ref, l0_ref, l1_ref, l2_ref, m_ref, b0_ref, b1_ref, b2_ref):
        merged = None
        for b, (l_ref, b_ref) in enumerate(((l0_ref, b0_ref), (l1_ref, b1_ref), (l2_ref, b2_ref))):
            acc = _dot(ys_ref[:, b * BRANCH_W:(b + 1) * BRANCH_W], w_ref[b])
            b_ref[...] = acc.astype(b_ref.dtype)
            term = _sigmoid(l_ref[...]) * acc
            merged = term if merged is None else merged + term
        m_ref[...] = merged.astype(m_ref.dtype)

    nj = d // tn
    tile = pl.BlockSpec((tm, tn), lambda i, j: (i, j))
    logit = lambda b: pl.BlockSpec((tm, tn), functools.partial(lambda i, j, o: (i, j + o), o=b * nj))
    return pl.pallas_call(
        body, name=name,
        grid=(s // tm, nj),
        in_specs=[pl.BlockSpec((tm, N_BRANCH * BRANCH_W), lambda i, j: (i, 0)),
                  pl.BlockSpec((N_BRANCH, BRANCH_W, tn), lambda i, j: (0, 0, j)), logit(0), logit(1), logit(2)],
        out_specs=[tile] * 4,
        out_shape=[jax.ShapeDtypeStruct((s, d), BF16)] * 4,
        compiler_params=_cparams(("parallel", "parallel")),
    )(ys, w_branch, logits, logits, logits)


def _prep_in(w_in):
    rope_cols = jnp.pad(w_in[:, 2048:2048 + MLA_ROPE], ((0, 0), (0, 128 - MLA_ROPE)))
    return dict(wa=w_in[:, :2048], wb=w_in[:, 2112:5184], wc=w_in[:, 5184:8256], wd=w_in[:, 8256:14400], we=rope_cols)


def _prep_rest(full):
    wq = full["mla_w_uq"].reshape(512, N_HEADS, MLA_NOPE + MLA_ROPE)
    wkv = full["mla_w_ukv"].reshape(512, N_HEADS, 2 * HEAD_D)
    return dict(
        w_uq=jnp.pad(wq, ((0, 0), (0, 0), (0, 2 * HEAD_D - MLA_NOPE - MLA_ROPE))).reshape(512, 2 * N_HEADS * HEAD_D),
        w_ukv=jnp.concatenate([wkv[:, :, :HEAD_D].reshape(512, 1024), wkv[:, :, HEAD_D:].reshape(512, 1024)], axis=1),
        w_glu=full["ssm_w_glu"], w_branch=full["w_branch"], w_out=full["w_out"],
        w_ple_gate=full["w_ple_gate"], w_ple_proj=full["w_ple_proj"])


def _prep_ssm(sp):
    ab_re, ab_im, bb_re, bb_im = _s5_disc(sp["lam_re"], sp["lam_im"], sp["log_dt"], sp["b_re"], sp["b_im"])
    g4 = lambda a: a.reshape(SSM_GB, SSM_GB, a.shape[1], a.shape[2])
    bbd_re = _blockdiag(jnp.swapaxes(g4(bb_re), 2, 3)).astype(BF16)
    bbd_im = _blockdiag(jnp.swapaxes(g4(bb_im), 2, 3)).astype(BF16)
    cd_re = _blockdiag(jnp.swapaxes(g4(sp["c_re"]), 2, 3)).astype(BF16)
    cd_im = _blockdiag(jnp.swapaxes(g4(sp["c_im"]), 2, 3)).astype(BF16)
    return dict(bb_re=bbd_re, bb_im=bbd_im, bbt_re=jnp.swapaxes(bbd_re, 1, 2), bbt_im=jnp.swapaxes(bbd_im, 1, 2),
                c_re=cd_re, c_im=cd_im, ct_re=jnp.swapaxes(cd_re, 1, 2), ct_im=jnp.swapaxes(cd_im, 1, 2),
                dsk=sp["d"].reshape(1, BRANCH_W))


GELU_C = math.sqrt(2.0 / math.pi)


def _gelu(x):
    return 0.5 * x * (1.0 + jnp.tanh(GELU_C * (x + 0.044715 * x * x * x)))


def _gelu_grad(x):
    t = jnp.tanh(GELU_C * (x + 0.044715 * x * x * x))
    return 0.5 * (1.0 + t) + 0.5 * x * (1.0 - t * t) * GELU_C * (1.0 + 3.0 * 0.044715 * x * x)


def _layer_fwd(x_in, p_l, hooks, ssm, tabs, sm, rope, li):
    tg = lambda nm: nm
    ct, sp_, sm_ = rope
    h = _rmsnorm(x_in, sm["ln_g"], name=tg("ln"))
    w = dict(hooks.weights_in(li, x_in))
    seg_a = _mm(h, w["wa"], name=tg("in_a"))
    seg_b = _mm(h, w["wb"], name=tg("in_b"), out_dtypes=(BF16,))
    seg_c = _mm(h, w["wc"], name=tg("in_c"))
    seg_d = _mm(h, w["wd"], name=tg("in_d"))
    seg_e = _mm(h, w["we"], name=tg("in_e"))

    o_sb, tot = _sb_fwd(seg_b, name=tg("sb_fwd"))
    w.update(hooks.weights_rest(li, o_sb))

    y_pre, hc_re, hc_im, hs_re, hs_im = _s5_fwd(seg_a, ssm["bb_re"], ssm["bb_im"], ssm["c_re"], ssm["c_im"], ssm["dsk"], tabs,
                                  name=tg("s5_fwd"))
    yg = _ew(_gelu, [y_pre], [(BRANCH_W, F32)], name=tg("gelu"))
    y_ssm, z_glu = _mm(yg, w["w_glu"], name=tg("glu"), extras=[(yg, 0)],
                       epilogue=lambda acc, ygv: (ygv * _sigmoid(acc), acc), out_dtypes=(F32, F32))

    q_lat = seg_a[:, 1024:1536]
    kv_lat = seg_a[:, 1536:2048]
    qn = _rmsnorm(q_lat, sm["mla_g_q"], name=tg("q_norm"))
    kvn = _rmsnorm(kv_lat, sm["mla_g_kv"], name=tg("kv_norm"))
    q_raw = _mm(qn, w["w_uq"], name=tg("uq"))
    kv_b = _mm(kvn, w["w_ukv"], name=tg("ukv"), out_dtypes=(BF16,))

    def rope_q(qv, c_, p_, m_):
        outs = []
        for hh in range(N_HEADS):
            outs.append(qv[:, 256 * hh:256 * hh + 128])
            outs.append(_rope_group(qv[:, 256 * hh + 128:256 * hh + 256], c_, p_, m_))
        return jnp.concatenate(outs, axis=1)

    def rope_k(kvv, kr, c_, p_, m_):
        kr_rot = _rope_group(kr, c_, p_, m_).astype(BF16)
        outs = []
        for hh in range(N_HEADS):
            outs.append(kvv[:, 128 * hh:128 * hh + 128])
            outs.append(kr_rot)
        return jnp.concatenate(outs, axis=1)

    q_b = _ew(rope_q, [q_raw, ct, sp_, sm_], [(2048, BF16)], name=tg("rope_q"))
    k_b = _ew(rope_k, [kv_b, seg_e, ct, sp_, sm_], [(2048, BF16)], name=tg("rope_k"))
    o_mla, lse = _mla_fwd(q_b, k_b, kv_b, name=tg("mla_fwd"))

    def gate_fn(a, b, c_, gp):
        ys = jnp.concatenate([a, b, c_], axis=1)
        return ys * (gp * _sigmoid(gp))

    ys = _ew(gate_fn, [y_ssm, o_mla, o_sb, seg_c], [(3 * BRANCH_W, BF16)], name=tg("gate"))
    merged_b, *bos = _branch_merge(ys, w["w_branch"], seg_d, name=tg("branch"))
    x1 = _mm(merged_b, w["w_out"], name=tg("out"), extras=[(x_in, 0)], epilogue=lambda acc, xv: (xv + acc,))

    hn2 = _rmsnorm(x1, sm["ple_g"], name=tg("ple_norm"))
    e = _mm(p_l, w["w_ple_proj"], name=tg("ple_proj"))
    x2, gl = _mm(hn2, w["w_ple_gate"], name=tg("ple_gate"), extras=[(x1, 0), (e, 0)],
                 epilogue=lambda acc, xv, ev: (xv + _sigmoid(acc) * ev, acc), out_dtypes=(F32, F32))
    saved = dict(x_in=x_in, h=h, seg_a=seg_a, seg_b=seg_b, seg_c=seg_c, seg_d=seg_d, y_pre=y_pre, hc_re=hc_re,
                 hc_im=hc_im, hs_re=hs_re, hs_im=hs_im, yg=yg, y_ssm=y_ssm, z_glu=z_glu, q_lat=q_lat, kv_lat=kv_lat, qn=qn, kvn=kvn, q_b=q_b,
                 kv_b=kv_b, k_b=k_b, o_mla=o_mla, lse=lse, o_sb=o_sb, tot=tot, ys=ys, bos=bos, merged_b=merged_b,
                 x1=x1, hn2=hn2, e=e, gl=gl, p_l=p_l, w=w)
    return x2, saved


def _layer_bwd(dx2, sv, hooks, ssm, tabs, sm, rope, li):
    tg = lambda nm: nm
    w = sv["w"]
    ct, sp_, sm_ = rope
    gw, gs = {}, {}

    def ple_fn(d, ev, glv):
        gt = _sigmoid(glv)
        return d * ev * gt * (1.0 - gt), d * gt

    dgl, de = _ew(ple_fn, [dx2, sv["e"], sv["gl"]], [(D_MODEL, BF16), (D_MODEL, BF16)], name=tg("ple_bwd"))
    gw["w_ple_proj"] = _mm(sv["p_l"], de, ta=True, out_dtypes=(BF16,), name=tg("d_ple_proj"))
    gw["w_ple_gate"] = _mm(sv["hn2"], dgl, ta=True, out_dtypes=(BF16,), name=tg("d_ple_gate"))
    dhn2 = _mm(dgl, w["w_ple_gate"], tb=True, name=tg("ple_gate_t"))
    dx1, gs["ple_g"] = _rmsnorm_bwd(sv["x1"], sm["ple_g"], dhn2, dx2, name=tg("ple_norm_bwd"))

    dx1_b = dx1.astype(BF16)
    gw["w_out"] = _mm(sv["merged_b"], dx1_b, ta=True, out_dtypes=(BF16,), name=tg("d_out"))
    dmerged = _mm(dx1_b, w["w_out"], tb=True, name=tg("out_t"))

    def merge_fn(dm, ml, b0, b1, b2):
        dbo, dml = [], []
        for b, bo in enumerate((b0, b1, b2)):
            gt = _sigmoid(ml[:, b * D_MODEL:(b + 1) * D_MODEL])
            dbo.append(dm * gt)
            dml.append(dm * bo.astype(F32) * gt * (1.0 - gt))
        return jnp.concatenate(dbo, axis=1), jnp.concatenate(dml, axis=1)

    dbo, dseg_d = _ew(merge_fn, [dmerged, sv["seg_d"]] + sv["bos"], [(3 * D_MODEL, BF16), (3 * D_MODEL, BF16)],
                      name=tg("merge_bwd"))
    dys, dwb = [], []
    for b in range(3):
        dbo_b = dbo[:, b * D_MODEL:(b + 1) * D_MODEL]
        dwb.append(_mm(sv["ys"][:, b * BRANCH_W:(b + 1) * BRANCH_W], dbo_b, ta=True, out_dtypes=(BF16,), name=tg(f"d_branch{b}")))
        dys.append(_mm(dbo_b, w["w_branch"][b], tb=True, name=tg(f"branch{b}_t")))
    gw["w_branch"] = jnp.stack(dwb)

    def gate_bwd(d0, d1, d2, a, b, c_, gp):
        dy_all = jnp.concatenate([d0, d1, d2], axis=1)
        y_all = jnp.concatenate([a, b, c_], axis=1)
        sg = _sigmoid(gp)
        silu = gp * sg
        dsilu = sg * (1.0 + gp * (1.0 - sg))
        dyv = dy_all * silu
        return (dyv[:, :BRANCH_W], dyv[:, BRANCH_W:2 * BRANCH_W], dyv[:, 2 * BRANCH_W:], dy_all * y_all * dsilu)

    dy_ssm, do_mla, do_sb, dseg_c = _ew(
        gate_bwd, dys + [sv["y_ssm"], sv["o_mla"], sv["o_sb"], sv["seg_c"]],
        [(BRANCH_W, F32), (BRANCH_W, F32), (BRANCH_W, F32), (3 * BRANCH_W, BF16)], name=tg("gate_bwd"))

    dq_sb, dk_sb, dv_sb = _sb_bwd(sv["seg_b"], do_sb, sv["tot"], name=tg("sb_bwd"))
    dseg_b = jnp.concatenate([dq_sb, dk_sb, dv_sb], axis=1).astype(BF16)

    dq_, dk_, dv_ = _mla_bwd(sv["q_b"], sv["k_b"], sv["kv_b"], sv["o_mla"], do_mla, sv["lse"], name=tg("mla_bwd"))

    def unrope_q(dq, c_, p_, m_):
        outs = []
        for hh in range(N_HEADS):
            outs.append(dq[:, 256 * hh:256 * hh + 128])
            outs.append(_rope_group_t(dq[:, 256 * hh + 128:256 * hh + 256], c_, p_, m_))
        return jnp.concatenate(outs, axis=1)

    dq_raw = _ew(unrope_q, [dq_, ct, sp_, sm_], [(2048, BF16)], name=tg("unrope_q"))

    def unrope_k(dk, dvv, c_, p_, m_):
        tot_ = dk[:, 128:256]
        for hh in range(1, N_HEADS):
            tot_ = tot_ + dk[:, 256 * hh + 128:256 * hh + 256]
        dkn = [dk[:, 256 * hh:256 * hh + 128] for hh in range(N_HEADS)]
        return jnp.concatenate(dkn + [dvv], axis=1), _rope_group_t(tot_, c_, p_, m_)

    dkv_raw, dseg_e = _ew(unrope_k, [dk_, dv_, ct, sp_, sm_], [(2048, BF16), (128, BF16)], name=tg("unrope_k"))
    dw_uq = _mm(sv["qn"], dq_raw, ta=True, out_dtypes=(BF16,), name=tg("d_uq"))
    dw_ukv = _mm(sv["kvn"], dkv_raw, ta=True, out_dtypes=(BF16,), name=tg("d_ukv"))
    dqn = _mm(dq_raw, w["w_uq"], tb=True, name=tg("uq_t"))
    dkvn = _mm(dkv_raw, w["w_ukv"], tb=True, name=tg("ukv_t"))
    dq_lat, gs["mla_g_q"] = _rmsnorm_bwd(sv["q_lat"], sm["mla_g_q"], dqn, None, name=tg("q_norm_bwd"))
    dkv_lat, gs["mla_g_kv"] = _rmsnorm_bwd(sv["kv_lat"], sm["mla_g_kv"], dkvn, None, name=tg("kv_norm_bwd"))
    gw["mla_w_uq"] = dw_uq.reshape(512, N_HEADS, 2 * HEAD_D)[:, :, :MLA_NOPE + MLA_ROPE].reshape(
        512, N_HEADS * (MLA_NOPE + MLA_ROPE))
    dwk = dw_ukv[:, :1024].reshape(512, N_HEADS, HEAD_D)
    dwv = dw_ukv[:, 1024:].reshape(512, N_HEADS, HEAD_D)
    gw["mla_w_ukv"] = jnp.concatenate([dwk, dwv], axis=2).reshape(512, 2 * N_HEADS * HEAD_D)

    def glu_bwd(d, ygv, zv):
        sg = _sigmoid(zv)
        return d * ygv * sg * (1.0 - sg), d * sg

    dz, dyg0 = _ew(glu_bwd, [dy_ssm, sv["yg"], sv["z_glu"]], [(BRANCH_W, BF16), (BRANCH_W, F32)], name=tg("glu_bwd"))
    gw["ssm_w_glu"] = _mm(sv["yg"], dz, ta=True, out_dtypes=(BF16,), name=tg("d_glu"))
    dy_pre = _mm(dz, w["w_glu"], tb=True, name=tg("glu_t"), extras=[(dyg0, 0), (sv["y_pre"], 0)],
                 epilogue=lambda acc, d0, yp: ((acc + d0) * _gelu_grad(yp),))
    token = hooks.grads_rest(li, gw)
    dsk = ssm["dsk"] if token is None else ssm["dsk"] + token[0, 0]
    du, dbb_re, dbb_im, dc_re, dc_im, da_re, da_im, dd = _s5_bwd(
        sv["seg_a"], dy_pre, sv["hc_re"], sv["hc_im"], sv["hs_re"], sv["hs_im"], ssm["bbt_re"], ssm["bbt_im"],
        ssm["ct_re"], ssm["ct_im"], dsk, tabs, name=tg("s5_bwd"))
    gs["ssm_d"] = dd.reshape(BRANCH_W)
    gs["ssm_c_re"] = jnp.swapaxes(_blockdiag_extract(dc_re, SSM_STATE, SSM_GROUP), 2, 3).reshape(
        SSM_GROUPS, SSM_GROUP, SSM_STATE)
    gs["ssm_c_im"] = jnp.swapaxes(_blockdiag_extract(dc_im, SSM_STATE, SSM_GROUP), 2, 3).reshape(
        SSM_GROUPS, SSM_GROUP, SSM_STATE)
    gs["_dbb_re"] = jnp.swapaxes(_blockdiag_extract(dbb_re, SSM_GROUP, SSM_STATE), 2, 3).reshape(
        SSM_GROUPS, SSM_STATE, SSM_GROUP)
    gs["_dbb_im"] = jnp.swapaxes(_blockdiag_extract(dbb_im, SSM_GROUP, SSM_STATE), 2, 3).reshape(
        SSM_GROUPS, SSM_STATE, SSM_GROUP)
    gs["_da_re"] = da_re.reshape(SSM_GROUPS, SSM_STATE)
    gs["_da_im"] = da_im.reshape(SSM_GROUPS, SSM_STATE)

    dseg_a = jnp.concatenate([du, dq_lat, dkv_lat], axis=1).astype(BF16)
    hb = sv["h"]
    dwa = _mm(hb, dseg_a, ta=True, out_dtypes=(BF16,), name=tg("d_in_a"))
    dwb_ = _mm(hb, dseg_b, ta=True, out_dtypes=(BF16,), name=tg("d_in_b"))
    dwc = _mm(hb, dseg_c, ta=True, out_dtypes=(BF16,), name=tg("d_in_c"))
    dwd = _mm(hb, dseg_d, ta=True, out_dtypes=(BF16,), name=tg("d_in_d"))
    dwe = _mm(hb, dseg_e, ta=True, out_dtypes=(BF16,), name=tg("d_in_e"))
    gw["w_in"] = jnp.concatenate([dwa, dwe[:, :MLA_ROPE], dwb_, dwc, dwd], axis=1)
    token = hooks.grads_in(li, gw["w_in"])
    wa = w["wa"] if token is None else w["wa"] + token[0, 0].astype(BF16)
    add = lambda acc, prev: (acc + prev,)
    dh = _mm(dseg_a, wa, tb=True, name=tg("in_a_t"))
    dh = _mm(dseg_b, w["wb"], tb=True, name=tg("in_b_t"), extras=[(dh, 0)], epilogue=add)
    dh = _mm(dseg_c, w["wc"], tb=True, name=tg("in_c_t"), extras=[(dh, 0)], epilogue=add)
    dh = _mm(dseg_d, w["wd"], tb=True, name=tg("in_d_t"), extras=[(dh, 0)], epilogue=add)
    dh = _mm(dseg_e, w["we"], tb=True, name=tg("in_e_t"), extras=[(dh, 0)], epilogue=add)
    dx_in, gs["ln_g"] = _rmsnorm_bwd(sv["x_in"], sm["ln_g"], dh, dx1, name=tg("ln_bwd"))
    return dx_in, gw, gs


class _LocalWeights:
    def __init__(self, fulls):
        self.fulls = fulls

    def weights_in(self, li, after):
        return _prep_in(self.fulls[li]["w_in"])

    def weights_rest(self, li, after):
        return _prep_rest(self.fulls[li])

    def grads_rest(self, li, gw):
        return None

    def grads_in(self, li, g):
        return None

    def grads_layer(self, li, gw):
        return None


def _local_step(x, p, target, hooks, small):
    s = x.shape[0]
    depth = small["ln_g"].shape[0]
    rope = _rope_tables(s)
    ssms, tabss, sms, saved = [], [], [], []
    xc = x
    for li in range(depth):
        sp = dict(lam_re=small["ssm_lam_re"][li], lam_im=small["ssm_lam_im"][li], log_dt=small["ssm_log_dt"][li],
                  b_re=small["ssm_b_re"][li], b_im=small["ssm_b_im"][li], c_re=small["ssm_c_re"][li],
                  c_im=small["ssm_c_im"][li], d=small["ssm_d"][li])
        ssm = _prep_ssm(sp)
        tabs = _s5_tables(sp["lam_re"], sp["lam_im"], sp["log_dt"])
        sm = dict(ln_g=small["ln_g"][li], mla_g_q=small["mla_g_q"][li], mla_g_kv=small["mla_g_kv"][li],
                  ple_g=small["ple_g"][li])
        xc, sv = _layer_fwd(xc, p[li], hooks, ssm, tabs, sm, rope, li)
        ssms.append(ssm), tabss.append(tabs), sms.append(sm), saved.append(sv)
    dx, dgf, loss = _loss_head(xc, small["final_g"], target, name="loss_head")
    gws, gss = [None] * depth, [None] * depth
    token = None
    for li in reversed(range(depth)):
        sm = sms[li] if token is None else dict(sms[li], ple_g=sms[li]["ple_g"] + token[0, 0])
        dx, gw, gs = _layer_bwd(dx, saved[li], hooks, ssms[li], tabss[li], sm, rope, li)
        token = hooks.grads_layer(li, gw)
        args = (small["ssm_lam_re"][li], small["ssm_lam_im"][li], small["ssm_log_dt"][li], small["ssm_b_re"][li],
                small["ssm_b_im"][li])
        _, vjp = jax.vjp(_s5_disc, *args)
        g_lr, g_li, g_dt, g_br, g_bi = vjp((gs.pop("_da_re"), gs.pop("_da_im"), gs.pop("_dbb_re"), gs.pop("_dbb_im")))
        gs.update(ssm_lam_re=g_lr, ssm_lam_im=g_li, ssm_log_dt=g_dt, ssm_b_re=g_br, ssm_b_im=g_bi)
        gws[li], gss[li] = gw, gs
    gsmall = {nm: jnp.stack([gss[li][nm].reshape(small[nm].shape[1:]) for li in range(depth)])
              for nm in SMALL if nm != "final_g"}
    gsmall["final_g"] = dgf.reshape(-1)
    return loss[0, 0], dx, gws, gsmall


REST = tuple(t for t in SHARDED if t[0] != "w_in")


def _to_full(spec, got):
    _, fshape, ax = spec
    return jnp.moveaxis(got, 0, ax).reshape(fshape)


def _to_slabs(spec, g):
    _, fshape, ax = spec
    g = g.reshape(tuple(fshape[:ax]) + (N_DEV, fshape[ax] // N_DEV) + tuple(fshape[ax + 1:]))
    return jnp.moveaxis(g, ax, 0)


class _Exchanged:
    def __init__(self, shards):
        bf = lambda nm, li: shards[nm][li].astype(BF16)
        w_in0 = _gather_via_sibling(bf("w_in", 0), name="gather_w_in")
        self.w_in0 = _to_full(SHARDED[0], w_in0)
        first = bf(REST[-1][0], 0) + (w_in0[0, 0, 0] * 0).astype(BF16)
        rest0 = [bf(nm, 0) for nm, _, _ in REST[:-1]] + [first]
        self.rest0, tok_a = _exchange_start(rest0, all_gather=True, name="gather_rest_start")
        all1 = [bf(nm, 1) for nm, _, _ in SHARDED[:-1]] + [bf(SHARDED[-1][0], 1) + tok_a[0, 0].astype(BF16)]
        self.all1, tok_b = _exchange_start(all1, all_gather=True, name="gather_next_start")
        self.token = tok_a[0, 0] + tok_b[0, 0]
        self.full1 = None
        self.sent = {}

    def weights_in(self, li, after):
        if li == 0:
            return _prep_in(self.w_in0)
        got = _exchange_finish(self.all1, after, all_gather=True, name="gather_next_finish")
        self.full1 = {t[0]: _to_full(t, g) for t, g in zip(SHARDED, got)}
        return _prep_in(self.full1["w_in"])

    def weights_rest(self, li, after):
        if li == 1:
            return _prep_rest(self.full1)
        got = _exchange_finish(self.rest0, after, all_gather=True, name="gather_rest_finish")
        return _prep_rest({t[0]: _to_full(t, g) for t, g in zip(REST, got)})

    def grads_rest(self, li, gw):
        if li != 0:
            return None
        self.sent["rest0"], token = _exchange_start([_to_slabs(t, gw[t[0]]) for t in REST], all_gather=False,
                                                    name="scatter_rest_start")
        return token

    def grads_in(self, li, g):
        if li != 0:
            return None
        self.sent["w_in0"], token = _exchange_start([_to_slabs(SHARDED[0], g)], all_gather=False,
                                                    name="scatter_w_in_start")
        return token

    def grads_layer(self, li, gw):
        if li == 0:
            return None
        self.sent["all1"], token = _exchange_start([_to_slabs(t, gw[t[0]]) for t in SHARDED], all_gather=False,
                                                   name="scatter_prev_start")
        return token

    def partial_grads(self, after):
        got1 = _exchange_finish(self.sent["all1"], after, all_gather=False, name="scatter_prev_finish")
        got0 = _exchange_finish(self.sent["rest0"], after, all_gather=False, name="scatter_rest_finish")
        w_in0 = _exchange_finish(self.sent["w_in0"], after, all_gather=False, name="scatter_w_in_finish")[0]
        p0 = {t[0]: g for t, g in zip(REST, got0)}
        p0["w_in"] = w_in0
        return [p0, {t[0]: g for t, g in zip(SHARDED, got1)}]


def kernel(x, p, ln_g, w_in, ssm_lam_re, ssm_lam_im, ssm_log_dt, ssm_b_re, ssm_b_im, ssm_c_re, ssm_c_im, ssm_d, ssm_w_glu, mla_g_q, mla_g_kv, mla_w_uq, mla_w_ukv, w_branch, w_out, ple_g, w_ple_gate, w_ple_proj, final_g, loss_target, m_ln_g, m_w_in, m_ssm_lam_re, m_ssm_lam_im, m_ssm_log_dt, m_ssm_b_re, m_ssm_b_im, m_ssm_c_re, m_ssm_c_im, m_ssm_d, m_ssm_w_glu, m_mla_g_q, m_mla_g_kv, m_mla_w_uq, m_mla_w_ukv, m_w_branch, m_w_out, m_ple_g, m_w_ple_gate, m_w_ple_proj, m_final_g, v_ln_g, v_w_in, v_ssm_lam_re, v_ssm_lam_im, v_ssm_log_dt, v_ssm_b_re, v_ssm_b_im, v_ssm_c_re, v_ssm_c_im, v_ssm_d, v_ssm_w_glu, v_mla_g_q, v_mla_g_kv, v_mla_w_uq, v_mla_w_ukv, v_w_branch, v_w_out, v_ple_g, v_w_ple_gate, v_w_ple_proj, v_final_g):
    weights = dict(ln_g=ln_g, w_in=w_in, ssm_lam_re=ssm_lam_re, ssm_lam_im=ssm_lam_im, ssm_log_dt=ssm_log_dt,
                   ssm_b_re=ssm_b_re, ssm_b_im=ssm_b_im, ssm_c_re=ssm_c_re, ssm_c_im=ssm_c_im, ssm_d=ssm_d,
                   ssm_w_glu=ssm_w_glu, mla_g_q=mla_g_q, mla_g_kv=mla_g_kv, mla_w_uq=mla_w_uq, mla_w_ukv=mla_w_ukv,
                   w_branch=w_branch, w_out=w_out, ple_g=ple_g, w_ple_gate=w_ple_gate, w_ple_proj=w_ple_proj,
                   final_g=final_g)
    mom_m = dict(ln_g=m_ln_g, w_in=m_w_in, ssm_lam_re=m_ssm_lam_re, ssm_lam_im=m_ssm_lam_im, ssm_log_dt=m_ssm_log_dt,
                 ssm_b_re=m_ssm_b_re, ssm_b_im=m_ssm_b_im, ssm_c_re=m_ssm_c_re, ssm_c_im=m_ssm_c_im, ssm_d=m_ssm_d,
                 ssm_w_glu=m_ssm_w_glu, mla_g_q=m_mla_g_q, mla_g_kv=m_mla_g_kv, mla_w_uq=m_mla_w_uq,
                 mla_w_ukv=m_mla_w_ukv, w_branch=m_w_branch, w_out=m_w_out, ple_g=m_ple_g, w_ple_gate=m_w_ple_gate,
                 w_ple_proj=m_w_ple_proj, final_g=m_final_g)
    mom_v = dict(ln_g=v_ln_g, w_in=v_w_in, ssm_lam_re=v_ssm_lam_re, ssm_lam_im=v_ssm_lam_im, ssm_log_dt=v_ssm_log_dt,
                 ssm_b_re=v_ssm_b_re, ssm_b_im=v_ssm_b_im, ssm_c_re=v_ssm_c_re, ssm_c_im=v_ssm_c_im, ssm_d=v_ssm_d,
                 ssm_w_glu=v_ssm_w_glu, mla_g_q=v_mla_g_q, mla_g_kv=v_mla_g_kv, mla_w_uq=v_mla_w_uq,
                 mla_w_ukv=v_mla_w_ukv, w_branch=v_w_branch, w_out=v_w_out, ple_g=v_ple_g, w_ple_gate=v_w_ple_gate,
                 w_ple_proj=v_w_ple_proj, final_g=v_final_g)
    depth = ln_g.shape[0]
    assert depth == 2
    hooks = _Exchanged(weights)
    small = {nm: weights[nm] for nm in SMALL}
    small["ln_g"] = small["ln_g"].at[0, 0].add(hooks.token)
    loss_local, dx, gws, gsmall = _local_step(x[0], p[:, 0], loss_target[0], hooks, small)
    loss = lax.psum(loss_local, ("x", "y", "c"))

    grads, delta, new_m, new_v = {}, {}, {}, {}
    flat = jnp.concatenate([gsmall[nm].reshape(-1) for nm in SMALL])
    n_small = flat.shape[0]
    small_sent, token = _exchange_start([_pack_rows(flat, 8)], all_gather=True, name="gather_small_start")

    parts = hooks.partial_grads(dx)
    for nm, _, _ in SHARDED:
        w_nm = weights[nm] + token[0, 0] if nm == "w_ple_proj" else weights[nm]
        grads[nm], delta[nm], new_m[nm], new_v[nm] = _adamw_sharded(
            w_nm, [parts[li][nm] for li in range(depth)], mom_m[nm], mom_v[nm], name=f"adamw_{nm}")
    done = sum(new_v[nm][(0,) * new_v[nm].ndim] for nm, _, _ in SHARDED).reshape(1)
    got = _exchange_finish(small_sent, done, all_gather=True, name="gather_small_finish")[0]
    gsum = _sum8(got, name="sum_small_grads").reshape(-1)[:n_small]
    off = 0
    for nm in SMALL:
        cnt = weights[nm].size
        grads[nm] = gsum[off:off + cnt].reshape(weights[nm].shape)
        off += cnt
    pk = lambda d: _pack_rows(jnp.concatenate([d[nm].reshape(-1) for nm in SMALL]), 8)
    d_s, m_s, v_s = _adamw(pk(weights), pk(grads), pk(mom_m), pk(mom_v), name="adamw_small")
    off = 0
    for nm in SMALL:
        cnt = weights[nm].size
        shp = weights[nm].shape
        delta[nm] = d_s.reshape(-1)[off:off + cnt].reshape(shp)
        new_m[nm] = m_s.reshape(-1)[off:off + cnt].reshape(shp)
        new_v[nm] = v_s.reshape(-1)[off:off + cnt].reshape(shp)
        off += cnt
    return (loss, dx[None], *[grads[nm] for nm in WEIGHT_ORDER], *[delta[nm] for nm in WEIGHT_ORDER],
            *[new_m[nm] for nm in WEIGHT_ORDER], *[new_v[nm] for nm in WEIGHT_ORDER])
```
